```python
import math
import jax, jax.numpy as jnp
from jax import lax
import numpy as np

D_MODEL = 2048
BATCH = 1
SEQ = 16384
DEPTH = 1

RET_HEADS = 8
RET_DK = 128
RET_DV = 256
RET_CHUNK = 128
ROPE_BASE = 10000.0
MOBA_HEADS = 16
MOBA_HD = 128
MOBA_BLOCK = 256
MOBA_TOPK = 3
MOBA_Q_CHUNK = 64
D_FF = 5632
CONV_WIDTH = 3
NORM_EPS = 1e-6
GN_EPS = 1e-5

RET_QK_W = RET_HEADS * RET_DK
RET_V_W = RET_HEADS * RET_DV
MOBA_W = MOBA_HEADS * MOBA_HD
IN_SPLITS = (RET_QK_W, RET_QK_W, RET_V_W, RET_V_W, MOBA_W, MOBA_W, MOBA_W, D_MODEL, D_MODEL)
IN_WIDTH = sum(IN_SPLITS)
IN_OFFSETS = [int(o) for o in np.cumsum(IN_SPLITS)[:-1]]

kernel_name = "hybrid_retention_moba_convffn_block"


def rms_norm(x, w):
    xf = x.astype(jnp.float32)
    y = xf * lax.rsqrt(jnp.mean(xf * xf, axis=-1, keepdims=True) + NORM_EPS)
    return (y * w.astype(jnp.float32)).astype(x.dtype)


def rotary(x, pos):
    half = x.shape[-1] // 2
    inv = ROPE_BASE ** (-jnp.arange(half, dtype=jnp.float32) / half)
    ang = pos.astype(jnp.float32)[:, None] * inv[None, :]
    cos = jnp.cos(ang)[None, :, None, :]
    sin = jnp.sin(ang)[None, :, None, :]
    x1, x2 = x[..., :half], x[..., half:]
    return jnp.concatenate([x1 * cos - x2 * sin, x1 * sin + x2 * cos], axis=-1)


def retention(q, k, v):
    B, S, H, dk = q.shape
    dv = v.shape[-1]
    C = RET_CHUNK
    N = S // C
    f32 = jnp.float32
    log_g = jnp.log1p(-jnp.exp2(-5.0 - jnp.arange(H, dtype=f32)))
    q = q.astype(f32).reshape(B, N, C, H, dk)
    k = (k.astype(f32) * (dk ** -0.5)).reshape(B, N, C, H, dk)
    v = v.astype(f32).reshape(B, N, C, H, dv)
    idx = jnp.arange(C)
    diff = idx[:, None] - idx[None, :]
    dec = jnp.where(diff[None] >= 0,
                    jnp.exp(jnp.maximum(diff, 0).astype(f32)[None] * log_g[:, None, None]), 0.0)
    s = jnp.einsum('bnihd,bnjhd->bnhij', q, k) * dec[None, None]
    inner = jnp.einsum('bnhij,bnjhe->bnihe', s, v)
    zeta = jnp.exp((C - 1 - idx).astype(f32)[None, :] * log_g[:, None])
    kv = jnp.einsum('bnjhd,bnjhe->bnhde', k * zeta.T[None, None, :, :, None], v)
    g_chunk = jnp.exp(C * log_g)[None, :, None, None]

    def step(R, kv_n):
        return R * g_chunk + kv_n, R

    _, R_prev = lax.scan(step, jnp.zeros((B, H, dk, dv), f32), jnp.moveaxis(kv, 1, 0))
    R_prev = jnp.moveaxis(R_prev, 0, 1)
    xi = jnp.exp((idx + 1).astype(f32)[None, :] * log_g[:, None])
    cross = jnp.einsum('bnihd,bnhde->bnihe', q * xi.T[None, None, :, :, None], R_prev)
    return (inner + cross).reshape(B, S, H, dv)


def group_norm_heads(o):
    mu = jnp.mean(o, axis=-1, keepdims=True)
    var = jnp.mean(jnp.square(o - mu), axis=-1, keepdims=True)
    return (o - mu) * lax.rsqrt(var + GN_EPS)


def moba_attention(q, k, v):
    B, S, H, hd = q.shape
    dtype = q.dtype
    nb = -(-S // MOBA_BLOCK)
    pad = nb * MOBA_BLOCK - S
    kp = jnp.pad(k, ((0, 0), (0, pad), (0, 0), (0, 0))).reshape(B, nb, MOBA_BLOCK, H, hd)
    vp = jnp.pad(v, ((0, 0), (0, pad), (0, 0), (0, 0))).reshape(B, nb, MOBA_BLOCK, H, hd)
    k_mean = jnp.mean(kp.astype(jnp.float32), axis=2)
    kbh = jnp.transpose(kp, (0, 3, 1, 2, 4))
    vbh = jnp.transpose(vp, (0, 3, 1, 2, 4))
    n_sel = min(MOBA_TOPK, nb - 1)
    scale = 1.0 / math.sqrt(hd)
    QC = MOBA_Q_CHUNK
    nq = S // QC
    qc = jnp.moveaxis(q.reshape(B, nq, QC, H, hd), 1, 0)
    b_ix = jnp.arange(B)[:, None, None, None]
    h_ix = jnp.arange(H)[None, :, None, None]

    def step(args):
        qi, ci = args
        q_pos = ci * QC + jnp.arange(QC)
        qblk = (ci * QC) // MOBA_BLOCK
        k_own = lax.dynamic_index_in_dim(kp, qblk, axis=1, keepdims=False)
        v_own = lax.dynamic_index_in_dim(vp, qblk, axis=1, keepdims=False)
        k_pos = qblk * MOBA_BLOCK + jnp.arange(MOBA_BLOCK)
        s_own = jnp.einsum('bqhd,bkhd->bhqk', qi, k_own).astype(jnp.float32) * scale
        s_own = jnp.where((k_pos[None, :] <= q_pos[:, None])[None, None], s_own, -jnp.inf)
        if n_sel > 0:
            gate = jnp.einsum('bqhd,bnhd->bhqn', qi.astype(jnp.float32), k_mean)
            gate = jnp.where((jnp.arange(nb) < qblk)[None, None, None], gate, -jnp.inf)
            _, sel = lax.top_k(gate, n_sel)
            valid = sel < qblk
            k_sel = kbh[b_ix, h_ix, sel]
            v_sel = vbh[b_ix, h_ix, sel]
            s_sel = jnp.einsum('bqhd,bhqskd->bhqsk', qi, k_sel).astype(jnp.float32) * scale
            s_sel = jnp.where(valid[..., None], s_sel, -jnp.inf).reshape(B, H, QC, n_sel * MOBA_BLOCK)
            p = jax.nn.softmax(jnp.concatenate([s_own, s_sel], axis=-1), axis=-1).astype(dtype)
            p_own = p[..., :MOBA_BLOCK]
            p_sel = p[..., MOBA_BLOCK:].reshape(B, H, QC, n_sel, MOBA_BLOCK)
            o = (jnp.einsum('bhqk,bkhd->bqhd', p_own, v_own)
                 + jnp.einsum('bhqsk,bhqskd->bqhd', p_sel, v_sel))
        else:
            p_own = jax.nn.softmax(s_own, axis=-1).astype(dtype)
            o = jnp.einsum('bhqk,bkhd->bqhd', p_own, v_own)
        return o

    out = lax.map(step, (qc, jnp.arange(nq)))
    return jnp.moveaxis(out, 0, 1).reshape(B, S, H, hd)


def token_mixer(u, w_in, ret_w_o, moba_w_o, w_out):
    B, S, _ = u.shape
    dtype = u.dtype
    proj = u @ w_in
    rq, rk, rv, rg, mq, mk, mv, ga, gb = jnp.split(proj, IN_OFFSETS, axis=-1)
    pos = jnp.arange(S)
    rq = rotary(rq.astype(jnp.float32).reshape(B, S, RET_HEADS, RET_DK), pos)
    rk = rotary(rk.astype(jnp.float32).reshape(B, S, RET_HEADS, RET_DK), pos)
    o_ret = group_norm_heads(retention(rq, rk, rv.reshape(B, S, RET_HEADS, RET_DV)))
    o_ret = (jax.nn.silu(rg.astype(jnp.float32)) * o_ret.reshape(B, S, RET_V_W)).astype(dtype)
    y_ret = o_ret @ ret_w_o
    o_moba = moba_attention(mq.reshape(B, S, MOBA_HEADS, MOBA_HD),
                            mk.reshape(B, S, MOBA_HEADS, MOBA_HD),
                            mv.reshape(B, S, MOBA_HEADS, MOBA_HD))
    y_moba = o_moba.reshape(B, S, MOBA_W) @ moba_w_o
    merged = jax.nn.sigmoid(ga) * y_ret + jax.nn.sigmoid(gb) * y_moba
    return merged @ w_out


def conv_ffn(u, w_up, conv_w, conv_b, w_down):
    S = u.shape[1]
    h = u @ w_up
    hp = jnp.pad(h, ((0, 0), (CONV_WIDTH - 1, 0), (0, 0)))
    h = sum(hp[:, i:i + S] * conv_w[i] for i in range(CONV_WIDTH)) + conv_b
    a, b = jnp.split(h, 2, axis=-1)
    return (jax.nn.silu(a) * b) @ w_down


def setup_inputs(seed: int = 0) -> dict:
    key = jax.random.key(seed)
    ks = jax.random.split(key, 18)
    f32 = jnp.float32
    L, D = DEPTH, D_MODEL

    def nrm(k, shape, scale):
        return jax.random.normal(k, shape, f32) * scale

    return {
        "x": nrm(ks[0], (BATCH, SEQ, D), 1.0),
        "c": nrm(ks[1], (BATCH, D), 1.0),
        "ada_w": nrm(ks[2], (L, D, 6 * D), 0.01),
        "ada_b": nrm(ks[3], (L, 6 * D), 0.01),
        "mix_norm_pre": 1.0 + nrm(ks[4], (L, D), 0.05),
        "mix_norm_post": 1.0 + nrm(ks[5], (L, D), 0.05),
        "w_in": nrm(ks[6], (L, D, IN_WIDTH), D ** -0.5),
        "ret_w_o": nrm(ks[7], (L, RET_V_W, D), RET_V_W ** -0.5),
        "moba_w_o": nrm(ks[8], (L, MOBA_W, D), MOBA_W ** -0.5),
        "w_out": nrm(ks[9], (L, D, D), D ** -0.5),
        "ffn_norm_pre": 1.0 + nrm(ks[10], (L, D), 0.05),
        "ffn_norm_post": 1.0 + nrm(ks[11], (L, D), 0.05),
        "ffn_w_up": nrm(ks[12], (L, D, 2 * D_FF), D ** -0.5),
        "ffn_conv_w": nrm(ks[13], (L, CONV_WIDTH, 2 * D_FF), CONV_WIDTH ** -0.5),
        "ffn_conv_b": nrm(ks[14], (L, 2 * D_FF), 0.01),
        "ffn_w_down": nrm(ks[15], (L, D_FF, D), D_FF ** -0.5),
    }


def reference(x, c, ada_w, ada_b, mix_norm_pre, mix_norm_post, w_in, ret_w_o, moba_w_o, w_out,
              ffn_norm_pre, ffn_norm_post, ffn_w_up, ffn_conv_w, ffn_conv_b, ffn_w_down):
    c_act = jax.nn.silu(c)
    for l in range(DEPTH):
        mod = c_act @ ada_w[l] + ada_b[l]
        sh1, sc1, g1, sh2, sc2, g2 = [m[:, None, :] for m in jnp.split(mod, 6, axis=-1)]
        u = rms_norm(x, mix_norm_pre[l]) * (1.0 + sc1) + sh1
        y = token_mixer(u, w_in[l], ret_w_o[l], moba_w_o[l], w_out[l])
        x = x + (1.0 + g1) * rms_norm(y, mix_norm_post[l])
        u = rms_norm(x, ffn_norm_pre[l]) * (1.0 + sc2) + sh2
        y = conv_ffn(u, ffn_w_up[l], ffn_conv_w[l], ffn_conv_b[l], ffn_w_down[l])
        x = x + (1.0 + g2) * rms_norm(y, ffn_norm_post[l])
    return x
```

```python
import functools
import math

import jax
import jax.numpy as jnp
from jax import lax
from jax.experimental import pallas as pl
from jax.experimental.pallas import tpu as pltpu

D_MODEL = 2048
RET_HEADS = 8
RET_DK = 128
RET_DV = 256
ROPE_BASE = 10000.0
MOBA_HEADS = 16
MOBA_HD = 128
MOBA_BLOCK = 256
MOBA_TOPK = 3
D_FF = 5632
CONV_WIDTH = 3
NORM_EPS = 1e-6
GN_EPS = 1e-5

RET_QK_W = RET_HEADS * RET_DK
RET_V_W = RET_HEADS * RET_DV
MOBA_W = MOBA_HEADS * MOBA_HD
OFF_RQ = 0
OFF_RK = OFF_RQ + RET_QK_W
OFF_RV = OFF_RK + RET_QK_W
OFF_RG = OFF_RV + RET_V_W
OFF_MQ = OFF_RG + RET_V_W
OFF_MK = OFF_MQ + MOBA_W
OFF_MV = OFF_MK + MOBA_W
OFF_GA = OFF_MV + MOBA_W
OFF_GB = OFF_GA + D_MODEL
IN_WIDTH = OFF_GB + D_MODEL

LANES = 128
SUBLANES = 8
NEG = -1e30
VMEM_LIMIT = 56 * 1024 * 1024

F32 = jnp.float32
BF16 = jnp.bfloat16


def _cparams(*sem):
    return pltpu.CompilerParams(dimension_semantics=sem, vmem_limit_bytes=VMEM_LIMIT)


def _dot(a, b):
    return jnp.dot(a, b, preferred_element_type=F32)


def _dot_nt(a, b):
    return lax.dot_general(a, b, (((1,), (1,)), ((), ())), preferred_element_type=F32)


def _split_bf16(x):
    hi = x.astype(BF16)
    lo = (x - hi.astype(F32)).astype(BF16)
    return hi, lo


def _rms(x):
    return x * lax.rsqrt(jnp.mean(x * x, axis=-1, keepdims=True) + NORM_EPS)


def _silu(x):
    return x * jax.nn.sigmoid(x)


def _ada_kernel(c_ref, w_ref, b_ref, o_ref):
    c = jnp.broadcast_to(_silu(c_ref[...]), (SUBLANES, c_ref.shape[1]))
    c_hi, c_lo = _split_bf16(c)
    w_hi, w_lo = _split_bf16(w_ref[...])
    r = _dot(c_hi, w_hi) + _dot(c_lo, w_hi) + _dot(c_hi, w_lo)
    o_ref[...] = r[0:1] + b_ref[...]


def _ada(c, w, b, tn=1024):
    d, n = w.shape
    return pl.pallas_call(
        _ada_kernel,
        grid=(n // tn,),
        in_specs=[pl.BlockSpec((1, d), lambda j: (0, 0)),
                  pl.BlockSpec((d, tn), lambda j: (0, j)),
                  pl.BlockSpec((1, tn), lambda j: (0, j))],
        out_specs=pl.BlockSpec((1, tn), lambda j: (0, j)),
        out_shape=jax.ShapeDtypeStruct((1, n), F32),
        compiler_params=_cparams("arbitrary"),
        name="ada",
    )(c, w, b)


def _rope_kernel(inv_ref, cos_ref, sin_ref):
    ts = cos_ref.shape[0]
    row = lax.broadcasted_iota(jnp.int32, (ts, LANES), 0) + pl.program_id(0) * ts
    lane = lax.broadcasted_iota(jnp.int32, (ts, LANES), 1)
    ang = row.astype(F32) * inv_ref[...]
    sin = jnp.sin(ang)
    cos_ref[...] = jnp.cos(ang)
    sin_ref[...] = jnp.where(lane < RET_DK // 2, -sin, sin)


def _rope_tables(s, ts=2048):
    ts = min(ts, s)
    half = RET_DK // 2
    inv = ROPE_BASE ** (-jnp.arange(half, dtype=F32) / half)
    inv2 = jnp.concatenate([inv, inv])[None, :]
    return pl.pallas_call(
        _rope_kernel,
        grid=(s // ts,),
        in_specs=[pl.BlockSpec((1, LANES), lambda i: (0, 0))],
        out_specs=[pl.BlockSpec((ts, LANES), lambda i: (i, 0))] * 2,
        out_shape=[jax.ShapeDtypeStruct((s, LANES), F32)] * 2,
        compiler_params=_cparams("arbitrary"),
        name="rope",
    )(inv2)


def _inproj_kernel(x_ref, nw_ref, sc_ref, sh_ref, cos_ref, sin_ref, w_ref, o_ref, u_ref):
    j = pl.program_id(1)
    tn = o_ref.shape[1]

    @pl.when(j == 0)
    def _():
        y = _rms(x_ref[...]) * nw_ref[...]
        u_ref[...] = (y * (1.0 + sc_ref[...]) + sh_ref[...]).astype(BF16)

    r = _dot(u_ref[...], w_ref[...])
    q_tiles = RET_QK_W // tn

    @pl.when(j < 2 * q_tiles)
    def _():
        scale = jnp.where(j >= q_tiles, RET_DK ** -0.5, 1.0).astype(F32)
        cos = cos_ref[...]
        sin = sin_ref[...]
        for h in range(tn // RET_DK):
            xh = r[:, h * RET_DK:(h + 1) * RET_DK]
            rot = xh * cos + pltpu.roll(xh, RET_DK // 2, axis=1) * sin
            o_ref[:, h * RET_DK:(h + 1) * RET_DK] = (rot * scale).astype(BF16)

    @pl.when(j >= 2 * q_tiles)
    def _():
        o_ref[...] = r.astype(BF16)


def _inproj(x, nw, sc, sh, cos2, sin2, w, tm=1024, tn=1024):
    s, d = x.shape
    n = w.shape[1]
    tm = min(tm, s)
    vec = pl.BlockSpec((1, d), lambda i, j: (0, 0))
    tab = pl.BlockSpec((tm, LANES), lambda i, j: (i, 0))
    return pl.pallas_call(
        _inproj_kernel,
        grid=(s // tm, n // tn),
        in_specs=[pl.BlockSpec((tm, d), lambda i, j: (i, 0)), vec, vec, vec, tab, tab,
                  pl.BlockSpec((d, tn), lambda i, j: (0, j))],
        out_specs=pl.BlockSpec((tm, tn), lambda i, j: (i, j)),
        out_shape=jax.ShapeDtypeStruct((s, n), BF16),
        scratch_shapes=[pltpu.VMEM((tm, d), BF16)],
        compiler_params=_cparams("arbitrary", "arbitrary"),
        name="inproj",
    )(x, nw, sc, sh, cos2, sin2, w)


def _ret_kernel(lg_ref, q_ref, k_ref, v_ref, g_ref, o_ref, r_ref, dec_ref, xi_ref, zeta_ref):
    n = pl.program_id(1)
    c = q_ref.shape[0]
    lg = lg_ref[0, 0:1, 0:1]

    @pl.when(n == 0)
    def _():
        r_ref[...] = jnp.zeros_like(r_ref)
        i = lax.broadcasted_iota(jnp.int32, (c, c), 0)
        j = lax.broadcasted_iota(jnp.int32, (c, c), 1)
        d = i - j
        dec_ref[...] = jnp.where(d >= 0, jnp.exp(jnp.maximum(d, 0).astype(F32) * lg), 0.0)
        row = lax.broadcasted_iota(jnp.int32, (c, RET_DK), 0)
        xi_ref[...] = jnp.exp((row + 1).astype(F32) * lg)
        zeta_ref[...] = jnp.exp((c - 1 - row).astype(F32) * lg)

    q = q_ref[...]
    k = k_ref[...]
    v = v_ref[...]
    s = _dot_nt(q, k) * dec_ref[...]
    inner = _dot(s.astype(BF16), v)
    r = r_ref[...]
    cross = _dot((q.astype(F32) * xi_ref[...]).astype(BF16), r.astype(BF16))
    o = inner + cross
    kz = (k.astype(F32) * zeta_ref[...]).T.astype(BF16)
    r_ref[...] = r * jnp.exp(lg * c) + _dot(kz, v)
    oc = o - jnp.mean(o, axis=-1, keepdims=True)
    on = oc * lax.rsqrt(jnp.mean(oc * oc, axis=-1, keepdims=True) + GN_EPS)
    o_ref[...] = (_silu(g_ref[...].astype(F32)) * on).astype(BF16)


def _retention(proj, chunk=512):
    s = proj.shape[0]
    chunk = min(chunk, s)
    log_g = jnp.log1p(-jnp.exp2(-5.0 - jnp.arange(RET_HEADS, dtype=F32)))
    lg = jnp.broadcast_to(log_g[:, None, None], (RET_HEADS, SUBLANES, LANES))
    return pl.pallas_call(
        _ret_kernel,
        grid=(RET_HEADS, s // chunk),
        in_specs=[pl.BlockSpec((1, SUBLANES, LANES), lambda h, n: (h, 0, 0)),
                  pl.BlockSpec((chunk, RET_DK), lambda h, n: (n, OFF_RQ // RET_DK + h)),
                  pl.BlockSpec((chunk, RET_DK), lambda h, n: (n, OFF_RK // RET_DK + h)),
                  pl.BlockSpec((chunk, RET_DV), lambda h, n: (n, OFF_RV // RET_DV + h)),
                  pl.BlockSpec((chunk, RET_DV), lambda h, n: (n, OFF_RG // RET_DV + h))],
        out_specs=pl.BlockSpec((chunk, RET_DV), lambda h, n: (n, h)),
        out_shape=jax.ShapeDtypeStruct((s, RET_V_W), BF16),
        scratch_shapes=[pltpu.VMEM((RET_DK, RET_DV), F32),
                        pltpu.VMEM((chunk, chunk), F32),
                        pltpu.VMEM((chunk, RET_DK), F32),
                        pltpu.VMEM((chunk, RET_DK), F32)],
        compiler_params=_cparams("arbitrary", "arbitrary"),
        name="ret",
    )(lg, proj, proj, proj, proj)


def _moba_kernel(q_ref, k_ref, v_ref, o_ref, km_ref):
    b = pl.program_id(1)
    blk = q_ref.shape[0]
    nb = k_ref.shape[0] // blk
    scale = 1.0 / math.sqrt(MOBA_HD)

    @pl.when(b == 0)
    def _():
        km_ref[...] = jnp.zeros_like(km_ref)

        def body(n, carry):
            kb = k_ref[pl.ds(pl.multiple_of(n * blk, blk), blk), :].astype(F32)
            km_ref[pl.ds(n, 1), :] = jnp.mean(kb, axis=0, keepdims=True)
            return carry

        lax.fori_loop(0, nb, body, 0)

    q = q_ref[...]
    off = pl.multiple_of(b * blk, blk)
    s = _dot_nt(q, k_ref[pl.ds(off, blk), :]) * scale
    row = lax.broadcasted_iota(jnp.int32, (blk, blk), 0)
    col = lax.broadcasted_iota(jnp.int32, (blk, blk), 1)
    s = jnp.where(col <= row, s, NEG)
    m0 = jnp.max(s, axis=-1, keepdims=True)
    p = jnp.exp(s - m0)
    l0 = jnp.sum(p, axis=-1, keepdims=True)
    acc0 = _dot(p.astype(BF16), v_ref[pl.ds(off, blk), :])

    km_hi, km_lo = _split_bf16(km_ref[...])
    gate = _dot_nt(q, km_hi) + _dot_nt(q, km_lo)
    lane_i = lax.broadcasted_iota(jnp.int32, (blk, LANES), 1)
    lane = lane_i.astype(F32)
    g = jnp.where(lane_i < b, gate, NEG)
    sel = jnp.zeros((blk, LANES), jnp.bool_)
    for _ in range(MOBA_TOPK):
        mx = jnp.max(g, axis=-1, keepdims=True)
        idx = jnp.min(jnp.where(g == mx, lane, float(LANES)), axis=-1, keepdims=True)
        hit = lane == idx
        sel = sel | (hit & (mx > 0.5 * NEG))
        g = jnp.where(hit, 2.0 * NEG, g)
    bias = jnp.where(sel, 0.0, NEG)

    def body(n, carry):
        m, l, acc = carry
        o = pl.multiple_of(n * blk, blk)
        sn = _dot_nt(q, k_ref[pl.ds(o, blk), :]) * scale
        bcol = jnp.sum(jnp.where(lane_i == n, bias, 0.0), axis=-1, keepdims=True)
        m_new = jnp.maximum(m, jnp.max(sn, axis=-1, keepdims=True) + bcol)
        alpha = jnp.exp(m - m_new)
        pn = jnp.exp(sn + (bcol - m_new))
        l = alpha * l + jnp.sum(pn, axis=-1, keepdims=True)
        acc = alpha * acc + _dot(pn.astype(BF16), v_ref[pl.ds(o, blk), :])
        return m_new, l, acc

    m, l, acc = lax.fori_loop(0, b, body, (m0, l0, acc0))
    o_ref[...] = (acc / l).astype(BF16)


def _moba(proj):
    s = proj.shape[0]
    nb = s // MOBA_BLOCK
    assert s % MOBA_BLOCK == 0 and nb <= LANES
    return pl.pallas_call(
        _moba_kernel,
        grid=(MOBA_HEADS, nb),
        in_specs=[pl.BlockSpec((MOBA_BLOCK, MOBA_HD), lambda h, b: (b, OFF_MQ // MOBA_HD + h)),
                  pl.BlockSpec((s, MOBA_HD), lambda h, b: (0, OFF_MK // MOBA_HD + h)),
                  pl.BlockSpec((s, MOBA_HD), lambda h, b: (0, OFF_MV // MOBA_HD + h))],
        out_specs=pl.BlockSpec((MOBA_BLOCK, MOBA_HD), lambda h, b: (b, h)),
        out_shape=jax.ShapeDtypeStruct((s, MOBA_W), BF16),
        scratch_shapes=[pltpu.VMEM((LANES, MOBA_HD), F32)],
        compiler_params=_cparams("arbitrary", "arbitrary"),
        name="moba",
    )(proj, proj, proj)


def _merge_kernel(or_ref, om_ref, ga_ref, gb_ref, wr_ref, wm_ref, o_ref):
    yr = _dot(or_ref[...], wr_ref[...])
    ym = _dot(om_ref[...], wm_ref[...])
    ga = jax.nn.sigmoid(ga_ref[...].astype(F32))
    gb = jax.nn.sigmoid(gb_ref[...].astype(F32))
    o_ref[...] = (ga * yr + gb * ym).astype(BF16)


def _merge(o_ret, o_moba, proj, w_r, w_m, tm=1024, tn=512):
    s = o_ret.shape[0]
    tm = min(tm, s)
    d = D_MODEL
    return pl.pallas_call(
        _merge_kernel,
        grid=(s // tm, d // tn),
        in_specs=[pl.BlockSpec((tm, RET_V_W), lambda i, j: (i, 0)),
                  pl.BlockSpec((tm, MOBA_W), lambda i, j: (i, 0)),
                  pl.BlockSpec((tm, tn), lambda i, j: (i, OFF_GA // tn + j)),
                  pl.BlockSpec((tm, tn), lambda i, j: (i, OFF_GB // tn + j)),
                  pl.BlockSpec((RET_V_W, tn), lambda i, j: (0, j)),
                  pl.BlockSpec((MOBA_W, tn), lambda i, j: (0, j))],
        out_specs=pl.BlockSpec((tm, tn), lambda i, j: (i, j)),
        out_shape=jax.ShapeDtypeStruct((s, d), BF16),
        compiler_params=_cparams("arbitrary", "arbitrary"),
        name="merge",
    )(o_ret, o_moba, proj, proj, w_r, w_m)


def _mixout_kernel(m_ref, wo_ref, x_ref, npost_ref, g1_ref, npre_ref, sc_ref, sh_ref, x1_ref, u2_ref):
    y = _dot(m_ref[...], wo_ref[...])
    x1 = x_ref[...] + (1.0 + g1_ref[...]) * (_rms(y) * npost_ref[...])
    x1_ref[...] = x1
    u2_ref[...] = ((_rms(x1) * npre_ref[...]) * (1.0 + sc_ref[...]) + sh_ref[...]).astype(BF16)


def _mixout(merged, w_o, x, npost, g1, npre, sc2, sh2, tm=256):
    s, d = x.shape
    tm = min(tm, s)
    vec = pl.BlockSpec((1, d), lambda i: (0, 0))
    rows = pl.BlockSpec((tm, d), lambda i: (i, 0))
    return pl.pallas_call(
        _mixout_kernel,
        grid=(s // tm,),
        in_specs=[rows, pl.BlockSpec((d, d), lambda i: (0, 0)), rows, vec, vec, vec, vec, vec],
        out_specs=[rows, rows],
        out_shape=[jax.ShapeDtypeStruct((s, d), F32), jax.ShapeDtypeStruct((s, d), BF16)],
        compiler_params=_cparams("arbitrary"),
        name="mixout",
    )(merged, w_o, x, npost, g1, npre, sc2, sh2)


def _ffn_kernel(u_ref, wa_ref, wb_ref, cwa_ref, cwb_ref, cba_ref, cbb_ref, wd_ref, x1_ref, npost_ref, g2_ref,
                o_ref, acc_ref, ha_ref, hb_ref, tail_ref):
    i = pl.program_id(0)
    j = pl.program_id(1)
    tm = u_ref.shape[0]
    halo = SUBLANES
    u = u_ref[...]

    def conv(h, hbuf, cw_ref, cb_ref, slot):
        @pl.when(i == 0)
        def _():
            hbuf[0:halo, :] = jnp.zeros((halo, h.shape[1]), F32)

        @pl.when(i > 0)
        def _():
            hbuf[0:halo, :] = tail_ref[j, slot]

        hbuf[halo:halo + tm, :] = h
        tail_ref[j, slot] = h[tm - halo:tm, :]
        cw = cw_ref[...]
        out = h * cw[CONV_WIDTH - 1:CONV_WIDTH, :] + cb_ref[...]
        for t in range(1, CONV_WIDTH):
            out = out + hbuf[halo - t:halo - t + tm, :] * cw[CONV_WIDTH - 1 - t:CONV_WIDTH - t, :]
        return out

    a = conv(_dot(u, wa_ref[...]), ha_ref, cwa_ref, cba_ref, 0)
    bgate = conv(_dot(u, wb_ref[...]), hb_ref, cwb_ref, cbb_ref, 1)
    part = _dot((_silu(a) * bgate).astype(BF16), wd_ref[...])

    @pl.when(j == 0)
    def _():
        acc_ref[...] = part

    @pl.when(j > 0)
    def _():
        acc_ref[...] += part

    @pl.when(j == pl.num_programs(1) - 1)
    def _():
        y = acc_ref[...]
        o_ref[...] = x1_ref[...] + (1.0 + g2_ref[...]) * (_rms(y) * npost_ref[...])


def _ffn(u2, w_up, conv_w, conv_b, w_down, x1, npost, g2, tm=512, tn=512):
    s, d = x1.shape
    tm = min(tm, s)
    nj = D_FF // tn
    assert D_FF % tn == 0
    vec = pl.BlockSpec((1, d), lambda i, j: (0, 0))
    rows = pl.BlockSpec((tm, d), lambda i, j: (i, 0))
    return pl.pallas_call(
        _ffn_kernel,
        grid=(s // tm, nj),
        in_specs=[rows,
                  pl.BlockSpec((d, tn), lambda i, j: (0, j)),
                  pl.BlockSpec((d, tn), lambda i, j: (0, j + nj)),
                  pl.BlockSpec((CONV_WIDTH, tn), lambda i, j: (0, j)),
                  pl.BlockSpec((CONV_WIDTH, tn), lambda i, j: (0, j + nj)),
                  pl.BlockSpec((1, tn), lambda i, j: (0, j)),
                  pl.BlockSpec((1, tn), lambda i, j: (0, j + nj)),
                  pl.BlockSpec((tn, d), lambda i, j: (j, 0)),
                  rows, vec, vec],
        out_specs=rows,
        out_shape=jax.ShapeDtypeStruct((s, d), F32),
        scratch_shapes=[pltpu.VMEM((tm, d), F32),
                        pltpu.VMEM((tm + SUBLANES, tn), F32),
                        pltpu.VMEM((tm + SUBLANES, tn), F32),
                        pltpu.VMEM((nj, 2, SUBLANES, tn), F32)],
        compiler_params=_cparams("arbitrary", "arbitrary"),
        name="ffn",
    )(u2, w_up, w_up, conv_w, conv_w, conv_b, conv_b, w_down, x1, npost, g2)


def _block(x, c, ada_w, ada_b, mix_norm_pre, mix_norm_post, w_in, ret_w_o, moba_w_o, w_out,
           ffn_norm_pre, ffn_norm_post, ffn_w_up, ffn_conv_w, ffn_conv_b, ffn_w_down, cos2, sin2):
    d = x.shape[1]
    mod = _ada(c, ada_w, ada_b[None, :])
    sh1, sc1, g1, sh2, sc2, g2 = [mod[:, t * d:(t + 1) * d] for t in range(6)]
    proj = _inproj(x, mix_norm_pre[None, :], sc1, sh1, cos2, sin2, w_in.astype(BF16))
    o_ret = _retention(proj)
    o_moba = _moba(proj)
    merged = _merge(o_ret, o_moba, proj, ret_w_o.astype(BF16), moba_w_o.astype(BF16))
    x1, u2 = _mixout(merged, w_out.astype(BF16), x, mix_norm_post[None, :], g1,
                     ffn_norm_pre[None, :], sc2, sh2)
    return _ffn(u2, ffn_w_up.astype(BF16), ffn_conv_w, ffn_conv_b[None, :], ffn_w_down.astype(BF16),
                x1, ffn_norm_post[None, :], g2)


def kernel(x, c, ada_w, ada_b, mix_norm_pre, mix_norm_post, w_in, ret_w_o, moba_w_o, w_out,
           ffn_norm_pre, ffn_norm_post, ffn_w_up, ffn_conv_w, ffn_conv_b, ffn_w_down):
    batch, s, _ = x.shape
    depth = ada_w.shape[0]
    cos2, sin2 = _rope_tables(s)
    outs = []
    for bi in range(batch):
        xb = x[bi]
        for l in range(depth):
            xb = _block(xb, c[bi:bi + 1], ada_w[l], ada_b[l], mix_norm_pre[l], mix_norm_post[l], w_in[l],
                        ret_w_o[l], moba_w_o[l], w_out[l], ffn_norm_pre[l], ffn_norm_post[l], ffn_w_up[l],
                        ffn_conv_w[l], ffn_conv_b[l], ffn_w_down[l], cos2, sin2)
        outs.append(xb)
    return jnp.stack(outs)
```

```python
import functools
import math

import jax
import jax.numpy as jnp
from jax import lax
from jax.experimental import pallas as pl
from jax.experimental.pallas import tpu as pltpu

D_MODEL = 2048
RET_HEADS = 8
RET_DK = 128
RET_DV = 256
ROPE_BASE = 10000.0
MOBA_HEADS = 16
MOBA_HD = 128
MOBA_BLOCK = 256
MOBA_TOPK = 3
D_FF = 5632
CONV_WIDTH = 3
NORM_EPS = 1e-6
GN_EPS = 1e-5

RET_QK_W = RET_HEADS * RET_DK
RET_V_W = RET_HEADS * RET_DV
MOBA_W = MOBA_HEADS * MOBA_HD
OFF_RQ = 0
OFF_RK = OFF_RQ + RET_QK_W
OFF_RV = OFF_RK + RET_QK_W
OFF_RG = OFF_RV + RET_V_W
OFF_MQ = OFF_RG + RET_V_W
OFF_MK = OFF_MQ + MOBA_W
OFF_MV = OFF_MK + MOBA_W
OFF_GA = OFF_MV + MOBA_W
OFF_GB = OFF_GA + D_MODEL
IN_WIDTH = OFF_GB + D_MODEL

LANES = 128
SUBLANES = 8
NEG = -1e30
MOBA_QSCALE = math.log2(math.e) / math.sqrt(MOBA_HD)
MOBA_GROUP = 4
MOBA_PAIR = 2
VMEM_LIMIT = 56 * 1024 * 1024

F32 = jnp.float32
BF16 = jnp.bfloat16


def _cparams(*sem):
    return pltpu.CompilerParams(dimension_semantics=sem, vmem_limit_bytes=VMEM_LIMIT)


def _dot(a, b):
    return jnp.dot(a, b, preferred_element_type=F32)


def _dot_nt(a, b):
    return lax.dot_general(a, b, (((1,), (1,)), ((), ())), preferred_element_type=F32)


def _split_bf16(x):
    hi = x.astype(BF16)
    lo = (x - hi.astype(F32)).astype(BF16)
    return hi, lo


def _rms(x):
    return x * lax.rsqrt(jnp.mean(x * x, axis=-1, keepdims=True) + NORM_EPS)


def _silu(x):
    return x * jax.nn.sigmoid(x)


def _ada_kernel(c_ref, w_ref, b_ref, o_ref):
    c = jnp.broadcast_to(_silu(c_ref[...]), (SUBLANES, c_ref.shape[1]))
    c_hi, c_lo = _split_bf16(c)
    w_hi, w_lo = _split_bf16(w_ref[...])
    r = _dot(c_hi, w_hi) + _dot(c_lo, w_hi) + _dot(c_hi, w_lo)
    o_ref[...] = r[0:1] + b_ref[...]


def _ada(c, w, b, tn=1024):
    d, n = w.shape
    return pl.pallas_call(
        _ada_kernel,
        grid=(n // tn,),
        in_specs=[pl.BlockSpec((1, d), lambda j: (0, 0)),
                  pl.BlockSpec((d, tn), lambda j: (0, j)),
                  pl.BlockSpec((1, tn), lambda j: (0, j))],
        out_specs=pl.BlockSpec((1, tn), lambda j: (0, j)),
        out_shape=jax.ShapeDtypeStruct((1, n), F32),
        compiler_params=_cparams("arbitrary"),
        name="ada",
    )(c, w, b)


def _rope_kernel(inv_ref, cos_ref, sin_ref):
    ts = cos_ref.shape[0]
    row = lax.broadcasted_iota(jnp.int32, (ts, LANES), 0) + pl.program_id(0) * ts
    lane = lax.broadcasted_iota(jnp.int32, (ts, LANES), 1)
    ang = row.astype(F32) * inv_ref[...]
    sin = jnp.sin(ang)
    cos_ref[...] = jnp.cos(ang)
    sin_ref[...] = jnp.where(lane < RET_DK // 2, -sin, sin)


def _rope_tables(s, ts=2048):
    ts = min(ts, s)
    half = RET_DK // 2
    inv = ROPE_BASE ** (-jnp.arange(half, dtype=F32) / half)
    inv2 = jnp.concatenate([inv, inv])[None, :]
    return pl.pallas_call(
        _rope_kernel,
        grid=(s // ts,),
        in_specs=[pl.BlockSpec((1, LANES), lambda i: (0, 0))],
        out_specs=[pl.BlockSpec((ts, LANES), lambda i: (i, 0))] * 2,
        out_shape=[jax.ShapeDtypeStruct((s, LANES), F32)] * 2,
        compiler_params=_cparams("arbitrary"),
        name="rope",
    )(inv2)


def _inproj_kernel(x_ref, nw_ref, sc_ref, sh_ref, cos_ref, sin_ref, w_ref, o_ref, u_ref):
    j = pl.program_id(1)
    tn = o_ref.shape[1]

    @pl.when(j == 0)
    def _():
        y = _rms(x_ref[...]) * nw_ref[...]
        u_ref[...] = (y * (1.0 + sc_ref[...]) + sh_ref[...]).astype(BF16)

    r = _dot(u_ref[...], w_ref[...])
    q_tiles = RET_QK_W // tn

    @pl.when(j < 2 * q_tiles)
    def _():
        scale = jnp.where(j >= q_tiles, RET_DK ** -0.5, 1.0).astype(F32)
        cos = cos_ref[...]
        sin = sin_ref[...]
        for h in range(tn // RET_DK):
            xh = r[:, h * RET_DK:(h + 1) * RET_DK]
            rot = xh * cos + pltpu.roll(xh, RET_DK // 2, axis=1) * sin
            o_ref[:, h * RET_DK:(h + 1) * RET_DK] = (rot * scale).astype(BF16)

    is_mq = (j >= OFF_MQ // tn) & (j < OFF_MK // tn)

    @pl.when(is_mq)
    def _():
        o_ref[...] = (r * MOBA_QSCALE).astype(BF16)

    @pl.when((j >= 2 * q_tiles) & jnp.logical_not(is_mq))
    def _():
        o_ref[...] = r.astype(BF16)


def _inproj(x, nw, sc, sh, cos2, sin2, w, tm=1024, tn=1024):
    s, d = x.shape
    n = w.shape[1]
    tm = min(tm, s)
    vec = pl.BlockSpec((1, d), lambda i, j: (0, 0))
    tab = pl.BlockSpec((tm, LANES), lambda i, j: (i, 0))
    return pl.pallas_call(
        _inproj_kernel,
        grid=(s // tm, n // tn),
        in_specs=[pl.BlockSpec((tm, d), lambda i, j: (i, 0)), vec, vec, vec, tab, tab,
                  pl.BlockSpec((d, tn), lambda i, j: (0, j))],
        out_specs=pl.BlockSpec((tm, tn), lambda i, j: (i, j)),
        out_shape=jax.ShapeDtypeStruct((s, n), BF16),
        scratch_shapes=[pltpu.VMEM((tm, d), BF16)],
        compiler_params=_cparams("arbitrary", "arbitrary"),
        name="inproj",
    )(x, nw, sc, sh, cos2, sin2, w)


def _ret_kernel(lg_ref, q_ref, k_ref, v_ref, g_ref, o_ref, r_ref, dec_ref, xi_ref, zeta_ref):
    n = pl.program_id(1)
    c = q_ref.shape[0]
    lg = lg_ref[0, 0:1, 0:1]

    @pl.when(n == 0)
    def _():
        r_ref[...] = jnp.zeros_like(r_ref)
        i = lax.broadcasted_iota(jnp.int32, (c, c), 0)
        j = lax.broadcasted_iota(jnp.int32, (c, c), 1)
        d = i - j
        dec_ref[...] = jnp.where(d >= 0, jnp.exp(jnp.maximum(d, 0).astype(F32) * lg), 0.0)
        row = lax.broadcasted_iota(jnp.int32, (c, RET_DK), 0)
        xi_ref[...] = jnp.exp((row + 1).astype(F32) * lg)
        zeta_ref[...] = jnp.exp((c - 1 - row).astype(F32) * lg)

    q = q_ref[...]
    k = k_ref[...]
    v = v_ref[...]
    s = _dot_nt(q, k) * dec_ref[...]
    inner = _dot(s.astype(BF16), v)
    r = r_ref[...]
    cross = _dot((q.astype(F32) * xi_ref[...]).astype(BF16), r.astype(BF16))
    o = inner + cross
    kz = (k.astype(F32) * zeta_ref[...]).T.astype(BF16)
    r_ref[...] = r * jnp.exp(lg * c) + _dot(kz, v)
    oc = o - jnp.mean(o, axis=-1, keepdims=True)
    on = oc * lax.rsqrt(jnp.mean(oc * oc, axis=-1, keepdims=True) + GN_EPS)
    o_ref[...] = (_silu(g_ref[...].astype(F32)) * on).astype(BF16)


def _retention(proj, chunk=512):
    s = proj.shape[0]
    chunk = min(chunk, s)
    log_g = jnp.log1p(-jnp.exp2(-5.0 - jnp.arange(RET_HEADS, dtype=F32)))
    lg = jnp.broadcast_to(log_g[:, None, None], (RET_HEADS, SUBLANES, LANES))
    return pl.pallas_call(
        _ret_kernel,
        grid=(RET_HEADS, s // chunk),
        in_specs=[pl.BlockSpec((1, SUBLANES, LANES), lambda h, n: (h, 0, 0)),
                  pl.BlockSpec((chunk, RET_DK), lambda h, n: (n, OFF_RQ // RET_DK + h)),
                  pl.BlockSpec((chunk, RET_DK), lambda h, n: (n, OFF_RK // RET_DK + h)),
                  pl.BlockSpec((chunk, RET_DV), lambda h, n: (n, OFF_RV // RET_DV + h)),
                  pl.BlockSpec((chunk, RET_DV), lambda h, n: (n, OFF_RG // RET_DV + h))],
        out_specs=pl.BlockSpec((chunk, RET_DV), lambda h, n: (n, h)),
        out_shape=jax.ShapeDtypeStruct((s, RET_V_W), BF16),
        scratch_shapes=[pltpu.VMEM((RET_DK, RET_DV), F32),
                        pltpu.VMEM((chunk, chunk), F32),
                        pltpu.VMEM((chunk, RET_DK), F32),
                        pltpu.VMEM((chunk, RET_DK), F32)],
        compiler_params=_cparams("arbitrary", "arbitrary"),
        name="ret",
    )(lg, proj, proj, proj, proj)


def _moba_kernel(q_ref, k_ref, v_ref, o_ref, km_ref, oh_ref, qx_ref, acc_ref, m_ref, s0_ref, s1_ref):
    hp = pl.program_id(0)
    b = pl.program_id(1)
    blk = q_ref.shape[0]
    hd = MOBA_HD
    nb = k_ref.shape[0] // blk
    gw = MOBA_GROUP * blk
    lane_i = lax.broadcasted_iota(jnp.int32, (blk, LANES), 1)

    @pl.when((hp == 0) & (b == 0))
    def _():
        def body(n, carry):
            oh_ref[pl.ds(pl.multiple_of(n * blk, blk), blk), :] = jnp.where(lane_i == n, 1.0, 0.0).astype(BF16)
            return carry

        lax.fori_loop(0, nb, body, 0)

    @pl.when(b == 0)
    def _():
        km_ref[...] = jnp.zeros_like(km_ref)

        def body(n, carry):
            kb = k_ref[pl.ds(pl.multiple_of(n * blk, blk), blk), :].astype(F32)
            km_ref[pl.ds(n, 1), :] = jnp.mean(kb, axis=0, keepdims=True)
            return carry

        lax.fori_loop(0, nb, body, 0)

    lane = lane_i.astype(F32)
    for h in range(MOBA_PAIR):
        q = q_ref[:, h * hd:(h + 1) * hd]
        km_hi, km_lo = _split_bf16(km_ref[:, h * hd:(h + 1) * hd])
        g = jnp.where(lane_i < b, _dot_nt(q, km_hi) + _dot_nt(q, km_lo), NEG)
        sel = lane_i == b
        for _ in range(MOBA_TOPK):
            mx = jnp.max(g, axis=-1, keepdims=True)
            idx = jnp.min(jnp.where(g == mx, lane, float(LANES)), axis=-1, keepdims=True)
            hit = lane == idx
            sel = sel | (hit & (mx > 0.5 * NEG))
            g = jnp.where(hit, 2.0 * NEG, g)
        qx_ref[h] = jnp.concatenate([q, jnp.where(sel, 0.0, NEG).astype(BF16)], axis=1)
        m_ref[h] = jnp.full((blk, 1), NEG, F32)
        acc_ref[h] = jnp.zeros((blk, 2 * hd), F32)

    ones = jnp.ones((gw, hd), BF16)

    def rows_of(g):
        return pl.ds(pl.multiple_of(g * gw, gw), gw)

    def scores(g, slot_ref):
        rows = rows_of(g)
        oh = oh_ref[rows, :]
        for h in range(MOBA_PAIR):
            k_ext = jnp.concatenate([k_ref[rows, h * hd:(h + 1) * hd], oh], axis=1)
            slot_ref[h] = _dot_nt(qx_ref[h], k_ext)

    def accumulate(g, slot_ref, causal):
        rows = rows_of(g)
        if causal:
            key_pos = g * gw + lax.broadcasted_iota(jnp.int32, (blk, gw), 1)
            qry_pos = b * blk + lax.broadcasted_iota(jnp.int32, (blk, gw), 0)
            keep = key_pos <= qry_pos
        for h in range(MOBA_PAIR):
            s = slot_ref[h]
            if causal:
                s = jnp.where(keep, s, NEG)
            m_old = m_ref[h]
            m_new = jnp.maximum(m_old, jnp.max(s, axis=-1, keepdims=True))
            p = jnp.exp2(s - m_new).astype(BF16)
            v_ext = jnp.concatenate([v_ref[rows, h * hd:(h + 1) * hd], ones], axis=1)
            acc_ref[h] = jnp.exp2(m_old - m_new) * acc_ref[h] + _dot(p, v_ext)
            m_ref[h] = m_new

    full_groups = b // MOBA_GROUP
    scores(0, s0_ref)

    def body(i, carry):
        g = 2 * i
        scores(g + 1, s1_ref)
        accumulate(g, s0_ref, False)
        scores(g + 2, s0_ref)
        accumulate(g + 1, s1_ref, False)
        return carry

    lax.fori_loop(0, full_groups // 2, body, 0)

    @pl.when(full_groups % 2 == 0)
    def _():
        accumulate(full_groups, s0_ref, True)

    @pl.when(full_groups % 2 == 1)
    def _():
        scores(full_groups, s1_ref)
        accumulate(full_groups - 1, s0_ref, False)
        accumulate(full_groups, s1_ref, True)

    for h in range(MOBA_PAIR):
        acc = acc_ref[h]
        o_ref[:, h * hd:(h + 1) * hd] = (acc[:, :hd] / acc[:, hd:]).astype(BF16)


def _moba(proj):
    s = proj.shape[0]
    nb = s // MOBA_BLOCK
    w = MOBA_PAIR * MOBA_HD
    assert s % (MOBA_GROUP * MOBA_BLOCK) == 0 and nb <= LANES and MOBA_HEADS % MOBA_PAIR == 0
    resident = dict(pipeline_mode=pl.Buffered(1))
    return pl.pallas_call(
        _moba_kernel,
        grid=(MOBA_HEADS // MOBA_PAIR, nb),
        in_specs=[pl.BlockSpec((MOBA_BLOCK, w), lambda h, b: (b, OFF_MQ // w + h)),
                  pl.BlockSpec((s, w), lambda h, b: (0, OFF_MK // w + h), **resident),
                  pl.BlockSpec((s, w), lambda h, b: (0, OFF_MV // w + h), **resident)],
        out_specs=pl.BlockSpec((MOBA_BLOCK, w), lambda h, b: (b, h)),
        out_shape=jax.ShapeDtypeStruct((s, MOBA_W), BF16),
        scratch_shapes=[pltpu.VMEM((LANES, w), F32),
                        pltpu.VMEM((s, LANES), BF16),
                        pltpu.VMEM((MOBA_PAIR, MOBA_BLOCK, 2 * MOBA_HD), BF16),
                        pltpu.VMEM((MOBA_PAIR, MOBA_BLOCK, 2 * MOBA_HD), F32),
                        pltpu.VMEM((MOBA_PAIR, MOBA_BLOCK, 1), F32),
                        pltpu.VMEM((MOBA_PAIR, MOBA_BLOCK, MOBA_GROUP * MOBA_BLOCK), F32),
                        pltpu.VMEM((MOBA_PAIR, MOBA_BLOCK, MOBA_GROUP * MOBA_BLOCK), F32)],
        compiler_params=_cparams("arbitrary", "arbitrary"),
        name="moba",
    )(proj, proj, proj)


def _merge_kernel(or_ref, om_ref, ga_ref, gb_ref, wr_ref, wm_ref, o_ref):
    yr = _dot(or_ref[...], wr_ref[...])
    ym = _dot(om_ref[...], wm_ref[...])
    ga = jax.nn.sigmoid(ga_ref[...].astype(F32))
    gb = jax.nn.sigmoid(gb_ref[...].astype(F32))
    o_ref[...] = (ga * yr + gb * ym).astype(BF16)


def _merge(o_ret, o_moba, proj, w_r, w_m, tm=1024, tn=512):
    s = o_ret.shape[0]
    tm = min(tm, s)
    d = D_MODEL
    return pl.pallas_call(
        _merge_kernel,
        grid=(s // tm, d // tn),
        in_specs=[pl.BlockSpec((tm, RET_V_W), lambda i, j: (i, 0)),
                  pl.BlockSpec((tm, MOBA_W), lambda i, j: (i, 0)),
                  pl.BlockSpec((tm, tn), lambda i, j: (i, OFF_GA // tn + j)),
                  pl.BlockSpec((tm, tn), lambda i, j: (i, OFF_GB // tn + j)),
                  pl.BlockSpec((RET_V_W, tn), lambda i, j: (0, j)),
                  pl.BlockSpec((MOBA_W, tn), lambda i, j: (0, j))],
        out_specs=pl.BlockSpec((tm, tn), lambda i, j: (i, j)),
        out_shape=jax.ShapeDtypeStruct((s, d), BF16),
        compiler_params=_cparams("arbitrary", "arbitrary"),
        name="merge",
    )(o_ret, o_moba, proj, proj, w_r, w_m)


def _mixout_kernel(m_ref, wo_ref, x_ref, npost_ref, g1_ref, npre_ref, sc_ref, sh_ref, x1_ref, u2_ref):
    y = _dot(m_ref[...], wo_ref[...])
    x1 = x_ref[...] + (1.0 + g1_ref[...]) * (_rms(y) * npost_ref[...])
    x1_ref[...] = x1
    u2_ref[...] = ((_rms(x1) * npre_ref[...]) * (1.0 + sc_ref[...]) + sh_ref[...]).astype(BF16)


def _mixout(merged, w_o, x, npost, g1, npre, sc2, sh2, tm=256):
    s, d = x.shape
    tm = min(tm, s)
    vec = pl.BlockSpec((1, d), lambda i: (0, 0))
    rows = pl.BlockSpec((tm, d), lambda i: (i, 0))
    return pl.pallas_call(
        _mixout_kernel,
        grid=(s // tm,),
        in_specs=[rows, pl.BlockSpec((d, d), lambda i: (0, 0)), rows, vec, vec, vec, vec, vec],
        out_specs=[rows, rows],
        out_shape=[jax.ShapeDtypeStruct((s, d), F32), jax.ShapeDtypeStruct((s, d), BF16)],
        compiler_params=_cparams("arbitrary"),
        name="mixout",
    )(merged, w_o, x, npost, g1, npre, sc2, sh2)


def _ffn_kernel(u_ref, wa_ref, wb_ref, cwa_ref, cwb_ref, cba_ref, cbb_ref, wd_ref, x1_ref, npost_ref, g2_ref,
                o_ref, acc_ref, ha_ref, hb_ref, tail_ref):
    i = pl.program_id(0)
    j = pl.program_id(1)
    tm = u_ref.shape[0]
    halo = SUBLANES
    u = u_ref[...]

    def conv(h, hbuf, cw_ref, cb_ref, slot):
        @pl.when(i == 0)
        def _():
            hbuf[0:halo, :] = jnp.zeros((halo, h.shape[1]), F32)

        @pl.when(i > 0)
        def _():
            hbuf[0:halo, :] = tail_ref[j, slot]

        hbuf[halo:halo + tm, :] = h
        tail_ref[j, slot] = h[tm - halo:tm, :]
        cw = cw_ref[...]
        out = h * cw[CONV_WIDTH - 1:CONV_WIDTH, :] + cb_ref[...]
        for t in range(1, CONV_WIDTH):
            out = out + hbuf[halo - t:halo - t + tm, :] * cw[CONV_WIDTH - 1 - t:CONV_WIDTH - t, :]
        return out

    a = conv(_dot(u, wa_ref[...]), ha_ref, cwa_ref, cba_ref, 0)
    bgate = conv(_dot(u, wb_ref[...]), hb_ref, cwb_ref, cbb_ref, 1)
    part = _dot((_silu(a) * bgate).astype(BF16), wd_ref[...])

    @pl.when(j == 0)
    def _():
        acc_ref[...] = part

    @pl.when(j > 0)
    def _():
        acc_ref[...] += part

    @pl.when(j == pl.num_programs(1) - 1)
    def _():
        y = acc_ref[...]
        o_ref[...] = x1_ref[...] + (1.0 + g2_ref[...]) * (_rms(y) * npost_ref[...])


def _ffn(u2, w_up, conv_w, conv_b, w_down, x1, npost, g2, tm=512, tn=512):
    s, d = x1.shape
    tm = min(tm, s)
    nj = D_FF // tn
    assert D_FF % tn == 0
    vec = pl.BlockSpec((1, d), lambda i, j: (0, 0))
    rows = pl.BlockSpec((tm, d), lambda i, j: (i, 0))
    return pl.pallas_call(
        _ffn_kernel,
        grid=(s // tm, nj),
        in_specs=[rows,
                  pl.BlockSpec((d, tn), lambda i, j: (0, j)),
                  pl.BlockSpec((d, tn), lambda i, j: (0, j + nj)),
                  pl.BlockSpec((CONV_WIDTH, tn), lambda i, j: (0, j)),
                  pl.BlockSpec((CONV_WIDTH, tn), lambda i, j: (0, j + nj)),
                  pl.BlockSpec((1, tn), lambda i, j: (0, j)),
                  pl.BlockSpec((1, tn), lambda i, j: (0, j + nj)),
                  pl.BlockSpec((tn, d), lambda i, j: (j, 0)),
                  rows, vec, vec],
        out_specs=rows,
        out_shape=jax.ShapeDtypeStruct((s, d), F32),
        scratch_shapes=[pltpu.VMEM((tm, d), F32),
                        pltpu.VMEM((tm + SUBLANES, tn), F32),
                        pltpu.VMEM((tm + SUBLANES, tn), F32),
                        pltpu.VMEM((nj, 2, SUBLANES, tn), F32)],
        compiler_params=_cparams("arbitrary", "arbitrary"),
        name="ffn",
    )(u2, w_up, w_up, conv_w, conv_w, conv_b, conv_b, w_down, x1, npost, g2)


def _block(x, c, ada_w, ada_b, mix_norm_pre, mix_norm_post, w_in, ret_w_o, moba_w_o, w_out,
           ffn_norm_pre, ffn_norm_post, ffn_w_up, ffn_conv_w, ffn_conv_b, ffn_w_down, cos2, sin2):
    d = x.shape[1]
    mod = _ada(c, ada_w, ada_b[None, :])
    sh1, sc1, g1, sh2, sc2, g2 = [mod[:, t * d:(t + 1) * d] for t in range(6)]
    proj = _inproj(x, mix_norm_pre[None, :], sc1, sh1, cos2, sin2, w_in.astype(BF16))
    o_ret = _retention(proj)
    o_moba = _moba(proj)
    merged = _merge(o_ret, o_moba, proj, ret_w_o.astype(BF16), moba_w_o.astype(BF16))
    x1, u2 = _mixout(merged, w_out.astype(BF16), x, mix_norm_post[None, :], g1,
                     ffn_norm_pre[None, :], sc2, sh2)
    return _ffn(u2, ffn_w_up.astype(BF16), ffn_conv_w, ffn_conv_b[None, :], ffn_w_down.astype(BF16),
                x1, ffn_norm_post[None, :], g2)


def kernel(x, c, ada_w, ada_b, mix_norm_pre, mix_norm_post, w_in, ret_w_o, moba_w_o, w_out,
           ffn_norm_pre, ffn_norm_post, ffn_w_up, ffn_conv_w, ffn_conv_b, ffn_w_down):
    batch, s, _ = x.shape
    depth = ada_w.shape[0]
    cos2, sin2 = _rope_tables(s)
    outs = []
    for bi in range(batch):
        xb = x[bi]
        for l in range(depth):
            xb = _block(xb, c[bi:bi + 1], ada_w[l], ada_b[l], mix_norm_pre[l], mix_norm_post[l], w_in[l],
                        ret_w_o[l], moba_w_o[l], w_out[l], ffn_norm_pre[l], ffn_norm_post[l], ffn_w_up[l],
                        ffn_conv_w[l], ffn_conv_b[l], ffn_w_down[l], cos2, sin2)
        outs.append(xb)
    return jnp.stack(outs)
```

```python
import functools
import math

import jax
import jax.numpy as jnp
from jax import lax
from jax.experimental import pallas as pl
from jax.experimental.pallas import tpu as pltpu

D_MODEL = 2048
RET_HEADS = 8
RET_DK = 128
RET_DV = 256
ROPE_BASE = 10000.0
MOBA_HEADS = 16
MOBA_HD = 128
MOBA_BLOCK = 256
MOBA_TOPK = 3
D_FF = 5632
CONV_WIDTH = 3
NORM_EPS = 1e-6
GN_EPS = 1e-5

RET_QK_W = RET_HEADS * RET_DK
RET_V_W = RET_HEADS * RET_DV
MOBA_W = MOBA_HEADS * MOBA_HD
OFF_RQ = 0
OFF_RK = OFF_RQ + RET_QK_W
OFF_RV = OFF_RK + RET_QK_W
OFF_RG = OFF_RV + RET_V_W
OFF_MQ = OFF_RG + RET_V_W
OFF_MK = OFF_MQ + MOBA_W
OFF_MV = OFF_MK + MOBA_W
OFF_GA = OFF_MV + MOBA_W
OFF_GB = OFF_GA + D_MODEL
IN_WIDTH = OFF_GB + D_MODEL

LANES = 128
SUBLANES = 8
NEG = -1e30
MOBA_QSCALE = math.log2(math.e) / math.sqrt(MOBA_HD)
MOBA_GROUP = 4
MOBA_PAIR = 2
FFN_CHUNK = 256
FFN_ROWS = 64
VMEM_LIMIT = 56 * 1024 * 1024

F32 = jnp.float32
BF16 = jnp.bfloat16


def _cparams(*sem):
    return pltpu.CompilerParams(dimension_semantics=sem, vmem_limit_bytes=VMEM_LIMIT)


def _dot(a, b):
    return jnp.dot(a, b, preferred_element_type=F32)


def _dot_nt(a, b):
    return lax.dot_general(a, b, (((1,), (1,)), ((), ())), preferred_element_type=F32)


def _split_bf16(x):
    hi = x.astype(BF16)
    lo = (x - hi.astype(F32)).astype(BF16)
    return hi, lo


def _rms(x):
    return x * lax.rsqrt(jnp.mean(x * x, axis=-1, keepdims=True) + NORM_EPS)


def _silu(x):
    return x * jax.nn.sigmoid(x)


def _ada_kernel(c_ref, w_ref, b_ref, o_ref):
    c = jnp.broadcast_to(_silu(c_ref[...]), (SUBLANES, c_ref.shape[1]))
    c_hi, c_lo = _split_bf16(c)
    w_hi, w_lo = _split_bf16(w_ref[...])
    r = _dot(c_hi, w_hi) + _dot(c_lo, w_hi) + _dot(c_hi, w_lo)
    o_ref[...] = r[0:1] + b_ref[...]


def _ada(c, w, b, tn=1024):
    d, n = w.shape
    return pl.pallas_call(
        _ada_kernel,
        grid=(n // tn,),
        in_specs=[pl.BlockSpec((1, d), lambda j: (0, 0)),
                  pl.BlockSpec((d, tn), lambda j: (0, j)),
                  pl.BlockSpec((1, tn), lambda j: (0, j))],
        out_specs=pl.BlockSpec((1, tn), lambda j: (0, j)),
        out_shape=jax.ShapeDtypeStruct((1, n), F32),
        compiler_params=_cparams("arbitrary"),
        name="ada",
    )(c, w, b)


def _rope_kernel(inv_ref, cos_ref, sin_ref):
    ts = cos_ref.shape[0]
    row = lax.broadcasted_iota(jnp.int32, (ts, LANES), 0) + pl.program_id(0) * ts
    lane = lax.broadcasted_iota(jnp.int32, (ts, LANES), 1)
    ang = row.astype(F32) * inv_ref[...]
    sin = jnp.sin(ang)
    cos_ref[...] = jnp.cos(ang)
    sin_ref[...] = jnp.where(lane < RET_DK // 2, -sin, sin)


def _rope_tables(s, ts=2048):
    ts = min(ts, s)
    half = RET_DK // 2
    inv = ROPE_BASE ** (-jnp.arange(half, dtype=F32) / half)
    inv2 = jnp.concatenate([inv, inv])[None, :]
    return pl.pallas_call(
        _rope_kernel,
        grid=(s // ts,),
        in_specs=[pl.BlockSpec((1, LANES), lambda i: (0, 0))],
        out_specs=[pl.BlockSpec((ts, LANES), lambda i: (i, 0))] * 2,
        out_shape=[jax.ShapeDtypeStruct((s, LANES), F32)] * 2,
        compiler_params=_cparams("arbitrary"),
        name="rope",
    )(inv2)


def _inproj_kernel(x_ref, nw_ref, sc_ref, sh_ref, cos_ref, sin_ref, w_ref, o_ref, u_ref):
    j = pl.program_id(1)
    tn = o_ref.shape[1]

    @pl.when(j == 0)
    def _():
        y = _rms(x_ref[...]) * nw_ref[...]
        u_ref[...] = (y * (1.0 + sc_ref[...]) + sh_ref[...]).astype(BF16)

    r = _dot(u_ref[...], w_ref[...])
    q_tiles = RET_QK_W // tn
    is_mq = (j >= OFF_MQ // tn) & (j < OFF_MK // tn)
    o_ref[...] = (r * jnp.where(is_mq, MOBA_QSCALE, 1.0).astype(F32)).astype(BF16)

    @pl.when(j < 2 * q_tiles)
    def _():
        scale = jnp.where(j >= q_tiles, RET_DK ** -0.5, 1.0).astype(F32)
        cos = cos_ref[...]
        sin = sin_ref[...]
        for h in range(tn // RET_DK):
            xh = r[:, h * RET_DK:(h + 1) * RET_DK]
            rot = xh * cos + pltpu.roll(xh, RET_DK // 2, axis=1) * sin
            o_ref[:, h * RET_DK:(h + 1) * RET_DK] = (rot * scale).astype(BF16)


def _inproj(x, nw, sc, sh, cos2, sin2, w, tm=1024, tn=1024):
    s, d = x.shape
    n = w.shape[1]
    tm = min(tm, s)
    vec = pl.BlockSpec((1, d), lambda i, j: (0, 0))
    tab = pl.BlockSpec((tm, LANES), lambda i, j: (i, 0))
    return pl.pallas_call(
        _inproj_kernel,
        grid=(s // tm, n // tn),
        in_specs=[pl.BlockSpec((tm, d), lambda i, j: (i, 0)), vec, vec, vec, tab, tab,
                  pl.BlockSpec((d, tn), lambda i, j: (0, j))],
        out_specs=pl.BlockSpec((tm, tn), lambda i, j: (i, j)),
        out_shape=jax.ShapeDtypeStruct((s, n), BF16),
        scratch_shapes=[pltpu.VMEM((tm, d), BF16)],
        compiler_params=_cparams("arbitrary", "arbitrary"),
        name="inproj",
    )(x, nw, sc, sh, cos2, sin2, w)


def _ret_kernel(lg_ref, q_ref, k_ref, v_ref, g_ref, o_ref, r_ref, dec_ref, xi_ref, zeta_ref):
    n = pl.program_id(1)
    c = q_ref.shape[0]
    lg = lg_ref[0, 0:1, 0:1]

    @pl.when(n == 0)
    def _():
        r_ref[...] = jnp.zeros_like(r_ref)
        i = lax.broadcasted_iota(jnp.int32, (c, c), 0)
        j = lax.broadcasted_iota(jnp.int32, (c, c), 1)
        d = i - j
        dec_ref[...] = jnp.where(d >= 0, jnp.exp(jnp.maximum(d, 0).astype(F32) * lg), 0.0)
        row = lax.broadcasted_iota(jnp.int32, (c, RET_DK), 0)
        xi_ref[...] = jnp.exp((row + 1).astype(F32) * lg)
        zeta_ref[...] = jnp.exp((c - 1 - row).astype(F32) * lg)

    q = q_ref[...]
    k = k_ref[...]
    v = v_ref[...]
    s = _dot_nt(q, k) * dec_ref[...]
    inner = _dot(s.astype(BF16), v)
    r = r_ref[...]
    cross = _dot((q.astype(F32) * xi_ref[...]).astype(BF16), r.astype(BF16))
    o = inner + cross
    kz = (k.astype(F32) * zeta_ref[...]).T.astype(BF16)
    r_ref[...] = r * jnp.exp(lg * c) + _dot(kz, v)
    oc = o - jnp.mean(o, axis=-1, keepdims=True)
    on = oc * lax.rsqrt(jnp.mean(oc * oc, axis=-1, keepdims=True) + GN_EPS)
    o_ref[...] = (_silu(g_ref[...].astype(F32)) * on).astype(BF16)


def _retention(proj, chunk=512):
    s = proj.shape[0]
    chunk = min(chunk, s)
    log_g = jnp.log1p(-jnp.exp2(-5.0 - jnp.arange(RET_HEADS, dtype=F32)))
    lg = jnp.broadcast_to(log_g[:, None, None], (RET_HEADS, SUBLANES, LANES))
    return pl.pallas_call(
        _ret_kernel,
        grid=(RET_HEADS, s // chunk),
        in_specs=[pl.BlockSpec((1, SUBLANES, LANES), lambda h, n: (h, 0, 0)),
                  pl.BlockSpec((chunk, RET_DK), lambda h, n: (n, OFF_RQ // RET_DK + h)),
                  pl.BlockSpec((chunk, RET_DK), lambda h, n: (n, OFF_RK // RET_DK + h)),
                  pl.BlockSpec((chunk, RET_DV), lambda h, n: (n, OFF_RV // RET_DV + h)),
                  pl.BlockSpec((chunk, RET_DV), lambda h, n: (n, OFF_RG // RET_DV + h))],
        out_specs=pl.BlockSpec((chunk, RET_DV), lambda h, n: (n, h)),
        out_shape=jax.ShapeDtypeStruct((s, RET_V_W), BF16),
        scratch_shapes=[pltpu.VMEM((RET_DK, RET_DV), F32),
                        pltpu.VMEM((chunk, chunk), F32),
                        pltpu.VMEM((chunk, RET_DK), F32),
                        pltpu.VMEM((chunk, RET_DK), F32)],
        compiler_params=_cparams("arbitrary", "arbitrary"),
        name="ret",
    )(lg, proj, proj, proj, proj)


def _moba_kernel(q_ref, k_ref, v_ref, o_ref, km_ref, oh_ref, qx_ref, acc_ref, m_ref, s0_ref, s1_ref):
    hp = pl.program_id(0)
    b = pl.program_id(1)
    blk = q_ref.shape[0]
    hd = MOBA_HD
    nb = k_ref.shape[0] // blk
    gw = MOBA_GROUP * blk
    lane_i = lax.broadcasted_iota(jnp.int32, (blk, LANES), 1)

    @pl.when((hp == 0) & (b == 0))
    def _():
        def body(n, carry):
            oh_ref[pl.ds(pl.multiple_of(n * blk, blk), blk), :] = jnp.where(lane_i == n, 1.0, 0.0).astype(BF16)
            return carry

        lax.fori_loop(0, nb, body, 0)

    @pl.when(b == 0)
    def _():
        km_ref[...] = jnp.zeros_like(km_ref)

        def body(n, carry):
            kb = k_ref[pl.ds(pl.multiple_of(n * blk, blk), blk), :].astype(F32)
            km_ref[pl.ds(n, 1), :] = jnp.mean(kb, axis=0, keepdims=True)
            return carry

        lax.fori_loop(0, nb, body, 0)

    lane = lane_i.astype(F32)
    for h in range(MOBA_PAIR):
        q = q_ref[:, h * hd:(h + 1) * hd]
        km_hi, km_lo = _split_bf16(km_ref[:, h * hd:(h + 1) * hd])
        g = jnp.where(lane_i < b, _dot_nt(q, km_hi) + _dot_nt(q, km_lo), NEG)
        sel = lane_i == b
        for _ in range(MOBA_TOPK):
            mx = jnp.max(g, axis=-1, keepdims=True)
            idx = jnp.min(jnp.where(g == mx, lane, float(LANES)), axis=-1, keepdims=True)
            hit = lane == idx
            sel = sel | (hit & (mx > 0.5 * NEG))
            g = jnp.where(hit, 2.0 * NEG, g)
        qx_ref[h] = jnp.concatenate([q, jnp.where(sel, 0.0, NEG).astype(BF16)], axis=1)
        m_ref[h] = jnp.full((blk, 1), NEG, F32)
        acc_ref[h] = jnp.zeros((blk, 2 * hd), F32)

    ones = jnp.ones((gw, hd), BF16)

    def rows_of(g):
        return pl.ds(pl.multiple_of(g * gw, gw), gw)

    def scores(g, slot_ref):
        rows = rows_of(g)
        oh = oh_ref[rows, :]
        for h in range(MOBA_PAIR):
            k_ext = jnp.concatenate([k_ref[rows, h * hd:(h + 1) * hd], oh], axis=1)
            slot_ref[h] = _dot_nt(qx_ref[h], k_ext)

    def accumulate(g, slot_ref, causal):
        rows = rows_of(g)
        if causal:
            key_pos = g * gw + lax.broadcasted_iota(jnp.int32, (blk, gw), 1)
            qry_pos = b * blk + lax.broadcasted_iota(jnp.int32, (blk, gw), 0)
            keep = key_pos <= qry_pos
        for h in range(MOBA_PAIR):
            s = slot_ref[h]
            if causal:
                s = jnp.where(keep, s, NEG)
            m_old = m_ref[h]
            m_new = jnp.maximum(m_old, jnp.max(s, axis=-1, keepdims=True))
            p = jnp.exp2(s - m_new).astype(BF16)
            v_ext = jnp.concatenate([v_ref[rows, h * hd:(h + 1) * hd], ones], axis=1)
            acc_ref[h] = jnp.exp2(m_old - m_new) * acc_ref[h] + _dot(p, v_ext)
            m_ref[h] = m_new

    full_groups = b // MOBA_GROUP
    scores(0, s0_ref)

    def body(i, carry):
        g = 2 * i
        scores(g + 1, s1_ref)
        accumulate(g, s0_ref, False)
        scores(g + 2, s0_ref)
        accumulate(g + 1, s1_ref, False)
        return carry

    lax.fori_loop(0, full_groups // 2, body, 0)

    @pl.when(full_groups % 2 == 0)
    def _():
        accumulate(full_groups, s0_ref, True)

    @pl.when(full_groups % 2 == 1)
    def _():
        scores(full_groups, s1_ref)
        accumulate(full_groups - 1, s0_ref, False)
        accumulate(full_groups, s1_ref, True)

    for h in range(MOBA_PAIR):
        acc = acc_ref[h]
        o_ref[:, h * hd:(h + 1) * hd] = (acc[:, :hd] / acc[:, hd:]).astype(BF16)


def _moba(proj):
    s = proj.shape[0]
    nb = s // MOBA_BLOCK
    w = MOBA_PAIR * MOBA_HD
    assert s % (MOBA_GROUP * MOBA_BLOCK) == 0 and nb <= LANES and MOBA_HEADS % MOBA_PAIR == 0
    resident = dict(pipeline_mode=pl.Buffered(1))
    return pl.pallas_call(
        _moba_kernel,
        grid=(MOBA_HEADS // MOBA_PAIR, nb),
        in_specs=[pl.BlockSpec((MOBA_BLOCK, w), lambda h, b: (b, OFF_MQ // w + h)),
                  pl.BlockSpec((s, w), lambda h, b: (0, OFF_MK // w + h), **resident),
                  pl.BlockSpec((s, w), lambda h, b: (0, OFF_MV // w + h), **resident)],
        out_specs=pl.BlockSpec((MOBA_BLOCK, w), lambda h, b: (b, h)),
        out_shape=jax.ShapeDtypeStruct((s, MOBA_W), BF16),
        scratch_shapes=[pltpu.VMEM((LANES, w), F32),
                        pltpu.VMEM((s, LANES), BF16),
                        pltpu.VMEM((MOBA_PAIR, MOBA_BLOCK, 2 * MOBA_HD), BF16),
                        pltpu.VMEM((MOBA_PAIR, MOBA_BLOCK, 2 * MOBA_HD), F32),
                        pltpu.VMEM((MOBA_PAIR, MOBA_BLOCK, 1), F32),
                        pltpu.VMEM((MOBA_PAIR, MOBA_BLOCK, MOBA_GROUP * MOBA_BLOCK), F32),
                        pltpu.VMEM((MOBA_PAIR, MOBA_BLOCK, MOBA_GROUP * MOBA_BLOCK), F32)],
        compiler_params=_cparams("arbitrary", "arbitrary"),
        name="moba",
    )(proj, proj, proj)


def _merge_kernel(or_ref, om_ref, ga_ref, gb_ref, wr_ref, wm_ref, o_ref):
    yr = _dot(or_ref[...], wr_ref[...])
    ym = _dot(om_ref[...], wm_ref[...])
    ga = jax.nn.sigmoid(ga_ref[...].astype(F32))
    gb = jax.nn.sigmoid(gb_ref[...].astype(F32))
    o_ref[...] = (ga * yr + gb * ym).astype(BF16)


def _merge(o_ret, o_moba, proj, w_r, w_m, tm=1024, tn=512):
    s = o_ret.shape[0]
    tm = min(tm, s)
    d = D_MODEL
    return pl.pallas_call(
        _merge_kernel,
        grid=(s // tm, d // tn),
        in_specs=[pl.BlockSpec((tm, RET_V_W), lambda i, j: (i, 0)),
                  pl.BlockSpec((tm, MOBA_W), lambda i, j: (i, 0)),
                  pl.BlockSpec((tm, tn), lambda i, j: (i, OFF_GA // tn + j)),
                  pl.BlockSpec((tm, tn), lambda i, j: (i, OFF_GB // tn + j)),
                  pl.BlockSpec((RET_V_W, tn), lambda i, j: (0, j)),
                  pl.BlockSpec((MOBA_W, tn), lambda i, j: (0, j))],
        out_specs=pl.BlockSpec((tm, tn), lambda i, j: (i, j)),
        out_shape=jax.ShapeDtypeStruct((s, d), BF16),
        compiler_params=_cparams("arbitrary", "arbitrary"),
        name="merge",
    )(o_ret, o_moba, proj, proj, w_r, w_m)


def _mixout_kernel(m_ref, wo_ref, x_ref, npost_ref, g1_ref, npre_ref, sc_ref, sh_ref, x1_ref, u2_ref):
    y = _dot(m_ref[...], wo_ref[...])
    x1 = x_ref[...] + (1.0 + g1_ref[...]) * (_rms(y) * npost_ref[...])
    x1_ref[...] = x1
    u2_ref[...] = ((_rms(x1) * npre_ref[...]) * (1.0 + sc_ref[...]) + sh_ref[...]).astype(BF16)


def _mixout(merged, w_o, x, npost, g1, npre, sc2, sh2, tm=256):
    s, d = x.shape
    tm = min(tm, s)
    vec = pl.BlockSpec((1, d), lambda i: (0, 0))
    rows = pl.BlockSpec((tm, d), lambda i: (i, 0))
    return pl.pallas_call(
        _mixout_kernel,
        grid=(s // tm,),
        in_specs=[rows, pl.BlockSpec((d, d), lambda i: (0, 0)), rows, vec, vec, vec, vec, vec],
        out_specs=[rows, rows],
        out_shape=[jax.ShapeDtypeStruct((s, d), F32), jax.ShapeDtypeStruct((s, d), BF16)],
        compiler_params=_cparams("arbitrary"),
        name="mixout",
    )(merged, w_o, x, npost, g1, npre, sc2, sh2)


def _ffn_kernel(u_ref, wa_ref, wb_ref, cwa_ref, cwb_ref, cba_ref, cbb_ref, wd_ref, x1_ref, npost_ref, g2_ref,
                o_ref, acc_ref, ha_ref, hb_ref, act_ref, tail_ref):
    i = pl.program_id(0)
    j = pl.program_id(1)
    tm = u_ref.shape[0]
    tn = wa_ref.shape[1]
    halo = SUBLANES

    @pl.when((i == 0) & (j == 0))
    def _():
        tail_ref[...] = jnp.zeros_like(tail_ref)

    @pl.when(j == 0)
    def _():
        acc_ref[...] = jnp.zeros_like(acc_ref)

    u = u_ref[...]
    ha_ref[0:halo, :] = tail_ref[j, 0]
    hb_ref[0:halo, :] = tail_ref[j, 1]
    for c in range(tn // FFN_CHUNK):
        cs = slice(c * FFN_CHUNK, (c + 1) * FFN_CHUNK)
        ha_ref[halo:halo + tm, cs] = _dot(u, wa_ref[:, cs])
        hb_ref[halo:halo + tm, cs] = _dot(u, wb_ref[:, cs])
    tail_ref[j, 0] = ha_ref[tm:tm + halo, :]
    tail_ref[j, 1] = hb_ref[tm:tm + halo, :]

    def conv(hbuf, cw, cb, r0, cs):
        out = cb
        for t in range(CONV_WIDTH):
            out = out + hbuf[halo + r0 - t:halo + r0 - t + FFN_ROWS, cs] * cw[CONV_WIDTH - 1 - t:CONV_WIDTH - t, :]
        return out

    for c in range(tn // FFN_CHUNK):
        cs = slice(c * FFN_CHUNK, (c + 1) * FFN_CHUNK)
        cwa, cwb, cba, cbb = cwa_ref[:, cs], cwb_ref[:, cs], cba_ref[:, cs], cbb_ref[:, cs]
        for r0 in range(0, tm, FFN_ROWS):
            a = conv(ha_ref, cwa, cba, r0, cs)
            bgate = conv(hb_ref, cwb, cbb, r0, cs)
            act_ref[r0:r0 + FFN_ROWS, cs] = (_silu(a) * bgate).astype(BF16)
        acc_ref[...] += _dot(act_ref[:, cs], wd_ref[cs, :])

    @pl.when(j == pl.num_programs(1) - 1)
    def _():
        y = acc_ref[...]
        o_ref[...] = x1_ref[...] + (1.0 + g2_ref[...]) * (_rms(y) * npost_ref[...])


def _ffn(u2, w_up, conv_w, conv_b, w_down, x1, npost, g2, tm=512, tn=512):
    s, d = x1.shape
    tm = min(tm, s)
    nj = D_FF // tn
    assert D_FF % tn == 0
    vec = pl.BlockSpec((1, d), lambda i, j: (0, 0))
    rows = pl.BlockSpec((tm, d), lambda i, j: (i, 0))
    return pl.pallas_call(
        _ffn_kernel,
        grid=(s // tm, nj),
        in_specs=[rows,
                  pl.BlockSpec((d, tn), lambda i, j: (0, j)),
                  pl.BlockSpec((d, tn), lambda i, j: (0, j + nj)),
                  pl.BlockSpec((CONV_WIDTH, tn), lambda i, j: (0, j)),
                  pl.BlockSpec((CONV_WIDTH, tn), lambda i, j: (0, j + nj)),
                  pl.BlockSpec((1, tn), lambda i, j: (0, j)),
                  pl.BlockSpec((1, tn), lambda i, j: (0, j + nj)),
                  pl.BlockSpec((tn, d), lambda i, j: (j, 0)),
                  rows, vec, vec],
        out_specs=rows,
        out_shape=jax.ShapeDtypeStruct((s, d), F32),
        scratch_shapes=[pltpu.VMEM((tm, d), F32),
                        pltpu.VMEM((tm + SUBLANES, tn), F32),
                        pltpu.VMEM((tm + SUBLANES, tn), F32),
                        pltpu.VMEM((tm, tn), BF16),
                        pltpu.VMEM((nj, 2, SUBLANES, tn), F32)],
        compiler_params=_cparams("arbitrary", "arbitrary"),
        name="ffn",
    )(u2, w_up, w_up, conv_w, conv_w, conv_b, conv_b, w_down, x1, npost, g2)


def _block(x, c, ada_w, ada_b, mix_norm_pre, mix_norm_post, w_in, ret_w_o, moba_w_o, w_out,
           ffn_norm_pre, ffn_norm_post, ffn_w_up, ffn_conv_w, ffn_conv_b, ffn_w_down, cos2, sin2):
    d = x.shape[1]
    mod = _ada(c, ada_w, ada_b[None, :])
    sh1, sc1, g1, sh2, sc2, g2 = [mod[:, t * d:(t + 1) * d] for t in range(6)]
    proj = _inproj(x, mix_norm_pre[None, :], sc1, sh1, cos2, sin2, w_in.astype(BF16))
    o_ret = _retention(proj)
    o_moba = _moba(proj)
    merged = _merge(o_ret, o_moba, proj, ret_w_o.astype(BF16), moba_w_o.astype(BF16))
    x1, u2 = _mixout(merged, w_out.astype(BF16), x, mix_norm_post[None, :], g1,
                     ffn_norm_pre[None, :], sc2, sh2)
    return _ffn(u2, ffn_w_up.astype(BF16), ffn_conv_w, ffn_conv_b[None, :], ffn_w_down.astype(BF16),
                x1, ffn_norm_post[None, :], g2)


def kernel(x, c, ada_w, ada_b, mix_norm_pre, mix_norm_post, w_in, ret_w_o, moba_w_o, w_out,
           ffn_norm_pre, ffn_norm_post, ffn_w_up, ffn_conv_w, ffn_conv_b, ffn_w_down):
    batch, s, _ = x.shape
    depth = ada_w.shape[0]
    cos2, sin2 = _rope_tables(s)
    outs = []
    for bi in range(batch):
        xb = x[bi]
        for l in range(depth):
            xb = _block(xb, c[bi:bi + 1], ada_w[l], ada_b[l], mix_norm_pre[l], mix_norm_post[l], w_in[l],
                        ret_w_o[l], moba_w_o[l], w_out[l], ffn_norm_pre[l], ffn_norm_post[l], ffn_w_up[l],
                        ffn_conv_w[l], ffn_conv_b[l], ffn_w_down[l], cos2, sin2)
        outs.append(xb)
    return jnp.stack(outs)
```

```python
import functools
import math

import jax
import jax.numpy as jnp
from jax import lax
from jax.experimental import pallas as pl
from jax.experimental.pallas import tpu as pltpu

D_MODEL = 2048
RET_HEADS = 8
RET_DK = 128
RET_DV = 256
ROPE_BASE = 10000.0
MOBA_HEADS = 16
MOBA_HD = 128
MOBA_BLOCK = 256
MOBA_TOPK = 3
D_FF = 5632
CONV_WIDTH = 3
NORM_EPS = 1e-6
GN_EPS = 1e-5

RET_QK_W = RET_HEADS * RET_DK
RET_V_W = RET_HEADS * RET_DV
MOBA_W = MOBA_HEADS * MOBA_HD
OFF_RQ = 0
OFF_RK = OFF_RQ + RET_QK_W
OFF_RV = OFF_RK + RET_QK_W
OFF_RG = OFF_RV + RET_V_W
OFF_MQ = OFF_RG + RET_V_W
OFF_MK = OFF_MQ + MOBA_W
OFF_MV = OFF_MK + MOBA_W
OFF_GA = OFF_MV + MOBA_W
OFF_GB = OFF_GA + D_MODEL
IN_WIDTH = OFF_GB + D_MODEL

LANES = 128
SUBLANES = 8
NEG = -1e30
MOBA_QSCALE = math.log2(math.e) / math.sqrt(MOBA_HD)
MOBA_GROUP = 4
MOBA_PAIR = 2
FFN_CHUNK = 256
FFN_ROWS = 64
VMEM_LIMIT = 56 * 1024 * 1024

F32 = jnp.float32
BF16 = jnp.bfloat16


def _cparams(*sem):
    return pltpu.CompilerParams(dimension_semantics=sem, vmem_limit_bytes=VMEM_LIMIT)


def _dot(a, b):
    return jnp.dot(a, b, preferred_element_type=F32)


def _dot_nt(a, b):
    return lax.dot_general(a, b, (((1,), (1,)), ((), ())), preferred_element_type=F32)


def _split_bf16(x):
    hi = x.astype(BF16)
    lo = (x - hi.astype(F32)).astype(BF16)
    return hi, lo


def _rms(x):
    return x * lax.rsqrt(jnp.mean(x * x, axis=-1, keepdims=True) + NORM_EPS)


def _silu(x):
    return x * jax.nn.sigmoid(x)


def _ada_kernel(c_ref, w_ref, b_ref, o_ref):
    c = jnp.broadcast_to(_silu(c_ref[...]), (SUBLANES, c_ref.shape[1]))
    c_hi, c_lo = _split_bf16(c)
    w_hi, w_lo = _split_bf16(w_ref[...])
    r = _dot(c_hi, w_hi) + _dot(c_lo, w_hi) + _dot(c_hi, w_lo)
    o_ref[...] = r[0:1] + b_ref[...]


def _ada(c, w, b, tn=1024):
    d, n = w.shape
    return pl.pallas_call(
        _ada_kernel,
        grid=(n // tn,),
        in_specs=[pl.BlockSpec((1, d), lambda j: (0, 0)),
                  pl.BlockSpec((d, tn), lambda j: (0, j)),
                  pl.BlockSpec((1, tn), lambda j: (0, j))],
        out_specs=pl.BlockSpec((1, tn), lambda j: (0, j)),
        out_shape=jax.ShapeDtypeStruct((1, n), F32),
        compiler_params=_cparams("arbitrary"),
        name="ada",
    )(c, w, b)


def _rope_kernel(inv_ref, cos_ref, sin_ref):
    ts = cos_ref.shape[0]
    row = lax.broadcasted_iota(jnp.int32, (ts, LANES), 0) + pl.program_id(0) * ts
    lane = lax.broadcasted_iota(jnp.int32, (ts, LANES), 1)
    ang = row.astype(F32) * inv_ref[...]
    sin = jnp.sin(ang)
    cos_ref[...] = jnp.cos(ang)
    sin_ref[...] = jnp.where(lane < RET_DK // 2, -sin, sin)


def _rope_tables(s, ts=2048):
    ts = min(ts, s)
    half = RET_DK // 2
    inv = ROPE_BASE ** (-jnp.arange(half, dtype=F32) / half)
    inv2 = jnp.concatenate([inv, inv])[None, :]
    return pl.pallas_call(
        _rope_kernel,
        grid=(s // ts,),
        in_specs=[pl.BlockSpec((1, LANES), lambda i: (0, 0))],
        out_specs=[pl.BlockSpec((ts, LANES), lambda i: (i, 0))] * 2,
        out_shape=[jax.ShapeDtypeStruct((s, LANES), F32)] * 2,
        compiler_params=_cparams("arbitrary"),
        name="rope",
    )(inv2)


def _inproj_kernel(x_ref, nw_ref, sc_ref, sh_ref, cos_ref, sin_ref, w_ref, o_ref, u_ref):
    j = pl.program_id(1)
    tn = o_ref.shape[1]

    @pl.when(j == 0)
    def _():
        y = _rms(x_ref[...]) * nw_ref[...]
        u_ref[...] = (y * (1.0 + sc_ref[...]) + sh_ref[...]).astype(BF16)

    r = _dot(u_ref[...], w_ref[...])
    q_tiles = RET_QK_W // tn
    is_mq = (j >= OFF_MQ // tn) & (j < OFF_MK // tn)
    o_ref[...] = (r * jnp.where(is_mq, MOBA_QSCALE, 1.0).astype(F32)).astype(BF16)

    @pl.when(j < 2 * q_tiles)
    def _():
        scale = jnp.where(j >= q_tiles, RET_DK ** -0.5, 1.0).astype(F32)
        cos = cos_ref[...]
        sin = sin_ref[...]
        for h in range(tn // RET_DK):
            xh = r[:, h * RET_DK:(h + 1) * RET_DK]
            rot = xh * cos + pltpu.roll(xh, RET_DK // 2, axis=1) * sin
            o_ref[:, h * RET_DK:(h + 1) * RET_DK] = (rot * scale).astype(BF16)


def _inproj(x, nw, sc, sh, cos2, sin2, w, tm=1024, tn=1024):
    s, d = x.shape
    n = w.shape[1]
    tm = min(tm, s)
    vec = pl.BlockSpec((1, d), lambda i, j: (0, 0))
    tab = pl.BlockSpec((tm, LANES), lambda i, j: (i, 0))
    return pl.pallas_call(
        _inproj_kernel,
        grid=(s // tm, n // tn),
        in_specs=[pl.BlockSpec((tm, d), lambda i, j: (i, 0)), vec, vec, vec, tab, tab,
                  pl.BlockSpec((d, tn), lambda i, j: (0, j))],
        out_specs=pl.BlockSpec((tm, tn), lambda i, j: (i, j)),
        out_shape=jax.ShapeDtypeStruct((s, n), BF16),
        scratch_shapes=[pltpu.VMEM((tm, d), BF16)],
        compiler_params=_cparams("arbitrary", "arbitrary"),
        name="inproj",
    )(x, nw, sc, sh, cos2, sin2, w)


def _ret_kernel(lg_ref, q_ref, k_ref, v_ref, g_ref, o_ref, r_ref, dec_ref, xi_ref, zeta_ref):
    n = pl.program_id(1)
    c = q_ref.shape[0]
    lg = lg_ref[0, 0:1, 0:1]

    @pl.when(n == 0)
    def _():
        r_ref[...] = jnp.zeros_like(r_ref)
        i = lax.broadcasted_iota(jnp.int32, (c, c), 0)
        j = lax.broadcasted_iota(jnp.int32, (c, c), 1)
        d = i - j
        dec_ref[...] = jnp.where(d >= 0, jnp.exp(jnp.maximum(d, 0).astype(F32) * lg), 0.0)
        row = lax.broadcasted_iota(jnp.int32, (c, RET_DK), 0)
        xi_ref[...] = jnp.exp((row + 1).astype(F32) * lg)
        zeta_ref[...] = jnp.exp((c - 1 - row).astype(F32) * lg)

    q = q_ref[...]
    k = k_ref[...]
    v = v_ref[...]
    s = _dot_nt(q, k) * dec_ref[...]
    inner = _dot(s.astype(BF16), v)
    r = r_ref[...]
    cross = _dot((q.astype(F32) * xi_ref[...]).astype(BF16), r.astype(BF16))
    o = inner + cross
    kz = (k.astype(F32) * zeta_ref[...]).T.astype(BF16)
    r_ref[...] = r * jnp.exp(lg * c) + _dot(kz, v)
    oc = o - jnp.mean(o, axis=-1, keepdims=True)
    on = oc * lax.rsqrt(jnp.mean(oc * oc, axis=-1, keepdims=True) + GN_EPS)
    o_ref[...] = (_silu(g_ref[...].astype(F32)) * on).astype(BF16)


def _retention(proj, chunk=512):
    s = proj.shape[0]
    chunk = min(chunk, s)
    log_g = jnp.log1p(-jnp.exp2(-5.0 - jnp.arange(RET_HEADS, dtype=F32)))
    lg = jnp.broadcast_to(log_g[:, None, None], (RET_HEADS, SUBLANES, LANES))
    return pl.pallas_call(
        _ret_kernel,
        grid=(RET_HEADS, s // chunk),
        in_specs=[pl.BlockSpec((1, SUBLANES, LANES), lambda h, n: (h, 0, 0)),
                  pl.BlockSpec((chunk, RET_DK), lambda h, n: (n, OFF_RQ // RET_DK + h)),
                  pl.BlockSpec((chunk, RET_DK), lambda h, n: (n, OFF_RK // RET_DK + h)),
                  pl.BlockSpec((chunk, RET_DV), lambda h, n: (n, OFF_RV // RET_DV + h)),
                  pl.BlockSpec((chunk, RET_DV), lambda h, n: (n, OFF_RG // RET_DV + h))],
        out_specs=pl.BlockSpec((chunk, RET_DV), lambda h, n: (n, h)),
        out_shape=jax.ShapeDtypeStruct((s, RET_V_W), BF16),
        scratch_shapes=[pltpu.VMEM((RET_DK, RET_DV), F32),
                        pltpu.VMEM((chunk, chunk), F32),
                        pltpu.VMEM((chunk, RET_DK), F32),
                        pltpu.VMEM((chunk, RET_DK), F32)],
        compiler_params=_cparams("arbitrary", "arbitrary"),
        name="ret",
    )(lg, proj, proj, proj, proj)


def _moba_kernel(q_ref, k_ref, v_ref, o_ref, km_ref, oh_ref, qx_ref, acc_ref, m_ref, s0_ref, s1_ref):
    hp = pl.program_id(0)
    b = pl.program_id(1)
    blk = q_ref.shape[0]
    hd = MOBA_HD
    nb = k_ref.shape[0] // blk
    gw = MOBA_GROUP * blk
    lane_i = lax.broadcasted_iota(jnp.int32, (blk, LANES), 1)

    @pl.when((hp == 0) & (b == 0))
    def _():
        def body(n, carry):
            oh_ref[pl.ds(pl.multiple_of(n * blk, blk), blk), :] = jnp.where(lane_i == n, 1.0, 0.0).astype(BF16)
            return carry

        lax.fori_loop(0, nb, body, 0)

    @pl.when(b == 0)
    def _():
        km_ref[...] = jnp.zeros_like(km_ref)

        def body(n, carry):
            kb = k_ref[pl.ds(pl.multiple_of(n * blk, blk), blk), :].astype(F32)
            km_ref[pl.ds(n, 1), :] = jnp.mean(kb, axis=0, keepdims=True)
            return carry

        lax.fori_loop(0, nb, body, 0)

    lane = lane_i.astype(F32)
    for h in range(MOBA_PAIR):
        q = q_ref[:, h * hd:(h + 1) * hd]
        km_hi, km_lo = _split_bf16(km_ref[:, h * hd:(h + 1) * hd])
        g = jnp.where(lane_i < b, _dot_nt(q, km_hi) + _dot_nt(q, km_lo), NEG)
        sel = lane_i == b
        for _ in range(MOBA_TOPK):
            mx = jnp.max(g, axis=-1, keepdims=True)
            idx = jnp.min(jnp.where(g == mx, lane, float(LANES)), axis=-1, keepdims=True)
            hit = lane == idx
            sel = sel | (hit & (mx > 0.5 * NEG))
            g = jnp.where(hit, 2.0 * NEG, g)
        qx_ref[h] = jnp.concatenate([q, jnp.where(sel, 0.0, NEG).astype(BF16)], axis=1)
        m_ref[h] = jnp.full((blk, 1), NEG, F32)
        acc_ref[h] = jnp.zeros((blk, 2 * hd), F32)

    ones = jnp.ones((gw, hd), BF16)

    def rows_of(g):
        return pl.ds(pl.multiple_of(g * gw, gw), gw)

    def scores(g, slot_ref):
        rows = rows_of(g)
        oh = oh_ref[rows, :]
        for h in range(MOBA_PAIR):
            k_ext = jnp.concatenate([k_ref[rows, h * hd:(h + 1) * hd], oh], axis=1)
            slot_ref[h] = _dot_nt(qx_ref[h], k_ext)

    def accumulate(g, slot_ref, causal):
        rows = rows_of(g)
        if causal:
            key_pos = g * gw + lax.broadcasted_iota(jnp.int32, (blk, gw), 1)
            qry_pos = b * blk + lax.broadcasted_iota(jnp.int32, (blk, gw), 0)
            keep = key_pos <= qry_pos
        for h in range(MOBA_PAIR):
            s = slot_ref[h]
            if causal:
                s = jnp.where(keep, s, NEG)
            m_old = m_ref[h]
            m_new = jnp.maximum(m_old, jnp.max(s, axis=-1, keepdims=True))
            p = jnp.exp2(s - m_new).astype(BF16)
            v_ext = jnp.concatenate([v_ref[rows, h * hd:(h + 1) * hd], ones], axis=1)
            acc_ref[h] = jnp.exp2(m_old - m_new) * acc_ref[h] + _dot(p, v_ext)
            m_ref[h] = m_new

    full_groups = b // MOBA_GROUP
    scores(0, s0_ref)

    def body(i, carry):
        g = 2 * i
        scores(g + 1, s1_ref)
        accumulate(g, s0_ref, False)
        scores(g + 2, s0_ref)
        accumulate(g + 1, s1_ref, False)
        return carry

    lax.fori_loop(0, full_groups // 2, body, 0)

    @pl.when(full_groups % 2 == 0)
    def _():
        accumulate(full_groups, s0_ref, True)

    @pl.when(full_groups % 2 == 1)
    def _():
        scores(full_groups, s1_ref)
        accumulate(full_groups - 1, s0_ref, False)
        accumulate(full_groups, s1_ref, True)

    for h in range(MOBA_PAIR):
        acc = acc_ref[h]
        o_ref[:, h * hd:(h + 1) * hd] = (acc[:, :hd] / acc[:, hd:]).astype(BF16)


def _moba(proj):
    s = proj.shape[0]
    nb = s // MOBA_BLOCK
    w = MOBA_PAIR * MOBA_HD
    assert s % (MOBA_GROUP * MOBA_BLOCK) == 0 and nb <= LANES and MOBA_HEADS % MOBA_PAIR == 0
    resident = dict(pipeline_mode=pl.Buffered(1))
    return pl.pallas_call(
        _moba_kernel,
        grid=(MOBA_HEADS // MOBA_PAIR, nb),
        in_specs=[pl.BlockSpec((MOBA_BLOCK, w), lambda h, b: (b, OFF_MQ // w + h)),
                  pl.BlockSpec((s, w), lambda h, b: (0, OFF_MK // w + h), **resident),
                  pl.BlockSpec((s, w), lambda h, b: (0, OFF_MV // w + h), **resident)],
        out_specs=pl.BlockSpec((MOBA_BLOCK, w), lambda h, b: (b, h)),
        out_shape=jax.ShapeDtypeStruct((s, MOBA_W), BF16),
        scratch_shapes=[pltpu.VMEM((LANES, w), F32),
                        pltpu.VMEM((s, LANES), BF16),
                        pltpu.VMEM((MOBA_PAIR, MOBA_BLOCK, 2 * MOBA_HD), BF16),
                        pltpu.VMEM((MOBA_PAIR, MOBA_BLOCK, 2 * MOBA_HD), F32),
                        pltpu.VMEM((MOBA_PAIR, MOBA_BLOCK, 1), F32),
                        pltpu.VMEM((MOBA_PAIR, MOBA_BLOCK, MOBA_GROUP * MOBA_BLOCK), F32),
                        pltpu.VMEM((MOBA_PAIR, MOBA_BLOCK, MOBA_GROUP * MOBA_BLOCK), F32)],
        compiler_params=_cparams("arbitrary", "arbitrary"),
        name="moba",
    )(proj, proj, proj)


def _moba_t_kernel(q_ref, k_ref, v_ref, o_ref, km_ref, oh_ref, vt_ref, qt_ref, acc_ref, m_ref, s0_ref, s1_ref):
    hp = pl.program_id(0)
    bq = pl.program_id(1)
    tq = q_ref.shape[0]
    hd = MOBA_HD
    blk = MOBA_BLOCK
    nb = k_ref.shape[0] // blk
    gw = MOBA_GROUP * blk
    vrows = vt_ref.shape[2]

    @pl.when((hp == 0) & (bq == 0))
    def _():
        lane_i = lax.broadcasted_iota(jnp.int32, (blk, LANES), 1)

        def body(n, carry):
            oh_ref[pl.ds(pl.multiple_of(n * blk, blk), blk), :] = jnp.where(lane_i == n, 1.0, 0.0).astype(BF16)
            return carry

        lax.fori_loop(0, nb, body, 0)

    @pl.when(bq == 0)
    def _():
        km_ref[...] = jnp.zeros_like(km_ref)

        def body(n, carry):
            rows = pl.ds(pl.multiple_of(n * blk, blk), blk)
            km_ref[pl.ds(n, 1), :] = jnp.mean(k_ref[rows, :].astype(F32), axis=0, keepdims=True)
            for h in range(MOBA_PAIR):
                vt_ref[h, n, 0:hd, :] = v_ref[rows, h * hd:(h + 1) * hd].astype(F32).T.astype(BF16)
                vt_ref[h, n, hd:vrows, :] = jnp.ones((vrows - hd, blk), BF16)
            return carry

        lax.fori_loop(0, nb, body, 0)

    row_i = lax.broadcasted_iota(jnp.int32, (LANES, tq), 0)
    row_f = row_i.astype(F32)
    own = bq * (tq // blk) + lax.broadcasted_iota(jnp.int32, (LANES, tq), 1) // blk
    for h in range(MOBA_PAIR):
        qt = q_ref[:, h * hd:(h + 1) * hd].astype(F32).T.astype(BF16)
        km_hi, km_lo = _split_bf16(km_ref[:, h * hd:(h + 1) * hd])
        g = jnp.where(row_i < own, _dot(km_hi, qt) + _dot(km_lo, qt), NEG)
        sel = row_i == own
        for _ in range(MOBA_TOPK):
            mx = jnp.max(g, axis=0, keepdims=True)
            idx = jnp.min(jnp.where(g == mx, row_f, float(LANES)), axis=0, keepdims=True)
            hit = row_f == idx
            sel = sel | (hit & (mx > 0.5 * NEG))
            g = jnp.where(hit, 2.0 * NEG, g)
        qt_ref[h] = jnp.concatenate([qt, jnp.where(sel, 0.0, NEG).astype(BF16)], axis=0)
        m_ref[h] = jnp.full((1, tq), NEG, F32)
        acc_ref[h] = jnp.zeros((vrows, tq), F32)

    def scores(g, slot_ref):
        rows = pl.ds(pl.multiple_of(g * gw, gw), gw)
        oh = oh_ref[rows, :]
        for h in range(MOBA_PAIR):
            k_ext = jnp.concatenate([k_ref[rows, h * hd:(h + 1) * hd], oh], axis=1)
            slot_ref[h] = _dot(k_ext, qt_ref[h])

    def accumulate(g, slot_ref, causal):
        if causal:
            key_pos = g * gw + lax.broadcasted_iota(jnp.int32, (gw, tq), 0)
            qry_pos = bq * tq + lax.broadcasted_iota(jnp.int32, (gw, tq), 1)
            keep = key_pos <= qry_pos
        for h in range(MOBA_PAIR):
            s = slot_ref[h]
            if causal:
                s = jnp.where(keep, s, NEG)
            m_old = m_ref[h]
            m_new = jnp.maximum(m_old, jnp.max(s, axis=0, keepdims=True))
            p = jnp.exp2(s - m_new).astype(BF16)
            pv = _dot(vt_ref[h, g * MOBA_GROUP], p[0:blk, :])
            for t in range(1, MOBA_GROUP):
                pv = pv + _dot(vt_ref[h, g * MOBA_GROUP + t], p[t * blk:(t + 1) * blk, :])
            acc_ref[h] = jnp.exp2(m_old - m_new) * acc_ref[h] + pv
            m_ref[h] = m_new

    full_groups = (bq * (tq // blk)) // MOBA_GROUP
    scores(0, s0_ref)

    def body(i, carry):
        g = 2 * i
        scores(g + 1, s1_ref)
        accumulate(g, s0_ref, False)
        scores(g + 2, s0_ref)
        accumulate(g + 1, s1_ref, False)
        return carry

    lax.fori_loop(0, full_groups // 2, body, 0)

    @pl.when(full_groups % 2 == 0)
    def _():
        accumulate(full_groups, s0_ref, True)

    @pl.when(full_groups % 2 == 1)
    def _():
        scores(full_groups, s1_ref)
        accumulate(full_groups - 1, s0_ref, False)
        accumulate(full_groups, s1_ref, True)

    for h in range(MOBA_PAIR):
        acc = acc_ref[h]
        o_ref[:, h * hd:(h + 1) * hd] = (acc[0:hd, :] / acc[hd:hd + 1, :]).T.astype(BF16)


def _moba_t(proj, tq=2 * MOBA_BLOCK):
    s = proj.shape[0]
    nb = s // MOBA_BLOCK
    w = MOBA_PAIR * MOBA_HD
    gw = MOBA_GROUP * MOBA_BLOCK
    vrows = MOBA_HD + 2 * SUBLANES
    assert s % gw == 0 and nb <= LANES and MOBA_HEADS % MOBA_PAIR == 0
    assert tq % MOBA_BLOCK == 0 and gw % tq == 0
    resident = dict(pipeline_mode=pl.Buffered(1))
    return pl.pallas_call(
        _moba_t_kernel,
        grid=(MOBA_HEADS // MOBA_PAIR, s // tq),
        in_specs=[pl.BlockSpec((tq, w), lambda h, b: (b, OFF_MQ // w + h)),
                  pl.BlockSpec((s, w), lambda h, b: (0, OFF_MK // w + h), **resident),
                  pl.BlockSpec((s, w), lambda h, b: (0, OFF_MV // w + h), **resident)],
        out_specs=pl.BlockSpec((tq, w), lambda h, b: (b, h)),
        out_shape=jax.ShapeDtypeStruct((s, MOBA_W), BF16),
        scratch_shapes=[pltpu.VMEM((LANES, w), F32),
                        pltpu.VMEM((s, LANES), BF16),
                        pltpu.VMEM((MOBA_PAIR, nb, vrows, MOBA_BLOCK), BF16),
                        pltpu.VMEM((MOBA_PAIR, 2 * MOBA_HD, tq), BF16),
                        pltpu.VMEM((MOBA_PAIR, vrows, tq), F32),
                        pltpu.VMEM((MOBA_PAIR, 1, tq), F32),
                        pltpu.VMEM((MOBA_PAIR, gw, tq), F32),
                        pltpu.VMEM((MOBA_PAIR, gw, tq), F32)],
        compiler_params=_cparams("arbitrary", "arbitrary"),
        name="moba",
    )(proj, proj, proj)


def _merge_kernel(or_ref, om_ref, ga_ref, gb_ref, wr_ref, wm_ref, o_ref):
    yr = _dot(or_ref[...], wr_ref[...])
    ym = _dot(om_ref[...], wm_ref[...])
    ga = jax.nn.sigmoid(ga_ref[...].astype(F32))
    gb = jax.nn.sigmoid(gb_ref[...].astype(F32))
    o_ref[...] = (ga * yr + gb * ym).astype(BF16)


def _merge(o_ret, o_moba, proj, w_r, w_m, tm=1024, tn=512):
    s = o_ret.shape[0]
    tm = min(tm, s)
    d = D_MODEL
    return pl.pallas_call(
        _merge_kernel,
        grid=(s // tm, d // tn),
        in_specs=[pl.BlockSpec((tm, RET_V_W), lambda i, j: (i, 0)),
                  pl.BlockSpec((tm, MOBA_W), lambda i, j: (i, 0)),
                  pl.BlockSpec((tm, tn), lambda i, j: (i, OFF_GA // tn + j)),
                  pl.BlockSpec((tm, tn), lambda i, j: (i, OFF_GB // tn + j)),
                  pl.BlockSpec((RET_V_W, tn), lambda i, j: (0, j)),
                  pl.BlockSpec((MOBA_W, tn), lambda i, j: (0, j))],
        out_specs=pl.BlockSpec((tm, tn), lambda i, j: (i, j)),
        out_shape=jax.ShapeDtypeStruct((s, d), BF16),
        compiler_params=_cparams("arbitrary", "arbitrary"),
        name="merge",
    )(o_ret, o_moba, proj, proj, w_r, w_m)


def _mixout_kernel(m_ref, wo_ref, x_ref, npost_ref, g1_ref, npre_ref, sc_ref, sh_ref, x1_ref, u2_ref):
    y = _dot(m_ref[...], wo_ref[...])
    x1 = x_ref[...] + (1.0 + g1_ref[...]) * (_rms(y) * npost_ref[...])
    x1_ref[...] = x1
    u2_ref[...] = ((_rms(x1) * npre_ref[...]) * (1.0 + sc_ref[...]) + sh_ref[...]).astype(BF16)


def _mixout(merged, w_o, x, npost, g1, npre, sc2, sh2, tm=256):
    s, d = x.shape
    tm = min(tm, s)
    vec = pl.BlockSpec((1, d), lambda i: (0, 0))
    rows = pl.BlockSpec((tm, d), lambda i: (i, 0))
    return pl.pallas_call(
        _mixout_kernel,
        grid=(s // tm,),
        in_specs=[rows, pl.BlockSpec((d, d), lambda i: (0, 0)), rows, vec, vec, vec, vec, vec],
        out_specs=[rows, rows],
        out_shape=[jax.ShapeDtypeStruct((s, d), F32), jax.ShapeDtypeStruct((s, d), BF16)],
        compiler_params=_cparams("arbitrary"),
        name="mixout",
    )(merged, w_o, x, npost, g1, npre, sc2, sh2)


def _ffn_kernel(u_ref, wa_ref, wb_ref, cwa_ref, cwb_ref, cba_ref, cbb_ref, wd_ref, x1_ref, npost_ref, g2_ref,
                o_ref, acc_ref, ha_ref, hb_ref, act_ref, tail_ref):
    i = pl.program_id(0)
    j = pl.program_id(1)
    tm = u_ref.shape[0]
    tn = wa_ref.shape[1]
    halo = SUBLANES

    @pl.when((i == 0) & (j == 0))
    def _():
        tail_ref[...] = jnp.zeros_like(tail_ref)

    @pl.when(j == 0)
    def _():
        acc_ref[...] = jnp.zeros_like(acc_ref)

    u = u_ref[...]
    ha_ref[0:halo, :] = tail_ref[j, 0]
    hb_ref[0:halo, :] = tail_ref[j, 1]
    for c in range(tn // FFN_CHUNK):
        cs = slice(c * FFN_CHUNK, (c + 1) * FFN_CHUNK)
        ha_ref[halo:halo + tm, cs] = _dot(u, wa_ref[:, cs])
        hb_ref[halo:halo + tm, cs] = _dot(u, wb_ref[:, cs])
    tail_ref[j, 0] = ha_ref[tm:tm + halo, :]
    tail_ref[j, 1] = hb_ref[tm:tm + halo, :]

    def conv(hbuf, cw, cb, r0, cs):
        out = cb
        for t in range(CONV_WIDTH):
            out = out + hbuf[halo + r0 - t:halo + r0 - t + FFN_ROWS, cs] * cw[CONV_WIDTH - 1 - t:CONV_WIDTH - t, :]
        return out

    for c in range(tn // FFN_CHUNK):
        cs = slice(c * FFN_CHUNK, (c + 1) * FFN_CHUNK)
        cwa, cwb, cba, cbb = cwa_ref[:, cs], cwb_ref[:, cs], cba_ref[:, cs], cbb_ref[:, cs]
        for r0 in range(0, tm, FFN_ROWS):
            a = conv(ha_ref, cwa, cba, r0, cs)
            bgate = conv(hb_ref, cwb, cbb, r0, cs)
            act_ref[r0:r0 + FFN_ROWS, cs] = (_silu(a) * bgate).astype(BF16)
        acc_ref[...] += _dot(act_ref[:, cs], wd_ref[cs, :])

    @pl.when(j == pl.num_programs(1) - 1)
    def _():
        y = acc_ref[...]
        o_ref[...] = x1_ref[...] + (1.0 + g2_ref[...]) * (_rms(y) * npost_ref[...])


def _ffn(u2, w_up, conv_w, conv_b, w_down, x1, npost, g2, tm=512, tn=512):
    s, d = x1.shape
    tm = min(tm, s)
    nj = D_FF // tn
    assert D_FF % tn == 0
    vec = pl.BlockSpec((1, d), lambda i, j: (0, 0))
    rows = pl.BlockSpec((tm, d), lambda i, j: (i, 0))
    return pl.pallas_call(
        _ffn_kernel,
        grid=(s // tm, nj),
        in_specs=[rows,
                  pl.BlockSpec((d, tn), lambda i, j: (0, j)),
                  pl.BlockSpec((d, tn), lambda i, j: (0, j + nj)),
                  pl.BlockSpec((CONV_WIDTH, tn), lambda i, j: (0, j)),
                  pl.BlockSpec((CONV_WIDTH, tn), lambda i, j: (0, j + nj)),
                  pl.BlockSpec((1, tn), lambda i, j: (0, j)),
                  pl.BlockSpec((1, tn), lambda i, j: (0, j + nj)),
                  pl.BlockSpec((tn, d), lambda i, j: (j, 0)),
                  rows, vec, vec],
        out_specs=rows,
        out_shape=jax.ShapeDtypeStruct((s, d), F32),
        scratch_shapes=[pltpu.VMEM((tm, d), F32),
                        pltpu.VMEM((tm + SUBLANES, tn), F32),
                        pltpu.VMEM((tm + SUBLANES, tn), F32),
                        pltpu.VMEM((tm, tn), BF16),
                        pltpu.VMEM((nj, 2, SUBLANES, tn), F32)],
        compiler_params=_cparams("arbitrary", "arbitrary"),
        name="ffn",
    )(u2, w_up, w_up, conv_w, conv_w, conv_b, conv_b, w_down, x1, npost, g2)


def _block(x, c, ada_w, ada_b, mix_norm_pre, mix_norm_post, w_in, ret_w_o, moba_w_o, w_out,
           ffn_norm_pre, ffn_norm_post, ffn_w_up, ffn_conv_w, ffn_conv_b, ffn_w_down, cos2, sin2):
    d = x.shape[1]
    mod = _ada(c, ada_w, ada_b[None, :])
    sh1, sc1, g1, sh2, sc2, g2 = [mod[:, t * d:(t + 1) * d] for t in range(6)]
    proj = _inproj(x, mix_norm_pre[None, :], sc1, sh1, cos2, sin2, w_in.astype(BF16))
    o_ret = _retention(proj)
    o_moba = _moba_t(proj)
    merged = _merge(o_ret, o_moba, proj, ret_w_o.astype(BF16), moba_w_o.astype(BF16))
    x1, u2 = _mixout(merged, w_out.astype(BF16), x, mix_norm_post[None, :], g1,
                     ffn_norm_pre[None, :], sc2, sh2)
    return _ffn(u2, ffn_w_up.astype(BF16), ffn_conv_w, ffn_conv_b[None, :], ffn_w_down.astype(BF16),
                x1, ffn_norm_post[None, :], g2)


def kernel(x, c, ada_w, ada_b, mix_norm_pre, mix_norm_post, w_in, ret_w_o, moba_w_o, w_out,
           ffn_norm_pre, ffn_norm_post, ffn_w_up, ffn_conv_w, ffn_conv_b, ffn_w_down):
    batch, s, _ = x.shape
    depth = ada_w.shape[0]
    cos2, sin2 = _rope_tables(s)
    outs = []
    for bi in range(batch):
        xb = x[bi]
        for l in range(depth):
            xb = _block(xb, c[bi:bi + 1], ada_w[l], ada_b[l], mix_norm_pre[l], mix_norm_post[l], w_in[l],
                        ret_w_o[l], moba_w_o[l], w_out[l], ffn_norm_pre[l], ffn_norm_post[l], ffn_w_up[l],
                        ffn_conv_w[l], ffn_conv_b[l], ffn_w_down[l], cos2, sin2)
        outs.append(xb)
    return jnp.stack(outs)
```

```python
import functools
import math

import jax
import jax.numpy as jnp
from jax import lax
from jax.experimental import pallas as pl
from jax.experimental.pallas import tpu as pltpu

D_MODEL = 2048
RET_HEADS = 8
RET_DK = 128
RET_DV = 256
ROPE_BASE = 10000.0
MOBA_HEADS = 16
MOBA_HD = 128
MOBA_BLOCK = 256
MOBA_TOPK = 3
D_FF = 5632
CONV_WIDTH = 3
NORM_EPS = 1e-6
GN_EPS = 1e-5

RET_QK_W = RET_HEADS * RET_DK
RET_V_W = RET_HEADS * RET_DV
MOBA_W = MOBA_HEADS * MOBA_HD
OFF_RQ = 0
OFF_RK = OFF_RQ + RET_QK_W
OFF_RV = OFF_RK + RET_QK_W
OFF_RG = OFF_RV + RET_V_W
OFF_MQ = OFF_RG + RET_V_W
OFF_MK = OFF_MQ + MOBA_W
OFF_MV = OFF_MK + MOBA_W
OFF_GA = OFF_MV + MOBA_W
OFF_GB = OFF_GA + D_MODEL
IN_WIDTH = OFF_GB + D_MODEL

LANES = 128
SUBLANES = 8
NEG = -1e30
MOBA_QSCALE = math.log2(math.e) / math.sqrt(MOBA_HD)
MOBA_GROUP = 4
MOBA_PAIR = 2
FFN_CHUNK = 256
FFN_ROWS = 64
FFN_DOWN_COLS = 512
NORM_ROWS = 32
VMEM_LIMIT = 56 * 1024 * 1024

F32 = jnp.float32
BF16 = jnp.bfloat16


def _cparams(*sem):
    return pltpu.CompilerParams(dimension_semantics=sem, vmem_limit_bytes=VMEM_LIMIT)


def _dot(a, b):
    return jnp.dot(a, b, preferred_element_type=F32)


def _dot_nt(a, b):
    return lax.dot_general(a, b, (((1,), (1,)), ((), ())), preferred_element_type=F32)


def _split_bf16(x):
    hi = x.astype(BF16)
    lo = (x - hi.astype(F32)).astype(BF16)
    return hi, lo


def _rms(x):
    return x * lax.rsqrt(jnp.mean(x * x, axis=-1, keepdims=True) + NORM_EPS)


def _silu(x):
    return x * jax.nn.sigmoid(x)


def _for_row_chunks(n_rows, fn):
    def body(r, carry):
        fn(pl.ds(pl.multiple_of(r * NORM_ROWS, NORM_ROWS), NORM_ROWS))
        return carry

    lax.fori_loop(0, n_rows // NORM_ROWS, body, 0)


def _ada_kernel(c_ref, w_ref, b_ref, o_ref):
    c = jnp.broadcast_to(_silu(c_ref[...]), (SUBLANES, c_ref.shape[1]))
    c_hi, c_lo = _split_bf16(c)
    w_hi, w_lo = _split_bf16(w_ref[...])
    r = _dot(c_hi, w_hi) + _dot(c_lo, w_hi) + _dot(c_hi, w_lo)
    o_ref[...] = r[0:1] + b_ref[...]


def _ada(c, w, b, tn=1024):
    d, n = w.shape
    return pl.pallas_call(
        _ada_kernel,
        grid=(n // tn,),
        in_specs=[pl.BlockSpec((1, d), lambda j: (0, 0)),
                  pl.BlockSpec((d, tn), lambda j: (0, j)),
                  pl.BlockSpec((1, tn), lambda j: (0, j))],
        out_specs=pl.BlockSpec((1, tn), lambda j: (0, j)),
        out_shape=jax.ShapeDtypeStruct((1, n), F32),
        compiler_params=_cparams("arbitrary"),
        name="ada",
    )(c, w, b)


def _rope_kernel(inv_ref, cos_ref, sin_ref):
    ts = cos_ref.shape[0]
    row = lax.broadcasted_iota(jnp.int32, (ts, LANES), 0) + pl.program_id(0) * ts
    lane = lax.broadcasted_iota(jnp.int32, (ts, LANES), 1)
    ang = row.astype(F32) * inv_ref[...]
    sin = jnp.sin(ang)
    cos_ref[...] = jnp.cos(ang)
    sin_ref[...] = jnp.where(lane < RET_DK // 2, -sin, sin)


def _rope_tables(s, ts=2048):
    ts = min(ts, s)
    half = RET_DK // 2
    inv = ROPE_BASE ** (-jnp.arange(half, dtype=F32) / half)
    inv2 = jnp.concatenate([inv, inv])[None, :]
    return pl.pallas_call(
        _rope_kernel,
        grid=(s // ts,),
        in_specs=[pl.BlockSpec((1, LANES), lambda i: (0, 0))],
        out_specs=[pl.BlockSpec((ts, LANES), lambda i: (i, 0))] * 2,
        out_shape=[jax.ShapeDtypeStruct((s, LANES), F32)] * 2,
        compiler_params=_cparams("arbitrary"),
        name="rope",
    )(inv2)


def _inproj_kernel(x_ref, nw_ref, sc_ref, sh_ref, cos_ref, sin_ref, w_ref, o_ref, u_ref):
    j = pl.program_id(1)
    tn = o_ref.shape[1]

    @pl.when(j == 0)
    def _():
        def piece(rows):
            y = _rms(x_ref[rows, :]) * nw_ref[...]
            u_ref[rows, :] = (y * (1.0 + sc_ref[...]) + sh_ref[...]).astype(BF16)

        _for_row_chunks(x_ref.shape[0], piece)

    r = _dot(u_ref[...], w_ref[...])
    q_tiles = RET_QK_W // tn
    is_mq = (j >= OFF_MQ // tn) & (j < OFF_MK // tn)
    o_ref[...] = (r * jnp.where(is_mq, MOBA_QSCALE, 1.0).astype(F32)).astype(BF16)

    @pl.when(j < 2 * q_tiles)
    def _():
        scale = jnp.where(j >= q_tiles, RET_DK ** -0.5, 1.0).astype(F32)
        cos = cos_ref[...]
        sin = sin_ref[...]
        for h in range(tn // RET_DK):
            xh = r[:, h * RET_DK:(h + 1) * RET_DK]
            rot = xh * cos + pltpu.roll(xh, RET_DK // 2, axis=1) * sin
            o_ref[:, h * RET_DK:(h + 1) * RET_DK] = (rot * scale).astype(BF16)


def _inproj(x, nw, sc, sh, cos2, sin2, w, tm=1024, tn=1024):
    s, d = x.shape
    n = w.shape[1]
    tm = min(tm, s)
    vec = pl.BlockSpec((1, d), lambda i, j: (0, 0))
    tab = pl.BlockSpec((tm, LANES), lambda i, j: (i, 0))
    return pl.pallas_call(
        _inproj_kernel,
        grid=(s // tm, n // tn),
        in_specs=[pl.BlockSpec((tm, d), lambda i, j: (i, 0)), vec, vec, vec, tab, tab,
                  pl.BlockSpec((d, tn), lambda i, j: (0, j))],
        out_specs=pl.BlockSpec((tm, tn), lambda i, j: (i, j)),
        out_shape=jax.ShapeDtypeStruct((s, n), BF16),
        scratch_shapes=[pltpu.VMEM((tm, d), BF16)],
        compiler_params=_cparams("arbitrary", "arbitrary"),
        name="inproj",
    )(x, nw, sc, sh, cos2, sin2, w)


def _ret_kernel(lg_ref, q_ref, k_ref, v_ref, g_ref, o_ref, r_ref, dec_ref, xi_ref, zeta_ref):
    n = pl.program_id(1)
    c = q_ref.shape[0]
    lg = lg_ref[0, 0:1, 0:1]

    @pl.when(n == 0)
    def _():
        r_ref[...] = jnp.zeros_like(r_ref)
        i = lax.broadcasted_iota(jnp.int32, (c, c), 0)
        j = lax.broadcasted_iota(jnp.int32, (c, c), 1)
        d = i - j
        dec_ref[...] = jnp.where(d >= 0, jnp.exp(jnp.maximum(d, 0).astype(F32) * lg), 0.0)
        row = lax.broadcasted_iota(jnp.int32, (c, RET_DK), 0)
        xi_ref[...] = jnp.exp((row + 1).astype(F32) * lg)
        zeta_ref[...] = jnp.exp((c - 1 - row).astype(F32) * lg)

    q = q_ref[...]
    k = k_ref[...]
    v = v_ref[...]
    s = _dot_nt(q, k) * dec_ref[...]
    inner = _dot(s.astype(BF16), v)
    r = r_ref[...]
    cross = _dot((q.astype(F32) * xi_ref[...]).astype(BF16), r.astype(BF16))
    o = inner + cross
    kz = (k.astype(F32) * zeta_ref[...]).T.astype(BF16)
    r_ref[...] = r * jnp.exp(lg * c) + _dot(kz, v)
    oc = o - jnp.mean(o, axis=-1, keepdims=True)
    on = oc * lax.rsqrt(jnp.mean(oc * oc, axis=-1, keepdims=True) + GN_EPS)
    o_ref[...] = (_silu(g_ref[...].astype(F32)) * on).astype(BF16)


def _retention(proj, chunk=512):
    s = proj.shape[0]
    chunk = min(chunk, s)
    log_g = jnp.log1p(-jnp.exp2(-5.0 - jnp.arange(RET_HEADS, dtype=F32)))
    lg = jnp.broadcast_to(log_g[:, None, None], (RET_HEADS, SUBLANES, LANES))
    return pl.pallas_call(
        _ret_kernel,
        grid=(RET_HEADS, s // chunk),
        in_specs=[pl.BlockSpec((1, SUBLANES, LANES), lambda h, n: (h, 0, 0)),
                  pl.BlockSpec((chunk, RET_DK), lambda h, n: (n, OFF_RQ // RET_DK + h)),
                  pl.BlockSpec((chunk, RET_DK), lambda h, n: (n, OFF_RK // RET_DK + h)),
                  pl.BlockSpec((chunk, RET_DV), lambda h, n: (n, OFF_RV // RET_DV + h)),
                  pl.BlockSpec((chunk, RET_DV), lambda h, n: (n, OFF_RG // RET_DV + h))],
        out_specs=pl.BlockSpec((chunk, RET_DV), lambda h, n: (n, h)),
        out_shape=jax.ShapeDtypeStruct((s, RET_V_W), BF16),
        scratch_shapes=[pltpu.VMEM((RET_DK, RET_DV), F32),
                        pltpu.VMEM((chunk, chunk), F32),
                        pltpu.VMEM((chunk, RET_DK), F32),
                        pltpu.VMEM((chunk, RET_DK), F32)],
        compiler_params=_cparams("arbitrary", "arbitrary"),
        name="ret",
    )(lg, proj, proj, proj, proj)


def _moba_kernel(q_ref, k_ref, v_ref, o_ref, km_ref, oh_ref, qx_ref, acc_ref, m_ref, s0_ref, s1_ref):
    hp = pl.program_id(0)
    b = pl.program_id(1)
    blk = q_ref.shape[0]
    hd = MOBA_HD
    nb = k_ref.shape[0] // blk
    gw = MOBA_GROUP * blk
    lane_i = lax.broadcasted_iota(jnp.int32, (blk, LANES), 1)

    @pl.when((hp == 0) & (b == 0))
    def _():
        def body(n, carry):
            oh_ref[pl.ds(pl.multiple_of(n * blk, blk), blk), :] = jnp.where(lane_i == n, 1.0, 0.0).astype(BF16)
            return carry

        lax.fori_loop(0, nb, body, 0)

    @pl.when(b == 0)
    def _():
        km_ref[...] = jnp.zeros_like(km_ref)

        def body(n, carry):
            kb = k_ref[pl.ds(pl.multiple_of(n * blk, blk), blk), :].astype(F32)
            km_ref[pl.ds(n, 1), :] = jnp.mean(kb, axis=0, keepdims=True)
            return carry

        lax.fori_loop(0, nb, body, 0)

    lane = lane_i.astype(F32)
    for h in range(MOBA_PAIR):
        q = q_ref[:, h * hd:(h + 1) * hd]
        km_hi, km_lo = _split_bf16(km_ref[:, h * hd:(h + 1) * hd])
        g = jnp.where(lane_i < b, _dot_nt(q, km_hi) + _dot_nt(q, km_lo), NEG)
        sel = lane_i == b
        for _ in range(MOBA_TOPK):
            mx = jnp.max(g, axis=-1, keepdims=True)
            idx = jnp.min(jnp.where(g == mx, lane, float(LANES)), axis=-1, keepdims=True)
            hit = lane == idx
            sel = sel | (hit & (mx > 0.5 * NEG))
            g = jnp.where(hit, 2.0 * NEG, g)
        qx_ref[h] = jnp.concatenate([q, jnp.where(sel, 0.0, NEG).astype(BF16)], axis=1)
        m_ref[h] = jnp.full((blk, 1), NEG, F32)
        acc_ref[h] = jnp.zeros((blk, 2 * hd), F32)

    ones = jnp.ones((gw, hd), BF16)

    def rows_of(g):
        return pl.ds(pl.multiple_of(g * gw, gw), gw)

    def scores(g, slot_ref):
        rows = rows_of(g)
        oh = oh_ref[rows, :]
        for h in range(MOBA_PAIR):
            k_ext = jnp.concatenate([k_ref[rows, h * hd:(h + 1) * hd], oh], axis=1)
            slot_ref[h] = _dot_nt(qx_ref[h], k_ext)

    def accumulate(g, slot_ref, causal):
        rows = rows_of(g)
        if causal:
            key_pos = g * gw + lax.broadcasted_iota(jnp.int32, (blk, gw), 1)
            qry_pos = b * blk + lax.broadcasted_iota(jnp.int32, (blk, gw), 0)
            keep = key_pos <= qry_pos
        for h in range(MOBA_PAIR):
            s = slot_ref[h]
            if causal:
                s = jnp.where(keep, s, NEG)
            m_old = m_ref[h]
            m_new = jnp.maximum(m_old, jnp.max(s, axis=-1, keepdims=True))
            p = jnp.exp2(s - m_new).astype(BF16)
            v_ext = jnp.concatenate([v_ref[rows, h * hd:(h + 1) * hd], ones], axis=1)
            acc_ref[h] = jnp.exp2(m_old - m_new) * acc_ref[h] + _dot(p, v_ext)
            m_ref[h] = m_new

    full_groups = b // MOBA_GROUP
    scores(0, s0_ref)

    def body(i, carry):
        g = 2 * i
        scores(g + 1, s1_ref)
        accumulate(g, s0_ref, False)
        scores(g + 2, s0_ref)
        accumulate(g + 1, s1_ref, False)
        return carry

    lax.fori_loop(0, full_groups // 2, body, 0)

    @pl.when(full_groups % 2 == 0)
    def _():
        accumulate(full_groups, s0_ref, True)

    @pl.when(full_groups % 2 == 1)
    def _():
        scores(full_groups, s1_ref)
        accumulate(full_groups - 1, s0_ref, False)
        accumulate(full_groups, s1_ref, True)

    for h in range(MOBA_PAIR):
        acc = acc_ref[h]
        o_ref[:, h * hd:(h + 1) * hd] = (acc[:, :hd] / acc[:, hd:]).astype(BF16)


def _moba(proj):
    s = proj.shape[0]
    nb = s // MOBA_BLOCK
    w = MOBA_PAIR * MOBA_HD
    assert s % (MOBA_GROUP * MOBA_BLOCK) == 0 and nb <= LANES and MOBA_HEADS % MOBA_PAIR == 0
    resident = dict(pipeline_mode=pl.Buffered(1))
    return pl.pallas_call(
        _moba_kernel,
        grid=(MOBA_HEADS // MOBA_PAIR, nb),
        in_specs=[pl.BlockSpec((MOBA_BLOCK, w), lambda h, b: (b, OFF_MQ // w + h)),
                  pl.BlockSpec((s, w), lambda h, b: (0, OFF_MK // w + h), **resident),
                  pl.BlockSpec((s, w), lambda h, b: (0, OFF_MV // w + h), **resident)],
        out_specs=pl.BlockSpec((MOBA_BLOCK, w), lambda h, b: (b, h)),
        out_shape=jax.ShapeDtypeStruct((s, MOBA_W), BF16),
        scratch_shapes=[pltpu.VMEM((LANES, w), F32),
                        pltpu.VMEM((s, LANES), BF16),
                        pltpu.VMEM((MOBA_PAIR, MOBA_BLOCK, 2 * MOBA_HD), BF16),
                        pltpu.VMEM((MOBA_PAIR, MOBA_BLOCK, 2 * MOBA_HD), F32),
                        pltpu.VMEM((MOBA_PAIR, MOBA_BLOCK, 1), F32),
                        pltpu.VMEM((MOBA_PAIR, MOBA_BLOCK, MOBA_GROUP * MOBA_BLOCK), F32),
                        pltpu.VMEM((MOBA_PAIR, MOBA_BLOCK, MOBA_GROUP * MOBA_BLOCK), F32)],
        compiler_params=_cparams("arbitrary", "arbitrary"),
        name="moba",
    )(proj, proj, proj)


def _moba_t_kernel(q_ref, k_ref, v_ref, o_ref, km_ref, oh_ref, vt_ref, qt_ref, acc_ref, m_ref, s0_ref, s1_ref):
    hp = pl.program_id(0)
    bq = pl.program_id(1)
    tq = q_ref.shape[0]
    hd = MOBA_HD
    blk = MOBA_BLOCK
    nb = k_ref.shape[0] // blk
    gw = MOBA_GROUP * blk
    vrows = vt_ref.shape[2]

    @pl.when((hp == 0) & (bq == 0))
    def _():
        lane_i = lax.broadcasted_iota(jnp.int32, (blk, LANES), 1)

        def body(n, carry):
            oh_ref[pl.ds(pl.multiple_of(n * blk, blk), blk), :] = jnp.where(lane_i == n, 1.0, 0.0).astype(BF16)
            return carry

        lax.fori_loop(0, nb, body, 0)

    @pl.when(bq == 0)
    def _():
        km_ref[...] = jnp.zeros_like(km_ref)

        def body(n, carry):
            rows = pl.ds(pl.multiple_of(n * blk, blk), blk)
            km_ref[pl.ds(n, 1), :] = jnp.mean(k_ref[rows, :].astype(F32), axis=0, keepdims=True)
            for h in range(MOBA_PAIR):
                vt_ref[h, n, 0:hd, :] = v_ref[rows, h * hd:(h + 1) * hd].astype(F32).T.astype(BF16)
                vt_ref[h, n, hd:vrows, :] = jnp.ones((vrows - hd, blk), BF16)
            return carry

        lax.fori_loop(0, nb, body, 0)

    row_i = lax.broadcasted_iota(jnp.int32, (LANES, tq), 0)
    row_f = row_i.astype(F32)
    own = bq * (tq // blk) + lax.broadcasted_iota(jnp.int32, (LANES, tq), 1) // blk
    for h in range(MOBA_PAIR):
        qt = q_ref[:, h * hd:(h + 1) * hd].astype(F32).T.astype(BF16)
        km_hi, km_lo = _split_bf16(km_ref[:, h * hd:(h + 1) * hd])
        g = jnp.where(row_i < own, _dot(km_hi, qt) + _dot(km_lo, qt), NEG)
        sel = row_i == own
        for _ in range(MOBA_TOPK):
            mx = jnp.max(g, axis=0, keepdims=True)
            idx = jnp.min(jnp.where(g == mx, row_f, float(LANES)), axis=0, keepdims=True)
            hit = row_f == idx
            sel = sel | (hit & (mx > 0.5 * NEG))
            g = jnp.where(hit, 2.0 * NEG, g)
        qt_ref[h] = jnp.concatenate([qt, jnp.where(sel, 0.0, NEG).astype(BF16)], axis=0)
        m_ref[h] = jnp.full((1, tq), NEG, F32)
        acc_ref[h] = jnp.zeros((vrows, tq), F32)

    def scores(g, slot_ref):
        rows = pl.ds(pl.multiple_of(g * gw, gw), gw)
        oh = oh_ref[rows, :]
        for h in range(MOBA_PAIR):
            k_ext = jnp.concatenate([k_ref[rows, h * hd:(h + 1) * hd], oh], axis=1)
            slot_ref[h] = _dot(k_ext, qt_ref[h])

    def accumulate(g, slot_ref, causal):
        if causal:
            key_pos = g * gw + lax.broadcasted_iota(jnp.int32, (gw, tq), 0)
            qry_pos = bq * tq + lax.broadcasted_iota(jnp.int32, (gw, tq), 1)
            keep = key_pos <= qry_pos
        for h in range(MOBA_PAIR):
            s = slot_ref[h]
            if causal:
                s = jnp.where(keep, s, NEG)
            m_old = m_ref[h]
            m_new = jnp.maximum(m_old, jnp.max(s, axis=0, keepdims=True))
            p = jnp.exp2(s - m_new).astype(BF16)
            pv = _dot(vt_ref[h, g * MOBA_GROUP], p[0:blk, :])
            for t in range(1, MOBA_GROUP):
                pv = pv + _dot(vt_ref[h, g * MOBA_GROUP + t], p[t * blk:(t + 1) * blk, :])
            acc_ref[h] = jnp.exp2(m_old - m_new) * acc_ref[h] + pv
            m_ref[h] = m_new

    full_groups = (bq * (tq // blk)) // MOBA_GROUP
    scores(0, s0_ref)

    def body(i, carry):
        g = 2 * i
        scores(g + 1, s1_ref)
        accumulate(g, s0_ref, False)
        scores(g + 2, s0_ref)
        accumulate(g + 1, s1_ref, False)
        return carry

    lax.fori_loop(0, full_groups // 2, body, 0)

    @pl.when(full_groups % 2 == 0)
    def _():
        accumulate(full_groups, s0_ref, True)

    @pl.when(full_groups % 2 == 1)
    def _():
        scores(full_groups, s1_ref)
        accumulate(full_groups - 1, s0_ref, False)
        accumulate(full_groups, s1_ref, True)

    for h in range(MOBA_PAIR):
        acc = acc_ref[h]
        o_ref[:, h * hd:(h + 1) * hd] = (acc[0:hd, :] / acc[hd:hd + 1, :]).T.astype(BF16)


def _moba_t(proj, tq=2 * MOBA_BLOCK):
    s = proj.shape[0]
    nb = s // MOBA_BLOCK
    w = MOBA_PAIR * MOBA_HD
    gw = MOBA_GROUP * MOBA_BLOCK
    vrows = MOBA_HD + 2 * SUBLANES
    assert s % gw == 0 and nb <= LANES and MOBA_HEADS % MOBA_PAIR == 0
    assert tq % MOBA_BLOCK == 0 and gw % tq == 0
    resident = dict(pipeline_mode=pl.Buffered(1))
    return pl.pallas_call(
        _moba_t_kernel,
        grid=(MOBA_HEADS // MOBA_PAIR, s // tq),
        in_specs=[pl.BlockSpec((tq, w), lambda h, b: (b, OFF_MQ // w + h)),
                  pl.BlockSpec((s, w), lambda h, b: (0, OFF_MK // w + h), **resident),
                  pl.BlockSpec((s, w), lambda h, b: (0, OFF_MV // w + h), **resident)],
        out_specs=pl.BlockSpec((tq, w), lambda h, b: (b, h)),
        out_shape=jax.ShapeDtypeStruct((s, MOBA_W), BF16),
        scratch_shapes=[pltpu.VMEM((LANES, w), F32),
                        pltpu.VMEM((s, LANES), BF16),
                        pltpu.VMEM((MOBA_PAIR, nb, vrows, MOBA_BLOCK), BF16),
                        pltpu.VMEM((MOBA_PAIR, 2 * MOBA_HD, tq), BF16),
                        pltpu.VMEM((MOBA_PAIR, vrows, tq), F32),
                        pltpu.VMEM((MOBA_PAIR, 1, tq), F32),
                        pltpu.VMEM((MOBA_PAIR, gw, tq), F32),
                        pltpu.VMEM((MOBA_PAIR, gw, tq), F32)],
        compiler_params=_cparams("arbitrary", "arbitrary"),
        name="moba",
    )(proj, proj, proj)


def _merge_kernel(or_ref, om_ref, ga_ref, gb_ref, wr_ref, wm_ref, o_ref):
    yr = _dot(or_ref[...], wr_ref[...])
    ym = _dot(om_ref[...], wm_ref[...])
    ga = jax.nn.sigmoid(ga_ref[...].astype(F32))
    gb = jax.nn.sigmoid(gb_ref[...].astype(F32))
    o_ref[...] = (ga * yr + gb * ym).astype(BF16)


def _merge(o_ret, o_moba, proj, w_r, w_m, tm=1024, tn=512):
    s = o_ret.shape[0]
    tm = min(tm, s)
    d = D_MODEL
    return pl.pallas_call(
        _merge_kernel,
        grid=(s // tm, d // tn),
        in_specs=[pl.BlockSpec((tm, RET_V_W), lambda i, j: (i, 0)),
                  pl.BlockSpec((tm, MOBA_W), lambda i, j: (i, 0)),
                  pl.BlockSpec((tm, tn), lambda i, j: (i, OFF_GA // tn + j)),
                  pl.BlockSpec((tm, tn), lambda i, j: (i, OFF_GB // tn + j)),
                  pl.BlockSpec((RET_V_W, tn), lambda i, j: (0, j)),
                  pl.BlockSpec((MOBA_W, tn), lambda i, j: (0, j))],
        out_specs=pl.BlockSpec((tm, tn), lambda i, j: (i, j)),
        out_shape=jax.ShapeDtypeStruct((s, d), BF16),
        compiler_params=_cparams("arbitrary", "arbitrary"),
        name="merge",
    )(o_ret, o_moba, proj, proj, w_r, w_m)


def _mixout_kernel(m_ref, wo_ref, x_ref, npost_ref, g1_ref, npre_ref, sc_ref, sh_ref, x1_ref, u2_ref):
    y = _dot(m_ref[...], wo_ref[...])
    x1 = x_ref[...] + (1.0 + g1_ref[...]) * (_rms(y) * npost_ref[...])
    x1_ref[...] = x1
    u2_ref[...] = ((_rms(x1) * npre_ref[...]) * (1.0 + sc_ref[...]) + sh_ref[...]).astype(BF16)


def _mixout(merged, w_o, x, npost, g1, npre, sc2, sh2, tm=256):
    s, d = x.shape
    tm = min(tm, s)
    vec = pl.BlockSpec((1, d), lambda i: (0, 0))
    rows = pl.BlockSpec((tm, d), lambda i: (i, 0))
    return pl.pallas_call(
        _mixout_kernel,
        grid=(s // tm,),
        in_specs=[rows, pl.BlockSpec((d, d), lambda i: (0, 0)), rows, vec, vec, vec, vec, vec],
        out_specs=[rows, rows],
        out_shape=[jax.ShapeDtypeStruct((s, d), F32), jax.ShapeDtypeStruct((s, d), BF16)],
        compiler_params=_cparams("arbitrary"),
        name="mixout",
    )(merged, w_o, x, npost, g1, npre, sc2, sh2)


def _ffn_kernel(u_ref, wa_ref, wb_ref, cwa_ref, cwb_ref, cba_ref, cbb_ref, wd_ref, x1_ref, npost_ref, g2_ref,
                o_ref, ha_ref, hb_ref, act_ref, tail_ref):
    i = pl.program_id(0)
    j = pl.program_id(1)
    tm = u_ref.shape[0]
    tn = wa_ref.shape[1]
    halo = SUBLANES

    @pl.when((i == 0) & (j == 0))
    def _():
        tail_ref[...] = jnp.zeros_like(tail_ref)

    @pl.when(j == 0)
    def _():
        o_ref[...] = jnp.zeros_like(o_ref)

    ha_ref[0:halo, :] = tail_ref[j, 0]
    hb_ref[0:halo, :] = tail_ref[j, 1]
    for c in range(tn // FFN_CHUNK):
        cs = slice(c * FFN_CHUNK, (c + 1) * FFN_CHUNK)
        ha_ref[halo:halo + tm, cs] = _dot(u_ref[...], wa_ref[:, cs])
        hb_ref[halo:halo + tm, cs] = _dot(u_ref[...], wb_ref[:, cs])
    tail_ref[j, 0] = ha_ref[tm:tm + halo, :]
    tail_ref[j, 1] = hb_ref[tm:tm + halo, :]

    def conv(hbuf, cw, cb, r0, cs):
        out = cb
        for t in range(CONV_WIDTH):
            out = out + hbuf[halo + r0 - t:halo + r0 - t + FFN_ROWS, cs] * cw[CONV_WIDTH - 1 - t:CONV_WIDTH - t, :]
        return out

    for c in range(tn // FFN_CHUNK):
        cs = slice(c * FFN_CHUNK, (c + 1) * FFN_CHUNK)
        cwa, cwb, cba, cbb = cwa_ref[:, cs], cwb_ref[:, cs], cba_ref[:, cs], cbb_ref[:, cs]
        for r0 in range(0, tm, FFN_ROWS):
            a = conv(ha_ref, cwa, cba, r0, cs)
            bgate = conv(hb_ref, cwb, cbb, r0, cs)
            act_ref[r0:r0 + FFN_ROWS, cs] = (_silu(a) * bgate).astype(BF16)
        for n0 in range(0, o_ref.shape[1], FFN_DOWN_COLS):
            ns = slice(n0, n0 + FFN_DOWN_COLS)
            o_ref[:, ns] += _dot(act_ref[:, cs], wd_ref[cs, ns])

    @pl.when(j == pl.num_programs(1) - 1)
    def _():
        def piece(rows):
            y = o_ref[rows, :]
            o_ref[rows, :] = x1_ref[rows, :] + (1.0 + g2_ref[...]) * (_rms(y) * npost_ref[...])

        _for_row_chunks(tm, piece)


def _ffn(u2, w_up, conv_w, conv_b, w_down, x1, npost, g2, tm=1024, tn=512):
    s, d = x1.shape
    tm = min(tm, s)
    nj = D_FF // tn
    assert D_FF % tn == 0
    vec = pl.BlockSpec((1, d), lambda i, j: (0, 0))
    rows = pl.BlockSpec((tm, d), lambda i, j: (i, 0))
    resid = pl.BlockSpec((tm, d), lambda i, j: (i, 0), pipeline_mode=pl.Buffered(1))
    return pl.pallas_call(
        _ffn_kernel,
        grid=(s // tm, nj),
        in_specs=[rows,
                  pl.BlockSpec((d, tn), lambda i, j: (0, j)),
                  pl.BlockSpec((d, tn), lambda i, j: (0, j + nj)),
                  pl.BlockSpec((CONV_WIDTH, tn), lambda i, j: (0, j)),
                  pl.BlockSpec((CONV_WIDTH, tn), lambda i, j: (0, j + nj)),
                  pl.BlockSpec((1, tn), lambda i, j: (0, j)),
                  pl.BlockSpec((1, tn), lambda i, j: (0, j + nj)),
                  pl.BlockSpec((tn, d), lambda i, j: (j, 0)),
                  resid, vec, vec],
        out_specs=rows,
        out_shape=jax.ShapeDtypeStruct((s, d), F32),
        scratch_shapes=[pltpu.VMEM((tm + SUBLANES, tn), F32),
                        pltpu.VMEM((tm + SUBLANES, tn), F32),
                        pltpu.VMEM((tm, tn), BF16),
                        pltpu.VMEM((nj, 2, SUBLANES, tn), F32)],
        compiler_params=_cparams("arbitrary", "arbitrary"),
        name="ffn",
    )(u2, w_up, w_up, conv_w, conv_w, conv_b, conv_b, w_down, x1, npost, g2)


def _block(x, c, ada_w, ada_b, mix_norm_pre, mix_norm_post, w_in, ret_w_o, moba_w_o, w_out,
           ffn_norm_pre, ffn_norm_post, ffn_w_up, ffn_conv_w, ffn_conv_b, ffn_w_down, cos2, sin2):
    d = x.shape[1]
    mod = _ada(c, ada_w, ada_b[None, :])
    sh1, sc1, g1, sh2, sc2, g2 = [mod[:, t * d:(t + 1) * d] for t in range(6)]
    proj = _inproj(x, mix_norm_pre[None, :], sc1, sh1, cos2, sin2, w_in.astype(BF16))
    o_ret = _retention(proj)
    o_moba = _moba_t(proj)
    merged = _merge(o_ret, o_moba, proj, ret_w_o.astype(BF16), moba_w_o.astype(BF16))
    x1, u2 = _mixout(merged, w_out.astype(BF16), x, mix_norm_post[None, :], g1,
                     ffn_norm_pre[None, :], sc2, sh2)
    return _ffn(u2, ffn_w_up.astype(BF16), ffn_conv_w, ffn_conv_b[None, :], ffn_w_down.astype(BF16),
                x1, ffn_norm_post[None, :], g2)


def kernel(x, c, ada_w, ada_b, mix_norm_pre, mix_norm_post, w_in, ret_w_o, moba_w_o, w_out,
           ffn_norm_pre, ffn_norm_post, ffn_w_up, ffn_conv_w, ffn_conv_b, ffn_w_down):
    batch, s, _ = x.shape
    depth = ada_w.shape[0]
    cos2, sin2 = _rope_tables(s)
    outs = []
    for bi in range(batch):
        xb = x[bi]
        for l in range(depth):
            xb = _block(xb, c[bi:bi + 1], ada_w[l], ada_b[l], mix_norm_pre[l], mix_norm_post[l], w_in[l],
                        ret_w_o[l], moba_w_o[l], w_out[l], ffn_norm_pre[l], ffn_norm_post[l], ffn_w_up[l],
                        ffn_conv_w[l], ffn_conv_b[l], ffn_w_down[l], cos2, sin2)
        outs.append(xb)
    return jnp.stack(outs)
```

```python
import functools
import math

import jax
import jax.numpy as jnp
from jax import lax
from jax.experimental import pallas as pl
from jax.experimental.pallas import tpu as pltpu

D_MODEL = 2048
RET_HEADS = 8
RET_DK = 128
RET_DV = 256
ROPE_BASE = 10000.0
MOBA_HEADS = 16
MOBA_HD = 128
MOBA_BLOCK = 256
MOBA_TOPK = 3
D_FF = 5632
CONV_WIDTH = 3
NORM_EPS = 1e-6
GN_EPS = 1e-5

RET_QK_W = RET_HEADS * RET_DK
RET_V_W = RET_HEADS * RET_DV
MOBA_W = MOBA_HEADS * MOBA_HD
OFF_RQ = 0
OFF_RK = OFF_RQ + RET_QK_W
OFF_RV = OFF_RK + RET_QK_W
OFF_RG = OFF_RV + RET_V_W
OFF_MQ = OFF_RG + RET_V_W
OFF_MK = OFF_MQ + MOBA_W
OFF_MV = OFF_MK + MOBA_W
OFF_GA = OFF_MV + MOBA_W
OFF_GB = OFF_GA + D_MODEL
IN_WIDTH = OFF_GB + D_MODEL

LANES = 128
SUBLANES = 8
NEG = -1e30
MOBA_QSCALE = math.log2(math.e) / math.sqrt(MOBA_HD)
MOBA_GROUP = 4
MOBA_PAIR = 2
FFN_CHUNK = 256
FFN_ROWS = 64
FFN_DOWN_COLS = 512
NORM_ROWS = 32
NORM_UNROLL = 4
VMEM_LIMIT = 56 * 1024 * 1024

F32 = jnp.float32
BF16 = jnp.bfloat16


def _cparams(*sem):
    return pltpu.CompilerParams(dimension_semantics=sem, vmem_limit_bytes=VMEM_LIMIT)


def _dot(a, b):
    return jnp.dot(a, b, preferred_element_type=F32)


def _dot_nt(a, b):
    return lax.dot_general(a, b, (((1,), (1,)), ((), ())), preferred_element_type=F32)


def _split_bf16(x):
    hi = x.astype(BF16)
    lo = (x - hi.astype(F32)).astype(BF16)
    return hi, lo


def _rms(x):
    return x * lax.rsqrt(jnp.mean(x * x, axis=-1, keepdims=True) + NORM_EPS)


def _silu(x):
    return x * jax.nn.sigmoid(x)


def _for_row_chunks(n_rows, fn):
    step = NORM_ROWS * NORM_UNROLL

    def body(r, carry):
        for t in range(NORM_UNROLL):
            fn(pl.ds(pl.multiple_of(r * step + t * NORM_ROWS, NORM_ROWS), NORM_ROWS))
        return carry

    lax.fori_loop(0, n_rows // step, body, 0)


def _ada_kernel(c_ref, w_ref, b_ref, o_ref):
    c = jnp.broadcast_to(_silu(c_ref[...]), (SUBLANES, c_ref.shape[1]))
    c_hi, c_lo = _split_bf16(c)
    w_hi, w_lo = _split_bf16(w_ref[...])
    r = _dot(c_hi, w_hi) + _dot(c_lo, w_hi) + _dot(c_hi, w_lo)
    o_ref[...] = r[0:1] + b_ref[...]


def _ada(c, w, b, tn=1024):
    d, n = w.shape
    return pl.pallas_call(
        _ada_kernel,
        grid=(n // tn,),
        in_specs=[pl.BlockSpec((1, d), lambda j: (0, 0)),
                  pl.BlockSpec((d, tn), lambda j: (0, j)),
                  pl.BlockSpec((1, tn), lambda j: (0, j))],
        out_specs=pl.BlockSpec((1, tn), lambda j: (0, j)),
        out_shape=jax.ShapeDtypeStruct((1, n), F32),
        compiler_params=_cparams("arbitrary"),
        name="ada",
    )(c, w, b)


def _rope_kernel(inv_ref, cos_ref, sin_ref):
    ts = cos_ref.shape[0]
    row = lax.broadcasted_iota(jnp.int32, (ts, LANES), 0) + pl.program_id(0) * ts
    lane = lax.broadcasted_iota(jnp.int32, (ts, LANES), 1)
    ang = row.astype(F32) * inv_ref[...]
    sin = jnp.sin(ang)
    cos_ref[...] = jnp.cos(ang)
    sin_ref[...] = jnp.where(lane < RET_DK // 2, -sin, sin)


def _rope_tables(s, ts=2048):
    ts = min(ts, s)
    half = RET_DK // 2
    inv = ROPE_BASE ** (-jnp.arange(half, dtype=F32) / half)
    inv2 = jnp.concatenate([inv, inv])[None, :]
    return pl.pallas_call(
        _rope_kernel,
        grid=(s // ts,),
        in_specs=[pl.BlockSpec((1, LANES), lambda i: (0, 0))],
        out_specs=[pl.BlockSpec((ts, LANES), lambda i: (i, 0))] * 2,
        out_shape=[jax.ShapeDtypeStruct((s, LANES), F32)] * 2,
        compiler_params=_cparams("arbitrary"),
        name="rope",
    )(inv2)


def _inproj_kernel(x_ref, nw_ref, sc_ref, sh_ref, cos_ref, sin_ref, w_ref, o_ref, u_ref):
    j = pl.program_id(1)
    tn = o_ref.shape[1]

    @pl.when(j == 0)
    def _():
        def piece(rows):
            y = _rms(x_ref[rows, :]) * nw_ref[...]
            u_ref[rows, :] = (y * (1.0 + sc_ref[...]) + sh_ref[...]).astype(BF16)

        _for_row_chunks(x_ref.shape[0], piece)

    r = _dot(u_ref[...], w_ref[...])
    q_tiles = RET_QK_W // tn
    is_mq = (j >= OFF_MQ // tn) & (j < OFF_MK // tn)
    o_ref[...] = (r * jnp.where(is_mq, MOBA_QSCALE, 1.0).astype(F32)).astype(BF16)

    @pl.when(j < 2 * q_tiles)
    def _():
        scale = jnp.where(j >= q_tiles, RET_DK ** -0.5, 1.0).astype(F32)
        cos = cos_ref[...]
        sin = sin_ref[...]
        for h in range(tn // RET_DK):
            xh = r[:, h * RET_DK:(h + 1) * RET_DK]
            rot = xh * cos + pltpu.roll(xh, RET_DK // 2, axis=1) * sin
            o_ref[:, h * RET_DK:(h + 1) * RET_DK] = (rot * scale).astype(BF16)


def _inproj(x, nw, sc, sh, cos2, sin2, w, tm=1024, tn=1024):
    s, d = x.shape
    n = w.shape[1]
    tm = min(tm, s)
    vec = pl.BlockSpec((1, d), lambda i, j: (0, 0))
    tab = pl.BlockSpec((tm, LANES), lambda i, j: (i, 0))
    return pl.pallas_call(
        _inproj_kernel,
        grid=(s // tm, n // tn),
        in_specs=[pl.BlockSpec((tm, d), lambda i, j: (i, 0)), vec, vec, vec, tab, tab,
                  pl.BlockSpec((d, tn), lambda i, j: (0, j))],
        out_specs=pl.BlockSpec((tm, tn), lambda i, j: (i, j)),
        out_shape=jax.ShapeDtypeStruct((s, n), BF16),
        scratch_shapes=[pltpu.VMEM((tm, d), BF16)],
        compiler_params=_cparams("arbitrary", "arbitrary"),
        name="inproj",
    )(x, nw, sc, sh, cos2, sin2, w)


def _ret_kernel(lg_ref, q_ref, k_ref, v_ref, g_ref, o_ref, r_ref, dec_ref, xi_ref, zeta_ref):
    n = pl.program_id(1)
    c = q_ref.shape[0]
    lg = lg_ref[0, 0:1, 0:1]

    @pl.when(n == 0)
    def _():
        r_ref[...] = jnp.zeros_like(r_ref)
        i = lax.broadcasted_iota(jnp.int32, (c, c), 0)
        j = lax.broadcasted_iota(jnp.int32, (c, c), 1)
        d = i - j
        dec_ref[...] = jnp.where(d >= 0, jnp.exp(jnp.maximum(d, 0).astype(F32) * lg), 0.0)
        row = lax.broadcasted_iota(jnp.int32, (c, RET_DK), 0)
        xi_ref[...] = jnp.exp((row + 1).astype(F32) * lg)
        zeta_ref[...] = jnp.exp((c - 1 - row).astype(F32) * lg)

    q = q_ref[...]
    k = k_ref[...]
    v = v_ref[...]
    s = _dot_nt(q, k) * dec_ref[...]
    inner = _dot(s.astype(BF16), v)
    r = r_ref[...]
    cross = _dot((q.astype(F32) * xi_ref[...]).astype(BF16), r.astype(BF16))
    o = inner + cross
    kz = (k.astype(F32) * zeta_ref[...]).T.astype(BF16)
    r_ref[...] = r * jnp.exp(lg * c) + _dot(kz, v)
    oc = o - jnp.mean(o, axis=-1, keepdims=True)
    on = oc * lax.rsqrt(jnp.mean(oc * oc, axis=-1, keepdims=True) + GN_EPS)
    o_ref[...] = (_silu(g_ref[...].astype(F32)) * on).astype(BF16)


def _retention(proj, chunk=512):
    s = proj.shape[0]
    chunk = min(chunk, s)
    log_g = jnp.log1p(-jnp.exp2(-5.0 - jnp.arange(RET_HEADS, dtype=F32)))
    lg = jnp.broadcast_to(log_g[:, None, None], (RET_HEADS, SUBLANES, LANES))
    return pl.pallas_call(
        _ret_kernel,
        grid=(RET_HEADS, s // chunk),
        in_specs=[pl.BlockSpec((1, SUBLANES, LANES), lambda h, n: (h, 0, 0)),
                  pl.BlockSpec((chunk, RET_DK), lambda h, n: (n, OFF_RQ // RET_DK + h)),
                  pl.BlockSpec((chunk, RET_DK), lambda h, n: (n, OFF_RK // RET_DK + h)),
                  pl.BlockSpec((chunk, RET_DV), lambda h, n: (n, OFF_RV // RET_DV + h)),
                  pl.BlockSpec((chunk, RET_DV), lambda h, n: (n, OFF_RG // RET_DV + h))],
        out_specs=pl.BlockSpec((chunk, RET_DV), lambda h, n: (n, h)),
        out_shape=jax.ShapeDtypeStruct((s, RET_V_W), BF16),
        scratch_shapes=[pltpu.VMEM((RET_DK, RET_DV), F32),
                        pltpu.VMEM((chunk, chunk), F32),
                        pltpu.VMEM((chunk, RET_DK), F32),
                        pltpu.VMEM((chunk, RET_DK), F32)],
        compiler_params=_cparams("arbitrary", "arbitrary"),
        name="ret",
    )(lg, proj, proj, proj, proj)


def _moba_kernel(q_ref, k_ref, v_ref, o_ref, km_ref, oh_ref, qx_ref, acc_ref, m_ref, s0_ref, s1_ref):
    hp = pl.program_id(0)
    b = pl.program_id(1)
    blk = q_ref.shape[0]
    hd = MOBA_HD
    nb = k_ref.shape[0] // blk
    gw = MOBA_GROUP * blk
    lane_i = lax.broadcasted_iota(jnp.int32, (blk, LANES), 1)

    @pl.when((hp == 0) & (b == 0))
    def _():
        def body(n, carry):
            oh_ref[pl.ds(pl.multiple_of(n * blk, blk), blk), :] = jnp.where(lane_i == n, 1.0, 0.0).astype(BF16)
            return carry

        lax.fori_loop(0, nb, body, 0)

    @pl.when(b == 0)
    def _():
        km_ref[...] = jnp.zeros_like(km_ref)

        def body(n, carry):
            kb = k_ref[pl.ds(pl.multiple_of(n * blk, blk), blk), :].astype(F32)
            km_ref[pl.ds(n, 1), :] = jnp.mean(kb, axis=0, keepdims=True)
            return carry

        lax.fori_loop(0, nb, body, 0)

    lane = lane_i.astype(F32)
    for h in range(MOBA_PAIR):
        q = q_ref[:, h * hd:(h + 1) * hd]
        km_hi, km_lo = _split_bf16(km_ref[:, h * hd:(h + 1) * hd])
        g = jnp.where(lane_i < b, _dot_nt(q, km_hi) + _dot_nt(q, km_lo), NEG)
        sel = lane_i == b
        for _ in range(MOBA_TOPK):
            mx = jnp.max(g, axis=-1, keepdims=True)
            idx = jnp.min(jnp.where(g == mx, lane, float(LANES)), axis=-1, keepdims=True)
            hit = lane == idx
            sel = sel | (hit & (mx > 0.5 * NEG))
            g = jnp.where(hit, 2.0 * NEG, g)
        qx_ref[h] = jnp.concatenate([q, jnp.where(sel, 0.0, NEG).astype(BF16)], axis=1)
        m_ref[h] = jnp.full((blk, 1), NEG, F32)
        acc_ref[h] = jnp.zeros((blk, 2 * hd), F32)

    ones = jnp.ones((gw, hd), BF16)

    def rows_of(g):
        return pl.ds(pl.multiple_of(g * gw, gw), gw)

    def scores(g, slot_ref):
        rows = rows_of(g)
        oh = oh_ref[rows, :]
        for h in range(MOBA_PAIR):
            k_ext = jnp.concatenate([k_ref[rows, h * hd:(h + 1) * hd], oh], axis=1)
            slot_ref[h] = _dot_nt(qx_ref[h], k_ext)

    def accumulate(g, slot_ref, causal):
        rows = rows_of(g)
        if causal:
            key_pos = g * gw + lax.broadcasted_iota(jnp.int32, (blk, gw), 1)
            qry_pos = b * blk + lax.broadcasted_iota(jnp.int32, (blk, gw), 0)
            keep = key_pos <= qry_pos
        for h in range(MOBA_PAIR):
            s = slot_ref[h]
            if causal:
                s = jnp.where(keep, s, NEG)
            m_old = m_ref[h]
            m_new = jnp.maximum(m_old, jnp.max(s, axis=-1, keepdims=True))
            p = jnp.exp2(s - m_new).astype(BF16)
            v_ext = jnp.concatenate([v_ref[rows, h * hd:(h + 1) * hd], ones], axis=1)
            acc_ref[h] = jnp.exp2(m_old - m_new) * acc_ref[h] + _dot(p, v_ext)
            m_ref[h] = m_new

    full_groups = b // MOBA_GROUP
    scores(0, s0_ref)

    def body(i, carry):
        g = 2 * i
        scores(g + 1, s1_ref)
        accumulate(g, s0_ref, False)
        scores(g + 2, s0_ref)
        accumulate(g + 1, s1_ref, False)
        return carry

    lax.fori_loop(0, full_groups // 2, body, 0)

    @pl.when(full_groups % 2 == 0)
    def _():
        accumulate(full_groups, s0_ref, True)

    @pl.when(full_groups % 2 == 1)
    def _():
        scores(full_groups, s1_ref)
        accumulate(full_groups - 1, s0_ref, False)
        accumulate(full_groups, s1_ref, True)

    for h in range(MOBA_PAIR):
        acc = acc_ref[h]
        o_ref[:, h * hd:(h + 1) * hd] = (acc[:, :hd] / acc[:, hd:]).astype(BF16)


def _moba(proj):
    s = proj.shape[0]
    nb = s // MOBA_BLOCK
    w = MOBA_PAIR * MOBA_HD
    assert s % (MOBA_GROUP * MOBA_BLOCK) == 0 and nb <= LANES and MOBA_HEADS % MOBA_PAIR == 0
    resident = dict(pipeline_mode=pl.Buffered(1))
    return pl.pallas_call(
        _moba_kernel,
        grid=(MOBA_HEADS // MOBA_PAIR, nb),
        in_specs=[pl.BlockSpec((MOBA_BLOCK, w), lambda h, b: (b, OFF_MQ // w + h)),
                  pl.BlockSpec((s, w), lambda h, b: (0, OFF_MK // w + h), **resident),
                  pl.BlockSpec((s, w), lambda h, b: (0, OFF_MV // w + h), **resident)],
        out_specs=pl.BlockSpec((MOBA_BLOCK, w), lambda h, b: (b, h)),
        out_shape=jax.ShapeDtypeStruct((s, MOBA_W), BF16),
        scratch_shapes=[pltpu.VMEM((LANES, w), F32),
                        pltpu.VMEM((s, LANES), BF16),
                        pltpu.VMEM((MOBA_PAIR, MOBA_BLOCK, 2 * MOBA_HD), BF16),
                        pltpu.VMEM((MOBA_PAIR, MOBA_BLOCK, 2 * MOBA_HD), F32),
                        pltpu.VMEM((MOBA_PAIR, MOBA_BLOCK, 1), F32),
                        pltpu.VMEM((MOBA_PAIR, MOBA_BLOCK, MOBA_GROUP * MOBA_BLOCK), F32),
                        pltpu.VMEM((MOBA_PAIR, MOBA_BLOCK, MOBA_GROUP * MOBA_BLOCK), F32)],
        compiler_params=_cparams("arbitrary", "arbitrary"),
        name="moba",
    )(proj, proj, proj)


def _moba_t_kernel(q_ref, k_ref, v_ref, o_ref, km_ref, oh_ref, vt_ref, qt_ref, acc_ref, m_ref, s0_ref, s1_ref):
    hp = pl.program_id(0)
    bq = pl.program_id(1)
    tq = q_ref.shape[0]
    hd = MOBA_HD
    blk = MOBA_BLOCK
    nb = k_ref.shape[0] // blk
    gw = MOBA_GROUP * blk
    vrows = vt_ref.shape[2]

    @pl.when((hp == 0) & (bq == 0))
    def _():
        lane_i = lax.broadcasted_iota(jnp.int32, (blk, LANES), 1)

        def body(n, carry):
            oh_ref[pl.ds(pl.multiple_of(n * blk, blk), blk), :] = jnp.where(lane_i == n, 1.0, 0.0).astype(BF16)
            return carry

        lax.fori_loop(0, nb, body, 0)

    @pl.when(bq == 0)
    def _():
        km_ref[...] = jnp.zeros_like(km_ref)

        def body(n, carry):
            rows = pl.ds(pl.multiple_of(n * blk, blk), blk)
            km_ref[pl.ds(n, 1), :] = jnp.mean(k_ref[rows, :].astype(F32), axis=0, keepdims=True)
            for h in range(MOBA_PAIR):
                vt_ref[h, n, 0:hd, :] = v_ref[rows, h * hd:(h + 1) * hd].astype(F32).T.astype(BF16)
                vt_ref[h, n, hd:vrows, :] = jnp.ones((vrows - hd, blk), BF16)
            return carry

        lax.fori_loop(0, nb, body, 0)

    gr = km_ref.shape[0]
    row_i = lax.broadcasted_iota(jnp.int32, (gr, tq), 0)
    row_f = row_i.astype(F32)
    own = bq * (tq // blk) + lax.broadcasted_iota(jnp.int32, (gr, tq), 1) // blk
    for h in range(MOBA_PAIR):
        qt = q_ref[:, h * hd:(h + 1) * hd].astype(F32).T.astype(BF16)
        km_hi, km_lo = _split_bf16(km_ref[:, h * hd:(h + 1) * hd])
        g = jnp.where(row_i < own, _dot(km_hi, qt) + _dot(km_lo, qt), NEG)
        bias = jnp.full((gr, tq), NEG, F32)
        for _ in range(MOBA_TOPK):
            mx = jnp.max(g, axis=0, keepdims=True)
            idx = jnp.min(jnp.where(g == mx, row_f, float(gr)), axis=0, keepdims=True)
            hit = row_f == idx
            bias = jnp.where(hit, jnp.where(mx > 0.5 * NEG, 0.0, NEG), bias)
            g = jnp.where(hit, 2.0 * NEG, g)
        bias = jnp.where(row_i == own, 0.0, bias)
        qt_ref[h] = jnp.concatenate(
            [qt, bias.astype(BF16), jnp.full((LANES - gr, tq), NEG, BF16)], axis=0)
        m_ref[h] = jnp.full((1, tq), NEG, F32)
        acc_ref[h] = jnp.zeros((vrows, tq), F32)

    def scores(g, slot_ref):
        rows = pl.ds(pl.multiple_of(g * gw, gw), gw)
        oh = oh_ref[rows, :]
        for h in range(MOBA_PAIR):
            k_ext = jnp.concatenate([k_ref[rows, h * hd:(h + 1) * hd], oh], axis=1)
            slot_ref[h] = _dot(k_ext, qt_ref[h])

    def accumulate(g, slot_ref, causal):
        if causal:
            own_rows = pl.ds(pl.multiple_of((bq * tq) % gw, tq), tq)
            keep = lax.broadcasted_iota(jnp.int32, (tq, tq), 0) <= lax.broadcasted_iota(jnp.int32, (tq, tq), 1)
            for h in range(MOBA_PAIR):
                slot_ref[h, own_rows, :] = jnp.where(keep, slot_ref[h, own_rows, :], NEG)
        for h in range(MOBA_PAIR):
            s = slot_ref[h]
            m_old = m_ref[h]
            m_new = jnp.maximum(m_old, jnp.max(s, axis=0, keepdims=True))
            p = jnp.exp2(s - m_new).astype(BF16)
            pv = _dot(vt_ref[h, g * MOBA_GROUP], p[0:blk, :])
            for t in range(1, MOBA_GROUP):
                pv = pv + _dot(vt_ref[h, g * MOBA_GROUP + t], p[t * blk:(t + 1) * blk, :])
            acc_ref[h] = jnp.exp2(m_old - m_new) * acc_ref[h] + pv
            m_ref[h] = m_new

    full_groups = (bq * (tq // blk)) // MOBA_GROUP
    scores(0, s0_ref)

    def body(i, carry):
        g = 2 * i
        scores(g + 1, s1_ref)
        accumulate(g, s0_ref, False)
        scores(g + 2, s0_ref)
        accumulate(g + 1, s1_ref, False)
        return carry

    lax.fori_loop(0, full_groups // 2, body, 0)

    @pl.when(full_groups % 2 == 0)
    def _():
        accumulate(full_groups, s0_ref, True)

    @pl.when(full_groups % 2 == 1)
    def _():
        scores(full_groups, s1_ref)
        accumulate(full_groups - 1, s0_ref, False)
        accumulate(full_groups, s1_ref, True)

    for h in range(MOBA_PAIR):
        acc = acc_ref[h]
        o_ref[:, h * hd:(h + 1) * hd] = (acc[0:hd, :] / acc[hd:hd + 1, :]).T.astype(BF16)


def _moba_t(proj, tq=2 * MOBA_BLOCK):
    s = proj.shape[0]
    nb = s // MOBA_BLOCK
    w = MOBA_PAIR * MOBA_HD
    gw = MOBA_GROUP * MOBA_BLOCK
    bf16_rows = 2 * SUBLANES
    vrows = MOBA_HD + bf16_rows
    gate_rows = -(-nb // bf16_rows) * bf16_rows
    assert s % gw == 0 and gate_rows <= LANES and MOBA_HEADS % MOBA_PAIR == 0
    assert tq % MOBA_BLOCK == 0 and gw % tq == 0
    resident = dict(pipeline_mode=pl.Buffered(1))
    return pl.pallas_call(
        _moba_t_kernel,
        grid=(MOBA_HEADS // MOBA_PAIR, s // tq),
        in_specs=[pl.BlockSpec((tq, w), lambda h, b: (b, OFF_MQ // w + h)),
                  pl.BlockSpec((s, w), lambda h, b: (0, OFF_MK // w + h), **resident),
                  pl.BlockSpec((s, w), lambda h, b: (0, OFF_MV // w + h), **resident)],
        out_specs=pl.BlockSpec((tq, w), lambda h, b: (b, h)),
        out_shape=jax.ShapeDtypeStruct((s, MOBA_W), BF16),
        scratch_shapes=[pltpu.VMEM((gate_rows, w), F32),
                        pltpu.VMEM((s, LANES), BF16),
                        pltpu.VMEM((MOBA_PAIR, nb, vrows, MOBA_BLOCK), BF16),
                        pltpu.VMEM((MOBA_PAIR, 2 * MOBA_HD, tq), BF16),
                        pltpu.VMEM((MOBA_PAIR, vrows, tq), F32),
                        pltpu.VMEM((MOBA_PAIR, 1, tq), F32),
                        pltpu.VMEM((MOBA_PAIR, gw, tq), F32),
                        pltpu.VMEM((MOBA_PAIR, gw, tq), F32)],
        compiler_params=_cparams("arbitrary", "arbitrary"),
        name="moba",
    )(proj, proj, proj)


def _merge_kernel(or_ref, om_ref, ga_ref, gb_ref, wr_ref, wm_ref, o_ref):
    yr = _dot(or_ref[...], wr_ref[...])
    ym = _dot(om_ref[...], wm_ref[...])
    ga = jax.nn.sigmoid(ga_ref[...].astype(F32))
    gb = jax.nn.sigmoid(gb_ref[...].astype(F32))
    o_ref[...] = (ga * yr + gb * ym).astype(BF16)


def _merge(o_ret, o_moba, proj, w_r, w_m, tm=1024, tn=512):
    s = o_ret.shape[0]
    tm = min(tm, s)
    d = D_MODEL
    return pl.pallas_call(
        _merge_kernel,
        grid=(s // tm, d // tn),
        in_specs=[pl.BlockSpec((tm, RET_V_W), lambda i, j: (i, 0)),
                  pl.BlockSpec((tm, MOBA_W), lambda i, j: (i, 0)),
                  pl.BlockSpec((tm, tn), lambda i, j: (i, OFF_GA // tn + j)),
                  pl.BlockSpec((tm, tn), lambda i, j: (i, OFF_GB // tn + j)),
                  pl.BlockSpec((RET_V_W, tn), lambda i, j: (0, j)),
                  pl.BlockSpec((MOBA_W, tn), lambda i, j: (0, j))],
        out_specs=pl.BlockSpec((tm, tn), lambda i, j: (i, j)),
        out_shape=jax.ShapeDtypeStruct((s, d), BF16),
        compiler_params=_cparams("arbitrary", "arbitrary"),
        name="merge",
    )(o_ret, o_moba, proj, proj, w_r, w_m)


def _mixout_kernel(m_ref, wo_ref, x_ref, npost_ref, g1_ref, npre_ref, sc_ref, sh_ref, x1_ref, u2_ref):
    y = _dot(m_ref[...], wo_ref[...])
    x1 = x_ref[...] + (1.0 + g1_ref[...]) * (_rms(y) * npost_ref[...])
    x1_ref[...] = x1
    u2_ref[...] = ((_rms(x1) * npre_ref[...]) * (1.0 + sc_ref[...]) + sh_ref[...]).astype(BF16)


def _mixout(merged, w_o, x, npost, g1, npre, sc2, sh2, tm=256):
    s, d = x.shape
    tm = min(tm, s)
    vec = pl.BlockSpec((1, d), lambda i: (0, 0))
    rows = pl.BlockSpec((tm, d), lambda i: (i, 0))
    return pl.pallas_call(
        _mixout_kernel,
        grid=(s // tm,),
        in_specs=[rows, pl.BlockSpec((d, d), lambda i: (0, 0)), rows, vec, vec, vec, vec, vec],
        out_specs=[rows, rows],
        out_shape=[jax.ShapeDtypeStruct((s, d), F32), jax.ShapeDtypeStruct((s, d), BF16)],
        compiler_params=_cparams("arbitrary"),
        name="mixout",
    )(merged, w_o, x, npost, g1, npre, sc2, sh2)


def _ffn_kernel(u_ref, wa_ref, wb_ref, cwa_ref, cwb_ref, cba_ref, cbb_ref, wd_ref, x1_ref, npost_ref, g2_ref,
                o_ref, ha_ref, hb_ref, act_ref, tail_ref):
    i = pl.program_id(0)
    j = pl.program_id(1)
    tm = u_ref.shape[0]
    tn = wa_ref.shape[1]
    halo = SUBLANES

    @pl.when((i == 0) & (j == 0))
    def _():
        tail_ref[...] = jnp.zeros_like(tail_ref)

    @pl.when(j == 0)
    def _():
        o_ref[...] = jnp.zeros_like(o_ref)

    ha_ref[0:halo, :] = tail_ref[j, 0]
    hb_ref[0:halo, :] = tail_ref[j, 1]
    for c in range(tn // FFN_CHUNK):
        cs = slice(c * FFN_CHUNK, (c + 1) * FFN_CHUNK)
        ha_ref[halo:halo + tm, cs] = _dot(u_ref[...], wa_ref[:, cs])
        hb_ref[halo:halo + tm, cs] = _dot(u_ref[...], wb_ref[:, cs])
    tail_ref[j, 0] = ha_ref[tm:tm + halo, :]
    tail_ref[j, 1] = hb_ref[tm:tm + halo, :]

    def conv(hbuf, cw, cb, r0, cs):
        out = cb
        for t in range(CONV_WIDTH):
            out = out + hbuf[halo + r0 - t:halo + r0 - t + FFN_ROWS, cs] * cw[CONV_WIDTH - 1 - t:CONV_WIDTH - t, :]
        return out

    for c in range(tn // FFN_CHUNK):
        cs = slice(c * FFN_CHUNK, (c + 1) * FFN_CHUNK)
        cwa, cwb, cba, cbb = cwa_ref[:, cs], cwb_ref[:, cs], cba_ref[:, cs], cbb_ref[:, cs]
        for r0 in range(0, tm, FFN_ROWS):
            a = conv(ha_ref, cwa, cba, r0, cs)
            bgate = conv(hb_ref, cwb, cbb, r0, cs)
            act_ref[r0:r0 + FFN_ROWS, cs] = (_silu(a) * bgate).astype(BF16)
        for n0 in range(0, o_ref.shape[1], FFN_DOWN_COLS):
            ns = slice(n0, n0 + FFN_DOWN_COLS)
            o_ref[:, ns] += _dot(act_ref[:, cs], wd_ref[cs, ns])

    @pl.when(j == pl.num_programs(1) - 1)
    def _():
        def piece(rows):
            y = o_ref[rows, :]
            o_ref[rows, :] = x1_ref[rows, :] + (1.0 + g2_ref[...]) * (_rms(y) * npost_ref[...])

        _for_row_chunks(tm, piece)


def _ffn(u2, w_up, conv_w, conv_b, w_down, x1, npost, g2, tm=1024, tn=512):
    s, d = x1.shape
    tm = min(tm, s)
    nj = D_FF // tn
    assert D_FF % tn == 0
    vec = pl.BlockSpec((1, d), lambda i, j: (0, 0))
    rows = pl.BlockSpec((tm, d), lambda i, j: (i, 0))
    resid = pl.BlockSpec((tm, d), lambda i, j: (i, 0), pipeline_mode=pl.Buffered(1))
    return pl.pallas_call(
        _ffn_kernel,
        grid=(s // tm, nj),
        in_specs=[rows,
                  pl.BlockSpec((d, tn), lambda i, j: (0, j)),
                  pl.BlockSpec((d, tn), lambda i, j: (0, j + nj)),
                  pl.BlockSpec((CONV_WIDTH, tn), lambda i, j: (0, j)),
                  pl.BlockSpec((CONV_WIDTH, tn), lambda i, j: (0, j + nj)),
                  pl.BlockSpec((1, tn), lambda i, j: (0, j)),
                  pl.BlockSpec((1, tn), lambda i, j: (0, j + nj)),
                  pl.BlockSpec((tn, d), lambda i, j: (j, 0)),
                  resid, vec, vec],
        out_specs=rows,
        out_shape=jax.ShapeDtypeStruct((s, d), F32),
        scratch_shapes=[pltpu.VMEM((tm + SUBLANES, tn), F32),
                        pltpu.VMEM((tm + SUBLANES, tn), F32),
                        pltpu.VMEM((tm, tn), BF16),
                        pltpu.VMEM((nj, 2, SUBLANES, tn), F32)],
        compiler_params=_cparams("arbitrary", "arbitrary"),
        name="ffn",
    )(u2, w_up, w_up, conv_w, conv_w, conv_b, conv_b, w_down, x1, npost, g2)


def _block(x, c, ada_w, ada_b, mix_norm_pre, mix_norm_post, w_in, ret_w_o, moba_w_o, w_out,
           ffn_norm_pre, ffn_norm_post, ffn_w_up, ffn_conv_w, ffn_conv_b, ffn_w_down, cos2, sin2):
    d = x.shape[1]
    mod = _ada(c, ada_w, ada_b[None, :])
    sh1, sc1, g1, sh2, sc2, g2 = [mod[:, t * d:(t + 1) * d] for t in range(6)]
    proj = _inproj(x, mix_norm_pre[None, :], sc1, sh1, cos2, sin2, w_in.astype(BF16))
    o_ret = _retention(proj)
    o_moba = _moba_t(proj)
    merged = _merge(o_ret, o_moba, proj, ret_w_o.astype(BF16), moba_w_o.astype(BF16))
    x1, u2 = _mixout(merged, w_out.astype(BF16), x, mix_norm_post[None, :], g1,
                     ffn_norm_pre[None, :], sc2, sh2)
    return _ffn(u2, ffn_w_up.astype(BF16), ffn_conv_w, ffn_conv_b[None, :], ffn_w_down.astype(BF16),
                x1, ffn_norm_post[None, :], g2)


def kernel(x, c, ada_w, ada_b, mix_norm_pre, mix_norm_post, w_in, ret_w_o, moba_w_o, w_out,
           ffn_norm_pre, ffn_norm_post, ffn_w_up, ffn_conv_w, ffn_conv_b, ffn_w_down):
    batch, s, _ = x.shape
    depth = ada_w.shape[0]
    cos2, sin2 = _rope_tables(s)
    outs = []
    for bi in range(batch):
        xb = x[bi]
        for l in range(depth):
            xb = _block(xb, c[bi:bi + 1], ada_w[l], ada_b[l], mix_norm_pre[l], mix_norm_post[l], w_in[l],
                        ret_w_o[l], moba_w_o[l], w_out[l], ffn_norm_pre[l], ffn_norm_post[l], ffn_w_up[l],
                        ffn_conv_w[l], ffn_conv_b[l], ffn_w_down[l], cos2, sin2)
        outs.append(xb)
    return jnp.stack(outs)
```

```python
import functools
import math

import jax
import jax.numpy as jnp
from jax import lax
from jax.experimental import pallas as pl
from jax.experimental.pallas import tpu as pltpu

D_MODEL = 2048
RET_HEADS = 8
RET_DK = 128
RET_DV = 256
ROPE_BASE = 10000.0
MOBA_HEADS = 16
MOBA_HD = 128
MOBA_BLOCK = 256
MOBA_TOPK = 3
D_FF = 5632
CONV_WIDTH = 3
NORM_EPS = 1e-6
GN_EPS = 1e-5

RET_QK_W = RET_HEADS * RET_DK
RET_V_W = RET_HEADS * RET_DV
MOBA_W = MOBA_HEADS * MOBA_HD
OFF_RQ = 0
OFF_RK = OFF_RQ + RET_QK_W
OFF_RV = OFF_RK + RET_QK_W
OFF_RG = OFF_RV + RET_V_W
OFF_MQ = OFF_RG + RET_V_W
OFF_MK = OFF_MQ + MOBA_W
OFF_MV = OFF_MK + MOBA_W
OFF_GA = OFF_MV + MOBA_W
OFF_GB = OFF_GA + D_MODEL
IN_WIDTH = OFF_GB + D_MODEL

LANES = 128
SUBLANES = 8
NEG = -1e30
MOBA_QSCALE = math.log2(math.e) / math.sqrt(MOBA_HD)
RET_GROUP = 4
MOBA_GROUP = 4
MOBA_PAIR = 2
FFN_CHUNK = 256
FFN_ROWS = 64
FFN_DOWN_COLS = 512
MIX_ROWS = 256
NORM_ROWS = 32
NORM_UNROLL = 4
VMEM_LIMIT = 56 * 1024 * 1024

F32 = jnp.float32
BF16 = jnp.bfloat16


def _cparams(*sem):
    return pltpu.CompilerParams(dimension_semantics=sem, vmem_limit_bytes=VMEM_LIMIT)


def _dot(a, b):
    return jnp.dot(a, b, preferred_element_type=F32)


def _dot_nt(a, b):
    return lax.dot_general(a, b, (((1,), (1,)), ((), ())), preferred_element_type=F32)


def _split_bf16(x):
    hi = x.astype(BF16)
    lo = (x - hi.astype(F32)).astype(BF16)
    return hi, lo


def _rms(x):
    return x * lax.rsqrt(jnp.mean(x * x, axis=-1, keepdims=True) + NORM_EPS)


def _silu(x):
    return x * jax.nn.sigmoid(x)


def _for_row_chunks(n_rows, fn):
    step = NORM_ROWS * NORM_UNROLL

    def body(r, carry):
        for t in range(NORM_UNROLL):
            fn(pl.ds(pl.multiple_of(r * step + t * NORM_ROWS, NORM_ROWS), NORM_ROWS))
        return carry

    lax.fori_loop(0, n_rows // step, body, 0)


def _ada_kernel(c_ref, w_ref, b_ref, o_ref):
    c = jnp.broadcast_to(_silu(c_ref[...]), (SUBLANES, c_ref.shape[1]))
    c_hi, c_lo = _split_bf16(c)
    w_hi, w_lo = _split_bf16(w_ref[...])
    r = _dot(c_hi, w_hi) + _dot(c_lo, w_hi) + _dot(c_hi, w_lo)
    o_ref[...] = r[0:1] + b_ref[...]


def _ada(c, w, b, tn=1024):
    d, n = w.shape
    return pl.pallas_call(
        _ada_kernel,
        grid=(n // tn,),
        in_specs=[pl.BlockSpec((1, d), lambda j: (0, 0)),
                  pl.BlockSpec((d, tn), lambda j: (0, j)),
                  pl.BlockSpec((1, tn), lambda j: (0, j))],
        out_specs=pl.BlockSpec((1, tn), lambda j: (0, j)),
        out_shape=jax.ShapeDtypeStruct((1, n), F32),
        compiler_params=_cparams("arbitrary"),
        name="ada",
    )(c, w, b)


def _rope_kernel(inv_ref, cos_ref, sin_ref):
    ts = cos_ref.shape[0]
    row = lax.broadcasted_iota(jnp.int32, (ts, LANES), 0) + pl.program_id(0) * ts
    lane = lax.broadcasted_iota(jnp.int32, (ts, LANES), 1)
    ang = row.astype(F32) * inv_ref[...]
    sin = jnp.sin(ang)
    cos_ref[...] = jnp.cos(ang)
    sin_ref[...] = jnp.where(lane < RET_DK // 2, -sin, sin)


def _rope_tables(s, ts=2048):
    ts = min(ts, s)
    half = RET_DK // 2
    inv = ROPE_BASE ** (-jnp.arange(half, dtype=F32) / half)
    inv2 = jnp.concatenate([inv, inv])[None, :]
    return pl.pallas_call(
        _rope_kernel,
        grid=(s // ts,),
        in_specs=[pl.BlockSpec((1, LANES), lambda i: (0, 0))],
        out_specs=[pl.BlockSpec((ts, LANES), lambda i: (i, 0))] * 2,
        out_shape=[jax.ShapeDtypeStruct((s, LANES), F32)] * 2,
        compiler_params=_cparams("arbitrary"),
        name="rope",
    )(inv2)


def _inproj_kernel(x_ref, nw_ref, sc_ref, sh_ref, cos_ref, sin_ref, w_ref, o_ref, u_ref):
    j = pl.program_id(1)
    tn = o_ref.shape[1]

    @pl.when(j == 0)
    def _():
        def piece(rows):
            y = _rms(x_ref[rows, :]) * nw_ref[...]
            u_ref[rows, :] = (y * (1.0 + sc_ref[...]) + sh_ref[...]).astype(BF16)

        _for_row_chunks(x_ref.shape[0], piece)

    r = _dot(u_ref[...], w_ref[...])
    q_tiles = RET_QK_W // tn
    is_mq = (j >= OFF_MQ // tn) & (j < OFF_MK // tn)
    o_ref[...] = (r * jnp.where(is_mq, MOBA_QSCALE, 1.0).astype(F32)).astype(BF16)

    @pl.when(j < 2 * q_tiles)
    def _():
        scale = jnp.where(j >= q_tiles, RET_DK ** -0.5, 1.0).astype(F32)
        cos = cos_ref[...]
        sin = sin_ref[...]
        for h in range(tn // RET_DK):
            xh = r[:, h * RET_DK:(h + 1) * RET_DK]
            rot = xh * cos + pltpu.roll(xh, RET_DK // 2, axis=1) * sin
            o_ref[:, h * RET_DK:(h + 1) * RET_DK] = (rot * scale).astype(BF16)


def _inproj(x, nw, sc, sh, cos2, sin2, w, tm=1024, tn=1024):
    s, d = x.shape
    n = w.shape[1]
    tm = min(tm, s)
    vec = pl.BlockSpec((1, d), lambda i, j: (0, 0))
    tab = pl.BlockSpec((tm, LANES), lambda i, j: (i, 0))
    return pl.pallas_call(
        _inproj_kernel,
        grid=(s // tm, n // tn),
        in_specs=[pl.BlockSpec((tm, d), lambda i, j: (i, 0)), vec, vec, vec, tab, tab,
                  pl.BlockSpec((d, tn), lambda i, j: (0, j))],
        out_specs=pl.BlockSpec((tm, tn), lambda i, j: (i, j)),
        out_shape=jax.ShapeDtypeStruct((s, n), BF16),
        scratch_shapes=[pltpu.VMEM((tm, d), BF16)],
        compiler_params=_cparams("arbitrary", "arbitrary"),
        name="inproj",
    )(x, nw, sc, sh, cos2, sin2, w)


def _ret_kernel(lg_ref, q_ref, k_ref, v_ref, g_ref, o_ref, r_ref, dec_ref, xi_ref, zeta_ref):
    n = pl.program_id(1)
    c = q_ref.shape[0]
    dk, dv = RET_DK, RET_DV

    @pl.when(n == 0)
    def _():
        r_ref[...] = jnp.zeros_like(r_ref)
        d = lax.broadcasted_iota(jnp.int32, (c, c), 0) - lax.broadcasted_iota(jnp.int32, (c, c), 1)
        row = lax.broadcasted_iota(jnp.int32, (c, dk), 0)
        for h in range(RET_GROUP):
            lg = lg_ref[h, 0:1, 0:1]
            dec_ref[h] = jnp.where(d >= 0, jnp.exp(jnp.maximum(d, 0).astype(F32) * lg), 0.0)
            xi_ref[h] = jnp.exp((row + 1).astype(F32) * lg)
            zeta_ref[h] = jnp.exp((c - 1 - row).astype(F32) * lg)

    for h in range(RET_GROUP):
        q = q_ref[:, h * dk:(h + 1) * dk]
        k = k_ref[:, h * dk:(h + 1) * dk]
        v = v_ref[:, h * dv:(h + 1) * dv]
        s = _dot_nt(q, k) * dec_ref[h]
        inner = _dot(s.astype(BF16), v)
        r = r_ref[h]
        cross = _dot((q.astype(F32) * xi_ref[h]).astype(BF16), r.astype(BF16))
        o = inner + cross
        kz = (k.astype(F32) * zeta_ref[h]).T.astype(BF16)
        r_ref[h] = r * jnp.exp(lg_ref[h, 0:1, 0:1] * c) + _dot(kz, v)
        oc = o - jnp.mean(o, axis=-1, keepdims=True)
        on = oc * lax.rsqrt(jnp.mean(oc * oc, axis=-1, keepdims=True) + GN_EPS)
        o_ref[:, h * dv:(h + 1) * dv] = (_silu(g_ref[:, h * dv:(h + 1) * dv].astype(F32)) * on).astype(BF16)


def _retention(proj, chunk=512):
    s = proj.shape[0]
    chunk = min(chunk, s)
    log_g = jnp.log1p(-jnp.exp2(-5.0 - jnp.arange(RET_HEADS, dtype=F32)))
    lg = jnp.broadcast_to(log_g[:, None, None], (RET_HEADS, SUBLANES, LANES))
    qk_w, v_w = RET_GROUP * RET_DK, RET_GROUP * RET_DV
    assert RET_HEADS % RET_GROUP == 0
    return pl.pallas_call(
        _ret_kernel,
        grid=(RET_HEADS // RET_GROUP, s // chunk),
        in_specs=[pl.BlockSpec((RET_GROUP, SUBLANES, LANES), lambda h, n: (h, 0, 0)),
                  pl.BlockSpec((chunk, qk_w), lambda h, n: (n, OFF_RQ // qk_w + h)),
                  pl.BlockSpec((chunk, qk_w), lambda h, n: (n, OFF_RK // qk_w + h)),
                  pl.BlockSpec((chunk, v_w), lambda h, n: (n, OFF_RV // v_w + h)),
                  pl.BlockSpec((chunk, v_w), lambda h, n: (n, OFF_RG // v_w + h))],
        out_specs=pl.BlockSpec((chunk, v_w), lambda h, n: (n, h)),
        out_shape=jax.ShapeDtypeStruct((s, RET_V_W), BF16),
        scratch_shapes=[pltpu.VMEM((RET_GROUP, RET_DK, RET_DV), F32),
                        pltpu.VMEM((RET_GROUP, chunk, chunk), F32),
                        pltpu.VMEM((RET_GROUP, chunk, RET_DK), F32),
                        pltpu.VMEM((RET_GROUP, chunk, RET_DK), F32)],
        compiler_params=_cparams("arbitrary", "arbitrary"),
        name="ret",
    )(lg, proj, proj, proj, proj)


def _moba_kernel(q_ref, k_ref, v_ref, o_ref, km_ref, oh_ref, qx_ref, acc_ref, m_ref, s0_ref, s1_ref):
    hp = pl.program_id(0)
    b = pl.program_id(1)
    blk = q_ref.shape[0]
    hd = MOBA_HD
    nb = k_ref.shape[0] // blk
    gw = MOBA_GROUP * blk
    lane_i = lax.broadcasted_iota(jnp.int32, (blk, LANES), 1)

    @pl.when((hp == 0) & (b == 0))
    def _():
        def body(n, carry):
            oh_ref[pl.ds(pl.multiple_of(n * blk, blk), blk), :] = jnp.where(lane_i == n, 1.0, 0.0).astype(BF16)
            return carry

        lax.fori_loop(0, nb, body, 0)

    @pl.when(b == 0)
    def _():
        km_ref[...] = jnp.zeros_like(km_ref)

        def body(n, carry):
            kb = k_ref[pl.ds(pl.multiple_of(n * blk, blk), blk), :].astype(F32)
            km_ref[pl.ds(n, 1), :] = jnp.mean(kb, axis=0, keepdims=True)
            return carry

        lax.fori_loop(0, nb, body, 0)

    lane = lane_i.astype(F32)
    for h in range(MOBA_PAIR):
        q = q_ref[:, h * hd:(h + 1) * hd]
        km_hi, km_lo = _split_bf16(km_ref[:, h * hd:(h + 1) * hd])
        g = jnp.where(lane_i < b, _dot_nt(q, km_hi) + _dot_nt(q, km_lo), NEG)
        sel = lane_i == b
        for _ in range(MOBA_TOPK):
            mx = jnp.max(g, axis=-1, keepdims=True)
            idx = jnp.min(jnp.where(g == mx, lane, float(LANES)), axis=-1, keepdims=True)
            hit = lane == idx
            sel = sel | (hit & (mx > 0.5 * NEG))
            g = jnp.where(hit, 2.0 * NEG, g)
        qx_ref[h] = jnp.concatenate([q, jnp.where(sel, 0.0, NEG).astype(BF16)], axis=1)
        m_ref[h] = jnp.full((blk, 1), NEG, F32)
        acc_ref[h] = jnp.zeros((blk, 2 * hd), F32)

    ones = jnp.ones((gw, hd), BF16)

    def rows_of(g):
        return pl.ds(pl.multiple_of(g * gw, gw), gw)

    def scores(g, slot_ref):
        rows = rows_of(g)
        oh = oh_ref[rows, :]
        for h in range(MOBA_PAIR):
            k_ext = jnp.concatenate([k_ref[rows, h * hd:(h + 1) * hd], oh], axis=1)
            slot_ref[h] = _dot_nt(qx_ref[h], k_ext)

    def accumulate(g, slot_ref, causal):
        rows = rows_of(g)
        if causal:
            key_pos = g * gw + lax.broadcasted_iota(jnp.int32, (blk, gw), 1)
            qry_pos = b * blk + lax.broadcasted_iota(jnp.int32, (blk, gw), 0)
            keep = key_pos <= qry_pos
        for h in range(MOBA_PAIR):
            s = slot_ref[h]
            if causal:
                s = jnp.where(keep, s, NEG)
            m_old = m_ref[h]
            m_new = jnp.maximum(m_old, jnp.max(s, axis=-1, keepdims=True))
            p = jnp.exp2(s - m_new).astype(BF16)
            v_ext = jnp.concatenate([v_ref[rows, h * hd:(h + 1) * hd], ones], axis=1)
            acc_ref[h] = jnp.exp2(m_old - m_new) * acc_ref[h] + _dot(p, v_ext)
            m_ref[h] = m_new

    full_groups = b // MOBA_GROUP
    scores(0, s0_ref)

    def body(i, carry):
        g = 2 * i
        scores(g + 1, s1_ref)
        accumulate(g, s0_ref, False)
        scores(g + 2, s0_ref)
        accumulate(g + 1, s1_ref, False)
        return carry

    lax.fori_loop(0, full_groups // 2, body, 0)

    @pl.when(full_groups % 2 == 0)
    def _():
        accumulate(full_groups, s0_ref, True)

    @pl.when(full_groups % 2 == 1)
    def _():
        scores(full_groups, s1_ref)
        accumulate(full_groups - 1, s0_ref, False)
        accumulate(full_groups, s1_ref, True)

    for h in range(MOBA_PAIR):
        acc = acc_ref[h]
        o_ref[:, h * hd:(h + 1) * hd] = (acc[:, :hd] / acc[:, hd:]).astype(BF16)


def _moba(proj):
    s = proj.shape[0]
    nb = s // MOBA_BLOCK
    w = MOBA_PAIR * MOBA_HD
    assert s % (MOBA_GROUP * MOBA_BLOCK) == 0 and nb <= LANES and MOBA_HEADS % MOBA_PAIR == 0
    resident = dict(pipeline_mode=pl.Buffered(1))
    return pl.pallas_call(
        _moba_kernel,
        grid=(MOBA_HEADS // MOBA_PAIR, nb),
        in_specs=[pl.BlockSpec((MOBA_BLOCK, w), lambda h, b: (b, OFF_MQ // w + h)),
                  pl.BlockSpec((s, w), lambda h, b: (0, OFF_MK // w + h), **resident),
                  pl.BlockSpec((s, w), lambda h, b: (0, OFF_MV // w + h), **resident)],
        out_specs=pl.BlockSpec((MOBA_BLOCK, w), lambda h, b: (b, h)),
        out_shape=jax.ShapeDtypeStruct((s, MOBA_W), BF16),
        scratch_shapes=[pltpu.VMEM((LANES, w), F32),
                        pltpu.VMEM((s, LANES), BF16),
                        pltpu.VMEM((MOBA_PAIR, MOBA_BLOCK, 2 * MOBA_HD), BF16),
                        pltpu.VMEM((MOBA_PAIR, MOBA_BLOCK, 2 * MOBA_HD), F32),
                        pltpu.VMEM((MOBA_PAIR, MOBA_BLOCK, 1), F32),
                        pltpu.VMEM((MOBA_PAIR, MOBA_BLOCK, MOBA_GROUP * MOBA_BLOCK), F32),
                        pltpu.VMEM((MOBA_PAIR, MOBA_BLOCK, MOBA_GROUP * MOBA_BLOCK), F32)],
        compiler_params=_cparams("arbitrary", "arbitrary"),
        name="moba",
    )(proj, proj, proj)


def _moba_t_kernel(q_ref, k_ref, v_ref, o_ref, km_ref, oh_ref, vt_ref, qt_ref, acc_ref, m_ref, s0_ref, s1_ref):
    hp = pl.program_id(0)
    bq = pl.program_id(1)
    tq = q_ref.shape[0]
    hd = MOBA_HD
    blk = MOBA_BLOCK
    nb = k_ref.shape[0] // blk
    gw = MOBA_GROUP * blk
    vrows = vt_ref.shape[2]

    @pl.when((hp == 0) & (bq == 0))
    def _():
        lane_i = lax.broadcasted_iota(jnp.int32, (blk, LANES), 1)

        def body(n, carry):
            oh_ref[pl.ds(pl.multiple_of(n * blk, blk), blk), :] = jnp.where(lane_i == n, 1.0, 0.0).astype(BF16)
            return carry

        lax.fori_loop(0, nb, body, 0)

    @pl.when(bq == 0)
    def _():
        km_ref[...] = jnp.zeros_like(km_ref)

        def body(n, carry):
            rows = pl.ds(pl.multiple_of(n * blk, blk), blk)
            km_ref[pl.ds(n, 1), :] = jnp.mean(k_ref[rows, :].astype(F32), axis=0, keepdims=True)
            for h in range(MOBA_PAIR):
                vt_ref[h, n, 0:hd, :] = v_ref[rows, h * hd:(h + 1) * hd].astype(F32).T.astype(BF16)
                vt_ref[h, n, hd:vrows, :] = jnp.ones((vrows - hd, blk), BF16)
            return carry

        lax.fori_loop(0, nb, body, 0)

    gr = km_ref.shape[0]
    row_i = lax.broadcasted_iota(jnp.int32, (gr, tq), 0)
    row_f = row_i.astype(F32)
    own = bq * (tq // blk) + lax.broadcasted_iota(jnp.int32, (gr, tq), 1) // blk
    for h in range(MOBA_PAIR):
        qt = q_ref[:, h * hd:(h + 1) * hd].astype(F32).T.astype(BF16)
        km_hi, km_lo = _split_bf16(km_ref[:, h * hd:(h + 1) * hd])
        g = jnp.where(row_i < own, _dot(km_hi, qt) + _dot(km_lo, qt), NEG)
        bias = jnp.full((gr, tq), NEG, F32)
        for _ in range(MOBA_TOPK):
            mx = jnp.max(g, axis=0, keepdims=True)
            idx = jnp.min(jnp.where(g == mx, row_f, float(gr)), axis=0, keepdims=True)
            hit = row_f == idx
            bias = jnp.where(hit, jnp.where(mx > 0.5 * NEG, 0.0, NEG), bias)
            g = jnp.where(hit, 2.0 * NEG, g)
        bias = jnp.where(row_i == own, 0.0, bias)
        qt_ref[h] = jnp.concatenate(
            [qt, bias.astype(BF16), jnp.full((LANES - gr, tq), NEG, BF16)], axis=0)
        m_ref[h] = jnp.full((1, tq), NEG, F32)
        acc_ref[h] = jnp.zeros((vrows, tq), F32)

    def scores(g, slot_ref):
        rows = pl.ds(pl.multiple_of(g * gw, gw), gw)
        oh = oh_ref[rows, :]
        for h in range(MOBA_PAIR):
            k_ext = jnp.concatenate([k_ref[rows, h * hd:(h + 1) * hd], oh], axis=1)
            slot_ref[h] = _dot(k_ext, qt_ref[h])

    def accumulate(g, slot_ref, causal):
        if causal:
            own_rows = pl.ds(pl.multiple_of((bq * tq) % gw, tq), tq)
            keep = lax.broadcasted_iota(jnp.int32, (tq, tq), 0) <= lax.broadcasted_iota(jnp.int32, (tq, tq), 1)
            for h in range(MOBA_PAIR):
                slot_ref[h, own_rows, :] = jnp.where(keep, slot_ref[h, own_rows, :], NEG)
        for h in range(MOBA_PAIR):
            s = slot_ref[h]
            m_old = m_ref[h]
            m_new = jnp.maximum(m_old, jnp.max(s, axis=0, keepdims=True))
            p = jnp.exp2(s - m_new).astype(BF16)
            pv = _dot(vt_ref[h, g * MOBA_GROUP], p[0:blk, :])
            for t in range(1, MOBA_GROUP):
                pv = pv + _dot(vt_ref[h, g * MOBA_GROUP + t], p[t * blk:(t + 1) * blk, :])
            acc_ref[h] = jnp.exp2(m_old - m_new) * acc_ref[h] + pv
            m_ref[h] = m_new

    full_groups = (bq * (tq // blk)) // MOBA_GROUP
    scores(0, s0_ref)

    def body(i, carry):
        g = 2 * i
        scores(g + 1, s1_ref)
        accumulate(g, s0_ref, False)
        scores(g + 2, s0_ref)
        accumulate(g + 1, s1_ref, False)
        return carry

    lax.fori_loop(0, full_groups // 2, body, 0)

    @pl.when(full_groups % 2 == 0)
    def _():
        accumulate(full_groups, s0_ref, True)

    @pl.when(full_groups % 2 == 1)
    def _():
        scores(full_groups, s1_ref)
        accumulate(full_groups - 1, s0_ref, False)
        accumulate(full_groups, s1_ref, True)

    for h in range(MOBA_PAIR):
        acc = acc_ref[h]
        o_ref[:, h * hd:(h + 1) * hd] = (acc[0:hd, :] / acc[hd:hd + 1, :]).T.astype(BF16)


def _moba_t(proj, tq=2 * MOBA_BLOCK):
    s = proj.shape[0]
    nb = s // MOBA_BLOCK
    w = MOBA_PAIR * MOBA_HD
    gw = MOBA_GROUP * MOBA_BLOCK
    bf16_rows = 2 * SUBLANES
    vrows = MOBA_HD + bf16_rows
    gate_rows = -(-nb // bf16_rows) * bf16_rows
    assert s % gw == 0 and gate_rows <= LANES and MOBA_HEADS % MOBA_PAIR == 0
    assert tq % MOBA_BLOCK == 0 and gw % tq == 0
    resident = dict(pipeline_mode=pl.Buffered(1))
    return pl.pallas_call(
        _moba_t_kernel,
        grid=(MOBA_HEADS // MOBA_PAIR, s // tq),
        in_specs=[pl.BlockSpec((tq, w), lambda h, b: (b, OFF_MQ // w + h)),
                  pl.BlockSpec((s, w), lambda h, b: (0, OFF_MK // w + h), **resident),
                  pl.BlockSpec((s, w), lambda h, b: (0, OFF_MV // w + h), **resident)],
        out_specs=pl.BlockSpec((tq, w), lambda h, b: (b, h)),
        out_shape=jax.ShapeDtypeStruct((s, MOBA_W), BF16),
        scratch_shapes=[pltpu.VMEM((gate_rows, w), F32),
                        pltpu.VMEM((s, LANES), BF16),
                        pltpu.VMEM((MOBA_PAIR, nb, vrows, MOBA_BLOCK), BF16),
                        pltpu.VMEM((MOBA_PAIR, 2 * MOBA_HD, tq), BF16),
                        pltpu.VMEM((MOBA_PAIR, vrows, tq), F32),
                        pltpu.VMEM((MOBA_PAIR, 1, tq), F32),
                        pltpu.VMEM((MOBA_PAIR, gw, tq), F32),
                        pltpu.VMEM((MOBA_PAIR, gw, tq), F32)],
        compiler_params=_cparams("arbitrary", "arbitrary"),
        name="moba",
    )(proj, proj, proj)


def _merge_kernel(or_ref, om_ref, ga_ref, gb_ref, wr_ref, wm_ref, o_ref):
    yr = _dot(or_ref[...], wr_ref[...])
    ym = _dot(om_ref[...], wm_ref[...])
    ga = jax.nn.sigmoid(ga_ref[...].astype(F32))
    gb = jax.nn.sigmoid(gb_ref[...].astype(F32))
    o_ref[...] = (ga * yr + gb * ym).astype(BF16)


def _merge(o_ret, o_moba, proj, w_r, w_m, tm=1024, tn=512):
    s = o_ret.shape[0]
    tm = min(tm, s)
    d = D_MODEL
    return pl.pallas_call(
        _merge_kernel,
        grid=(s // tm, d // tn),
        in_specs=[pl.BlockSpec((tm, RET_V_W), lambda i, j: (i, 0)),
                  pl.BlockSpec((tm, MOBA_W), lambda i, j: (i, 0)),
                  pl.BlockSpec((tm, tn), lambda i, j: (i, OFF_GA // tn + j)),
                  pl.BlockSpec((tm, tn), lambda i, j: (i, OFF_GB // tn + j)),
                  pl.BlockSpec((RET_V_W, tn), lambda i, j: (0, j)),
                  pl.BlockSpec((MOBA_W, tn), lambda i, j: (0, j))],
        out_specs=pl.BlockSpec((tm, tn), lambda i, j: (i, j)),
        out_shape=jax.ShapeDtypeStruct((s, d), BF16),
        compiler_params=_cparams("arbitrary", "arbitrary"),
        name="merge",
    )(o_ret, o_moba, proj, proj, w_r, w_m)


def _mixout_kernel(m_ref, wo_ref, x_ref, npost_ref, g1_ref, npre_ref, sc_ref, sh_ref, x1_ref, u2_ref):
    for r0 in range(0, m_ref.shape[0], MIX_ROWS):
        rows = slice(r0, r0 + MIX_ROWS)
        y = _dot(m_ref[rows, :], wo_ref[...])
        x1 = x_ref[rows, :] + (1.0 + g1_ref[...]) * (_rms(y) * npost_ref[...])
        x1_ref[rows, :] = x1
        u2_ref[rows, :] = ((_rms(x1) * npre_ref[...]) * (1.0 + sc_ref[...]) + sh_ref[...]).astype(BF16)


def _mixout(merged, w_o, x, npost, g1, npre, sc2, sh2, tm=512):
    s, d = x.shape
    tm = min(tm, s)
    vec = pl.BlockSpec((1, d), lambda i: (0, 0))
    rows = pl.BlockSpec((tm, d), lambda i: (i, 0))
    return pl.pallas_call(
        _mixout_kernel,
        grid=(s // tm,),
        in_specs=[rows, pl.BlockSpec((d, d), lambda i: (0, 0), pipeline_mode=pl.Buffered(1)), rows,
                  vec, vec, vec, vec, vec],
        out_specs=[rows, rows],
        out_shape=[jax.ShapeDtypeStruct((s, d), F32), jax.ShapeDtypeStruct((s, d), BF16)],
        compiler_params=_cparams("arbitrary"),
        name="mixout",
    )(merged, w_o, x, npost, g1, npre, sc2, sh2)


def _ffn_kernel(u_ref, wa_ref, wb_ref, cwa_ref, cwb_ref, cba_ref, cbb_ref, wd_ref, x1_ref, npost_ref, g2_ref,
                o_ref, ha_ref, hb_ref, act_ref, tail_ref):
    i = pl.program_id(0)
    j = pl.program_id(1)
    tm = u_ref.shape[0]
    tn = wa_ref.shape[1]
    halo = SUBLANES

    @pl.when((i == 0) & (j == 0))
    def _():
        tail_ref[...] = jnp.zeros_like(tail_ref)

    @pl.when(j == 0)
    def _():
        o_ref[...] = jnp.zeros_like(o_ref)

    ha_ref[0:halo, :] = tail_ref[j, 0]
    hb_ref[0:halo, :] = tail_ref[j, 1]
    for c in range(tn // FFN_CHUNK):
        cs = slice(c * FFN_CHUNK, (c + 1) * FFN_CHUNK)
        ha_ref[halo:halo + tm, cs] = _dot(u_ref[...], wa_ref[:, cs])
        hb_ref[halo:halo + tm, cs] = _dot(u_ref[...], wb_ref[:, cs])
    tail_ref[j, 0] = ha_ref[tm:tm + halo, :]
    tail_ref[j, 1] = hb_ref[tm:tm + halo, :]

    def conv(hbuf, cw, cb, r0, cs):
        out = cb
        for t in range(CONV_WIDTH):
            out = out + hbuf[halo + r0 - t:halo + r0 - t + FFN_ROWS, cs] * cw[CONV_WIDTH - 1 - t:CONV_WIDTH - t, :]
        return out

    for c in range(tn // FFN_CHUNK):
        cs = slice(c * FFN_CHUNK, (c + 1) * FFN_CHUNK)
        cwa, cwb, cba, cbb = cwa_ref[:, cs], cwb_ref[:, cs], cba_ref[:, cs], cbb_ref[:, cs]
        for r0 in range(0, tm, FFN_ROWS):
            a = conv(ha_ref, cwa, cba, r0, cs)
            bgate = conv(hb_ref, cwb, cbb, r0, cs)
            act_ref[r0:r0 + FFN_ROWS, cs] = (_silu(a) * bgate).astype(BF16)
        for n0 in range(0, o_ref.shape[1], FFN_DOWN_COLS):
            ns = slice(n0, n0 + FFN_DOWN_COLS)
            o_ref[:, ns] += _dot(act_ref[:, cs], wd_ref[cs, ns])

    @pl.when(j == pl.num_programs(1) - 1)
    def _():
        def piece(rows):
            y = o_ref[rows, :]
            o_ref[rows, :] = x1_ref[rows, :] + (1.0 + g2_ref[...]) * (_rms(y) * npost_ref[...])

        _for_row_chunks(tm, piece)


def _ffn(u2, w_up, conv_w, conv_b, w_down, x1, npost, g2, tm=1024, tn=512):
    s, d = x1.shape
    tm = min(tm, s)
    nj = D_FF // tn
    assert D_FF % tn == 0
    vec = pl.BlockSpec((1, d), lambda i, j: (0, 0))
    rows = pl.BlockSpec((tm, d), lambda i, j: (i, 0))
    resid = pl.BlockSpec((tm, d), lambda i, j: (i, 0), pipeline_mode=pl.Buffered(1))
    return pl.pallas_call(
        _ffn_kernel,
        grid=(s // tm, nj),
        in_specs=[rows,
                  pl.BlockSpec((d, tn), lambda i, j: (0, j)),
                  pl.BlockSpec((d, tn), lambda i, j: (0, j + nj)),
                  pl.BlockSpec((CONV_WIDTH, tn), lambda i, j: (0, j)),
                  pl.BlockSpec((CONV_WIDTH, tn), lambda i, j: (0, j + nj)),
                  pl.BlockSpec((1, tn), lambda i, j: (0, j)),
                  pl.BlockSpec((1, tn), lambda i, j: (0, j + nj)),
                  pl.BlockSpec((tn, d), lambda i, j: (j, 0)),
                  resid, vec, vec],
        out_specs=rows,
        out_shape=jax.ShapeDtypeStruct((s, d), F32),
        scratch_shapes=[pltpu.VMEM((tm + SUBLANES, tn), F32),
                        pltpu.VMEM((tm + SUBLANES, tn), F32),
                        pltpu.VMEM((tm, tn), BF16),
                        pltpu.VMEM((nj, 2, SUBLANES, tn), F32)],
        compiler_params=_cparams("arbitrary", "arbitrary"),
        name="ffn",
    )(u2, w_up, w_up, conv_w, conv_w, conv_b, conv_b, w_down, x1, npost, g2)


def _block(x, c, ada_w, ada_b, mix_norm_pre, mix_norm_post, w_in, ret_w_o, moba_w_o, w_out,
           ffn_norm_pre, ffn_norm_post, ffn_w_up, ffn_conv_w, ffn_conv_b, ffn_w_down, cos2, sin2):
    d = x.shape[1]
    mod = _ada(c, ada_w, ada_b[None, :])
    sh1, sc1, g1, sh2, sc2, g2 = [mod[:, t * d:(t + 1) * d] for t in range(6)]
    proj = _inproj(x, mix_norm_pre[None, :], sc1, sh1, cos2, sin2, w_in.astype(BF16))
    o_ret = _retention(proj)
    o_moba = _moba_t(proj)
    merged = _merge(o_ret, o_moba, proj, ret_w_o.astype(BF16), moba_w_o.astype(BF16))
    x1, u2 = _mixout(merged, w_out.astype(BF16), x, mix_norm_post[None, :], g1,
                     ffn_norm_pre[None, :], sc2, sh2)
    return _ffn(u2, ffn_w_up.astype(BF16), ffn_conv_w, ffn_conv_b[None, :], ffn_w_down.astype(BF16),
                x1, ffn_norm_post[None, :], g2)


def kernel(x, c, ada_w, ada_b, mix_norm_pre, mix_norm_post, w_in, ret_w_o, moba_w_o, w_out,
           ffn_norm_pre, ffn_norm_post, ffn_w_up, ffn_conv_w, ffn_conv_b, ffn_w_down):
    batch, s, _ = x.shape
    depth = ada_w.shape[0]
    cos2, sin2 = _rope_tables(s)
    outs = []
    for bi in range(batch):
        xb = x[bi]
        for l in range(depth):
            xb = _block(xb, c[bi:bi + 1], ada_w[l], ada_b[l], mix_norm_pre[l], mix_norm_post[l], w_in[l],
                        ret_w_o[l], moba_w_o[l], w_out[l], ffn_norm_pre[l], ffn_norm_post[l], ffn_w_up[l],
                        ffn_conv_w[l], ffn_conv_b[l], ffn_w_down[l], cos2, sin2)
        outs.append(xb)
    return jnp.stack(outs)
```

```python
import functools
import math

import jax
import jax.numpy as jnp
from jax import lax
from jax.experimental import pallas as pl
from jax.experimental.pallas import tpu as pltpu

D_MODEL = 2048
RET_HEADS = 8
RET_DK = 128
RET_DV = 256
ROPE_BASE = 10000.0
MOBA_HEADS = 16
MOBA_HD = 128
MOBA_BLOCK = 256
MOBA_TOPK = 3
D_FF = 5632
CONV_WIDTH = 3
NORM_EPS = 1e-6
GN_EPS = 1e-5

RET_QK_W = RET_HEADS * RET_DK
RET_V_W = RET_HEADS * RET_DV
MOBA_W = MOBA_HEADS * MOBA_HD
OFF_RQ = 0
OFF_RK = OFF_RQ + RET_QK_W
OFF_RV = OFF_RK + RET_QK_W
OFF_RG = OFF_RV + RET_V_W
OFF_MQ = OFF_RG + RET_V_W
OFF_MK = OFF_MQ + MOBA_W
OFF_MV = OFF_MK + MOBA_W
OFF_GA = OFF_MV + MOBA_W
OFF_GB = OFF_GA + D_MODEL
IN_WIDTH = OFF_GB + D_MODEL

LANES = 128
SUBLANES = 8
NEG = -1e30
MOBA_QSCALE = math.log2(math.e) / math.sqrt(MOBA_HD)
RET_GROUP = 4
MOBA_GROUP = 4
MOBA_PAIR = 2
FFN_CHUNK = 256
FFN_ROWS = 64
FFN_DOWN_COLS = 512
MIX_ROWS = 256
NORM_ROWS = 32
NORM_UNROLL = 4
VMEM_LIMIT = 56 * 1024 * 1024

F32 = jnp.float32
BF16 = jnp.bfloat16


def _cparams(*sem):
    return pltpu.CompilerParams(dimension_semantics=sem, vmem_limit_bytes=VMEM_LIMIT)


def _dot(a, b):
    return jnp.dot(a, b, preferred_element_type=F32)


def _dot_nt(a, b):
    return lax.dot_general(a, b, (((1,), (1,)), ((), ())), preferred_element_type=F32)


def _split_bf16(x):
    hi = x.astype(BF16)
    lo = (x - hi.astype(F32)).astype(BF16)
    return hi, lo


def _rms(x):
    return x * lax.rsqrt(jnp.mean(x * x, axis=-1, keepdims=True) + NORM_EPS)


def _silu(x):
    return x * jax.nn.sigmoid(x)


def _for_row_chunks(n_rows, fn):
    step = NORM_ROWS * NORM_UNROLL

    def body(r, carry):
        for t in range(NORM_UNROLL):
            fn(pl.ds(pl.multiple_of(r * step + t * NORM_ROWS, NORM_ROWS), NORM_ROWS))
        return carry

    lax.fori_loop(0, n_rows // step, body, 0)


def _ada_kernel(c_ref, w_ref, b_ref, o_ref):
    c = jnp.broadcast_to(_silu(c_ref[...]), (SUBLANES, c_ref.shape[1]))
    c_hi, c_lo = _split_bf16(c)
    w_hi, w_lo = _split_bf16(w_ref[...])
    r = _dot(c_hi, w_hi) + _dot(c_lo, w_hi) + _dot(c_hi, w_lo)
    o_ref[...] = r[0:1] + b_ref[...]


def _ada(c, w, b, tn=1024):
    d, n = w.shape
    return pl.pallas_call(
        _ada_kernel,
        grid=(n // tn,),
        in_specs=[pl.BlockSpec((1, d), lambda j: (0, 0)),
                  pl.BlockSpec((d, tn), lambda j: (0, j)),
                  pl.BlockSpec((1, tn), lambda j: (0, j))],
        out_specs=pl.BlockSpec((1, tn), lambda j: (0, j)),
        out_shape=jax.ShapeDtypeStruct((1, n), F32),
        compiler_params=_cparams("arbitrary"),
        name="ada",
    )(c, w, b)


def _rope_kernel(inv_ref, cos_ref, sin_ref):
    ts = cos_ref.shape[0]
    row = lax.broadcasted_iota(jnp.int32, (ts, LANES), 0) + pl.program_id(0) * ts
    lane = lax.broadcasted_iota(jnp.int32, (ts, LANES), 1)
    ang = row.astype(F32) * inv_ref[...]
    sin = jnp.sin(ang)
    cos_ref[...] = jnp.cos(ang)
    sin_ref[...] = jnp.where(lane < RET_DK // 2, -sin, sin)


def _rope_tables(s, ts=2048):
    ts = min(ts, s)
    half = RET_DK // 2
    inv = ROPE_BASE ** (-jnp.arange(half, dtype=F32) / half)
    inv2 = jnp.concatenate([inv, inv])[None, :]
    return pl.pallas_call(
        _rope_kernel,
        grid=(s // ts,),
        in_specs=[pl.BlockSpec((1, LANES), lambda i: (0, 0))],
        out_specs=[pl.BlockSpec((ts, LANES), lambda i: (i, 0))] * 2,
        out_shape=[jax.ShapeDtypeStruct((s, LANES), F32)] * 2,
        compiler_params=_cparams("arbitrary"),
        name="rope",
    )(inv2)


def _inproj_kernel(x_ref, nw_ref, sc_ref, sh_ref, cos_ref, sin_ref, w_ref, o_ref, u_ref):
    j = pl.program_id(1)
    tn = o_ref.shape[1]

    @pl.when(j == 0)
    def _():
        def piece(rows):
            y = _rms(x_ref[rows, :]) * nw_ref[...]
            u_ref[rows, :] = (y * (1.0 + sc_ref[...]) + sh_ref[...]).astype(BF16)

        _for_row_chunks(x_ref.shape[0], piece)

    r = _dot(u_ref[...], w_ref[...])
    q_tiles = RET_QK_W // tn
    is_mq = (j >= OFF_MQ // tn) & (j < OFF_MK // tn)
    o_ref[...] = (r * jnp.where(is_mq, MOBA_QSCALE, 1.0).astype(F32)).astype(BF16)

    @pl.when(j < 2 * q_tiles)
    def _():
        scale = jnp.where(j >= q_tiles, RET_DK ** -0.5, 1.0).astype(F32)
        cos = cos_ref[...]
        sin = sin_ref[...]
        for h in range(tn // RET_DK):
            xh = r[:, h * RET_DK:(h + 1) * RET_DK]
            rot = xh * cos + pltpu.roll(xh, RET_DK // 2, axis=1) * sin
            o_ref[:, h * RET_DK:(h + 1) * RET_DK] = (rot * scale).astype(BF16)


def _inproj(x, nw, sc, sh, cos2, sin2, w, tm=1024, tn=1024):
    s, d = x.shape
    n = w.shape[1]
    tm = min(tm, s)
    vec = pl.BlockSpec((1, d), lambda i, j: (0, 0))
    tab = pl.BlockSpec((tm, LANES), lambda i, j: (i, 0))
    return pl.pallas_call(
        _inproj_kernel,
        grid=(s // tm, n // tn),
        in_specs=[pl.BlockSpec((tm, d), lambda i, j: (i, 0)), vec, vec, vec, tab, tab,
                  pl.BlockSpec((d, tn), lambda i, j: (0, j))],
        out_specs=pl.BlockSpec((tm, tn), lambda i, j: (i, j)),
        out_shape=jax.ShapeDtypeStruct((s, n), BF16),
        scratch_shapes=[pltpu.VMEM((tm, d), BF16)],
        compiler_params=_cparams("arbitrary", "arbitrary"),
        name="inproj",
    )(x, nw, sc, sh, cos2, sin2, w)


def _ret_kernel(lg_ref, q_ref, k_ref, v_ref, g_ref, o_ref, r_ref, dec_ref, xi_ref, zeta_ref):
    n = pl.program_id(1)
    c = q_ref.shape[0]
    dk, dv = RET_DK, RET_DV

    @pl.when(n == 0)
    def _():
        r_ref[...] = jnp.zeros_like(r_ref)
        d = lax.broadcasted_iota(jnp.int32, (c, c), 0) - lax.broadcasted_iota(jnp.int32, (c, c), 1)
        row = lax.broadcasted_iota(jnp.int32, (c, dk), 0)
        for h in range(RET_GROUP):
            lg = lg_ref[h, 0:1, 0:1]
            dec_ref[h] = jnp.where(d >= 0, jnp.exp(jnp.maximum(d, 0).astype(F32) * lg), 0.0)
            xi_ref[h] = jnp.exp((row + 1).astype(F32) * lg)
            zeta_ref[h] = jnp.exp((c - 1 - row).astype(F32) * lg)

    for h in range(RET_GROUP):
        q = q_ref[:, h * dk:(h + 1) * dk]
        k = k_ref[:, h * dk:(h + 1) * dk]
        v = v_ref[:, h * dv:(h + 1) * dv]
        s = _dot_nt(q, k) * dec_ref[h]
        inner = _dot(s.astype(BF16), v)
        r = r_ref[h]
        cross = _dot((q.astype(F32) * xi_ref[h]).astype(BF16), r.astype(BF16))
        o = inner + cross
        kz = (k.astype(F32) * zeta_ref[h]).T.astype(BF16)
        r_ref[h] = r * jnp.exp(lg_ref[h, 0:1, 0:1] * c) + _dot(kz, v)
        oc = o - jnp.mean(o, axis=-1, keepdims=True)
        on = oc * lax.rsqrt(jnp.mean(oc * oc, axis=-1, keepdims=True) + GN_EPS)
        o_ref[:, h * dv:(h + 1) * dv] = (_silu(g_ref[:, h * dv:(h + 1) * dv].astype(F32)) * on).astype(BF16)


def _retention(proj, chunk=512):
    s = proj.shape[0]
    chunk = min(chunk, s)
    log_g = jnp.log1p(-jnp.exp2(-5.0 - jnp.arange(RET_HEADS, dtype=F32)))
    lg = jnp.broadcast_to(log_g[:, None, None], (RET_HEADS, SUBLANES, LANES))
    qk_w, v_w = RET_GROUP * RET_DK, RET_GROUP * RET_DV
    assert RET_HEADS % RET_GROUP == 0
    return pl.pallas_call(
        _ret_kernel,
        grid=(RET_HEADS // RET_GROUP, s // chunk),
        in_specs=[pl.BlockSpec((RET_GROUP, SUBLANES, LANES), lambda h, n: (h, 0, 0)),
                  pl.BlockSpec((chunk, qk_w), lambda h, n: (n, OFF_RQ // qk_w + h)),
                  pl.BlockSpec((chunk, qk_w), lambda h, n: (n, OFF_RK // qk_w + h)),
                  pl.BlockSpec((chunk, v_w), lambda h, n: (n, OFF_RV // v_w + h)),
                  pl.BlockSpec((chunk, v_w), lambda h, n: (n, OFF_RG // v_w + h))],
        out_specs=pl.BlockSpec((chunk, v_w), lambda h, n: (n, h)),
        out_shape=jax.ShapeDtypeStruct((s, RET_V_W), BF16),
        scratch_shapes=[pltpu.VMEM((RET_GROUP, RET_DK, RET_DV), F32),
                        pltpu.VMEM((RET_GROUP, chunk, chunk), F32),
                        pltpu.VMEM((RET_GROUP, chunk, RET_DK), F32),
                        pltpu.VMEM((RET_GROUP, chunk, RET_DK), F32)],
        compiler_params=_cparams("arbitrary", "arbitrary"),
        name="ret",
    )(lg, proj, proj, proj, proj)


def _moba_kernel(q_ref, k_ref, v_ref, o_ref, km_ref, oh_ref, qx_ref, acc_ref, m_ref, s0_ref, s1_ref):
    hp = pl.program_id(0)
    b = pl.program_id(1)
    blk = q_ref.shape[0]
    hd = MOBA_HD
    nb = k_ref.shape[0] // blk
    gw = MOBA_GROUP * blk
    lane_i = lax.broadcasted_iota(jnp.int32, (blk, LANES), 1)

    @pl.when((hp == 0) & (b == 0))
    def _():
        def body(n, carry):
            oh_ref[pl.ds(pl.multiple_of(n * blk, blk), blk), :] = jnp.where(lane_i == n, 1.0, 0.0).astype(BF16)
            return carry

        lax.fori_loop(0, nb, body, 0)

    @pl.when(b == 0)
    def _():
        km_ref[...] = jnp.zeros_like(km_ref)

        def body(n, carry):
            kb = k_ref[pl.ds(pl.multiple_of(n * blk, blk), blk), :].astype(F32)
            km_ref[pl.ds(n, 1), :] = jnp.mean(kb, axis=0, keepdims=True)
            return carry

        lax.fori_loop(0, nb, body, 0)

    lane = lane_i.astype(F32)
    for h in range(MOBA_PAIR):
        q = q_ref[:, h * hd:(h + 1) * hd]
        km_hi, km_lo = _split_bf16(km_ref[:, h * hd:(h + 1) * hd])
        g = jnp.where(lane_i < b, _dot_nt(q, km_hi) + _dot_nt(q, km_lo), NEG)
        sel = lane_i == b
        for _ in range(MOBA_TOPK):
            mx = jnp.max(g, axis=-1, keepdims=True)
            idx = jnp.min(jnp.where(g == mx, lane, float(LANES)), axis=-1, keepdims=True)
            hit = lane == idx
            sel = sel | (hit & (mx > 0.5 * NEG))
            g = jnp.where(hit, 2.0 * NEG, g)
        qx_ref[h] = jnp.concatenate([q, jnp.where(sel, 0.0, NEG).astype(BF16)], axis=1)
        m_ref[h] = jnp.full((blk, 1), NEG, F32)
        acc_ref[h] = jnp.zeros((blk, 2 * hd), F32)

    ones = jnp.ones((gw, hd), BF16)

    def rows_of(g):
        return pl.ds(pl.multiple_of(g * gw, gw), gw)

    def scores(g, slot_ref):
        rows = rows_of(g)
        oh = oh_ref[rows, :]
        for h in range(MOBA_PAIR):
            k_ext = jnp.concatenate([k_ref[rows, h * hd:(h + 1) * hd], oh], axis=1)
            slot_ref[h] = _dot_nt(qx_ref[h], k_ext)

    def accumulate(g, slot_ref, causal):
        rows = rows_of(g)
        if causal:
            key_pos = g * gw + lax.broadcasted_iota(jnp.int32, (blk, gw), 1)
            qry_pos = b * blk + lax.broadcasted_iota(jnp.int32, (blk, gw), 0)
            keep = key_pos <= qry_pos
        for h in range(MOBA_PAIR):
            s = slot_ref[h]
            if causal:
                s = jnp.where(keep, s, NEG)
            m_old = m_ref[h]
            m_new = jnp.maximum(m_old, jnp.max(s, axis=-1, keepdims=True))
            p = jnp.exp2(s - m_new).astype(BF16)
            v_ext = jnp.concatenate([v_ref[rows, h * hd:(h + 1) * hd], ones], axis=1)
            acc_ref[h] = jnp.exp2(m_old - m_new) * acc_ref[h] + _dot(p, v_ext)
            m_ref[h] = m_new

    full_groups = b // MOBA_GROUP
    scores(0, s0_ref)

    def body(i, carry):
        g = 2 * i
        scores(g + 1, s1_ref)
        accumulate(g, s0_ref, False)
        scores(g + 2, s0_ref)
        accumulate(g + 1, s1_ref, False)
        return carry

    lax.fori_loop(0, full_groups // 2, body, 0)

    @pl.when(full_groups % 2 == 0)
    def _():
        accumulate(full_groups, s0_ref, True)

    @pl.when(full_groups % 2 == 1)
    def _():
        scores(full_groups, s1_ref)
        accumulate(full_groups - 1, s0_ref, False)
        accumulate(full_groups, s1_ref, True)

    for h in range(MOBA_PAIR):
        acc = acc_ref[h]
        o_ref[:, h * hd:(h + 1) * hd] = (acc[:, :hd] / acc[:, hd:]).astype(BF16)


def _moba(proj):
    s = proj.shape[0]
    nb = s // MOBA_BLOCK
    w = MOBA_PAIR * MOBA_HD
    assert s % (MOBA_GROUP * MOBA_BLOCK) == 0 and nb <= LANES and MOBA_HEADS % MOBA_PAIR == 0
    resident = dict(pipeline_mode=pl.Buffered(1))
    return pl.pallas_call(
        _moba_kernel,
        grid=(MOBA_HEADS // MOBA_PAIR, nb),
        in_specs=[pl.BlockSpec((MOBA_BLOCK, w), lambda h, b: (b, OFF_MQ // w + h)),
                  pl.BlockSpec((s, w), lambda h, b: (0, OFF_MK // w + h), **resident),
                  pl.BlockSpec((s, w), lambda h, b: (0, OFF_MV // w + h), **resident)],
        out_specs=pl.BlockSpec((MOBA_BLOCK, w), lambda h, b: (b, h)),
        out_shape=jax.ShapeDtypeStruct((s, MOBA_W), BF16),
        scratch_shapes=[pltpu.VMEM((LANES, w), F32),
                        pltpu.VMEM((s, LANES), BF16),
                        pltpu.VMEM((MOBA_PAIR, MOBA_BLOCK, 2 * MOBA_HD), BF16),
                        pltpu.VMEM((MOBA_PAIR, MOBA_BLOCK, 2 * MOBA_HD), F32),
                        pltpu.VMEM((MOBA_PAIR, MOBA_BLOCK, 1), F32),
                        pltpu.VMEM((MOBA_PAIR, MOBA_BLOCK, MOBA_GROUP * MOBA_BLOCK), F32),
                        pltpu.VMEM((MOBA_PAIR, MOBA_BLOCK, MOBA_GROUP * MOBA_BLOCK), F32)],
        compiler_params=_cparams("arbitrary", "arbitrary"),
        name="moba",
    )(proj, proj, proj)


def _moba_t_kernel(q_ref, k_ref, v_ref, o_ref, km_ref, oh_ref, vt_ref, qt_ref, acc_ref, m_ref, s0_ref, s1_ref):
    hp = pl.program_id(0)
    bq = pl.program_id(1)
    tq = q_ref.shape[0]
    hd = MOBA_HD
    blk = MOBA_BLOCK
    nb = k_ref.shape[0] // blk
    gw = MOBA_GROUP * blk
    vrows = vt_ref.shape[2]

    @pl.when((hp == 0) & (bq == 0))
    def _():
        lane_i = lax.broadcasted_iota(jnp.int32, (blk, LANES), 1)

        def body(n, carry):
            oh_ref[pl.ds(pl.multiple_of(n * blk, blk), blk), :] = jnp.where(lane_i == n, 1.0, 0.0).astype(BF16)
            return carry

        lax.fori_loop(0, nb, body, 0)

    @pl.when(bq == 0)
    def _():
        km_ref[...] = jnp.zeros_like(km_ref)

        def body(g, carry):
            for t in range(MOBA_GROUP):
                n = g * MOBA_GROUP + t
                rows = pl.ds(pl.multiple_of(n * blk, blk), blk)
                km_ref[pl.ds(n, 1), :] = jnp.mean(k_ref[rows, :].astype(F32), axis=0, keepdims=True)
                for h in range(MOBA_PAIR):
                    vt_ref[h, g, 0:hd, t * blk:(t + 1) * blk] = (
                        v_ref[rows, h * hd:(h + 1) * hd].astype(F32).T.astype(BF16))
            for h in range(MOBA_PAIR):
                vt_ref[h, g, hd:vrows, :] = jnp.ones((vrows - hd, gw), BF16)
            return carry

        lax.fori_loop(0, nb // MOBA_GROUP, body, 0)

    gr = km_ref.shape[0]
    row_i = lax.broadcasted_iota(jnp.int32, (gr, tq), 0)
    row_f = row_i.astype(F32)
    own = bq * (tq // blk) + lax.broadcasted_iota(jnp.int32, (gr, tq), 1) // blk
    for h in range(MOBA_PAIR):
        qt = q_ref[:, h * hd:(h + 1) * hd].astype(F32).T.astype(BF16)
        km_hi, km_lo = _split_bf16(km_ref[:, h * hd:(h + 1) * hd])
        g = jnp.where(row_i < own, _dot(km_hi, qt) + _dot(km_lo, qt), NEG)
        bias = jnp.full((gr, tq), NEG, F32)
        for _ in range(MOBA_TOPK):
            mx = jnp.max(g, axis=0, keepdims=True)
            idx = jnp.min(jnp.where(g == mx, row_f, float(gr)), axis=0, keepdims=True)
            hit = row_f == idx
            bias = jnp.where(hit, jnp.where(mx > 0.5 * NEG, 0.0, NEG), bias)
            g = jnp.where(hit, 2.0 * NEG, g)
        bias = jnp.where(row_i == own, 0.0, bias)
        qt_ref[h] = jnp.concatenate(
            [qt, bias.astype(BF16), jnp.full((LANES - gr, tq), NEG, BF16)], axis=0)
        m_ref[h] = jnp.full((1, tq), NEG, F32)
        acc_ref[h] = jnp.zeros((vrows, tq), F32)

    def scores(g, slot_ref):
        rows = pl.ds(pl.multiple_of(g * gw, gw), gw)
        oh = oh_ref[rows, :]
        for h in range(MOBA_PAIR):
            k_ext = jnp.concatenate([k_ref[rows, h * hd:(h + 1) * hd], oh], axis=1)
            slot_ref[h] = _dot(k_ext, qt_ref[h])

    def accumulate(g, slot_ref, causal):
        if causal:
            own_rows = pl.ds(pl.multiple_of((bq * tq) % gw, tq), tq)
            keep = lax.broadcasted_iota(jnp.int32, (tq, tq), 0) <= lax.broadcasted_iota(jnp.int32, (tq, tq), 1)
            for h in range(MOBA_PAIR):
                slot_ref[h, own_rows, :] = jnp.where(keep, slot_ref[h, own_rows, :], NEG)
        for h in range(MOBA_PAIR):
            s = slot_ref[h]
            m_old = m_ref[h]
            m_new = jnp.maximum(m_old, jnp.max(s, axis=0, keepdims=True))
            p = jnp.exp2(s - m_new).astype(BF16)
            acc_ref[h] = jnp.exp2(m_old - m_new) * acc_ref[h] + _dot(vt_ref[h, g], p)
            m_ref[h] = m_new

    full_groups = (bq * (tq // blk)) // MOBA_GROUP
    scores(0, s0_ref)

    def body(i, carry):
        g = 2 * i
        scores(g + 1, s1_ref)
        accumulate(g, s0_ref, False)
        scores(g + 2, s0_ref)
        accumulate(g + 1, s1_ref, False)
        return carry

    lax.fori_loop(0, full_groups // 2, body, 0)

    @pl.when(full_groups % 2 == 0)
    def _():
        accumulate(full_groups, s0_ref, True)

    @pl.when(full_groups % 2 == 1)
    def _():
        scores(full_groups, s1_ref)
        accumulate(full_groups - 1, s0_ref, False)
        accumulate(full_groups, s1_ref, True)

    for h in range(MOBA_PAIR):
        acc = acc_ref[h]
        o_ref[:, h * hd:(h + 1) * hd] = (acc[0:hd, :] / acc[hd:hd + 1, :]).T.astype(BF16)


def _moba_t(proj, tq=2 * MOBA_BLOCK):
    s = proj.shape[0]
    nb = s // MOBA_BLOCK
    w = MOBA_PAIR * MOBA_HD
    gw = MOBA_GROUP * MOBA_BLOCK
    bf16_rows = 2 * SUBLANES
    vrows = MOBA_HD + bf16_rows
    gate_rows = -(-nb // bf16_rows) * bf16_rows
    assert s % gw == 0 and gate_rows <= LANES and MOBA_HEADS % MOBA_PAIR == 0
    assert tq % MOBA_BLOCK == 0 and gw % tq == 0
    resident = dict(pipeline_mode=pl.Buffered(1))
    return pl.pallas_call(
        _moba_t_kernel,
        grid=(MOBA_HEADS // MOBA_PAIR, s // tq),
        in_specs=[pl.BlockSpec((tq, w), lambda h, b: (b, OFF_MQ // w + h)),
                  pl.BlockSpec((s, w), lambda h, b: (0, OFF_MK // w + h)),
                  pl.BlockSpec((s, w), lambda h, b: (0, OFF_MV // w + h), **resident)],
        out_specs=pl.BlockSpec((tq, w), lambda h, b: (b, h)),
        out_shape=jax.ShapeDtypeStruct((s, MOBA_W), BF16),
        scratch_shapes=[pltpu.VMEM((gate_rows, w), F32),
                        pltpu.VMEM((s, LANES), BF16),
                        pltpu.VMEM((MOBA_PAIR, s // gw, vrows, gw), BF16),
                        pltpu.VMEM((MOBA_PAIR, 2 * MOBA_HD, tq), BF16),
                        pltpu.VMEM((MOBA_PAIR, vrows, tq), F32),
                        pltpu.VMEM((MOBA_PAIR, 1, tq), F32),
                        pltpu.VMEM((MOBA_PAIR, gw, tq), F32),
                        pltpu.VMEM((MOBA_PAIR, gw, tq), F32)],
        compiler_params=_cparams("arbitrary", "arbitrary"),
        name="moba",
    )(proj, proj, proj)


def _merge_kernel(or_ref, om_ref, ga_ref, gb_ref, wr_ref, wm_ref, o_ref):
    yr = _dot(or_ref[...], wr_ref[...])
    ym = _dot(om_ref[...], wm_ref[...])
    ga = jax.nn.sigmoid(ga_ref[...].astype(F32))
    gb = jax.nn.sigmoid(gb_ref[...].astype(F32))
    o_ref[...] = (ga * yr + gb * ym).astype(BF16)


def _merge(o_ret, o_moba, proj, w_r, w_m, tm=1024, tn=512):
    s = o_ret.shape[0]
    tm = min(tm, s)
    d = D_MODEL
    return pl.pallas_call(
        _merge_kernel,
        grid=(s // tm, d // tn),
        in_specs=[pl.BlockSpec((tm, RET_V_W), lambda i, j: (i, 0)),
                  pl.BlockSpec((tm, MOBA_W), lambda i, j: (i, 0)),
                  pl.BlockSpec((tm, tn), lambda i, j: (i, OFF_GA // tn + j)),
                  pl.BlockSpec((tm, tn), lambda i, j: (i, OFF_GB // tn + j)),
                  pl.BlockSpec((RET_V_W, tn), lambda i, j: (0, j)),
                  pl.BlockSpec((MOBA_W, tn), lambda i, j: (0, j))],
        out_specs=pl.BlockSpec((tm, tn), lambda i, j: (i, j)),
        out_shape=jax.ShapeDtypeStruct((s, d), BF16),
        compiler_params=_cparams("arbitrary", "arbitrary"),
        name="merge",
    )(o_ret, o_moba, proj, proj, w_r, w_m)


def _mixout_kernel(m_ref, wo_ref, x_ref, npost_ref, g1_ref, npre_ref, sc_ref, sh_ref, x1_ref, u2_ref):
    for r0 in range(0, m_ref.shape[0], MIX_ROWS):
        rows = slice(r0, r0 + MIX_ROWS)
        y = _dot(m_ref[rows, :], wo_ref[...])
        x1 = x_ref[rows, :] + (1.0 + g1_ref[...]) * (_rms(y) * npost_ref[...])
        x1_ref[rows, :] = x1
        u2_ref[rows, :] = ((_rms(x1) * npre_ref[...]) * (1.0 + sc_ref[...]) + sh_ref[...]).astype(BF16)


def _mixout(merged, w_o, x, npost, g1, npre, sc2, sh2, tm=512):
    s, d = x.shape
    tm = min(tm, s)
    vec = pl.BlockSpec((1, d), lambda i: (0, 0))
    rows = pl.BlockSpec((tm, d), lambda i: (i, 0))
    return pl.pallas_call(
        _mixout_kernel,
        grid=(s // tm,),
        in_specs=[rows, pl.BlockSpec((d, d), lambda i: (0, 0), pipeline_mode=pl.Buffered(1)), rows,
                  vec, vec, vec, vec, vec],
        out_specs=[rows, rows],
        out_shape=[jax.ShapeDtypeStruct((s, d), F32), jax.ShapeDtypeStruct((s, d), BF16)],
        compiler_params=_cparams("arbitrary"),
        name="mixout",
    )(merged, w_o, x, npost, g1, npre, sc2, sh2)


def _ffn_kernel(u_ref, wa_ref, wb_ref, cwa_ref, cwb_ref, cba_ref, cbb_ref, wd_ref, x1_ref, npost_ref, g2_ref,
                o_ref, ha_ref, hb_ref, act_ref, tail_ref):
    i = pl.program_id(0)
    j = pl.program_id(1)
    tm = u_ref.shape[0]
    tn = wa_ref.shape[1]
    halo = SUBLANES

    @pl.when((i == 0) & (j == 0))
    def _():
        tail_ref[...] = jnp.zeros_like(tail_ref)

    @pl.when(j == 0)
    def _():
        o_ref[...] = jnp.zeros_like(o_ref)

    ha_ref[0:halo, :] = tail_ref[j, 0]
    hb_ref[0:halo, :] = tail_ref[j, 1]
    for c in range(tn // FFN_CHUNK):
        cs = slice(c * FFN_CHUNK, (c + 1) * FFN_CHUNK)
        ha_ref[halo:halo + tm, cs] = _dot(u_ref[...], wa_ref[:, cs])
        hb_ref[halo:halo + tm, cs] = _dot(u_ref[...], wb_ref[:, cs])
    tail_ref[j, 0] = ha_ref[tm:tm + halo, :]
    tail_ref[j, 1] = hb_ref[tm:tm + halo, :]

    def conv(hbuf, cw, cb, r0, cs):
        out = cb
        for t in range(CONV_WIDTH):
            out = out + hbuf[halo + r0 - t:halo + r0 - t + FFN_ROWS, cs] * cw[CONV_WIDTH - 1 - t:CONV_WIDTH - t, :]
        return out

    for c in range(tn // FFN_CHUNK):
        cs = slice(c * FFN_CHUNK, (c + 1) * FFN_CHUNK)
        cwa, cwb, cba, cbb = cwa_ref[:, cs], cwb_ref[:, cs], cba_ref[:, cs], cbb_ref[:, cs]
        for r0 in range(0, tm, FFN_ROWS):
            a = conv(ha_ref, cwa, cba, r0, cs)
            bgate = conv(hb_ref, cwb, cbb, r0, cs)
            act_ref[r0:r0 + FFN_ROWS, cs] = (_silu(a) * bgate).astype(BF16)
        for n0 in range(0, o_ref.shape[1], FFN_DOWN_COLS):
            ns = slice(n0, n0 + FFN_DOWN_COLS)
            o_ref[:, ns] += _dot(act_ref[:, cs], wd_ref[cs, ns])

    @pl.when(j == pl.num_programs(1) - 1)
    def _():
        def piece(rows):
            y = o_ref[rows, :]
            o_ref[rows, :] = x1_ref[rows, :] + (1.0 + g2_ref[...]) * (_rms(y) * npost_ref[...])

        _for_row_chunks(tm, piece)


def _ffn(u2, w_up, conv_w, conv_b, w_down, x1, npost, g2, tm=1024, tn=512):
    s, d = x1.shape
    tm = min(tm, s)
    nj = D_FF // tn
    assert D_FF % tn == 0
    vec = pl.BlockSpec((1, d), lambda i, j: (0, 0))
    rows = pl.BlockSpec((tm, d), lambda i, j: (i, 0))
    resid = pl.BlockSpec((tm, d), lambda i, j: (i, 0), pipeline_mode=pl.Buffered(1))
    return pl.pallas_call(
        _ffn_kernel,
        grid=(s // tm, nj),
        in_specs=[rows,
                  pl.BlockSpec((d, tn), lambda i, j: (0, j)),
                  pl.BlockSpec((d, tn), lambda i, j: (0, j + nj)),
                  pl.BlockSpec((CONV_WIDTH, tn), lambda i, j: (0, j)),
                  pl.BlockSpec((CONV_WIDTH, tn), lambda i, j: (0, j + nj)),
                  pl.BlockSpec((1, tn), lambda i, j: (0, j)),
                  pl.BlockSpec((1, tn), lambda i, j: (0, j + nj)),
                  pl.BlockSpec((tn, d), lambda i, j: (j, 0)),
                  resid, vec, vec],
        out_specs=rows,
        out_shape=jax.ShapeDtypeStruct((s, d), F32),
        scratch_shapes=[pltpu.VMEM((tm + SUBLANES, tn), F32),
                        pltpu.VMEM((tm + SUBLANES, tn), F32),
                        pltpu.VMEM((tm, tn), BF16),
                        pltpu.VMEM((nj, 2, SUBLANES, tn), F32)],
        compiler_params=_cparams("arbitrary", "arbitrary"),
        name="ffn",
    )(u2, w_up, w_up, conv_w, conv_w, conv_b, conv_b, w_down, x1, npost, g2)


def _block(x, c, ada_w, ada_b, mix_norm_pre, mix_norm_post, w_in, ret_w_o, moba_w_o, w_out,
           ffn_norm_pre, ffn_norm_post, ffn_w_up, ffn_conv_w, ffn_conv_b, ffn_w_down, cos2, sin2):
    d = x.shape[1]
    mod = _ada(c, ada_w, ada_b[None, :])
    sh1, sc1, g1, sh2, sc2, g2 = [mod[:, t * d:(t + 1) * d] for t in range(6)]
    proj = _inproj(x, mix_norm_pre[None, :], sc1, sh1, cos2, sin2, w_in.astype(BF16))
    o_ret = _retention(proj)
    o_moba = _moba_t(proj)
    merged = _merge(o_ret, o_moba, proj, ret_w_o.astype(BF16), moba_w_o.astype(BF16))
    x1, u2 = _mixout(merged, w_out.astype(BF16), x, mix_norm_post[None, :], g1,
                     ffn_norm_pre[None, :], sc2, sh2)
    return _ffn(u2, ffn_w_up.astype(BF16), ffn_conv_w, ffn_conv_b[None, :], ffn_w_down.astype(BF16),
                x1, ffn_norm_post[None, :], g2)


def kernel(x, c, ada_w, ada_b, mix_norm_pre, mix_norm_post, w_in, ret_w_o, moba_w_o, w_out,
           ffn_norm_pre, ffn_norm_post, ffn_w_up, ffn_conv_w, ffn_conv_b, ffn_w_down):
    batch, s, _ = x.shape
    depth = ada_w.shape[0]
    cos2, sin2 = _rope_tables(s)
    outs = []
    for bi in range(batch):
        xb = x[bi]
        for l in range(depth):
            xb = _block(xb, c[bi:bi + 1], ada_w[l], ada_b[l], mix_norm_pre[l], mix_norm_post[l], w_in[l],
                        ret_w_o[l], moba_w_o[l], w_out[l], ffn_norm_pre[l], ffn_norm_post[l], ffn_w_up[l],
                        ffn_conv_w[l], ffn_conv_b[l], ffn_w_down[l], cos2, sin2)
        outs.append(xb)
    return jnp.stack(outs)
```

```python
import math

import jax
import jax.numpy as jnp
from jax import lax
from jax.experimental import pallas as pl
from jax.experimental.pallas import tpu as pltpu

D_MODEL = 2048
RET_HEADS = 8
RET_DK = 128
RET_DV = 256
ROPE_BASE = 10000.0
MOBA_HEADS = 16
MOBA_HD = 128
MOBA_BLOCK = 256
MOBA_TOPK = 3
D_FF = 5632
CONV_WIDTH = 3
NORM_EPS = 1e-6
GN_EPS = 1e-5

RET_QK_W = RET_HEADS * RET_DK
RET_V_W = RET_HEADS * RET_DV
MOBA_W = MOBA_HEADS * MOBA_HD
OFF_RQ = 0
OFF_RK = OFF_RQ + RET_QK_W
OFF_RV = OFF_RK + RET_QK_W
OFF_RG = OFF_RV + RET_V_W
OFF_MQ = OFF_RG + RET_V_W
OFF_MK = OFF_MQ + MOBA_W
OFF_MV = OFF_MK + MOBA_W
OFF_GA = OFF_MV + MOBA_W
OFF_GB = OFF_GA + D_MODEL
IN_WIDTH = OFF_GB + D_MODEL

LANES = 128
SUBLANES = 8
NEG = -1e30
MOBA_QSCALE = math.log2(math.e) / math.sqrt(MOBA_HD)
RET_GROUP = 4
MOBA_GROUP = 4
MOBA_PAIR = 2
FFN_CHUNK = 256
FFN_ROWS = 64
FFN_DOWN_COLS = 512
MIX_ROWS = 256
NORM_ROWS = 32
NORM_UNROLL = 4
VMEM_LIMIT = 56 * 1024 * 1024

F32 = jnp.float32
BF16 = jnp.bfloat16


def _cparams(*sem):
    return pltpu.CompilerParams(dimension_semantics=sem, vmem_limit_bytes=VMEM_LIMIT)


def _dot(a, b):
    return jnp.dot(a, b, preferred_element_type=F32)


def _dot_nt(a, b):
    return lax.dot_general(a, b, (((1,), (1,)), ((), ())), preferred_element_type=F32)


def _split_bf16(x):
    hi = x.astype(BF16)
    lo = (x - hi.astype(F32)).astype(BF16)
    return hi, lo


def _rms(x):
    return x * lax.rsqrt(jnp.mean(x * x, axis=-1, keepdims=True) + NORM_EPS)


def _silu(x):
    return x * jax.nn.sigmoid(x)


def _for_row_chunks(n_rows, fn):
    step = NORM_ROWS * NORM_UNROLL

    def body(r, carry):
        for t in range(NORM_UNROLL):
            fn(pl.ds(pl.multiple_of(r * step + t * NORM_ROWS, NORM_ROWS), NORM_ROWS))
        return carry

    lax.fori_loop(0, n_rows // step, body, 0)


def _ada_kernel(c_ref, w_ref, b_ref, o_ref):
    c = jnp.broadcast_to(_silu(c_ref[...]), (SUBLANES, c_ref.shape[1]))
    c_hi, c_lo = _split_bf16(c)
    w_hi, w_lo = _split_bf16(w_ref[...])
    r = _dot(c_hi, w_hi) + _dot(c_lo, w_hi) + _dot(c_hi, w_lo)
    o_ref[...] = r[0:1] + b_ref[...]


def _ada(c, w, b, tn=1024):
    d, n = w.shape
    return pl.pallas_call(
        _ada_kernel,
        grid=(n // tn,),
        in_specs=[pl.BlockSpec((1, d), lambda j: (0, 0)),
                  pl.BlockSpec((d, tn), lambda j: (0, j)),
                  pl.BlockSpec((1, tn), lambda j: (0, j))],
        out_specs=pl.BlockSpec((1, tn), lambda j: (0, j)),
        out_shape=jax.ShapeDtypeStruct((1, n), F32),
        compiler_params=_cparams("arbitrary"),
        name="ada",
    )(c, w, b)


def _rope_kernel(inv_ref, cos_ref, sin_ref):
    ts = cos_ref.shape[0]
    row = lax.broadcasted_iota(jnp.int32, (ts, LANES), 0) + pl.program_id(0) * ts
    lane = lax.broadcasted_iota(jnp.int32, (ts, LANES), 1)
    ang = row.astype(F32) * inv_ref[...]
    sin = jnp.sin(ang)
    cos_ref[...] = jnp.cos(ang)
    sin_ref[...] = jnp.where(lane < RET_DK // 2, -sin, sin)


def _rope_tables(s, ts=2048):
    ts = min(ts, s)
    half = RET_DK // 2
    inv = ROPE_BASE ** (-jnp.arange(half, dtype=F32) / half)
    inv2 = jnp.concatenate([inv, inv])[None, :]
    return pl.pallas_call(
        _rope_kernel,
        grid=(s // ts,),
        in_specs=[pl.BlockSpec((1, LANES), lambda i: (0, 0))],
        out_specs=[pl.BlockSpec((ts, LANES), lambda i: (i, 0))] * 2,
        out_shape=[jax.ShapeDtypeStruct((s, LANES), F32)] * 2,
        compiler_params=_cparams("arbitrary"),
        name="rope",
    )(inv2)


def _inproj_kernel(x_ref, nw_ref, sc_ref, sh_ref, cos_ref, sin_ref, w_ref, o_ref, u_ref):
    j = pl.program_id(1)
    tn = o_ref.shape[1]

    @pl.when(j == 0)
    def _():
        def piece(rows):
            y = _rms(x_ref[rows, :]) * nw_ref[...]
            u_ref[rows, :] = (y * (1.0 + sc_ref[...]) + sh_ref[...]).astype(BF16)

        _for_row_chunks(x_ref.shape[0], piece)

    r = _dot(u_ref[...], w_ref[...])
    q_tiles = RET_QK_W // tn
    is_mq = (j >= OFF_MQ // tn) & (j < OFF_MK // tn)
    o_ref[...] = (r * jnp.where(is_mq, MOBA_QSCALE, 1.0).astype(F32)).astype(BF16)

    @pl.when(j < 2 * q_tiles)
    def _():
        scale = jnp.where(j >= q_tiles, RET_DK ** -0.5, 1.0).astype(F32)
        cos = cos_ref[...]
        sin = sin_ref[...]
        for h in range(tn // RET_DK):
            xh = r[:, h * RET_DK:(h + 1) * RET_DK]
            rot = xh * cos + pltpu.roll(xh, RET_DK // 2, axis=1) * sin
            o_ref[:, h * RET_DK:(h + 1) * RET_DK] = (rot * scale).astype(BF16)


def _inproj(x, nw, sc, sh, cos2, sin2, w, tm=1024, tn=1024):
    s, d = x.shape
    n = w.shape[1]
    tm = min(tm, s)
    vec = pl.BlockSpec((1, d), lambda i, j: (0, 0))
    tab = pl.BlockSpec((tm, LANES), lambda i, j: (i, 0))
    return pl.pallas_call(
        _inproj_kernel,
        grid=(s // tm, n // tn),
        in_specs=[pl.BlockSpec((tm, d), lambda i, j: (i, 0)), vec, vec, vec, tab, tab,
                  pl.BlockSpec((d, tn), lambda i, j: (0, j))],
        out_specs=pl.BlockSpec((tm, tn), lambda i, j: (i, j)),
        out_shape=jax.ShapeDtypeStruct((s, n), BF16),
        scratch_shapes=[pltpu.VMEM((tm, d), BF16)],
        compiler_params=_cparams("arbitrary", "arbitrary"),
        name="inproj",
    )(x, nw, sc, sh, cos2, sin2, w)


def _ret_kernel(lg_ref, q_ref, k_ref, v_ref, g_ref, o_ref, r_ref, dec_ref, xi_ref, zeta_ref):
    n = pl.program_id(1)
    c = q_ref.shape[0]
    dk, dv = RET_DK, RET_DV

    @pl.when(n == 0)
    def _():
        r_ref[...] = jnp.zeros_like(r_ref)
        d = lax.broadcasted_iota(jnp.int32, (c, c), 0) - lax.broadcasted_iota(jnp.int32, (c, c), 1)
        row = lax.broadcasted_iota(jnp.int32, (c, dk), 0)
        for h in range(RET_GROUP):
            lg = lg_ref[h, 0:1, 0:1]
            dec_ref[h] = jnp.where(d >= 0, jnp.exp(jnp.maximum(d, 0).astype(F32) * lg), 0.0)
            xi_ref[h] = jnp.exp((row + 1).astype(F32) * lg)
            zeta_ref[h] = jnp.exp((c - 1 - row).astype(F32) * lg)

    for h in range(RET_GROUP):
        q = q_ref[:, h * dk:(h + 1) * dk]
        k = k_ref[:, h * dk:(h + 1) * dk]
        v = v_ref[:, h * dv:(h + 1) * dv]
        s = _dot_nt(q, k) * dec_ref[h]
        inner = _dot(s.astype(BF16), v)
        r = r_ref[h]
        cross = _dot((q.astype(F32) * xi_ref[h]).astype(BF16), r.astype(BF16))
        o = inner + cross
        kz = (k.astype(F32) * zeta_ref[h]).T.astype(BF16)
        r_ref[h] = r * jnp.exp(lg_ref[h, 0:1, 0:1] * c) + _dot(kz, v)
        oc = o - jnp.mean(o, axis=-1, keepdims=True)
        on = oc * lax.rsqrt(jnp.mean(oc * oc, axis=-1, keepdims=True) + GN_EPS)
        o_ref[:, h * dv:(h + 1) * dv] = (_silu(g_ref[:, h * dv:(h + 1) * dv].astype(F32)) * on).astype(BF16)


def _retention(proj, chunk=512):
    s = proj.shape[0]
    chunk = min(chunk, s)
    log_g = jnp.log1p(-jnp.exp2(-5.0 - jnp.arange(RET_HEADS, dtype=F32)))
    lg = jnp.broadcast_to(log_g[:, None, None], (RET_HEADS, SUBLANES, LANES))
    qk_w, v_w = RET_GROUP * RET_DK, RET_GROUP * RET_DV
    assert RET_HEADS % RET_GROUP == 0
    return pl.pallas_call(
        _ret_kernel,
        grid=(RET_HEADS // RET_GROUP, s // chunk),
        in_specs=[pl.BlockSpec((RET_GROUP, SUBLANES, LANES), lambda h, n: (h, 0, 0)),
                  pl.BlockSpec((chunk, qk_w), lambda h, n: (n, OFF_RQ // qk_w + h)),
                  pl.BlockSpec((chunk, qk_w), lambda h, n: (n, OFF_RK // qk_w + h)),
                  pl.BlockSpec((chunk, v_w), lambda h, n: (n, OFF_RV // v_w + h)),
                  pl.BlockSpec((chunk, v_w), lambda h, n: (n, OFF_RG // v_w + h))],
        out_specs=pl.BlockSpec((chunk, v_w), lambda h, n: (n, h)),
        out_shape=jax.ShapeDtypeStruct((s, RET_V_W), BF16),
        scratch_shapes=[pltpu.VMEM((RET_GROUP, RET_DK, RET_DV), F32),
                        pltpu.VMEM((RET_GROUP, chunk, chunk), F32),
                        pltpu.VMEM((RET_GROUP, chunk, RET_DK), F32),
                        pltpu.VMEM((RET_GROUP, chunk, RET_DK), F32)],
        compiler_params=_cparams("arbitrary", "arbitrary"),
        name="ret",
    )(lg, proj, proj, proj, proj)


def _moba_kernel(q_ref, k_ref, v_ref, o_ref, km_ref, oh_ref, vt_ref, qt_ref, acc_ref, m_ref, s0_ref, s1_ref):
    hp = pl.program_id(0)
    bq = pl.program_id(1)
    tq = q_ref.shape[0]
    hd = MOBA_HD
    blk = MOBA_BLOCK
    nb = k_ref.shape[0] // blk
    gw = MOBA_GROUP * blk
    vrows = vt_ref.shape[2]

    @pl.when((hp == 0) & (bq == 0))
    def _():
        lane_i = lax.broadcasted_iota(jnp.int32, (blk, LANES), 1)

        def body(n, carry):
            oh_ref[pl.ds(pl.multiple_of(n * blk, blk), blk), :] = jnp.where(lane_i == n, 1.0, 0.0).astype(BF16)
            return carry

        lax.fori_loop(0, nb, body, 0)

    @pl.when(bq == 0)
    def _():
        km_ref[...] = jnp.zeros_like(km_ref)

        def body(g, carry):
            for t in range(MOBA_GROUP):
                n = g * MOBA_GROUP + t
                rows = pl.ds(pl.multiple_of(n * blk, blk), blk)
                km_ref[pl.ds(n, 1), :] = jnp.mean(k_ref[rows, :].astype(F32), axis=0, keepdims=True)
                for h in range(MOBA_PAIR):
                    vt_ref[h, g, 0:hd, t * blk:(t + 1) * blk] = (
                        v_ref[rows, h * hd:(h + 1) * hd].astype(F32).T.astype(BF16))
            for h in range(MOBA_PAIR):
                vt_ref[h, g, hd:vrows, :] = jnp.ones((vrows - hd, gw), BF16)
            return carry

        lax.fori_loop(0, nb // MOBA_GROUP, body, 0)

    gr = km_ref.shape[0]
    row_i = lax.broadcasted_iota(jnp.int32, (gr, tq), 0)
    row_f = row_i.astype(F32)
    own = bq * (tq // blk) + lax.broadcasted_iota(jnp.int32, (gr, tq), 1) // blk
    for h in range(MOBA_PAIR):
        qt = q_ref[:, h * hd:(h + 1) * hd].astype(F32).T.astype(BF16)
        km_hi, km_lo = _split_bf16(km_ref[:, h * hd:(h + 1) * hd])
        g = jnp.where(row_i < own, _dot(km_hi, qt) + _dot(km_lo, qt), NEG)
        bias = jnp.full((gr, tq), NEG, F32)
        for _ in range(MOBA_TOPK):
            mx = jnp.max(g, axis=0, keepdims=True)
            idx = jnp.min(jnp.where(g == mx, row_f, float(gr)), axis=0, keepdims=True)
            hit = row_f == idx
            bias = jnp.where(hit, jnp.where(mx > 0.5 * NEG, 0.0, NEG), bias)
            g = jnp.where(hit, 2.0 * NEG, g)
        bias = jnp.where(row_i == own, 0.0, bias)
        qt_ref[h] = jnp.concatenate(
            [qt, bias.astype(BF16), jnp.full((LANES - gr, tq), NEG, BF16)], axis=0)
        m_ref[h] = jnp.full((1, tq), NEG, F32)
        acc_ref[h] = jnp.zeros((vrows, tq), F32)

    def scores(g, slot_ref):
        rows = pl.ds(pl.multiple_of(g * gw, gw), gw)
        oh = oh_ref[rows, :]
        for h in range(MOBA_PAIR):
            k_ext = jnp.concatenate([k_ref[rows, h * hd:(h + 1) * hd], oh], axis=1)
            slot_ref[h] = _dot(k_ext, qt_ref[h])

    def accumulate(g, slot_ref, causal):
        if causal:
            own_rows = pl.ds(pl.multiple_of((bq * tq) % gw, tq), tq)
            keep = lax.broadcasted_iota(jnp.int32, (tq, tq), 0) <= lax.broadcasted_iota(jnp.int32, (tq, tq), 1)
            for h in range(MOBA_PAIR):
                slot_ref[h, own_rows, :] = jnp.where(keep, slot_ref[h, own_rows, :], NEG)
        for h in range(MOBA_PAIR):
            s = slot_ref[h]
            m_old = m_ref[h]
            m_new = jnp.maximum(m_old, jnp.max(s, axis=0, keepdims=True))
            p = jnp.exp2(s - m_new).astype(BF16)
            acc_ref[h] = jnp.exp2(m_old - m_new) * acc_ref[h] + _dot(vt_ref[h, g], p)
            m_ref[h] = m_new

    full_groups = (bq * (tq // blk)) // MOBA_GROUP
    scores(0, s0_ref)

    def body(i, carry):
        g = 2 * i
        scores(g + 1, s1_ref)
        accumulate(g, s0_ref, False)
        scores(g + 2, s0_ref)
        accumulate(g + 1, s1_ref, False)
        return carry

    lax.fori_loop(0, full_groups // 2, body, 0)

    @pl.when(full_groups % 2 == 0)
    def _():
        accumulate(full_groups, s0_ref, True)

    @pl.when(full_groups % 2 == 1)
    def _():
        scores(full_groups, s1_ref)
        accumulate(full_groups - 1, s0_ref, False)
        accumulate(full_groups, s1_ref, True)

    for h in range(MOBA_PAIR):
        acc = acc_ref[h]
        o_ref[:, h * hd:(h + 1) * hd] = (acc[0:hd, :] / acc[hd:hd + 1, :]).T.astype(BF16)


def _moba(proj, tq=2 * MOBA_BLOCK):
    s = proj.shape[0]
    nb = s // MOBA_BLOCK
    w = MOBA_PAIR * MOBA_HD
    gw = MOBA_GROUP * MOBA_BLOCK
    bf16_rows = 2 * SUBLANES
    vrows = MOBA_HD + bf16_rows
    gate_rows = -(-nb // bf16_rows) * bf16_rows
    assert s % gw == 0 and gate_rows <= LANES and MOBA_HEADS % MOBA_PAIR == 0
    assert tq % MOBA_BLOCK == 0 and gw % tq == 0
    resident = dict(pipeline_mode=pl.Buffered(1))
    return pl.pallas_call(
        _moba_kernel,
        grid=(MOBA_HEADS // MOBA_PAIR, s // tq),
        in_specs=[pl.BlockSpec((tq, w), lambda h, b: (b, OFF_MQ // w + h)),
                  pl.BlockSpec((s, w), lambda h, b: (0, OFF_MK // w + h)),
                  pl.BlockSpec((s, w), lambda h, b: (0, OFF_MV // w + h), **resident)],
        out_specs=pl.BlockSpec((tq, w), lambda h, b: (b, h)),
        out_shape=jax.ShapeDtypeStruct((s, MOBA_W), BF16),
        scratch_shapes=[pltpu.VMEM((gate_rows, w), F32),
                        pltpu.VMEM((s, LANES), BF16),
                        pltpu.VMEM((MOBA_PAIR, s // gw, vrows, gw), BF16),
                        pltpu.VMEM((MOBA_PAIR, 2 * MOBA_HD, tq), BF16),
                        pltpu.VMEM((MOBA_PAIR, vrows, tq), F32),
                        pltpu.VMEM((MOBA_PAIR, 1, tq), F32),
                        pltpu.VMEM((MOBA_PAIR, gw, tq), F32),
                        pltpu.VMEM((MOBA_PAIR, gw, tq), F32)],
        compiler_params=_cparams("arbitrary", "arbitrary"),
        name="moba",
    )(proj, proj, proj)


def _merge_kernel(or_ref, om_ref, ga_ref, gb_ref, wr_ref, wm_ref, o_ref):
    yr = _dot(or_ref[...], wr_ref[...])
    ym = _dot(om_ref[...], wm_ref[...])
    ga = jax.nn.sigmoid(ga_ref[...].astype(F32))
    gb = jax.nn.sigmoid(gb_ref[...].astype(F32))
    o_ref[...] = (ga * yr + gb * ym).astype(BF16)


def _merge(o_ret, o_moba, proj, w_r, w_m, tm=1024, tn=512):
    s = o_ret.shape[0]
    tm = min(tm, s)
    d = D_MODEL
    return pl.pallas_call(
        _merge_kernel,
        grid=(s // tm, d // tn),
        in_specs=[pl.BlockSpec((tm, RET_V_W), lambda i, j: (i, 0)),
                  pl.BlockSpec((tm, MOBA_W), lambda i, j: (i, 0)),
                  pl.BlockSpec((tm, tn), lambda i, j: (i, OFF_GA // tn + j)),
                  pl.BlockSpec((tm, tn), lambda i, j: (i, OFF_GB // tn + j)),
                  pl.BlockSpec((RET_V_W, tn), lambda i, j: (0, j)),
                  pl.BlockSpec((MOBA_W, tn), lambda i, j: (0, j))],
        out_specs=pl.BlockSpec((tm, tn), lambda i, j: (i, j)),
        out_shape=jax.ShapeDtypeStruct((s, d), BF16),
        compiler_params=_cparams("arbitrary", "arbitrary"),
        name="merge",
    )(o_ret, o_moba, proj, proj, w_r, w_m)


def _mixout_kernel(m_ref, wo_ref, x_ref, npost_ref, g1_ref, npre_ref, sc_ref, sh_ref, x1_ref, u2_ref):
    for r0 in range(0, m_ref.shape[0], MIX_ROWS):
        rows = slice(r0, r0 + MIX_ROWS)
        y = _dot(m_ref[rows, :], wo_ref[...])
        x1 = x_ref[rows, :] + (1.0 + g1_ref[...]) * (_rms(y) * npost_ref[...])
        x1_ref[rows, :] = x1
        u2_ref[rows, :] = ((_rms(x1) * npre_ref[...]) * (1.0 + sc_ref[...]) + sh_ref[...]).astype(BF16)


def _mixout(merged, w_o, x, npost, g1, npre, sc2, sh2, tm=512):
    s, d = x.shape
    tm = min(tm, s)
    vec = pl.BlockSpec((1, d), lambda i: (0, 0))
    rows = pl.BlockSpec((tm, d), lambda i: (i, 0))
    return pl.pallas_call(
        _mixout_kernel,
        grid=(s // tm,),
        in_specs=[rows, pl.BlockSpec((d, d), lambda i: (0, 0), pipeline_mode=pl.Buffered(1)), rows,
                  vec, vec, vec, vec, vec],
        out_specs=[rows, rows],
        out_shape=[jax.ShapeDtypeStruct((s, d), F32), jax.ShapeDtypeStruct((s, d), BF16)],
        compiler_params=_cparams("arbitrary"),
        name="mixout",
    )(merged, w_o, x, npost, g1, npre, sc2, sh2)


def _ffn_kernel(u_ref, wa_ref, wb_ref, cwa_ref, cwb_ref, cba_ref, cbb_ref, wd_ref, x1_ref, npost_ref, g2_ref,
                o_ref, ha_ref, hb_ref, act_ref, tail_ref):
    i = pl.program_id(0)
    j = pl.program_id(1)
    tm = u_ref.shape[0]
    tn = wa_ref.shape[1]
    halo = SUBLANES

    @pl.when((i == 0) & (j == 0))
    def _():
        tail_ref[...] = jnp.zeros_like(tail_ref)

    @pl.when(j == 0)
    def _():
        o_ref[...] = jnp.zeros_like(o_ref)

    ha_ref[0:halo, :] = tail_ref[j, 0]
    hb_ref[0:halo, :] = tail_ref[j, 1]
    for c in range(tn // FFN_CHUNK):
        cs = slice(c * FFN_CHUNK, (c + 1) * FFN_CHUNK)
        ha_ref[halo:halo + tm, cs] = _dot(u_ref[...], wa_ref[:, cs])
        hb_ref[halo:halo + tm, cs] = _dot(u_ref[...], wb_ref[:, cs])
    tail_ref[j, 0] = ha_ref[tm:tm + halo, :]
    tail_ref[j, 1] = hb_ref[tm:tm + halo, :]

    def conv(hbuf, cw, cb, r0, cs):
        out = cb
        for t in range(CONV_WIDTH):
            out = out + hbuf[halo + r0 - t:halo + r0 - t + FFN_ROWS, cs] * cw[CONV_WIDTH - 1 - t:CONV_WIDTH - t, :]
        return out

    for c in range(tn // FFN_CHUNK):
        cs = slice(c * FFN_CHUNK, (c + 1) * FFN_CHUNK)
        cwa, cwb, cba, cbb = cwa_ref[:, cs], cwb_ref[:, cs], cba_ref[:, cs], cbb_ref[:, cs]
        for r0 in range(0, tm, FFN_ROWS):
            a = conv(ha_ref, cwa, cba, r0, cs)
            bgate = conv(hb_ref, cwb, cbb, r0, cs)
            act_ref[r0:r0 + FFN_ROWS, cs] = (_silu(a) * bgate).astype(BF16)
        for n0 in range(0, o_ref.shape[1], FFN_DOWN_COLS):
            ns = slice(n0, n0 + FFN_DOWN_COLS)
            o_ref[:, ns] += _dot(act_ref[:, cs], wd_ref[cs, ns])

    @pl.when(j == pl.num_programs(1) - 1)
    def _():
        def piece(rows):
            y = o_ref[rows, :]
            o_ref[rows, :] = x1_ref[rows, :] + (1.0 + g2_ref[...]) * (_rms(y) * npost_ref[...])

        _for_row_chunks(tm, piece)


def _ffn(u2, w_up, conv_w, conv_b, w_down, x1, npost, g2, tm=1024, tn=512):
    s, d = x1.shape
    tm = min(tm, s)
    nj = D_FF // tn
    assert D_FF % tn == 0
    vec = pl.BlockSpec((1, d), lambda i, j: (0, 0))
    rows = pl.BlockSpec((tm, d), lambda i, j: (i, 0))
    resid = pl.BlockSpec((tm, d), lambda i, j: (i, 0), pipeline_mode=pl.Buffered(1))
    return pl.pallas_call(
        _ffn_kernel,
        grid=(s // tm, nj),
        in_specs=[rows,
                  pl.BlockSpec((d, tn), lambda i, j: (0, j)),
                  pl.BlockSpec((d, tn), lambda i, j: (0, j + nj)),
                  pl.BlockSpec((CONV_WIDTH, tn), lambda i, j: (0, j)),
                  pl.BlockSpec((CONV_WIDTH, tn), lambda i, j: (0, j + nj)),
                  pl.BlockSpec((1, tn), lambda i, j: (0, j)),
                  pl.BlockSpec((1, tn), lambda i, j: (0, j + nj)),
                  pl.BlockSpec((tn, d), lambda i, j: (j, 0)),
                  resid, vec, vec],
        out_specs=rows,
        out_shape=jax.ShapeDtypeStruct((s, d), F32),
        scratch_shapes=[pltpu.VMEM((tm + SUBLANES, tn), F32),
                        pltpu.VMEM((tm + SUBLANES, tn), F32),
                        pltpu.VMEM((tm, tn), BF16),
                        pltpu.VMEM((nj, 2, SUBLANES, tn), F32)],
        compiler_params=_cparams("arbitrary", "arbitrary"),
        name="ffn",
    )(u2, w_up, w_up, conv_w, conv_w, conv_b, conv_b, w_down, x1, npost, g2)


def _block(x, c, ada_w, ada_b, mix_norm_pre, mix_norm_post, w_in, ret_w_o, moba_w_o, w_out,
           ffn_norm_pre, ffn_norm_post, ffn_w_up, ffn_conv_w, ffn_conv_b, ffn_w_down, cos2, sin2):
    d = x.shape[1]
    mod = _ada(c, ada_w, ada_b[None, :])
    sh1, sc1, g1, sh2, sc2, g2 = [mod[:, t * d:(t + 1) * d] for t in range(6)]
    proj = _inproj(x, mix_norm_pre[None, :], sc1, sh1, cos2, sin2, w_in.astype(BF16))
    o_ret = _retention(proj)
    o_moba = _moba(proj)
    merged = _merge(o_ret, o_moba, proj, ret_w_o.astype(BF16), moba_w_o.astype(BF16))
    x1, u2 = _mixout(merged, w_out.astype(BF16), x, mix_norm_post[None, :], g1,
                     ffn_norm_pre[None, :], sc2, sh2)
    return _ffn(u2, ffn_w_up.astype(BF16), ffn_conv_w, ffn_conv_b[None, :], ffn_w_down.astype(BF16),
                x1, ffn_norm_post[None, :], g2)


def kernel(x, c, ada_w, ada_b, mix_norm_pre, mix_norm_post, w_in, ret_w_o, moba_w_o, w_out,
           ffn_norm_pre, ffn_norm_post, ffn_w_up, ffn_conv_w, ffn_conv_b, ffn_w_down):
    batch, s, _ = x.shape
    depth = ada_w.shape[0]
    cos2, sin2 = _rope_tables(s)
    outs = []
    for bi in range(batch):
        xb = x[bi]
        for l in range(depth):
            xb = _block(xb, c[bi:bi + 1], ada_w[l], ada_b[l], mix_norm_pre[l], mix_norm_post[l], w_in[l],
                        ret_w_o[l], moba_w_o[l], w_out[l], ffn_norm_pre[l], ffn_norm_post[l], ffn_w_up[l],
                        ffn_conv_w[l], ffn_conv_b[l], ffn_w_down[l], cos2, sin2)
        outs.append(xb)
    return jnp.stack(outs)
```

```python
import math

import jax
import jax.numpy as jnp
from jax import lax
from jax.experimental import pallas as pl
from jax.experimental.pallas import tpu as pltpu

D_MODEL = 2048
RET_HEADS = 8
RET_DK = 128
RET_DV = 256
ROPE_BASE = 10000.0
MOBA_HEADS = 16
MOBA_HD = 128
MOBA_BLOCK = 256
MOBA_TOPK = 3
D_FF = 5632
CONV_WIDTH = 3
NORM_EPS = 1e-6
GN_EPS = 1e-5

RET_QK_W = RET_HEADS * RET_DK
RET_V_W = RET_HEADS * RET_DV
MOBA_W = MOBA_HEADS * MOBA_HD
OFF_RQ = 0
OFF_RK = OFF_RQ + RET_QK_W
OFF_RV = OFF_RK + RET_QK_W
OFF_RG = OFF_RV + RET_V_W
OFF_MQ = OFF_RG + RET_V_W
OFF_MK = OFF_MQ + MOBA_W
OFF_MV = OFF_MK + MOBA_W
OFF_GA = OFF_MV + MOBA_W
OFF_GB = OFF_GA + D_MODEL
IN_WIDTH = OFF_GB + D_MODEL

LANES = 128
SUBLANES = 8
NEG = -1e30
MOBA_QSCALE = math.log2(math.e) / math.sqrt(MOBA_HD)
RET_GROUP = 4
MOBA_GROUP = 4
MOBA_PAIR = 2
FFN_CHUNK = 256
FFN_ROWS = 64
FFN_DOWN_COLS = 512
MIX_ROWS = 256
NORM_ROWS = 32
NORM_UNROLL = 4
VMEM_LIMIT = 58 * 1024 * 1024

F32 = jnp.float32
BF16 = jnp.bfloat16


def _cparams(*sem):
    return pltpu.CompilerParams(dimension_semantics=sem, vmem_limit_bytes=VMEM_LIMIT)


def _dot(a, b):
    return jnp.dot(a, b, preferred_element_type=F32)


def _dot_nt(a, b):
    return lax.dot_general(a, b, (((1,), (1,)), ((), ())), preferred_element_type=F32)


def _split_bf16(x):
    hi = x.astype(BF16)
    lo = (x - hi.astype(F32)).astype(BF16)
    return hi, lo


def _rms(x):
    return x * lax.rsqrt(jnp.mean(x * x, axis=-1, keepdims=True) + NORM_EPS)


def _silu(x):
    return x * jax.nn.sigmoid(x)


def _for_row_chunks(n_rows, load, finish):
    step = NORM_ROWS * NORM_UNROLL

    def body(r, carry):
        pieces = [pl.ds(pl.multiple_of(r * step + t * NORM_ROWS, NORM_ROWS), NORM_ROWS)
                  for t in range(NORM_UNROLL)]
        loaded = [load(rows) for rows in pieces]
        for rows, vals in zip(pieces, loaded):
            finish(rows, vals)
        return carry

    lax.fori_loop(0, n_rows // step, body, 0)


def _ada_kernel(c_ref, w_ref, b_ref, o_ref):
    c = jnp.broadcast_to(_silu(c_ref[...]), (SUBLANES, c_ref.shape[1]))
    c_hi, c_lo = _split_bf16(c)
    w_hi, w_lo = _split_bf16(w_ref[...])
    r = _dot(c_hi, w_hi) + _dot(c_lo, w_hi) + _dot(c_hi, w_lo)
    o_ref[...] = r[0:1] + b_ref[...]


def _ada(c, w, b, tn=1024):
    d, n = w.shape
    return pl.pallas_call(
        _ada_kernel,
        grid=(n // tn,),
        in_specs=[pl.BlockSpec((1, d), lambda j: (0, 0)),
                  pl.BlockSpec((d, tn), lambda j: (0, j)),
                  pl.BlockSpec((1, tn), lambda j: (0, j))],
        out_specs=pl.BlockSpec((1, tn), lambda j: (0, j)),
        out_shape=jax.ShapeDtypeStruct((1, n), F32),
        compiler_params=_cparams("arbitrary"),
        name="ada",
    )(c, w, b)


def _rope_kernel(inv_ref, cos_ref, sin_ref):
    ts = cos_ref.shape[0]
    row = lax.broadcasted_iota(jnp.int32, (ts, LANES), 0) + pl.program_id(0) * ts
    lane = lax.broadcasted_iota(jnp.int32, (ts, LANES), 1)
    ang = row.astype(F32) * inv_ref[...]
    sin = jnp.sin(ang)
    cos_ref[...] = jnp.cos(ang)
    sin_ref[...] = jnp.where(lane < RET_DK // 2, -sin, sin)


def _rope_tables(s, ts=2048):
    ts = min(ts, s)
    half = RET_DK // 2
    inv = ROPE_BASE ** (-jnp.arange(half, dtype=F32) / half)
    inv2 = jnp.concatenate([inv, inv])[None, :]
    return pl.pallas_call(
        _rope_kernel,
        grid=(s // ts,),
        in_specs=[pl.BlockSpec((1, LANES), lambda i: (0, 0))],
        out_specs=[pl.BlockSpec((ts, LANES), lambda i: (i, 0))] * 2,
        out_shape=[jax.ShapeDtypeStruct((s, LANES), F32)] * 2,
        compiler_params=_cparams("arbitrary"),
        name="rope",
    )(inv2)


def _inproj_kernel(x_ref, nw_ref, sc_ref, sh_ref, cos_ref, sin_ref, w_ref, o_ref, u_ref):
    j = pl.program_id(1)
    tn = o_ref.shape[1]

    @pl.when(j == 0)
    def _():
        def finish(rows, x):
            y = _rms(x) * nw_ref[...]
            u_ref[rows, :] = (y * (1.0 + sc_ref[...]) + sh_ref[...]).astype(BF16)

        _for_row_chunks(x_ref.shape[0], lambda rows: x_ref[rows, :], finish)

    r = _dot(u_ref[...], w_ref[...])
    q_tiles = RET_QK_W // tn
    is_mq = (j >= OFF_MQ // tn) & (j < OFF_MK // tn)
    o_ref[...] = (r * jnp.where(is_mq, MOBA_QSCALE, 1.0).astype(F32)).astype(BF16)

    @pl.when(j < 2 * q_tiles)
    def _():
        scale = jnp.where(j >= q_tiles, RET_DK ** -0.5, 1.0).astype(F32)
        cos = cos_ref[...]
        sin = sin_ref[...]
        for h in range(tn // RET_DK):
            xh = r[:, h * RET_DK:(h + 1) * RET_DK]
            rot = xh * cos + pltpu.roll(xh, RET_DK // 2, axis=1) * sin
            o_ref[:, h * RET_DK:(h + 1) * RET_DK] = (rot * scale).astype(BF16)


def _inproj(x, nw, sc, sh, cos2, sin2, w, tm=1024, tn=1024):
    s, d = x.shape
    n = w.shape[1]
    tm = min(tm, s)
    vec = pl.BlockSpec((1, d), lambda i, j: (0, 0))
    tab = pl.BlockSpec((tm, LANES), lambda i, j: (i, 0))
    return pl.pallas_call(
        _inproj_kernel,
        grid=(s // tm, n // tn),
        in_specs=[pl.BlockSpec((tm, d), lambda i, j: (i, 0)), vec, vec, vec, tab, tab,
                  pl.BlockSpec((d, tn), lambda i, j: (0, j))],
        out_specs=pl.BlockSpec((tm, tn), lambda i, j: (i, j)),
        out_shape=jax.ShapeDtypeStruct((s, n), BF16),
        scratch_shapes=[pltpu.VMEM((tm, d), BF16)],
        compiler_params=_cparams("arbitrary", "arbitrary"),
        name="inproj",
    )(x, nw, sc, sh, cos2, sin2, w)


def _ret_kernel(lg_ref, q_ref, k_ref, v_ref, g_ref, o_ref, r_ref, dec_ref, xi_ref, zeta_ref):
    n = pl.program_id(1)
    c = q_ref.shape[0]
    dk, dv = RET_DK, RET_DV

    @pl.when(n == 0)
    def _():
        r_ref[...] = jnp.zeros_like(r_ref)
        d = lax.broadcasted_iota(jnp.int32, (c, c), 0) - lax.broadcasted_iota(jnp.int32, (c, c), 1)
        row = lax.broadcasted_iota(jnp.int32, (c, dk), 0)
        for h in range(RET_GROUP):
            lg = lg_ref[h, 0:1, 0:1]
            dec_ref[h] = jnp.where(d >= 0, jnp.exp(jnp.maximum(d, 0).astype(F32) * lg), 0.0)
            xi_ref[h] = jnp.exp((row + 1).astype(F32) * lg)
            zeta_ref[h] = jnp.exp((c - 1 - row).astype(F32) * lg)

    for h in range(RET_GROUP):
        q = q_ref[:, h * dk:(h + 1) * dk]
        k = k_ref[:, h * dk:(h + 1) * dk]
        v = v_ref[:, h * dv:(h + 1) * dv]
        s = _dot_nt(q, k) * dec_ref[h]
        inner = _dot(s.astype(BF16), v)
        r = r_ref[h]
        cross = _dot((q.astype(F32) * xi_ref[h]).astype(BF16), r.astype(BF16))
        o = inner + cross
        kz = (k.astype(F32) * zeta_ref[h]).T.astype(BF16)
        r_ref[h] = r * jnp.exp(lg_ref[h, 0:1, 0:1] * c) + _dot(kz, v)
        oc = o - jnp.mean(o, axis=-1, keepdims=True)
        on = oc * lax.rsqrt(jnp.mean(oc * oc, axis=-1, keepdims=True) + GN_EPS)
        o_ref[:, h * dv:(h + 1) * dv] = (_silu(g_ref[:, h * dv:(h + 1) * dv].astype(F32)) * on).astype(BF16)


def _retention(proj, chunk=512):
    s = proj.shape[0]
    chunk = min(chunk, s)
    log_g = jnp.log1p(-jnp.exp2(-5.0 - jnp.arange(RET_HEADS, dtype=F32)))
    lg = jnp.broadcast_to(log_g[:, None, None], (RET_HEADS, SUBLANES, LANES))
    qk_w, v_w = RET_GROUP * RET_DK, RET_GROUP * RET_DV
    assert RET_HEADS % RET_GROUP == 0
    return pl.pallas_call(
        _ret_kernel,
        grid=(RET_HEADS // RET_GROUP, s // chunk),
        in_specs=[pl.BlockSpec((RET_GROUP, SUBLANES, LANES), lambda h, n: (h, 0, 0)),
                  pl.BlockSpec((chunk, qk_w), lambda h, n: (n, OFF_RQ // qk_w + h)),
                  pl.BlockSpec((chunk, qk_w), lambda h, n: (n, OFF_RK // qk_w + h)),
                  pl.BlockSpec((chunk, v_w), lambda h, n: (n, OFF_RV // v_w + h)),
                  pl.BlockSpec((chunk, v_w), lambda h, n: (n, OFF_RG // v_w + h))],
        out_specs=pl.BlockSpec((chunk, v_w), lambda h, n: (n, h)),
        out_shape=jax.ShapeDtypeStruct((s, RET_V_W), BF16),
        scratch_shapes=[pltpu.VMEM((RET_GROUP, RET_DK, RET_DV), F32),
                        pltpu.VMEM((RET_GROUP, chunk, chunk), F32),
                        pltpu.VMEM((RET_GROUP, chunk, RET_DK), F32),
                        pltpu.VMEM((RET_GROUP, chunk, RET_DK), F32)],
        compiler_params=_cparams("arbitrary", "arbitrary"),
        name="ret",
    )(lg, proj, proj, proj, proj)


def _moba_kernel(q_ref, k_ref, v_ref, o_ref, km_ref, oh_ref, vt_ref, qt_ref, acc_ref, m_ref, s0_ref, s1_ref):
    hp = pl.program_id(0)
    bq = pl.program_id(1)
    tq = q_ref.shape[0]
    hd = MOBA_HD
    blk = MOBA_BLOCK
    nb = k_ref.shape[0] // blk
    gw = MOBA_GROUP * blk
    vrows = vt_ref.shape[2]

    @pl.when((hp == 0) & (bq == 0))
    def _():
        lane_i = lax.broadcasted_iota(jnp.int32, (blk, LANES), 1)

        def body(n, carry):
            oh_ref[pl.ds(pl.multiple_of(n * blk, blk), blk), :] = jnp.where(lane_i == n, 1.0, 0.0).astype(BF16)
            return carry

        lax.fori_loop(0, nb, body, 0)

    @pl.when(bq == 0)
    def _():
        km_ref[...] = jnp.zeros_like(km_ref)

        def body(g, carry):
            for t in range(MOBA_GROUP):
                n = g * MOBA_GROUP + t
                rows = pl.ds(pl.multiple_of(n * blk, blk), blk)
                km_ref[pl.ds(n, 1), :] = jnp.mean(k_ref[rows, :].astype(F32), axis=0, keepdims=True)
                for h in range(MOBA_PAIR):
                    vt_ref[h, g, 0:hd, t * blk:(t + 1) * blk] = (
                        v_ref[rows, h * hd:(h + 1) * hd].astype(F32).T.astype(BF16))
            for h in range(MOBA_PAIR):
                vt_ref[h, g, hd:vrows, :] = jnp.ones((vrows - hd, gw), BF16)
            return carry

        lax.fori_loop(0, nb // MOBA_GROUP, body, 0)

    gr = km_ref.shape[0]
    row_i = lax.broadcasted_iota(jnp.int32, (gr, tq), 0)
    row_f = row_i.astype(F32)
    own = bq * (tq // blk) + lax.broadcasted_iota(jnp.int32, (gr, tq), 1) // blk
    for h in range(MOBA_PAIR):
        qt = q_ref[:, h * hd:(h + 1) * hd].astype(F32).T.astype(BF16)
        km_hi, km_lo = _split_bf16(km_ref[:, h * hd:(h + 1) * hd])
        g = jnp.where(row_i < own, _dot(km_hi, qt) + _dot(km_lo, qt), NEG)
        bias = jnp.full((gr, tq), NEG, F32)
        for _ in range(MOBA_TOPK):
            mx = jnp.max(g, axis=0, keepdims=True)
            idx = jnp.min(jnp.where(g == mx, row_f, float(gr)), axis=0, keepdims=True)
            hit = row_f == idx
            bias = jnp.where(hit, jnp.where(mx > 0.5 * NEG, 0.0, NEG), bias)
            g = jnp.where(hit, 2.0 * NEG, g)
        bias = jnp.where(row_i == own, 0.0, bias)
        qt_ref[h] = jnp.concatenate(
            [qt, bias.astype(BF16), jnp.full((LANES - gr, tq), NEG, BF16)], axis=0)
        m_ref[h] = jnp.full((1, tq), NEG, F32)
        acc_ref[h] = jnp.zeros((vrows, tq), F32)

    def scores(g, slot_ref):
        rows = pl.ds(pl.multiple_of(g * gw, gw), gw)
        oh = oh_ref[rows, :]
        for h in range(MOBA_PAIR):
            k_ext = jnp.concatenate([k_ref[rows, h * hd:(h + 1) * hd], oh], axis=1)
            slot_ref[h] = _dot(k_ext, qt_ref[h])

    def accumulate(g, slot_ref, causal):
        if causal:
            own_rows = pl.ds(pl.multiple_of((bq * tq) % gw, tq), tq)
            keep = lax.broadcasted_iota(jnp.int32, (tq, tq), 0) <= lax.broadcasted_iota(jnp.int32, (tq, tq), 1)
            for h in range(MOBA_PAIR):
                slot_ref[h, own_rows, :] = jnp.where(keep, slot_ref[h, own_rows, :], NEG)
        for h in range(MOBA_PAIR):
            s = slot_ref[h]
            m_old = m_ref[h]
            m_new = jnp.maximum(m_old, jnp.max(s, axis=0, keepdims=True))
            p = jnp.exp2(s - m_new).astype(BF16)
            acc_ref[h] = jnp.exp2(m_old - m_new) * acc_ref[h] + _dot(vt_ref[h, g], p)
            m_ref[h] = m_new

    full_groups = (bq * (tq // blk)) // MOBA_GROUP
    scores(0, s0_ref)

    def body(i, carry):
        g = 2 * i
        scores(g + 1, s1_ref)
        accumulate(g, s0_ref, False)
        scores(g + 2, s0_ref)
        accumulate(g + 1, s1_ref, False)
        return carry

    lax.fori_loop(0, full_groups // 2, body, 0)

    @pl.when(full_groups % 2 == 0)
    def _():
        accumulate(full_groups, s0_ref, True)

    @pl.when(full_groups % 2 == 1)
    def _():
        scores(full_groups, s1_ref)
        accumulate(full_groups - 1, s0_ref, False)
        accumulate(full_groups, s1_ref, True)

    for h in range(MOBA_PAIR):
        acc = acc_ref[h]
        o_ref[:, h * hd:(h + 1) * hd] = (acc[0:hd, :] / acc[hd:hd + 1, :]).T.astype(BF16)


def _moba(proj, tq=2 * MOBA_BLOCK):
    s = proj.shape[0]
    nb = s // MOBA_BLOCK
    w = MOBA_PAIR * MOBA_HD
    gw = MOBA_GROUP * MOBA_BLOCK
    bf16_rows = 2 * SUBLANES
    vrows = MOBA_HD + bf16_rows
    gate_rows = -(-nb // bf16_rows) * bf16_rows
    assert s % gw == 0 and gate_rows <= LANES and MOBA_HEADS % MOBA_PAIR == 0
    assert tq % MOBA_BLOCK == 0 and gw % tq == 0
    resident = dict(pipeline_mode=pl.Buffered(1))
    return pl.pallas_call(
        _moba_kernel,
        grid=(MOBA_HEADS // MOBA_PAIR, s // tq),
        in_specs=[pl.BlockSpec((tq, w), lambda h, b: (b, OFF_MQ // w + h)),
                  pl.BlockSpec((s, w), lambda h, b: (0, OFF_MK // w + h)),
                  pl.BlockSpec((s, w), lambda h, b: (0, OFF_MV // w + h), **resident)],
        out_specs=pl.BlockSpec((tq, w), lambda h, b: (b, h)),
        out_shape=jax.ShapeDtypeStruct((s, MOBA_W), BF16),
        scratch_shapes=[pltpu.VMEM((gate_rows, w), F32),
                        pltpu.VMEM((s, LANES), BF16),
                        pltpu.VMEM((MOBA_PAIR, s // gw, vrows, gw), BF16),
                        pltpu.VMEM((MOBA_PAIR, 2 * MOBA_HD, tq), BF16),
                        pltpu.VMEM((MOBA_PAIR, vrows, tq), F32),
                        pltpu.VMEM((MOBA_PAIR, 1, tq), F32),
                        pltpu.VMEM((MOBA_PAIR, gw, tq), F32),
                        pltpu.VMEM((MOBA_PAIR, gw, tq), F32)],
        compiler_params=_cparams("arbitrary", "arbitrary"),
        name="moba",
    )(proj, proj, proj)


def _merge_kernel(or_ref, om_ref, ga_ref, gb_ref, wr_ref, wm_ref, o_ref):
    yr = _dot(or_ref[...], wr_ref[...])
    ym = _dot(om_ref[...], wm_ref[...])
    ga = jax.nn.sigmoid(ga_ref[...].astype(F32))
    gb = jax.nn.sigmoid(gb_ref[...].astype(F32))
    o_ref[...] = (ga * yr + gb * ym).astype(BF16)


def _merge(o_ret, o_moba, proj, w_r, w_m, tm=1024, tn=512):
    s = o_ret.shape[0]
    tm = min(tm, s)
    d = D_MODEL
    return pl.pallas_call(
        _merge_kernel,
        grid=(s // tm, d // tn),
        in_specs=[pl.BlockSpec((tm, RET_V_W), lambda i, j: (i, 0)),
                  pl.BlockSpec((tm, MOBA_W), lambda i, j: (i, 0)),
                  pl.BlockSpec((tm, tn), lambda i, j: (i, OFF_GA // tn + j)),
                  pl.BlockSpec((tm, tn), lambda i, j: (i, OFF_GB // tn + j)),
                  pl.BlockSpec((RET_V_W, tn), lambda i, j: (0, j)),
                  pl.BlockSpec((MOBA_W, tn), lambda i, j: (0, j))],
        out_specs=pl.BlockSpec((tm, tn), lambda i, j: (i, j)),
        out_shape=jax.ShapeDtypeStruct((s, d), BF16),
        compiler_params=_cparams("arbitrary", "arbitrary"),
        name="merge",
    )(o_ret, o_moba, proj, proj, w_r, w_m)


def _mixout_kernel(m_ref, wo_ref, x_ref, npost_ref, g1_ref, npre_ref, sc_ref, sh_ref, x1_ref, u2_ref):
    for r0 in range(0, m_ref.shape[0], MIX_ROWS):
        rows = slice(r0, r0 + MIX_ROWS)
        y = _dot(m_ref[rows, :], wo_ref[...])
        x1 = x_ref[rows, :] + (1.0 + g1_ref[...]) * (_rms(y) * npost_ref[...])
        x1_ref[rows, :] = x1
        u2_ref[rows, :] = ((_rms(x1) * npre_ref[...]) * (1.0 + sc_ref[...]) + sh_ref[...]).astype(BF16)


def _mixout(merged, w_o, x, npost, g1, npre, sc2, sh2, tm=512):
    s, d = x.shape
    tm = min(tm, s)
    vec = pl.BlockSpec((1, d), lambda i: (0, 0))
    rows = pl.BlockSpec((tm, d), lambda i: (i, 0))
    return pl.pallas_call(
        _mixout_kernel,
        grid=(s // tm,),
        in_specs=[rows, pl.BlockSpec((d, d), lambda i: (0, 0), pipeline_mode=pl.Buffered(1)), rows,
                  vec, vec, vec, vec, vec],
        out_specs=[rows, rows],
        out_shape=[jax.ShapeDtypeStruct((s, d), F32), jax.ShapeDtypeStruct((s, d), BF16)],
        compiler_params=_cparams("arbitrary"),
        name="mixout",
    )(merged, w_o, x, npost, g1, npre, sc2, sh2)


def _ffn_kernel(u_ref, wa_ref, wb_ref, cwa_ref, cwb_ref, cba_ref, cbb_ref, wd_ref, x1_hbm, npost_ref, g2_ref,
                o_ref, ha_ref, hb_ref, act_ref, tail_ref, x1_ref, x1_sem):
    i = pl.program_id(0)
    j = pl.program_id(1)
    tm = u_ref.shape[0]
    tn = wa_ref.shape[1]
    halo = SUBLANES

    def residual_copy():
        return pltpu.make_async_copy(x1_hbm.at[pl.ds(pl.multiple_of(i * tm, tm), tm), :], x1_ref, x1_sem)

    @pl.when((i == 0) & (j == 0))
    def _():
        tail_ref[...] = jnp.zeros_like(tail_ref)

    @pl.when(j == 0)
    def _():
        residual_copy().start()
        o_ref[...] = jnp.zeros_like(o_ref)

    ha_ref[0:halo, :] = tail_ref[j, 0]
    hb_ref[0:halo, :] = tail_ref[j, 1]
    for c in range(tn // FFN_CHUNK):
        cs = slice(c * FFN_CHUNK, (c + 1) * FFN_CHUNK)
        ha_ref[halo:halo + tm, cs] = _dot(u_ref[...], wa_ref[:, cs])
        hb_ref[halo:halo + tm, cs] = _dot(u_ref[...], wb_ref[:, cs])
    tail_ref[j, 0] = ha_ref[tm:tm + halo, :]
    tail_ref[j, 1] = hb_ref[tm:tm + halo, :]

    def conv(hbuf, cw, cb, r0, cs):
        out = cb
        for t in range(CONV_WIDTH):
            out = out + hbuf[halo + r0 - t:halo + r0 - t + FFN_ROWS, cs] * cw[CONV_WIDTH - 1 - t:CONV_WIDTH - t, :]
        return out

    for c in range(tn // FFN_CHUNK):
        cs = slice(c * FFN_CHUNK, (c + 1) * FFN_CHUNK)
        cwa, cwb, cba, cbb = cwa_ref[:, cs], cwb_ref[:, cs], cba_ref[:, cs], cbb_ref[:, cs]
        for r0 in range(0, tm, FFN_ROWS):
            a = conv(ha_ref, cwa, cba, r0, cs)
            bgate = conv(hb_ref, cwb, cbb, r0, cs)
            act_ref[r0:r0 + FFN_ROWS, cs] = (_silu(a) * bgate).astype(BF16)
        for n0 in range(0, o_ref.shape[1], FFN_DOWN_COLS):
            ns = slice(n0, n0 + FFN_DOWN_COLS)
            o_ref[:, ns] += _dot(act_ref[:, cs], wd_ref[cs, ns])

    @pl.when(j == pl.num_programs(1) - 1)
    def _():
        residual_copy().wait()

        def finish(rows, y):
            o_ref[rows, :] = x1_ref[rows, :] + (1.0 + g2_ref[...]) * (_rms(y) * npost_ref[...])

        _for_row_chunks(tm, lambda rows: o_ref[rows, :], finish)


def _ffn(u2, w_up, conv_w, conv_b, w_down, x1, npost, g2, tm=1024, tn=512):
    s, d = x1.shape
    tm = min(tm, s)
    nj = D_FF // tn
    assert D_FF % tn == 0
    vec = pl.BlockSpec((1, d), lambda i, j: (0, 0))
    rows = pl.BlockSpec((tm, d), lambda i, j: (i, 0))
    resid = pl.BlockSpec(memory_space=pl.ANY)
    return pl.pallas_call(
        _ffn_kernel,
        grid=(s // tm, nj),
        in_specs=[rows,
                  pl.BlockSpec((d, tn), lambda i, j: (0, j)),
                  pl.BlockSpec((d, tn), lambda i, j: (0, j + nj)),
                  pl.BlockSpec((CONV_WIDTH, tn), lambda i, j: (0, j)),
                  pl.BlockSpec((CONV_WIDTH, tn), lambda i, j: (0, j + nj)),
                  pl.BlockSpec((1, tn), lambda i, j: (0, j)),
                  pl.BlockSpec((1, tn), lambda i, j: (0, j + nj)),
                  pl.BlockSpec((tn, d), lambda i, j: (j, 0)),
                  resid, vec, vec],
        out_specs=rows,
        out_shape=jax.ShapeDtypeStruct((s, d), F32),
        scratch_shapes=[pltpu.VMEM((tm + SUBLANES, tn), F32),
                        pltpu.VMEM((tm + SUBLANES, tn), F32),
                        pltpu.VMEM((tm, tn), BF16),
                        pltpu.VMEM((nj, 2, SUBLANES, tn), F32),
                        pltpu.VMEM((tm, d), F32),
                        pltpu.SemaphoreType.DMA(())],
        compiler_params=_cparams("arbitrary", "arbitrary"),
        name="ffn",
    )(u2, w_up, w_up, conv_w, conv_w, conv_b, conv_b, w_down, x1, npost, g2)


def _block(x, c, ada_w, ada_b, mix_norm_pre, mix_norm_post, w_in, ret_w_o, moba_w_o, w_out,
           ffn_norm_pre, ffn_norm_post, ffn_w_up, ffn_conv_w, ffn_conv_b, ffn_w_down, cos2, sin2):
    d = x.shape[1]
    mod = _ada(c, ada_w, ada_b[None, :])
    sh1, sc1, g1, sh2, sc2, g2 = [mod[:, t * d:(t + 1) * d] for t in range(6)]
    proj = _inproj(x, mix_norm_pre[None, :], sc1, sh1, cos2, sin2, w_in.astype(BF16))
    o_ret = _retention(proj)
    o_moba = _moba(proj)
    merged = _merge(o_ret, o_moba, proj, ret_w_o.astype(BF16), moba_w_o.astype(BF16))
    x1, u2 = _mixout(merged, w_out.astype(BF16), x, mix_norm_post[None, :], g1,
                     ffn_norm_pre[None, :], sc2, sh2)
    return _ffn(u2, ffn_w_up.astype(BF16), ffn_conv_w, ffn_conv_b[None, :], ffn_w_down.astype(BF16),
                x1, ffn_norm_post[None, :], g2)


def kernel(x, c, ada_w, ada_b, mix_norm_pre, mix_norm_post, w_in, ret_w_o, moba_w_o, w_out,
           ffn_norm_pre, ffn_norm_post, ffn_w_up, ffn_conv_w, ffn_conv_b, ffn_w_down):
    batch, s, _ = x.shape
    depth = ada_w.shape[0]
    cos2, sin2 = _rope_tables(s)
    outs = []
    for bi in range(batch):
        xb = x[bi]
        for l in range(depth):
            xb = _block(xb, c[bi:bi + 1], ada_w[l], ada_b[l], mix_norm_pre[l], mix_norm_post[l], w_in[l],
                        ret_w_o[l], moba_w_o[l], w_out[l], ffn_norm_pre[l], ffn_norm_post[l], ffn_w_up[l],
                        ffn_conv_w[l], ffn_conv_b[l], ffn_w_down[l], cos2, sin2)
        outs.append(xb)
    return jnp.stack(outs)
```

```python
import math

import jax
import jax.numpy as jnp
from jax import lax
from jax.experimental import pallas as pl
from jax.experimental.pallas import tpu as pltpu

D_MODEL = 2048
RET_HEADS = 8
RET_DK = 128
RET_DV = 256
ROPE_BASE = 10000.0
MOBA_HEADS = 16
MOBA_HD = 128
MOBA_BLOCK = 256
MOBA_TOPK = 3
D_FF = 5632
CONV_WIDTH = 3
NORM_EPS = 1e-6
GN_EPS = 1e-5

RET_QK_W = RET_HEADS * RET_DK
RET_V_W = RET_HEADS * RET_DV
MOBA_W = MOBA_HEADS * MOBA_HD
OFF_RQ = 0
OFF_RK = OFF_RQ + RET_QK_W
OFF_RV = OFF_RK + RET_QK_W
OFF_RG = OFF_RV + RET_V_W
OFF_MQ = OFF_RG + RET_V_W
OFF_MK = OFF_MQ + MOBA_W
OFF_MV = OFF_MK + MOBA_W
OFF_GA = OFF_MV + MOBA_W
OFF_GB = OFF_GA + D_MODEL
IN_WIDTH = OFF_GB + D_MODEL

LANES = 128
SUBLANES = 8
NEG = -1e30
MOBA_QSCALE = math.log2(math.e) / math.sqrt(MOBA_HD)
RET_GROUP = 4
MOBA_GROUP = 4
MOBA_PAIR = 2
FFN_CHUNK = 256
FFN_ROWS = 64
FFN_DOWN_COLS = 512
MIX_ROWS = 256
NORM_ROWS = 32
NORM_UNROLL = 4
VMEM_LIMIT = 58 * 1024 * 1024

F32 = jnp.float32
BF16 = jnp.bfloat16


def _cparams(*sem):
    return pltpu.CompilerParams(dimension_semantics=sem, vmem_limit_bytes=VMEM_LIMIT)


def _dot(a, b):
    return jnp.dot(a, b, preferred_element_type=F32)


def _dot_nt(a, b):
    return lax.dot_general(a, b, (((1,), (1,)), ((), ())), preferred_element_type=F32)


def _split_bf16(x):
    hi = x.astype(BF16)
    lo = (x - hi.astype(F32)).astype(BF16)
    return hi, lo


def _rms(x):
    return x * lax.rsqrt(jnp.mean(x * x, axis=-1, keepdims=True) + NORM_EPS)


def _silu(x):
    return x * jax.nn.sigmoid(x)


def _for_row_chunks(n_rows, load, finish):
    step = NORM_ROWS * NORM_UNROLL

    def body(r, carry):
        pieces = [pl.ds(pl.multiple_of(r * step + t * NORM_ROWS, NORM_ROWS), NORM_ROWS)
                  for t in range(NORM_UNROLL)]
        loaded = [load(rows) for rows in pieces]
        for rows, vals in zip(pieces, loaded):
            finish(rows, vals)
        return carry

    lax.fori_loop(0, n_rows // step, body, 0)


def _ada_kernel(c_ref, w_ref, b_ref, o_ref):
    c = jnp.broadcast_to(_silu(c_ref[...]), (SUBLANES, c_ref.shape[1]))
    c_hi, c_lo = _split_bf16(c)
    w_hi, w_lo = _split_bf16(w_ref[...])
    r = _dot(c_hi, w_hi) + _dot(c_lo, w_hi) + _dot(c_hi, w_lo)
    o_ref[...] = r[0:1] + b_ref[...]


def _ada(c, w, b, tn=1024):
    d, n = w.shape
    return pl.pallas_call(
        _ada_kernel,
        grid=(n // tn,),
        in_specs=[pl.BlockSpec((1, d), lambda j: (0, 0)),
                  pl.BlockSpec((d, tn), lambda j: (0, j)),
                  pl.BlockSpec((1, tn), lambda j: (0, j))],
        out_specs=pl.BlockSpec((1, tn), lambda j: (0, j)),
        out_shape=jax.ShapeDtypeStruct((1, n), F32),
        compiler_params=_cparams("arbitrary"),
        name="ada",
    )(c, w, b)


def _rope_kernel(inv_ref, cos_ref, sin_ref):
    ts = cos_ref.shape[0]
    row = lax.broadcasted_iota(jnp.int32, (ts, LANES), 0) + pl.program_id(0) * ts
    lane = lax.broadcasted_iota(jnp.int32, (ts, LANES), 1)
    ang = row.astype(F32) * inv_ref[...]
    sin = jnp.sin(ang)
    cos_ref[...] = jnp.cos(ang)
    sin_ref[...] = jnp.where(lane < RET_DK // 2, -sin, sin)


def _rope_tables(s, ts=2048):
    ts = min(ts, s)
    half = RET_DK // 2
    inv = ROPE_BASE ** (-jnp.arange(half, dtype=F32) / half)
    inv2 = jnp.concatenate([inv, inv])[None, :]
    return pl.pallas_call(
        _rope_kernel,
        grid=(s // ts,),
        in_specs=[pl.BlockSpec((1, LANES), lambda i: (0, 0))],
        out_specs=[pl.BlockSpec((ts, LANES), lambda i: (i, 0))] * 2,
        out_shape=[jax.ShapeDtypeStruct((s, LANES), F32)] * 2,
        compiler_params=_cparams("arbitrary"),
        name="rope",
    )(inv2)


def _inproj_kernel(x_ref, nw_ref, sc_ref, sh_ref, cos_ref, sin_ref, w_ref, o_ref, u_ref):
    j = pl.program_id(1)
    tn = o_ref.shape[1]

    @pl.when(j == 0)
    def _():
        def finish(rows, x):
            y = _rms(x) * nw_ref[...]
            u_ref[rows, :] = (y * (1.0 + sc_ref[...]) + sh_ref[...]).astype(BF16)

        _for_row_chunks(x_ref.shape[0], lambda rows: x_ref[rows, :], finish)

    r = _dot(u_ref[...], w_ref[...])
    q_tiles = RET_QK_W // tn
    is_mq = (j >= OFF_MQ // tn) & (j < OFF_MK // tn)
    o_ref[...] = (r * jnp.where(is_mq, MOBA_QSCALE, 1.0).astype(F32)).astype(BF16)

    @pl.when(j < 2 * q_tiles)
    def _():
        scale = jnp.where(j >= q_tiles, RET_DK ** -0.5, 1.0).astype(F32)
        cos = cos_ref[...]
        sin = sin_ref[...]
        for h in range(tn // RET_DK):
            xh = r[:, h * RET_DK:(h + 1) * RET_DK]
            rot = xh * cos + pltpu.roll(xh, RET_DK // 2, axis=1) * sin
            o_ref[:, h * RET_DK:(h + 1) * RET_DK] = (rot * scale).astype(BF16)


def _inproj(x, nw, sc, sh, cos2, sin2, w, tm=1024, tn=1024):
    s, d = x.shape
    n = w.shape[1]
    tm = min(tm, s)
    vec = pl.BlockSpec((1, d), lambda i, j: (0, 0))
    tab = pl.BlockSpec((tm, LANES), lambda i, j: (i, 0))
    return pl.pallas_call(
        _inproj_kernel,
        grid=(s // tm, n // tn),
        in_specs=[pl.BlockSpec((tm, d), lambda i, j: (i, 0)), vec, vec, vec, tab, tab,
                  pl.BlockSpec((d, tn), lambda i, j: (0, j))],
        out_specs=pl.BlockSpec((tm, tn), lambda i, j: (i, j)),
        out_shape=jax.ShapeDtypeStruct((s, n), BF16),
        scratch_shapes=[pltpu.VMEM((tm, d), BF16)],
        compiler_params=_cparams("arbitrary", "arbitrary"),
        name="inproj",
    )(x, nw, sc, sh, cos2, sin2, w)


def _ret_kernel(lg_ref, q_ref, k_ref, v_ref, g_ref, o_ref, r_ref, dec_ref, xi_ref, zeta_ref):
    n = pl.program_id(1)
    c = q_ref.shape[0]
    dk, dv = RET_DK, RET_DV

    @pl.when(n == 0)
    def _():
        r_ref[...] = jnp.zeros_like(r_ref)
        d = lax.broadcasted_iota(jnp.int32, (c, c), 0) - lax.broadcasted_iota(jnp.int32, (c, c), 1)
        row = lax.broadcasted_iota(jnp.int32, (c, dk), 0)
        for h in range(RET_GROUP):
            lg = lg_ref[h, 0:1, 0:1]
            dec_ref[h] = jnp.where(d >= 0, jnp.exp(jnp.maximum(d, 0).astype(F32) * lg), 0.0)
            xi_ref[h] = jnp.exp((row + 1).astype(F32) * lg)
            zeta_ref[h] = jnp.exp((c - 1 - row).astype(F32) * lg)

    for h in range(RET_GROUP):
        q = q_ref[:, h * dk:(h + 1) * dk]
        k = k_ref[:, h * dk:(h + 1) * dk]
        v = v_ref[:, h * dv:(h + 1) * dv]
        s = _dot_nt(q, k) * dec_ref[h]
        inner = _dot(s.astype(BF16), v)
        r = r_ref[h]
        cross = _dot((q.astype(F32) * xi_ref[h]).astype(BF16), r.astype(BF16))
        o = inner + cross
        kz = (k.astype(F32) * zeta_ref[h]).T.astype(BF16)
        r_ref[h] = r * jnp.exp(lg_ref[h, 0:1, 0:1] * c) + _dot(kz, v)
        oc = o - jnp.mean(o, axis=-1, keepdims=True)
        on = oc * lax.rsqrt(jnp.mean(oc * oc, axis=-1, keepdims=True) + GN_EPS)
        o_ref[:, h * dv:(h + 1) * dv] = (_silu(g_ref[:, h * dv:(h + 1) * dv].astype(F32)) * on).astype(BF16)


def _retention(proj, chunk=512):
    s = proj.shape[0]
    chunk = min(chunk, s)
    log_g = jnp.log1p(-jnp.exp2(-5.0 - jnp.arange(RET_HEADS, dtype=F32)))
    lg = jnp.broadcast_to(log_g[:, None, None], (RET_HEADS, SUBLANES, LANES))
    qk_w, v_w = RET_GROUP * RET_DK, RET_GROUP * RET_DV
    assert RET_HEADS % RET_GROUP == 0
    return pl.pallas_call(
        _ret_kernel,
        grid=(RET_HEADS // RET_GROUP, s // chunk),
        in_specs=[pl.BlockSpec((RET_GROUP, SUBLANES, LANES), lambda h, n: (h, 0, 0)),
                  pl.BlockSpec((chunk, qk_w), lambda h, n: (n, OFF_RQ // qk_w + h)),
                  pl.BlockSpec((chunk, qk_w), lambda h, n: (n, OFF_RK // qk_w + h)),
                  pl.BlockSpec((chunk, v_w), lambda h, n: (n, OFF_RV // v_w + h)),
                  pl.BlockSpec((chunk, v_w), lambda h, n: (n, OFF_RG // v_w + h))],
        out_specs=pl.BlockSpec((chunk, v_w), lambda h, n: (n, h)),
        out_shape=jax.ShapeDtypeStruct((s, RET_V_W), BF16),
        scratch_shapes=[pltpu.VMEM((RET_GROUP, RET_DK, RET_DV), F32),
                        pltpu.VMEM((RET_GROUP, chunk, chunk), F32),
                        pltpu.VMEM((RET_GROUP, chunk, RET_DK), F32),
                        pltpu.VMEM((RET_GROUP, chunk, RET_DK), F32)],
        compiler_params=_cparams("arbitrary", "arbitrary"),
        name="ret",
    )(lg, proj, proj, proj, proj)


def _moba_kernel(q_ref, k_ref, v_hbm, o_ref, km_ref, oh_ref, vt_ref, qt_ref, acc_ref, m_ref, s0_ref, s1_ref,
                 v_ref, v_sem):
    hp = pl.program_id(0)
    bq = pl.program_id(1)
    tq = q_ref.shape[0]
    hd = MOBA_HD
    blk = MOBA_BLOCK
    nb = k_ref.shape[0] // blk
    gw = MOBA_GROUP * blk
    vrows = vt_ref.shape[2]

    @pl.when((hp == 0) & (bq == 0))
    def _():
        lane_i = lax.broadcasted_iota(jnp.int32, (blk, LANES), 1)

        def body(n, carry):
            oh_ref[pl.ds(pl.multiple_of(n * blk, blk), blk), :] = jnp.where(lane_i == n, 1.0, 0.0).astype(BF16)
            return carry

        lax.fori_loop(0, nb, body, 0)

    def value_copy(pair):
        cols = pl.ds(pl.multiple_of(OFF_MV + pair * (MOBA_PAIR * hd), MOBA_PAIR * hd), MOBA_PAIR * hd)
        return pltpu.make_async_copy(v_hbm.at[:, cols], v_ref, v_sem)

    @pl.when((hp == 0) & (bq == 0))
    def _():
        value_copy(0).start()

    @pl.when(bq == 0)
    def _():
        value_copy(hp).wait()
        km_ref[...] = jnp.zeros_like(km_ref)

        def body(g, carry):
            for t in range(MOBA_GROUP):
                n = g * MOBA_GROUP + t
                rows = pl.ds(pl.multiple_of(n * blk, blk), blk)
                km_ref[pl.ds(n, 1), :] = jnp.mean(k_ref[rows, :].astype(F32), axis=0, keepdims=True)
                for h in range(MOBA_PAIR):
                    vt_ref[h, g, 0:hd, t * blk:(t + 1) * blk] = (
                        v_ref[rows, h * hd:(h + 1) * hd].astype(F32).T.astype(BF16))
            for h in range(MOBA_PAIR):
                vt_ref[h, g, hd:vrows, :] = jnp.ones((vrows - hd, gw), BF16)
            return carry

        lax.fori_loop(0, nb // MOBA_GROUP, body, 0)

        @pl.when(hp + 1 < pl.num_programs(0))
        def _():
            value_copy(hp + 1).start()

    gr = km_ref.shape[0]
    row_i = lax.broadcasted_iota(jnp.int32, (gr, tq), 0)
    row_f = row_i.astype(F32)
    own = bq * (tq // blk) + lax.broadcasted_iota(jnp.int32, (gr, tq), 1) // blk
    for h in range(MOBA_PAIR):
        qt = q_ref[:, h * hd:(h + 1) * hd].astype(F32).T.astype(BF16)
        km_hi, km_lo = _split_bf16(km_ref[:, h * hd:(h + 1) * hd])
        g = jnp.where(row_i < own, _dot(km_hi, qt) + _dot(km_lo, qt), NEG)
        bias = jnp.full((gr, tq), NEG, F32)
        for _ in range(MOBA_TOPK):
            mx = jnp.max(g, axis=0, keepdims=True)
            idx = jnp.min(jnp.where(g == mx, row_f, float(gr)), axis=0, keepdims=True)
            hit = row_f == idx
            bias = jnp.where(hit, jnp.where(mx > 0.5 * NEG, 0.0, NEG), bias)
            g = jnp.where(hit, 2.0 * NEG, g)
        bias = jnp.where(row_i == own, 0.0, bias)
        qt_ref[h] = jnp.concatenate(
            [qt, bias.astype(BF16), jnp.full((LANES - gr, tq), NEG, BF16)], axis=0)
        m_ref[h] = jnp.full((1, tq), NEG, F32)
        acc_ref[h] = jnp.zeros((vrows, tq), F32)

    def scores(g, slot_ref):
        rows = pl.ds(pl.multiple_of(g * gw, gw), gw)
        oh = oh_ref[rows, :]
        for h in range(MOBA_PAIR):
            k_ext = jnp.concatenate([k_ref[rows, h * hd:(h + 1) * hd], oh], axis=1)
            slot_ref[h] = _dot(k_ext, qt_ref[h])

    def accumulate(g, slot_ref, causal):
        if causal:
            own_rows = pl.ds(pl.multiple_of((bq * tq) % gw, tq), tq)
            keep = lax.broadcasted_iota(jnp.int32, (tq, tq), 0) <= lax.broadcasted_iota(jnp.int32, (tq, tq), 1)
            for h in range(MOBA_PAIR):
                slot_ref[h, own_rows, :] = jnp.where(keep, slot_ref[h, own_rows, :], NEG)
        for h in range(MOBA_PAIR):
            s = slot_ref[h]
            m_old = m_ref[h]
            m_new = jnp.maximum(m_old, jnp.max(s, axis=0, keepdims=True))
            p = jnp.exp2(s - m_new).astype(BF16)
            acc_ref[h] = jnp.exp2(m_old - m_new) * acc_ref[h] + _dot(vt_ref[h, g], p)
            m_ref[h] = m_new

    full_groups = (bq * (tq // blk)) // MOBA_GROUP
    scores(0, s0_ref)

    def body(i, carry):
        g = 2 * i
        scores(g + 1, s1_ref)
        accumulate(g, s0_ref, False)
        scores(g + 2, s0_ref)
        accumulate(g + 1, s1_ref, False)
        return carry

    lax.fori_loop(0, full_groups // 2, body, 0)

    @pl.when(full_groups % 2 == 0)
    def _():
        accumulate(full_groups, s0_ref, True)

    @pl.when(full_groups % 2 == 1)
    def _():
        scores(full_groups, s1_ref)
        accumulate(full_groups - 1, s0_ref, False)
        accumulate(full_groups, s1_ref, True)

    for h in range(MOBA_PAIR):
        acc = acc_ref[h]
        o_ref[:, h * hd:(h + 1) * hd] = (acc[0:hd, :] / acc[hd:hd + 1, :]).T.astype(BF16)


def _moba(proj, tq=2 * MOBA_BLOCK):
    s = proj.shape[0]
    nb = s // MOBA_BLOCK
    w = MOBA_PAIR * MOBA_HD
    gw = MOBA_GROUP * MOBA_BLOCK
    bf16_rows = 2 * SUBLANES
    vrows = MOBA_HD + bf16_rows
    gate_rows = -(-nb // bf16_rows) * bf16_rows
    assert s % gw == 0 and gate_rows <= LANES and MOBA_HEADS % MOBA_PAIR == 0
    assert tq % MOBA_BLOCK == 0 and gw % tq == 0
    return pl.pallas_call(
        _moba_kernel,
        grid=(MOBA_HEADS // MOBA_PAIR, s // tq),
        in_specs=[pl.BlockSpec((tq, w), lambda h, b: (b, OFF_MQ // w + h)),
                  pl.BlockSpec((s, w), lambda h, b: (0, OFF_MK // w + h)),
                  pl.BlockSpec(memory_space=pl.ANY)],
        out_specs=pl.BlockSpec((tq, w), lambda h, b: (b, h)),
        out_shape=jax.ShapeDtypeStruct((s, MOBA_W), BF16),
        scratch_shapes=[pltpu.VMEM((gate_rows, w), F32),
                        pltpu.VMEM((s, LANES), BF16),
                        pltpu.VMEM((MOBA_PAIR, s // gw, vrows, gw), BF16),
                        pltpu.VMEM((MOBA_PAIR, 2 * MOBA_HD, tq), BF16),
                        pltpu.VMEM((MOBA_PAIR, vrows, tq), F32),
                        pltpu.VMEM((MOBA_PAIR, 1, tq), F32),
                        pltpu.VMEM((MOBA_PAIR, gw, tq), F32),
                        pltpu.VMEM((MOBA_PAIR, gw, tq), F32),
                        pltpu.VMEM((s, w), BF16),
                        pltpu.SemaphoreType.DMA(())],
        compiler_params=_cparams("arbitrary", "arbitrary"),
        name="moba",
    )(proj, proj, proj)


def _merge_kernel(or_ref, om_ref, ga_ref, gb_ref, wr_ref, wm_ref, o_ref):
    yr = _dot(or_ref[...], wr_ref[...])
    ym = _dot(om_ref[...], wm_ref[...])
    ga = jax.nn.sigmoid(ga_ref[...].astype(F32))
    gb = jax.nn.sigmoid(gb_ref[...].astype(F32))
    o_ref[...] = (ga * yr + gb * ym).astype(BF16)


def _merge(o_ret, o_moba, proj, w_r, w_m, tm=1024, tn=512):
    s = o_ret.shape[0]
    tm = min(tm, s)
    d = D_MODEL
    return pl.pallas_call(
        _merge_kernel,
        grid=(s // tm, d // tn),
        in_specs=[pl.BlockSpec((tm, RET_V_W), lambda i, j: (i, 0)),
                  pl.BlockSpec((tm, MOBA_W), lambda i, j: (i, 0)),
                  pl.BlockSpec((tm, tn), lambda i, j: (i, OFF_GA // tn + j)),
                  pl.BlockSpec((tm, tn), lambda i, j: (i, OFF_GB // tn + j)),
                  pl.BlockSpec((RET_V_W, tn), lambda i, j: (0, j)),
                  pl.BlockSpec((MOBA_W, tn), lambda i, j: (0, j))],
        out_specs=pl.BlockSpec((tm, tn), lambda i, j: (i, j)),
        out_shape=jax.ShapeDtypeStruct((s, d), BF16),
        compiler_params=_cparams("arbitrary", "arbitrary"),
        name="merge",
    )(o_ret, o_moba, proj, proj, w_r, w_m)


def _mixout_kernel(m_ref, wo_ref, x_ref, npost_ref, g1_ref, npre_ref, sc_ref, sh_ref, x1_ref, u2_ref):
    for r0 in range(0, m_ref.shape[0], MIX_ROWS):
        rows = slice(r0, r0 + MIX_ROWS)
        y = _dot(m_ref[rows, :], wo_ref[...])
        x1 = x_ref[rows, :] + (1.0 + g1_ref[...]) * (_rms(y) * npost_ref[...])
        x1_ref[rows, :] = x1
        u2_ref[rows, :] = ((_rms(x1) * npre_ref[...]) * (1.0 + sc_ref[...]) + sh_ref[...]).astype(BF16)


def _mixout(merged, w_o, x, npost, g1, npre, sc2, sh2, tm=512):
    s, d = x.shape
    tm = min(tm, s)
    vec = pl.BlockSpec((1, d), lambda i: (0, 0))
    rows = pl.BlockSpec((tm, d), lambda i: (i, 0))
    return pl.pallas_call(
        _mixout_kernel,
        grid=(s // tm,),
        in_specs=[rows, pl.BlockSpec((d, d), lambda i: (0, 0), pipeline_mode=pl.Buffered(1)), rows,
                  vec, vec, vec, vec, vec],
        out_specs=[rows, rows],
        out_shape=[jax.ShapeDtypeStruct((s, d), F32), jax.ShapeDtypeStruct((s, d), BF16)],
        compiler_params=_cparams("arbitrary"),
        name="mixout",
    )(merged, w_o, x, npost, g1, npre, sc2, sh2)


def _ffn_kernel(u_ref, wa_ref, wb_ref, cwa_ref, cwb_ref, cba_ref, cbb_ref, wd_ref, x1_hbm, npost_ref, g2_ref,
                o_ref, ha_ref, hb_ref, act_ref, tail_ref, x1_ref, x1_sem):
    i = pl.program_id(0)
    j = pl.program_id(1)
    tm = u_ref.shape[0]
    tn = wa_ref.shape[1]
    halo = SUBLANES

    def residual_copy():
        return pltpu.make_async_copy(x1_hbm.at[pl.ds(pl.multiple_of(i * tm, tm), tm), :], x1_ref, x1_sem)

    @pl.when((i == 0) & (j == 0))
    def _():
        tail_ref[...] = jnp.zeros_like(tail_ref)

    @pl.when(j == 0)
    def _():
        residual_copy().start()
        o_ref[...] = jnp.zeros_like(o_ref)

    ha_ref[0:halo, :] = tail_ref[j, 0]
    hb_ref[0:halo, :] = tail_ref[j, 1]
    for c in range(tn // FFN_CHUNK):
        cs = slice(c * FFN_CHUNK, (c + 1) * FFN_CHUNK)
        ha_ref[halo:halo + tm, cs] = _dot(u_ref[...], wa_ref[:, cs])
        hb_ref[halo:halo + tm, cs] = _dot(u_ref[...], wb_ref[:, cs])
    tail_ref[j, 0] = ha_ref[tm:tm + halo, :]
    tail_ref[j, 1] = hb_ref[tm:tm + halo, :]

    def conv(hbuf, cw, cb, r0, cs):
        out = cb
        for t in range(CONV_WIDTH):
            out = out + hbuf[halo + r0 - t:halo + r0 - t + FFN_ROWS, cs] * cw[CONV_WIDTH - 1 - t:CONV_WIDTH - t, :]
        return out

    for c in range(tn // FFN_CHUNK):
        cs = slice(c * FFN_CHUNK, (c + 1) * FFN_CHUNK)
        cwa, cwb, cba, cbb = cwa_ref[:, cs], cwb_ref[:, cs], cba_ref[:, cs], cbb_ref[:, cs]
        for r0 in range(0, tm, FFN_ROWS):
            a = conv(ha_ref, cwa, cba, r0, cs)
            bgate = conv(hb_ref, cwb, cbb, r0, cs)
            act_ref[r0:r0 + FFN_ROWS, cs] = (_silu(a) * bgate).astype(BF16)
        for n0 in range(0, o_ref.shape[1], FFN_DOWN_COLS):
            ns = slice(n0, n0 + FFN_DOWN_COLS)
            o_ref[:, ns] += _dot(act_ref[:, cs], wd_ref[cs, ns])

    @pl.when(j == pl.num_programs(1) - 1)
    def _():
        residual_copy().wait()

        def finish(rows, y):
            o_ref[rows, :] = x1_ref[rows, :] + (1.0 + g2_ref[...]) * (_rms(y) * npost_ref[...])

        _for_row_chunks(tm, lambda rows: o_ref[rows, :], finish)


def _ffn(u2, w_up, conv_w, conv_b, w_down, x1, npost, g2, tm=1024, tn=512):
    s, d = x1.shape
    tm = min(tm, s)
    nj = D_FF // tn
    assert D_FF % tn == 0
    vec = pl.BlockSpec((1, d), lambda i, j: (0, 0))
    rows = pl.BlockSpec((tm, d), lambda i, j: (i, 0))
    resid = pl.BlockSpec(memory_space=pl.ANY)
    return pl.pallas_call(
        _ffn_kernel,
        grid=(s // tm, nj),
        in_specs=[rows,
                  pl.BlockSpec((d, tn), lambda i, j: (0, j)),
                  pl.BlockSpec((d, tn), lambda i, j: (0, j + nj)),
                  pl.BlockSpec((CONV_WIDTH, tn), lambda i, j: (0, j)),
                  pl.BlockSpec((CONV_WIDTH, tn), lambda i, j: (0, j + nj)),
                  pl.BlockSpec((1, tn), lambda i, j: (0, j)),
                  pl.BlockSpec((1, tn), lambda i, j: (0, j + nj)),
                  pl.BlockSpec((tn, d), lambda i, j: (j, 0)),
                  resid, vec, vec],
        out_specs=rows,
        out_shape=jax.ShapeDtypeStruct((s, d), F32),
        scratch_shapes=[pltpu.VMEM((tm + SUBLANES, tn), F32),
                        pltpu.VMEM((tm + SUBLANES, tn), F32),
                        pltpu.VMEM((tm, tn), BF16),
                        pltpu.VMEM((nj, 2, SUBLANES, tn), F32),
                        pltpu.VMEM((tm, d), F32),
                        pltpu.SemaphoreType.DMA(())],
        compiler_params=_cparams("arbitrary", "arbitrary"),
        name="ffn",
    )(u2, w_up, w_up, conv_w, conv_w, conv_b, conv_b, w_down, x1, npost, g2)


def _block(x, c, ada_w, ada_b, mix_norm_pre, mix_norm_post, w_in, ret_w_o, moba_w_o, w_out,
           ffn_norm_pre, ffn_norm_post, ffn_w_up, ffn_conv_w, ffn_conv_b, ffn_w_down, cos2, sin2):
    d = x.shape[1]
    mod = _ada(c, ada_w, ada_b[None, :])
    sh1, sc1, g1, sh2, sc2, g2 = [mod[:, t * d:(t + 1) * d] for t in range(6)]
    proj = _inproj(x, mix_norm_pre[None, :], sc1, sh1, cos2, sin2, w_in.astype(BF16))
    o_ret = _retention(proj)
    o_moba = _moba(proj)
    merged = _merge(o_ret, o_moba, proj, ret_w_o.astype(BF16), moba_w_o.astype(BF16))
    x1, u2 = _mixout(merged, w_out.astype(BF16), x, mix_norm_post[None, :], g1,
                     ffn_norm_pre[None, :], sc2, sh2)
    return _ffn(u2, ffn_w_up.astype(BF16), ffn_conv_w, ffn_conv_b[None, :], ffn_w_down.astype(BF16),
                x1, ffn_norm_post[None, :], g2)


def kernel(x, c, ada_w, ada_b, mix_norm_pre, mix_norm_post, w_in, ret_w_o, moba_w_o, w_out,
           ffn_norm_pre, ffn_norm_post, ffn_w_up, ffn_conv_w, ffn_conv_b, ffn_w_down):
    batch, s, _ = x.shape
    depth = ada_w.shape[0]
    cos2, sin2 = _rope_tables(s)
    outs = []
    for bi in range(batch):
        xb = x[bi]
        for l in range(depth):
            xb = _block(xb, c[bi:bi + 1], ada_w[l], ada_b[l], mix_norm_pre[l], mix_norm_post[l], w_in[l],
                        ret_w_o[l], moba_w_o[l], w_out[l], ffn_norm_pre[l], ffn_norm_post[l], ffn_w_up[l],
                        ffn_conv_w[l], ffn_conv_b[l], ffn_w_down[l], cos2, sin2)
        outs.append(xb)
    return jnp.stack(outs)
```

```python
import math

import jax
import jax.numpy as jnp
from jax import lax
from jax.experimental import pallas as pl
from jax.experimental.pallas import tpu as pltpu

D_MODEL = 2048
RET_HEADS = 8
RET_DK = 128
RET_DV = 256
ROPE_BASE = 10000.0
MOBA_HEADS = 16
MOBA_HD = 128
MOBA_BLOCK = 256
MOBA_TOPK = 3
D_FF = 5632
CONV_WIDTH = 3
NORM_EPS = 1e-6
GN_EPS = 1e-5

RET_QK_W = RET_HEADS * RET_DK
RET_V_W = RET_HEADS * RET_DV
MOBA_W = MOBA_HEADS * MOBA_HD
OFF_RQ = 0
OFF_RK = OFF_RQ + RET_QK_W
OFF_RV = OFF_RK + RET_QK_W
OFF_RG = OFF_RV + RET_V_W
OFF_MQ = OFF_RG + RET_V_W
OFF_MK = OFF_MQ + MOBA_W
OFF_MV = OFF_MK + MOBA_W
OFF_GA = OFF_MV + MOBA_W
OFF_GB = OFF_GA + D_MODEL
IN_WIDTH = OFF_GB + D_MODEL

LANES = 128
SUBLANES = 8
NEG = -1e30
MOBA_QSCALE = math.log2(math.e) / math.sqrt(MOBA_HD)
RET_GROUP = 4
MOBA_GROUP = 2
MOBA_SLOTS = 4
MOBA_PAIR = 2
FFN_CHUNK = 256
FFN_ROWS = 64
FFN_DOWN_COLS = 512
MIX_ROWS = 256
NORM_ROWS = 32
NORM_UNROLL = 4
VMEM_LIMIT = 58 * 1024 * 1024

F32 = jnp.float32
BF16 = jnp.bfloat16


def _cparams(*sem):
    return pltpu.CompilerParams(dimension_semantics=sem, vmem_limit_bytes=VMEM_LIMIT)


def _dot(a, b):
    return jnp.dot(a, b, preferred_element_type=F32)


def _dot_nt(a, b):
    return lax.dot_general(a, b, (((1,), (1,)), ((), ())), preferred_element_type=F32)


def _split_bf16(x):
    hi = x.astype(BF16)
    lo = (x - hi.astype(F32)).astype(BF16)
    return hi, lo


def _rms(x):
    return x * lax.rsqrt(jnp.mean(x * x, axis=-1, keepdims=True) + NORM_EPS)


def _silu(x):
    return x * jax.nn.sigmoid(x)


def _for_row_chunks(n_rows, load, finish):
    step = NORM_ROWS * NORM_UNROLL

    def body(r, carry):
        pieces = [pl.ds(pl.multiple_of(r * step + t * NORM_ROWS, NORM_ROWS), NORM_ROWS)
                  for t in range(NORM_UNROLL)]
        loaded = [load(rows) for rows in pieces]
        for rows, vals in zip(pieces, loaded):
            finish(rows, vals)
        return carry

    lax.fori_loop(0, n_rows // step, body, 0)


def _ada_kernel(c_ref, w_ref, b_ref, o_ref):
    c = jnp.broadcast_to(_silu(c_ref[...]), (SUBLANES, c_ref.shape[1]))
    c_hi, c_lo = _split_bf16(c)
    w_hi, w_lo = _split_bf16(w_ref[...])
    r = _dot(c_hi, w_hi) + _dot(c_lo, w_hi) + _dot(c_hi, w_lo)
    o_ref[...] = r[0:1] + b_ref[...]


def _ada(c, w, b, tn=1024):
    d, n = w.shape
    return pl.pallas_call(
        _ada_kernel,
        grid=(n // tn,),
        in_specs=[pl.BlockSpec((1, d), lambda j: (0, 0)),
                  pl.BlockSpec((d, tn), lambda j: (0, j)),
                  pl.BlockSpec((1, tn), lambda j: (0, j))],
        out_specs=pl.BlockSpec((1, tn), lambda j: (0, j)),
        out_shape=jax.ShapeDtypeStruct((1, n), F32),
        compiler_params=_cparams("arbitrary"),
        name="ada",
    )(c, w, b)


def _rope_kernel(inv_ref, cos_ref, sin_ref):
    ts = cos_ref.shape[0]
    row = lax.broadcasted_iota(jnp.int32, (ts, LANES), 0) + pl.program_id(0) * ts
    lane = lax.broadcasted_iota(jnp.int32, (ts, LANES), 1)
    ang = row.astype(F32) * inv_ref[...]
    sin = jnp.sin(ang)
    cos_ref[...] = jnp.cos(ang)
    sin_ref[...] = jnp.where(lane < RET_DK // 2, -sin, sin)


def _rope_tables(s, ts=2048):
    ts = min(ts, s)
    half = RET_DK // 2
    inv = ROPE_BASE ** (-jnp.arange(half, dtype=F32) / half)
    inv2 = jnp.concatenate([inv, inv])[None, :]
    return pl.pallas_call(
        _rope_kernel,
        grid=(s // ts,),
        in_specs=[pl.BlockSpec((1, LANES), lambda i: (0, 0))],
        out_specs=[pl.BlockSpec((ts, LANES), lambda i: (i, 0))] * 2,
        out_shape=[jax.ShapeDtypeStruct((s, LANES), F32)] * 2,
        compiler_params=_cparams("arbitrary"),
        name="rope",
    )(inv2)


def _inproj_kernel(x_ref, nw_ref, sc_ref, sh_ref, cos_ref, sin_ref, w_ref, o_ref, u_ref):
    j = pl.program_id(1)
    tn = o_ref.shape[1]

    @pl.when(j == 0)
    def _():
        def finish(rows, x):
            y = _rms(x) * nw_ref[...]
            u_ref[rows, :] = (y * (1.0 + sc_ref[...]) + sh_ref[...]).astype(BF16)

        _for_row_chunks(x_ref.shape[0], lambda rows: x_ref[rows, :], finish)

    r = _dot(u_ref[...], w_ref[...])
    q_tiles = RET_QK_W // tn
    is_mq = (j >= OFF_MQ // tn) & (j < OFF_MK // tn)
    o_ref[...] = (r * jnp.where(is_mq, MOBA_QSCALE, 1.0).astype(F32)).astype(BF16)

    @pl.when(j < 2 * q_tiles)
    def _():
        scale = jnp.where(j >= q_tiles, RET_DK ** -0.5, 1.0).astype(F32)
        cos = cos_ref[...]
        sin = sin_ref[...]
        for h in range(tn // RET_DK):
            xh = r[:, h * RET_DK:(h + 1) * RET_DK]
            rot = xh * cos + pltpu.roll(xh, RET_DK // 2, axis=1) * sin
            o_ref[:, h * RET_DK:(h + 1) * RET_DK] = (rot * scale).astype(BF16)


def _inproj(x, nw, sc, sh, cos2, sin2, w, tm=1024, tn=1024):
    s, d = x.shape
    n = w.shape[1]
    tm = min(tm, s)
    vec = pl.BlockSpec((1, d), lambda i, j: (0, 0))
    tab = pl.BlockSpec((tm, LANES), lambda i, j: (i, 0))
    return pl.pallas_call(
        _inproj_kernel,
        grid=(s // tm, n // tn),
        in_specs=[pl.BlockSpec((tm, d), lambda i, j: (i, 0)), vec, vec, vec, tab, tab,
                  pl.BlockSpec((d, tn), lambda i, j: (0, j))],
        out_specs=pl.BlockSpec((tm, tn), lambda i, j: (i, j)),
        out_shape=jax.ShapeDtypeStruct((s, n), BF16),
        scratch_shapes=[pltpu.VMEM((tm, d), BF16)],
        compiler_params=_cparams("arbitrary", "arbitrary"),
        name="inproj",
    )(x, nw, sc, sh, cos2, sin2, w)


def _ret_kernel(lg_ref, q_ref, k_ref, v_ref, g_ref, o_ref, r_ref, dec_ref, xi_ref, zeta_ref):
    n = pl.program_id(1)
    c = q_ref.shape[0]
    dk, dv = RET_DK, RET_DV

    @pl.when(n == 0)
    def _():
        r_ref[...] = jnp.zeros_like(r_ref)
        d = lax.broadcasted_iota(jnp.int32, (c, c), 0) - lax.broadcasted_iota(jnp.int32, (c, c), 1)
        row = lax.broadcasted_iota(jnp.int32, (c, dk), 0)
        for h in range(RET_GROUP):
            lg = lg_ref[h, 0:1, 0:1]
            dec_ref[h] = jnp.where(d >= 0, jnp.exp(jnp.maximum(d, 0).astype(F32) * lg), 0.0)
            xi_ref[h] = jnp.exp((row + 1).astype(F32) * lg)
            zeta_ref[h] = jnp.exp((c - 1 - row).astype(F32) * lg)

    for h in range(RET_GROUP):
        q = q_ref[:, h * dk:(h + 1) * dk]
        k = k_ref[:, h * dk:(h + 1) * dk]
        v = v_ref[:, h * dv:(h + 1) * dv]
        s = _dot_nt(q, k) * dec_ref[h]
        inner = _dot(s.astype(BF16), v)
        r = r_ref[h]
        cross = _dot((q.astype(F32) * xi_ref[h]).astype(BF16), r.astype(BF16))
        o = inner + cross
        kz = (k.astype(F32) * zeta_ref[h]).T.astype(BF16)
        r_ref[h] = r * jnp.exp(lg_ref[h, 0:1, 0:1] * c) + _dot(kz, v)
        oc = o - jnp.mean(o, axis=-1, keepdims=True)
        on = oc * lax.rsqrt(jnp.mean(oc * oc, axis=-1, keepdims=True) + GN_EPS)
        o_ref[:, h * dv:(h + 1) * dv] = (_silu(g_ref[:, h * dv:(h + 1) * dv].astype(F32)) * on).astype(BF16)


def _retention(proj, chunk=512):
    s = proj.shape[0]
    chunk = min(chunk, s)
    log_g = jnp.log1p(-jnp.exp2(-5.0 - jnp.arange(RET_HEADS, dtype=F32)))
    lg = jnp.broadcast_to(log_g[:, None, None], (RET_HEADS, SUBLANES, LANES))
    qk_w, v_w = RET_GROUP * RET_DK, RET_GROUP * RET_DV
    assert RET_HEADS % RET_GROUP == 0
    return pl.pallas_call(
        _ret_kernel,
        grid=(RET_HEADS // RET_GROUP, s // chunk),
        in_specs=[pl.BlockSpec((RET_GROUP, SUBLANES, LANES), lambda h, n: (h, 0, 0)),
                  pl.BlockSpec((chunk, qk_w), lambda h, n: (n, OFF_RQ // qk_w + h)),
                  pl.BlockSpec((chunk, qk_w), lambda h, n: (n, OFF_RK // qk_w + h)),
                  pl.BlockSpec((chunk, v_w), lambda h, n: (n, OFF_RV // v_w + h)),
                  pl.BlockSpec((chunk, v_w), lambda h, n: (n, OFF_RG // v_w + h))],
        out_specs=pl.BlockSpec((chunk, v_w), lambda h, n: (n, h)),
        out_shape=jax.ShapeDtypeStruct((s, RET_V_W), BF16),
        scratch_shapes=[pltpu.VMEM((RET_GROUP, RET_DK, RET_DV), F32),
                        pltpu.VMEM((RET_GROUP, chunk, chunk), F32),
                        pltpu.VMEM((RET_GROUP, chunk, RET_DK), F32),
                        pltpu.VMEM((RET_GROUP, chunk, RET_DK), F32)],
        compiler_params=_cparams("arbitrary", "arbitrary"),
        name="ret",
    )(lg, proj, proj, proj, proj)


def _moba_kernel(q_ref, k_ref, v_hbm, o_ref, km_ref, oh_ref, vt_ref, qt_ref, acc_ref, m_ref, v_ref, v_sem,
                 *slots):
    hp = pl.program_id(0)
    bq = pl.program_id(1)
    tq = q_ref.shape[0]
    hd = MOBA_HD
    blk = MOBA_BLOCK
    nb = k_ref.shape[0] // blk
    gw = MOBA_GROUP * blk
    vrows = vt_ref.shape[2]

    @pl.when((hp == 0) & (bq == 0))
    def _():
        lane_i = lax.broadcasted_iota(jnp.int32, (blk, LANES), 1)

        def body(n, carry):
            oh_ref[pl.ds(pl.multiple_of(n * blk, blk), blk), :] = jnp.where(lane_i == n, 1.0, 0.0).astype(BF16)
            return carry

        lax.fori_loop(0, nb, body, 0)

    def value_copy(pair):
        cols = pl.ds(pl.multiple_of(OFF_MV + pair * (MOBA_PAIR * hd), MOBA_PAIR * hd), MOBA_PAIR * hd)
        return pltpu.make_async_copy(v_hbm.at[:, cols], v_ref, v_sem)

    @pl.when((hp == 0) & (bq == 0))
    def _():
        value_copy(0).start()

    @pl.when(bq == 0)
    def _():
        value_copy(hp).wait()
        km_ref[...] = jnp.zeros_like(km_ref)

        def body(g, carry):
            for t in range(MOBA_GROUP):
                n = g * MOBA_GROUP + t
                rows = pl.ds(pl.multiple_of(n * blk, blk), blk)
                km_ref[pl.ds(n, 1), :] = jnp.mean(k_ref[rows, :].astype(F32), axis=0, keepdims=True)
                for h in range(MOBA_PAIR):
                    vt_ref[h, g, 0:hd, t * blk:(t + 1) * blk] = (
                        v_ref[rows, h * hd:(h + 1) * hd].astype(F32).T.astype(BF16))
            for h in range(MOBA_PAIR):
                vt_ref[h, g, hd:vrows, :] = jnp.ones((vrows - hd, gw), BF16)
            return carry

        lax.fori_loop(0, nb // MOBA_GROUP, body, 0)

        @pl.when(hp + 1 < pl.num_programs(0))
        def _():
            value_copy(hp + 1).start()

    gr = km_ref.shape[0]
    row_i = lax.broadcasted_iota(jnp.int32, (gr, tq), 0)
    row_f = row_i.astype(F32)
    own = bq * (tq // blk) + lax.broadcasted_iota(jnp.int32, (gr, tq), 1) // blk
    for h in range(MOBA_PAIR):
        qt = q_ref[:, h * hd:(h + 1) * hd].astype(F32).T.astype(BF16)
        km_hi, km_lo = _split_bf16(km_ref[:, h * hd:(h + 1) * hd])
        g = jnp.where(row_i < own, _dot(km_hi, qt) + _dot(km_lo, qt), NEG)
        bias = jnp.full((gr, tq), NEG, F32)
        for _ in range(MOBA_TOPK):
            mx = jnp.max(g, axis=0, keepdims=True)
            idx = jnp.min(jnp.where(g == mx, row_f, float(gr)), axis=0, keepdims=True)
            hit = row_f == idx
            bias = jnp.where(hit, jnp.where(mx > 0.5 * NEG, 0.0, NEG), bias)
            g = jnp.where(hit, 2.0 * NEG, g)
        bias = jnp.where(row_i == own, 0.0, bias)
        qt_ref[h] = jnp.concatenate(
            [qt, bias.astype(BF16), jnp.full((LANES - gr, tq), NEG, BF16)], axis=0)
        m_ref[h] = jnp.full((1, tq), NEG, F32)
        acc_ref[h] = jnp.zeros((vrows, tq), F32)

    def scores(g, slot_ref):
        rows = pl.ds(pl.multiple_of(g * gw, gw), gw)
        oh = oh_ref[rows, :]
        for h in range(MOBA_PAIR):
            k_ext = jnp.concatenate([k_ref[rows, h * hd:(h + 1) * hd], oh], axis=1)
            slot_ref[h] = _dot(k_ext, qt_ref[h])

    def accumulate(g, slot_ref, causal):
        if causal:
            own_rows = pl.ds(pl.multiple_of((bq * tq) % gw, tq), tq)
            keep = lax.broadcasted_iota(jnp.int32, (tq, tq), 0) <= lax.broadcasted_iota(jnp.int32, (tq, tq), 1)
            for h in range(MOBA_PAIR):
                slot_ref[h, own_rows, :] = jnp.where(keep, slot_ref[h, own_rows, :], NEG)
        for h in range(MOBA_PAIR):
            s = slot_ref[h]
            m_old = m_ref[h]
            m_new = jnp.maximum(m_old, jnp.max(s, axis=0, keepdims=True))
            p = jnp.exp2(s - m_new).astype(BF16)
            acc_ref[h] = jnp.exp2(m_old - m_new) * acc_ref[h] + _dot(vt_ref[h, g], p)
            m_ref[h] = m_new

    nslot = len(slots)
    full_groups = (bq * tq) // gw
    scores(0, slots[0])

    def body(i, carry):
        g = nslot * i
        for u in range(nslot):
            scores(g + u + 1, slots[(u + 1) % nslot])
            accumulate(g + u, slots[u], False)
        return carry

    lax.fori_loop(0, full_groups // nslot, body, 0)
    base = (full_groups // nslot) * nslot
    for r in range(nslot):
        @pl.when(full_groups - base == r)
        def _(r=r):
            for u in range(r):
                scores(base + u + 1, slots[u + 1])
                accumulate(base + u, slots[u], False)
            accumulate(base + r, slots[r], True)

    for h in range(MOBA_PAIR):
        acc = acc_ref[h]
        o_ref[:, h * hd:(h + 1) * hd] = (acc[0:hd, :] / acc[hd:hd + 1, :]).T.astype(BF16)


def _moba(proj, tq=2 * MOBA_BLOCK):
    s = proj.shape[0]
    nb = s // MOBA_BLOCK
    w = MOBA_PAIR * MOBA_HD
    gw = MOBA_GROUP * MOBA_BLOCK
    bf16_rows = 2 * SUBLANES
    vrows = MOBA_HD + bf16_rows
    gate_rows = -(-nb // bf16_rows) * bf16_rows
    assert s % gw == 0 and gate_rows <= LANES and MOBA_HEADS % MOBA_PAIR == 0
    assert tq % MOBA_BLOCK == 0 and gw % tq == 0
    return pl.pallas_call(
        _moba_kernel,
        grid=(MOBA_HEADS // MOBA_PAIR, s // tq),
        in_specs=[pl.BlockSpec((tq, w), lambda h, b: (b, OFF_MQ // w + h)),
                  pl.BlockSpec((s, w), lambda h, b: (0, OFF_MK // w + h)),
                  pl.BlockSpec(memory_space=pl.ANY)],
        out_specs=pl.BlockSpec((tq, w), lambda h, b: (b, h)),
        out_shape=jax.ShapeDtypeStruct((s, MOBA_W), BF16),
        scratch_shapes=[pltpu.VMEM((gate_rows, w), F32),
                        pltpu.VMEM((s, LANES), BF16),
                        pltpu.VMEM((MOBA_PAIR, s // gw, vrows, gw), BF16),
                        pltpu.VMEM((MOBA_PAIR, 2 * MOBA_HD, tq), BF16),
                        pltpu.VMEM((MOBA_PAIR, vrows, tq), F32),
                        pltpu.VMEM((MOBA_PAIR, 1, tq), F32),
                        pltpu.VMEM((s, w), BF16),
                        pltpu.SemaphoreType.DMA(())]
                       + [pltpu.VMEM((MOBA_PAIR, gw, tq), F32)] * MOBA_SLOTS,
        compiler_params=_cparams("arbitrary", "arbitrary"),
        name="moba",
    )(proj, proj, proj)


def _merge_kernel(or_ref, om_ref, ga_ref, gb_ref, wr_ref, wm_ref, o_ref):
    yr = _dot(or_ref[...], wr_ref[...])
    ym = _dot(om_ref[...], wm_ref[...])
    ga = jax.nn.sigmoid(ga_ref[...].astype(F32))
    gb = jax.nn.sigmoid(gb_ref[...].astype(F32))
    o_ref[...] = (ga * yr + gb * ym).astype(BF16)


def _merge(o_ret, o_moba, proj, w_r, w_m, tm=1024, tn=512):
    s = o_ret.shape[0]
    tm = min(tm, s)
    d = D_MODEL
    return pl.pallas_call(
        _merge_kernel,
        grid=(s // tm, d // tn),
        in_specs=[pl.BlockSpec((tm, RET_V_W), lambda i, j: (i, 0)),
                  pl.BlockSpec((tm, MOBA_W), lambda i, j: (i, 0)),
                  pl.BlockSpec((tm, tn), lambda i, j: (i, OFF_GA // tn + j)),
                  pl.BlockSpec((tm, tn), lambda i, j: (i, OFF_GB // tn + j)),
                  pl.BlockSpec((RET_V_W, tn), lambda i, j: (0, j)),
                  pl.BlockSpec((MOBA_W, tn), lambda i, j: (0, j))],
        out_specs=pl.BlockSpec((tm, tn), lambda i, j: (i, j)),
        out_shape=jax.ShapeDtypeStruct((s, d), BF16),
        compiler_params=_cparams("arbitrary", "arbitrary"),
        name="merge",
    )(o_ret, o_moba, proj, proj, w_r, w_m)


def _mixout_kernel(m_ref, wo_ref, x_ref, npost_ref, g1_ref, npre_ref, sc_ref, sh_ref, x1_ref, u2_ref):
    for r0 in range(0, m_ref.shape[0], MIX_ROWS):
        rows = slice(r0, r0 + MIX_ROWS)
        y = _dot(m_ref[rows, :], wo_ref[...])
        x1 = x_ref[rows, :] + (1.0 + g1_ref[...]) * (_rms(y) * npost_ref[...])
        x1_ref[rows, :] = x1
        u2_ref[rows, :] = ((_rms(x1) * npre_ref[...]) * (1.0 + sc_ref[...]) + sh_ref[...]).astype(BF16)


def _mixout(merged, w_o, x, npost, g1, npre, sc2, sh2, tm=512):
    s, d = x.shape
    tm = min(tm, s)
    vec = pl.BlockSpec((1, d), lambda i: (0, 0))
    rows = pl.BlockSpec((tm, d), lambda i: (i, 0))
    return pl.pallas_call(
        _mixout_kernel,
        grid=(s // tm,),
        in_specs=[rows, pl.BlockSpec((d, d), lambda i: (0, 0), pipeline_mode=pl.Buffered(1)), rows,
                  vec, vec, vec, vec, vec],
        out_specs=[rows, rows],
        out_shape=[jax.ShapeDtypeStruct((s, d), F32), jax.ShapeDtypeStruct((s, d), BF16)],
        compiler_params=_cparams("arbitrary"),
        name="mixout",
    )(merged, w_o, x, npost, g1, npre, sc2, sh2)


def _ffn_kernel(u_ref, wa_ref, wb_ref, cwa_ref, cwb_ref, cba_ref, cbb_ref, wd_ref, x1_hbm, npost_ref, g2_ref,
                o_ref, ha_ref, hb_ref, act_ref, tail_ref, x1_ref, x1_sem):
    i = pl.program_id(0)
    j = pl.program_id(1)
    tm = u_ref.shape[0]
    tn = wa_ref.shape[1]
    halo = SUBLANES

    def residual_copy():
        return pltpu.make_async_copy(x1_hbm.at[pl.ds(pl.multiple_of(i * tm, tm), tm), :], x1_ref, x1_sem)

    @pl.when((i == 0) & (j == 0))
    def _():
        tail_ref[...] = jnp.zeros_like(tail_ref)

    @pl.when(j == 0)
    def _():
        residual_copy().start()
        o_ref[...] = jnp.zeros_like(o_ref)

    ha_ref[0:halo, :] = tail_ref[j, 0]
    hb_ref[0:halo, :] = tail_ref[j, 1]
    for c in range(tn // FFN_CHUNK):
        cs = slice(c * FFN_CHUNK, (c + 1) * FFN_CHUNK)
        ha_ref[halo:halo + tm, cs] = _dot(u_ref[...], wa_ref[:, cs])
        hb_ref[halo:halo + tm, cs] = _dot(u_ref[...], wb_ref[:, cs])
    tail_ref[j, 0] = ha_ref[tm:tm + halo, :]
    tail_ref[j, 1] = hb_ref[tm:tm + halo, :]

    def conv(hbuf, cw, cb, r0, cs):
        out = cb
        for t in range(CONV_WIDTH):
            out = out + hbuf[halo + r0 - t:halo + r0 - t + FFN_ROWS, cs] * cw[CONV_WIDTH - 1 - t:CONV_WIDTH - t, :]
        return out

    for c in range(tn // FFN_CHUNK):
        cs = slice(c * FFN_CHUNK, (c + 1) * FFN_CHUNK)
        cwa, cwb, cba, cbb = cwa_ref[:, cs], cwb_ref[:, cs], cba_ref[:, cs], cbb_ref[:, cs]
        for r0 in range(0, tm, FFN_ROWS):
            a = conv(ha_ref, cwa, cba, r0, cs)
            bgate = conv(hb_ref, cwb, cbb, r0, cs)
            act_ref[r0:r0 + FFN_ROWS, cs] = (_silu(a) * bgate).astype(BF16)
        for n0 in range(0, o_ref.shape[1], FFN_DOWN_COLS):
            ns = slice(n0, n0 + FFN_DOWN_COLS)
            o_ref[:, ns] += _dot(act_ref[:, cs], wd_ref[cs, ns])

    @pl.when(j == pl.num_programs(1) - 1)
    def _():
        residual_copy().wait()

        def finish(rows, y):
            o_ref[rows, :] = x1_ref[rows, :] + (1.0 + g2_ref[...]) * (_rms(y) * npost_ref[...])

        _for_row_chunks(tm, lambda rows: o_ref[rows, :], finish)


def _ffn(u2, w_up, conv_w, conv_b, w_down, x1, npost, g2, tm=1024, tn=512):
    s, d = x1.shape
    tm = min(tm, s)
    nj = D_FF // tn
    assert D_FF % tn == 0
    vec = pl.BlockSpec((1, d), lambda i, j: (0, 0))
    rows = pl.BlockSpec((tm, d), lambda i, j: (i, 0))
    resid = pl.BlockSpec(memory_space=pl.ANY)
    return pl.pallas_call(
        _ffn_kernel,
        grid=(s // tm, nj),
        in_specs=[rows,
                  pl.BlockSpec((d, tn), lambda i, j: (0, j)),
                  pl.BlockSpec((d, tn), lambda i, j: (0, j + nj)),
                  pl.BlockSpec((CONV_WIDTH, tn), lambda i, j: (0, j)),
                  pl.BlockSpec((CONV_WIDTH, tn), lambda i, j: (0, j + nj)),
                  pl.BlockSpec((1, tn), lambda i, j: (0, j)),
                  pl.BlockSpec((1, tn), lambda i, j: (0, j + nj)),
                  pl.BlockSpec((tn, d), lambda i, j: (j, 0)),
                  resid, vec, vec],
        out_specs=rows,
        out_shape=jax.ShapeDtypeStruct((s, d), F32),
        scratch_shapes=[pltpu.VMEM((tm + SUBLANES, tn), F32),
                        pltpu.VMEM((tm + SUBLANES, tn), F32),
                        pltpu.VMEM((tm, tn), BF16),
                        pltpu.VMEM((nj, 2, SUBLANES, tn), F32),
                        pltpu.VMEM((tm, d), F32),
                        pltpu.SemaphoreType.DMA(())],
        compiler_params=_cparams("arbitrary", "arbitrary"),
        name="ffn",
    )(u2, w_up, w_up, conv_w, conv_w, conv_b, conv_b, w_down, x1, npost, g2)


def _block(x, c, ada_w, ada_b, mix_norm_pre, mix_norm_post, w_in, ret_w_o, moba_w_o, w_out,
           ffn_norm_pre, ffn_norm_post, ffn_w_up, ffn_conv_w, ffn_conv_b, ffn_w_down, cos2, sin2):
    d = x.shape[1]
    mod = _ada(c, ada_w, ada_b[None, :])
    sh1, sc1, g1, sh2, sc2, g2 = [mod[:, t * d:(t + 1) * d] for t in range(6)]
    proj = _inproj(x, mix_norm_pre[None, :], sc1, sh1, cos2, sin2, w_in.astype(BF16))
    o_ret = _retention(proj)
    o_moba = _moba(proj)
    merged = _merge(o_ret, o_moba, proj, ret_w_o.astype(BF16), moba_w_o.astype(BF16))
    x1, u2 = _mixout(merged, w_out.astype(BF16), x, mix_norm_post[None, :], g1,
                     ffn_norm_pre[None, :], sc2, sh2)
    return _ffn(u2, ffn_w_up.astype(BF16), ffn_conv_w, ffn_conv_b[None, :], ffn_w_down.astype(BF16),
                x1, ffn_norm_post[None, :], g2)


def kernel(x, c, ada_w, ada_b, mix_norm_pre, mix_norm_post, w_in, ret_w_o, moba_w_o, w_out,
           ffn_norm_pre, ffn_norm_post, ffn_w_up, ffn_conv_w, ffn_conv_b, ffn_w_down):
    batch, s, _ = x.shape
    depth = ada_w.shape[0]
    cos2, sin2 = _rope_tables(s)
    outs = []
    for bi in range(batch):
        xb = x[bi]
        for l in range(depth):
            xb = _block(xb, c[bi:bi + 1], ada_w[l], ada_b[l], mix_norm_pre[l], mix_norm_post[l], w_in[l],
                        ret_w_o[l], moba_w_o[l], w_out[l], ffn_norm_pre[l], ffn_norm_post[l], ffn_w_up[l],
                        ffn_conv_w[l], ffn_conv_b[l], ffn_w_down[l], cos2, sin2)
        outs.append(xb)
    return jnp.stack(outs)
```

```python
import math

import jax
import jax.numpy as jnp
from jax import lax
from jax.experimental import pallas as pl
from jax.experimental.pallas import tpu as pltpu

D_MODEL = 2048
RET_HEADS = 8
RET_DK = 128
RET_DV = 256
ROPE_BASE = 10000.0
MOBA_HEADS = 16
MOBA_HD = 128
MOBA_BLOCK = 256
MOBA_TOPK = 3
D_FF = 5632
CONV_WIDTH = 3
NORM_EPS = 1e-6
GN_EPS = 1e-5

RET_QK_W = RET_HEADS * RET_DK
RET_V_W = RET_HEADS * RET_DV
MOBA_W = MOBA_HEADS * MOBA_HD
OFF_RQ = 0
OFF_RK = OFF_RQ + RET_QK_W
OFF_RV = OFF_RK + RET_QK_W
OFF_RG = OFF_RV + RET_V_W
OFF_MQ = OFF_RG + RET_V_W
OFF_MK = OFF_MQ + MOBA_W
OFF_MV = OFF_MK + MOBA_W
OFF_GA = OFF_MV + MOBA_W
OFF_GB = OFF_GA + D_MODEL
IN_WIDTH = OFF_GB + D_MODEL

LANES = 128
SUBLANES = 8
NEG = -1e30
MOBA_QSCALE = math.log2(math.e) / math.sqrt(MOBA_HD)
RET_GROUP = 4
MOBA_GROUP = 2
MOBA_SLOTS = 6
MOBA_PAIR = 2
FFN_CHUNK = 256
FFN_ROWS = 64
FFN_DOWN_COLS = 512
MIX_ROWS = 256
NORM_ROWS = 32
NORM_UNROLL = 4
VMEM_LIMIT = 58 * 1024 * 1024

F32 = jnp.float32
BF16 = jnp.bfloat16


def _cparams(*sem):
    return pltpu.CompilerParams(dimension_semantics=sem, vmem_limit_bytes=VMEM_LIMIT)


def _dot(a, b):
    return jnp.dot(a, b, preferred_element_type=F32)


def _dot_nt(a, b):
    return lax.dot_general(a, b, (((1,), (1,)), ((), ())), preferred_element_type=F32)


def _split_bf16(x):
    hi = x.astype(BF16)
    lo = (x - hi.astype(F32)).astype(BF16)
    return hi, lo


def _rms(x):
    return x * lax.rsqrt(jnp.mean(x * x, axis=-1, keepdims=True) + NORM_EPS)


def _silu(x):
    return x * jax.nn.sigmoid(x)


def _for_row_chunks(n_rows, load, finish):
    step = NORM_ROWS * NORM_UNROLL

    def body(r, carry):
        pieces = [pl.ds(pl.multiple_of(r * step + t * NORM_ROWS, NORM_ROWS), NORM_ROWS)
                  for t in range(NORM_UNROLL)]
        loaded = [load(rows) for rows in pieces]
        for rows, vals in zip(pieces, loaded):
            finish(rows, vals)
        return carry

    lax.fori_loop(0, n_rows // step, body, 0)


def _ada_kernel(c_ref, w_ref, b_ref, o_ref):
    c = jnp.broadcast_to(_silu(c_ref[...]), (SUBLANES, c_ref.shape[1]))
    c_hi, c_lo = _split_bf16(c)
    w_hi, w_lo = _split_bf16(w_ref[...])
    r = _dot(c_hi, w_hi) + _dot(c_lo, w_hi) + _dot(c_hi, w_lo)
    o_ref[...] = r[0:1] + b_ref[...]


def _ada(c, w, b, tn=1024):
    d, n = w.shape
    return pl.pallas_call(
        _ada_kernel,
        grid=(n // tn,),
        in_specs=[pl.BlockSpec((1, d), lambda j: (0, 0)),
                  pl.BlockSpec((d, tn), lambda j: (0, j)),
                  pl.BlockSpec((1, tn), lambda j: (0, j))],
        out_specs=pl.BlockSpec((1, tn), lambda j: (0, j)),
        out_shape=jax.ShapeDtypeStruct((1, n), F32),
        compiler_params=_cparams("arbitrary"),
        name="ada",
    )(c, w, b)


def _rope_kernel(inv_ref, cos_ref, sin_ref):
    ts = cos_ref.shape[0]
    row = lax.broadcasted_iota(jnp.int32, (ts, LANES), 0) + pl.program_id(0) * ts
    lane = lax.broadcasted_iota(jnp.int32, (ts, LANES), 1)
    ang = row.astype(F32) * inv_ref[...]
    sin = jnp.sin(ang)
    cos_ref[...] = jnp.cos(ang)
    sin_ref[...] = jnp.where(lane < RET_DK // 2, -sin, sin)


def _rope_tables(s, ts=2048):
    ts = min(ts, s)
    half = RET_DK // 2
    inv = ROPE_BASE ** (-jnp.arange(half, dtype=F32) / half)
    inv2 = jnp.concatenate([inv, inv])[None, :]
    return pl.pallas_call(
        _rope_kernel,
        grid=(s // ts,),
        in_specs=[pl.BlockSpec((1, LANES), lambda i: (0, 0))],
        out_specs=[pl.BlockSpec((ts, LANES), lambda i: (i, 0))] * 2,
        out_shape=[jax.ShapeDtypeStruct((s, LANES), F32)] * 2,
        compiler_params=_cparams("arbitrary"),
        name="rope",
    )(inv2)


def _inproj_kernel(x_ref, nw_ref, sc_ref, sh_ref, cos_ref, sin_ref, w_ref, o_ref, u_ref):
    j = pl.program_id(1)
    tn = o_ref.shape[1]

    @pl.when(j == 0)
    def _():
        def finish(rows, x):
            y = _rms(x) * nw_ref[...]
            u_ref[rows, :] = (y * (1.0 + sc_ref[...]) + sh_ref[...]).astype(BF16)

        _for_row_chunks(x_ref.shape[0], lambda rows: x_ref[rows, :], finish)

    r = _dot(u_ref[...], w_ref[...])
    q_tiles = RET_QK_W // tn
    is_mq = (j >= OFF_MQ // tn) & (j < OFF_MK // tn)
    o_ref[...] = (r * jnp.where(is_mq, MOBA_QSCALE, 1.0).astype(F32)).astype(BF16)

    @pl.when(j < 2 * q_tiles)
    def _():
        scale = jnp.where(j >= q_tiles, RET_DK ** -0.5, 1.0).astype(F32)
        cos = cos_ref[...]
        sin = sin_ref[...]
        for h in range(tn // RET_DK):
            xh = r[:, h * RET_DK:(h + 1) * RET_DK]
            rot = xh * cos + pltpu.roll(xh, RET_DK // 2, axis=1) * sin
            o_ref[:, h * RET_DK:(h + 1) * RET_DK] = (rot * scale).astype(BF16)


def _inproj(x, nw, sc, sh, cos2, sin2, w, tm=1024, tn=1024):
    s, d = x.shape
    n = w.shape[1]
    tm = min(tm, s)
    vec = pl.BlockSpec((1, d), lambda i, j: (0, 0))
    tab = pl.BlockSpec((tm, LANES), lambda i, j: (i, 0))
    return pl.pallas_call(
        _inproj_kernel,
        grid=(s // tm, n // tn),
        in_specs=[pl.BlockSpec((tm, d), lambda i, j: (i, 0)), vec, vec, vec, tab, tab,
                  pl.BlockSpec((d, tn), lambda i, j: (0, j))],
        out_specs=pl.BlockSpec((tm, tn), lambda i, j: (i, j)),
        out_shape=jax.ShapeDtypeStruct((s, n), BF16),
        scratch_shapes=[pltpu.VMEM((tm, d), BF16)],
        compiler_params=_cparams("arbitrary", "arbitrary"),
        name="inproj",
    )(x, nw, sc, sh, cos2, sin2, w)


def _ret_kernel(lg_ref, q_ref, k_ref, v_ref, g_ref, o_ref, r_ref, dec_ref, xi_ref, zeta_ref):
    n = pl.program_id(1)
    c = q_ref.shape[0]
    dk, dv = RET_DK, RET_DV

    @pl.when(n == 0)
    def _():
        r_ref[...] = jnp.zeros_like(r_ref)
        d = lax.broadcasted_iota(jnp.int32, (c, c), 0) - lax.broadcasted_iota(jnp.int32, (c, c), 1)
        row = lax.broadcasted_iota(jnp.int32, (c, dk), 0)
        for h in range(RET_GROUP):
            lg = lg_ref[h, 0:1, 0:1]
            dec_ref[h] = jnp.where(d >= 0, jnp.exp(jnp.maximum(d, 0).astype(F32) * lg), 0.0)
            xi_ref[h] = jnp.exp((row + 1).astype(F32) * lg)
            zeta_ref[h] = jnp.exp((c - 1 - row).astype(F32) * lg)

    for h in range(RET_GROUP):
        q = q_ref[:, h * dk:(h + 1) * dk]
        k = k_ref[:, h * dk:(h + 1) * dk]
        v = v_ref[:, h * dv:(h + 1) * dv]
        s = _dot_nt(q, k) * dec_ref[h]
        inner = _dot(s.astype(BF16), v)
        r = r_ref[h]
        cross = _dot((q.astype(F32) * xi_ref[h]).astype(BF16), r.astype(BF16))
        o = inner + cross
        kz = (k.astype(F32) * zeta_ref[h]).T.astype(BF16)
        r_ref[h] = r * jnp.exp(lg_ref[h, 0:1, 0:1] * c) + _dot(kz, v)
        oc = o - jnp.mean(o, axis=-1, keepdims=True)
        on = oc * lax.rsqrt(jnp.mean(oc * oc, axis=-1, keepdims=True) + GN_EPS)
        o_ref[:, h * dv:(h + 1) * dv] = (_silu(g_ref[:, h * dv:(h + 1) * dv].astype(F32)) * on).astype(BF16)


def _retention(proj, chunk=512):
    s = proj.shape[0]
    chunk = min(chunk, s)
    log_g = jnp.log1p(-jnp.exp2(-5.0 - jnp.arange(RET_HEADS, dtype=F32)))
    lg = jnp.broadcast_to(log_g[:, None, None], (RET_HEADS, SUBLANES, LANES))
    qk_w, v_w = RET_GROUP * RET_DK, RET_GROUP * RET_DV
    assert RET_HEADS % RET_GROUP == 0
    return pl.pallas_call(
        _ret_kernel,
        grid=(RET_HEADS // RET_GROUP, s // chunk),
        in_specs=[pl.BlockSpec((RET_GROUP, SUBLANES, LANES), lambda h, n: (h, 0, 0)),
                  pl.BlockSpec((chunk, qk_w), lambda h, n: (n, OFF_RQ // qk_w + h)),
                  pl.BlockSpec((chunk, qk_w), lambda h, n: (n, OFF_RK // qk_w + h)),
                  pl.BlockSpec((chunk, v_w), lambda h, n: (n, OFF_RV // v_w + h)),
                  pl.BlockSpec((chunk, v_w), lambda h, n: (n, OFF_RG // v_w + h))],
        out_specs=pl.BlockSpec((chunk, v_w), lambda h, n: (n, h)),
        out_shape=jax.ShapeDtypeStruct((s, RET_V_W), BF16),
        scratch_shapes=[pltpu.VMEM((RET_GROUP, RET_DK, RET_DV), F32),
                        pltpu.VMEM((RET_GROUP, chunk, chunk), F32),
                        pltpu.VMEM((RET_GROUP, chunk, RET_DK), F32),
                        pltpu.VMEM((RET_GROUP, chunk, RET_DK), F32)],
        compiler_params=_cparams("arbitrary", "arbitrary"),
        name="ret",
    )(lg, proj, proj, proj, proj)


def _moba_kernel(q_ref, k_ref, v_hbm, o_ref, km_ref, oh_ref, vt_ref, qt_ref, acc_ref, m_ref, v_ref, v_sem,
                 *slots):
    hp = pl.program_id(0)
    bq = pl.program_id(1)
    tq = q_ref.shape[0]
    hd = MOBA_HD
    blk = MOBA_BLOCK
    nb = k_ref.shape[0] // blk
    gw = MOBA_GROUP * blk
    vrows = vt_ref.shape[2]

    @pl.when((hp == 0) & (bq == 0))
    def _():
        lane_i = lax.broadcasted_iota(jnp.int32, (blk, LANES), 1)

        def body(n, carry):
            oh_ref[pl.ds(pl.multiple_of(n * blk, blk), blk), :] = jnp.where(lane_i == n, 1.0, 0.0).astype(BF16)
            return carry

        lax.fori_loop(0, nb, body, 0)

    def value_copy(pair):
        cols = pl.ds(pl.multiple_of(OFF_MV + pair * (MOBA_PAIR * hd), MOBA_PAIR * hd), MOBA_PAIR * hd)
        return pltpu.make_async_copy(v_hbm.at[:, cols], v_ref, v_sem)

    @pl.when((hp == 0) & (bq == 0))
    def _():
        value_copy(0).start()

    @pl.when(bq == 0)
    def _():
        value_copy(hp).wait()
        km_ref[...] = jnp.zeros_like(km_ref)

        def body(g, carry):
            for t in range(MOBA_GROUP):
                n = g * MOBA_GROUP + t
                rows = pl.ds(pl.multiple_of(n * blk, blk), blk)
                km_ref[pl.ds(n, 1), :] = jnp.mean(k_ref[rows, :].astype(F32), axis=0, keepdims=True)
                for h in range(MOBA_PAIR):
                    vt_ref[h, g, 0:hd, t * blk:(t + 1) * blk] = (
                        v_ref[rows, h * hd:(h + 1) * hd].astype(F32).T.astype(BF16))
            for h in range(MOBA_PAIR):
                vt_ref[h, g, hd:vrows, :] = jnp.ones((vrows - hd, gw), BF16)
            return carry

        lax.fori_loop(0, nb // MOBA_GROUP, body, 0)

        @pl.when(hp + 1 < pl.num_programs(0))
        def _():
            value_copy(hp + 1).start()

    gr = km_ref.shape[0]
    row_i = lax.broadcasted_iota(jnp.int32, (gr, tq), 0)
    row_f = row_i.astype(F32)
    own = bq * (tq // blk) + lax.broadcasted_iota(jnp.int32, (gr, tq), 1) // blk
    for h in range(MOBA_PAIR):
        qt = q_ref[:, h * hd:(h + 1) * hd].astype(F32).T.astype(BF16)
        km_hi, km_lo = _split_bf16(km_ref[:, h * hd:(h + 1) * hd])
        g = jnp.where(row_i < own, _dot(km_hi, qt) + _dot(km_lo, qt), NEG)
        bias = jnp.full((gr, tq), NEG, F32)
        for _ in range(MOBA_TOPK):
            mx = jnp.max(g, axis=0, keepdims=True)
            idx = jnp.min(jnp.where(g == mx, row_f, float(gr)), axis=0, keepdims=True)
            hit = row_f == idx
            bias = jnp.where(hit, jnp.where(mx > 0.5 * NEG, 0.0, NEG), bias)
            g = jnp.where(hit, 2.0 * NEG, g)
        bias = jnp.where(row_i == own, 0.0, bias)
        qt_ref[h] = jnp.concatenate(
            [qt, bias.astype(BF16), jnp.full((LANES - gr, tq), NEG, BF16)], axis=0)
        m_ref[h] = jnp.full((1, tq), NEG, F32)
        acc_ref[h] = jnp.zeros((vrows, tq), F32)

    def scores(g, slot_ref):
        rows = pl.ds(pl.multiple_of(g * gw, gw), gw)
        oh = oh_ref[rows, :]
        for h in range(MOBA_PAIR):
            k_ext = jnp.concatenate([k_ref[rows, h * hd:(h + 1) * hd], oh], axis=1)
            slot_ref[h] = _dot(k_ext, qt_ref[h])

    def accumulate(g, slot_ref, causal):
        if causal:
            own_rows = pl.ds(pl.multiple_of((bq * tq) % gw, tq), tq)
            keep = lax.broadcasted_iota(jnp.int32, (tq, tq), 0) <= lax.broadcasted_iota(jnp.int32, (tq, tq), 1)
            for h in range(MOBA_PAIR):
                slot_ref[h, own_rows, :] = jnp.where(keep, slot_ref[h, own_rows, :], NEG)
        for h in range(MOBA_PAIR):
            s = slot_ref[h]
            m_old = m_ref[h]
            m_new = jnp.maximum(m_old, jnp.max(s, axis=0, keepdims=True))
            p = jnp.exp2(s - m_new).astype(BF16)
            acc_ref[h] = jnp.exp2(m_old - m_new) * acc_ref[h] + _dot(vt_ref[h, g], p)
            m_ref[h] = m_new

    nslot = len(slots)
    full_groups = (bq * tq) // gw
    scores(0, slots[0])

    def body(i, carry):
        g = nslot * i
        for u in range(nslot):
            scores(g + u + 1, slots[(u + 1) % nslot])
            accumulate(g + u, slots[u], False)
        return carry

    lax.fori_loop(0, full_groups // nslot, body, 0)
    base = (full_groups // nslot) * nslot
    for r in range(nslot):
        @pl.when(full_groups - base == r)
        def _(r=r):
            for u in range(r):
                scores(base + u + 1, slots[u + 1])
                accumulate(base + u, slots[u], False)
            accumulate(base + r, slots[r], True)

    for h in range(MOBA_PAIR):
        acc = acc_ref[h]
        o_ref[:, h * hd:(h + 1) * hd] = (acc[0:hd, :] / acc[hd:hd + 1, :]).T.astype(BF16)


def _moba(proj, tq=2 * MOBA_BLOCK):
    s = proj.shape[0]
    nb = s // MOBA_BLOCK
    w = MOBA_PAIR * MOBA_HD
    gw = MOBA_GROUP * MOBA_BLOCK
    bf16_rows = 2 * SUBLANES
    vrows = MOBA_HD + bf16_rows
    gate_rows = -(-nb // bf16_rows) * bf16_rows
    assert s % gw == 0 and gate_rows <= LANES and MOBA_HEADS % MOBA_PAIR == 0
    assert tq % MOBA_BLOCK == 0 and gw % tq == 0
    return pl.pallas_call(
        _moba_kernel,
        grid=(MOBA_HEADS // MOBA_PAIR, s // tq),
        in_specs=[pl.BlockSpec((tq, w), lambda h, b: (b, OFF_MQ // w + h)),
                  pl.BlockSpec((s, w), lambda h, b: (0, OFF_MK // w + h)),
                  pl.BlockSpec(memory_space=pl.ANY)],
        out_specs=pl.BlockSpec((tq, w), lambda h, b: (b, h)),
        out_shape=jax.ShapeDtypeStruct((s, MOBA_W), BF16),
        scratch_shapes=[pltpu.VMEM((gate_rows, w), F32),
                        pltpu.VMEM((s, LANES), BF16),
                        pltpu.VMEM((MOBA_PAIR, s // gw, vrows, gw), BF16),
                        pltpu.VMEM((MOBA_PAIR, 2 * MOBA_HD, tq), BF16),
                        pltpu.VMEM((MOBA_PAIR, vrows, tq), F32),
                        pltpu.VMEM((MOBA_PAIR, 1, tq), F32),
                        pltpu.VMEM((s, w), BF16),
                        pltpu.SemaphoreType.DMA(())]
                       + [pltpu.VMEM((MOBA_PAIR, gw, tq), F32)] * MOBA_SLOTS,
        compiler_params=_cparams("arbitrary", "arbitrary"),
        name="moba",
    )(proj, proj, proj)


def _merge_kernel(or_ref, om_ref, ga_ref, gb_ref, wr_ref, wm_ref, o_ref):
    yr = _dot(or_ref[...], wr_ref[...])
    ym = _dot(om_ref[...], wm_ref[...])
    ga = jax.nn.sigmoid(ga_ref[...].astype(F32))
    gb = jax.nn.sigmoid(gb_ref[...].astype(F32))
    o_ref[...] = (ga * yr + gb * ym).astype(BF16)


def _merge(o_ret, o_moba, proj, w_r, w_m, tm=1024, tn=512):
    s = o_ret.shape[0]
    tm = min(tm, s)
    d = D_MODEL
    return pl.pallas_call(
        _merge_kernel,
        grid=(s // tm, d // tn),
        in_specs=[pl.BlockSpec((tm, RET_V_W), lambda i, j: (i, 0)),
                  pl.BlockSpec((tm, MOBA_W), lambda i, j: (i, 0)),
                  pl.BlockSpec((tm, tn), lambda i, j: (i, OFF_GA // tn + j)),
                  pl.BlockSpec((tm, tn), lambda i, j: (i, OFF_GB // tn + j)),
                  pl.BlockSpec((RET_V_W, tn), lambda i, j: (0, j)),
                  pl.BlockSpec((MOBA_W, tn), lambda i, j: (0, j))],
        out_specs=pl.BlockSpec((tm, tn), lambda i, j: (i, j)),
        out_shape=jax.ShapeDtypeStruct((s, d), BF16),
        compiler_params=_cparams("arbitrary", "arbitrary"),
        name="merge",
    )(o_ret, o_moba, proj, proj, w_r, w_m)


def _mixout_kernel(m_ref, wo_ref, x_ref, npost_ref, g1_ref, npre_ref, sc_ref, sh_ref, x1_ref, u2_ref):
    for r0 in range(0, m_ref.shape[0], MIX_ROWS):
        rows = slice(r0, r0 + MIX_ROWS)
        y = _dot(m_ref[rows, :], wo_ref[...])
        x1 = x_ref[rows, :] + (1.0 + g1_ref[...]) * (_rms(y) * npost_ref[...])
        x1_ref[rows, :] = x1
        u2_ref[rows, :] = ((_rms(x1) * npre_ref[...]) * (1.0 + sc_ref[...]) + sh_ref[...]).astype(BF16)


def _mixout(merged, w_o, x, npost, g1, npre, sc2, sh2, tm=512):
    s, d = x.shape
    tm = min(tm, s)
    vec = pl.BlockSpec((1, d), lambda i: (0, 0))
    rows = pl.BlockSpec((tm, d), lambda i: (i, 0))
    return pl.pallas_call(
        _mixout_kernel,
        grid=(s // tm,),
        in_specs=[rows, pl.BlockSpec((d, d), lambda i: (0, 0), pipeline_mode=pl.Buffered(1)), rows,
                  vec, vec, vec, vec, vec],
        out_specs=[rows, rows],
        out_shape=[jax.ShapeDtypeStruct((s, d), F32), jax.ShapeDtypeStruct((s, d), BF16)],
        compiler_params=_cparams("arbitrary"),
        name="mixout",
    )(merged, w_o, x, npost, g1, npre, sc2, sh2)


def _ffn_kernel(u_ref, wa_ref, wb_ref, cwa_ref, cwb_ref, cba_ref, cbb_ref, wd_ref, x1_hbm, npost_ref, g2_ref,
                o_ref, ha_ref, hb_ref, act_ref, tail_ref, x1_ref, x1_sem):
    i = pl.program_id(0)
    j = pl.program_id(1)
    tm = u_ref.shape[0]
    tn = wa_ref.shape[1]
    halo = SUBLANES

    def residual_copy():
        return pltpu.make_async_copy(x1_hbm.at[pl.ds(pl.multiple_of(i * tm, tm), tm), :], x1_ref, x1_sem)

    @pl.when((i == 0) & (j == 0))
    def _():
        tail_ref[...] = jnp.zeros_like(tail_ref)

    @pl.when(j == 0)
    def _():
        residual_copy().start()
        o_ref[...] = jnp.zeros_like(o_ref)

    ha_ref[0:halo, :] = tail_ref[j, 0]
    hb_ref[0:halo, :] = tail_ref[j, 1]
    for c in range(tn // FFN_CHUNK):
        cs = slice(c * FFN_CHUNK, (c + 1) * FFN_CHUNK)
        ha_ref[halo:halo + tm, cs] = _dot(u_ref[...], wa_ref[:, cs])
        hb_ref[halo:halo + tm, cs] = _dot(u_ref[...], wb_ref[:, cs])
    tail_ref[j, 0] = ha_ref[tm:tm + halo, :]
    tail_ref[j, 1] = hb_ref[tm:tm + halo, :]

    def conv(hbuf, cw, cb, r0, cs):
        out = cb
        for t in range(CONV_WIDTH):
            out = out + hbuf[halo + r0 - t:halo + r0 - t + FFN_ROWS, cs] * cw[CONV_WIDTH - 1 - t:CONV_WIDTH - t, :]
        return out

    for c in range(tn // FFN_CHUNK):
        cs = slice(c * FFN_CHUNK, (c + 1) * FFN_CHUNK)
        cwa, cwb, cba, cbb = cwa_ref[:, cs], cwb_ref[:, cs], cba_ref[:, cs], cbb_ref[:, cs]
        for r0 in range(0, tm, FFN_ROWS):
            a = conv(ha_ref, cwa, cba, r0, cs)
            bgate = conv(hb_ref, cwb, cbb, r0, cs)
            act_ref[r0:r0 + FFN_ROWS, cs] = (_silu(a) * bgate).astype(BF16)
        for n0 in range(0, o_ref.shape[1], FFN_DOWN_COLS):
            ns = slice(n0, n0 + FFN_DOWN_COLS)
            o_ref[:, ns] += _dot(act_ref[:, cs], wd_ref[cs, ns])

    @pl.when(j == pl.num_programs(1) - 1)
    def _():
        residual_copy().wait()

        def finish(rows, y):
            o_ref[rows, :] = x1_ref[rows, :] + (1.0 + g2_ref[...]) * (_rms(y) * npost_ref[...])

        _for_row_chunks(tm, lambda rows: o_ref[rows, :], finish)


def _ffn(u2, w_up, conv_w, conv_b, w_down, x1, npost, g2, tm=1024, tn=512):
    s, d = x1.shape
    tm = min(tm, s)
    nj = D_FF // tn
    assert D_FF % tn == 0
    vec = pl.BlockSpec((1, d), lambda i, j: (0, 0))
    rows = pl.BlockSpec((tm, d), lambda i, j: (i, 0))
    resid = pl.BlockSpec(memory_space=pl.ANY)
    return pl.pallas_call(
        _ffn_kernel,
        grid=(s // tm, nj),
        in_specs=[rows,
                  pl.BlockSpec((d, tn), lambda i, j: (0, j)),
                  pl.BlockSpec((d, tn), lambda i, j: (0, j + nj)),
                  pl.BlockSpec((CONV_WIDTH, tn), lambda i, j: (0, j)),
                  pl.BlockSpec((CONV_WIDTH, tn), lambda i, j: (0, j + nj)),
                  pl.BlockSpec((1, tn), lambda i, j: (0, j)),
                  pl.BlockSpec((1, tn), lambda i, j: (0, j + nj)),
                  pl.BlockSpec((tn, d), lambda i, j: (j, 0)),
                  resid, vec, vec],
        out_specs=rows,
        out_shape=jax.ShapeDtypeStruct((s, d), F32),
        scratch_shapes=[pltpu.VMEM((tm + SUBLANES, tn), F32),
                        pltpu.VMEM((tm + SUBLANES, tn), F32),
                        pltpu.VMEM((tm, tn), BF16),
                        pltpu.VMEM((nj, 2, SUBLANES, tn), F32),
                        pltpu.VMEM((tm, d), F32),
                        pltpu.SemaphoreType.DMA(())],
        compiler_params=_cparams("arbitrary", "arbitrary"),
        name="ffn",
    )(u2, w_up, w_up, conv_w, conv_w, conv_b, conv_b, w_down, x1, npost, g2)


def _block(x, c, ada_w, ada_b, mix_norm_pre, mix_norm_post, w_in, ret_w_o, moba_w_o, w_out,
           ffn_norm_pre, ffn_norm_post, ffn_w_up, ffn_conv_w, ffn_conv_b, ffn_w_down, cos2, sin2):
    d = x.shape[1]
    mod = _ada(c, ada_w, ada_b[None, :])
    sh1, sc1, g1, sh2, sc2, g2 = [mod[:, t * d:(t + 1) * d] for t in range(6)]
    proj = _inproj(x, mix_norm_pre[None, :], sc1, sh1, cos2, sin2, w_in.astype(BF16))
    o_ret = _retention(proj)
    o_moba = _moba(proj)
    merged = _merge(o_ret, o_moba, proj, ret_w_o.astype(BF16), moba_w_o.astype(BF16))
    x1, u2 = _mixout(merged, w_out.astype(BF16), x, mix_norm_post[None, :], g1,
                     ffn_norm_pre[None, :], sc2, sh2)
    return _ffn(u2, ffn_w_up.astype(BF16), ffn_conv_w, ffn_conv_b[None, :], ffn_w_down.astype(BF16),
                x1, ffn_norm_post[None, :], g2)


def kernel(x, c, ada_w, ada_b, mix_norm_pre, mix_norm_post, w_in, ret_w_o, moba_w_o, w_out,
           ffn_norm_pre, ffn_norm_post, ffn_w_up, ffn_conv_w, ffn_conv_b, ffn_w_down):
    batch, s, _ = x.shape
    depth = ada_w.shape[0]
    cos2, sin2 = _rope_tables(s)
    outs = []
    for bi in range(batch):
        xb = x[bi]
        for l in range(depth):
            xb = _block(xb, c[bi:bi + 1], ada_w[l], ada_b[l], mix_norm_pre[l], mix_norm_post[l], w_in[l],
                        ret_w_o[l], moba_w_o[l], w_out[l], ffn_norm_pre[l], ffn_norm_post[l], ffn_w_up[l],
                        ffn_conv_w[l], ffn_conv_b[l], ffn_w_down[l], cos2, sin2)
        outs.append(xb)
    return jnp.stack(outs)
```

```python
import math

import jax
import jax.numpy as jnp
from jax import lax
from jax.experimental import pallas as pl
from jax.experimental.pallas import tpu as pltpu

D_MODEL = 2048
RET_HEADS = 8
RET_DK = 128
RET_DV = 256
ROPE_BASE = 10000.0
MOBA_HEADS = 16
MOBA_HD = 128
MOBA_BLOCK = 256
MOBA_TOPK = 3
D_FF = 5632
CONV_WIDTH = 3
NORM_EPS = 1e-6
GN_EPS = 1e-5

RET_QK_W = RET_HEADS * RET_DK
RET_V_W = RET_HEADS * RET_DV
MOBA_W = MOBA_HEADS * MOBA_HD
OFF_RQ = 0
OFF_RK = OFF_RQ + RET_QK_W
OFF_RV = OFF_RK + RET_QK_W
OFF_RG = OFF_RV + RET_V_W
OFF_MQ = OFF_RG + RET_V_W
OFF_MK = OFF_MQ + MOBA_W
OFF_MV = OFF_MK + MOBA_W
OFF_GA = OFF_MV + MOBA_W
OFF_GB = OFF_GA + D_MODEL
IN_WIDTH = OFF_GB + D_MODEL

LANES = 128
SUBLANES = 8
NEG = -1e30
MOBA_QSCALE = math.log2(math.e) / math.sqrt(MOBA_HD)
ROPE_SPAN = 128
RET_GROUP = 4
MOBA_GROUP = 2
MOBA_SLOTS = 6
MOBA_PAIR = 2
FFN_CHUNK = 256
FFN_ROWS = 64
FFN_DOWN_COLS = 512
MIX_ROWS = 256
NORM_ROWS = 32
NORM_UNROLL = 4
VMEM_LIMIT = 58 * 1024 * 1024

F32 = jnp.float32
BF16 = jnp.bfloat16


def _cparams(*sem):
    return pltpu.CompilerParams(dimension_semantics=sem, vmem_limit_bytes=VMEM_LIMIT)


def _dot(a, b):
    return jnp.dot(a, b, preferred_element_type=F32)


def _dot_nt(a, b):
    return lax.dot_general(a, b, (((1,), (1,)), ((), ())), preferred_element_type=F32)


def _split_bf16(x):
    hi = x.astype(BF16)
    lo = (x - hi.astype(F32)).astype(BF16)
    return hi, lo


def _rms(x):
    return x * lax.rsqrt(jnp.mean(x * x, axis=-1, keepdims=True) + NORM_EPS)


def _silu(x):
    return x * jax.nn.sigmoid(x)


def _for_row_chunks(n_rows, load, finish):
    step = NORM_ROWS * NORM_UNROLL

    def body(r, carry):
        pieces = [pl.ds(pl.multiple_of(r * step + t * NORM_ROWS, NORM_ROWS), NORM_ROWS)
                  for t in range(NORM_UNROLL)]
        loaded = [load(rows) for rows in pieces]
        for rows, vals in zip(pieces, loaded):
            finish(rows, vals)
        return carry

    lax.fori_loop(0, n_rows // step, body, 0)


def _ada_kernel(c_ref, w_ref, b_ref, o_ref):
    c = jnp.broadcast_to(_silu(c_ref[...]), (SUBLANES, c_ref.shape[1]))
    c_hi, c_lo = _split_bf16(c)
    w_hi, w_lo = _split_bf16(w_ref[...])
    r = _dot(c_hi, w_hi) + _dot(c_lo, w_hi) + _dot(c_hi, w_lo)
    o_ref[...] = r[0:1] + b_ref[...]


def _ada(c, w, b, tn=1024):
    d, n = w.shape
    return pl.pallas_call(
        _ada_kernel,
        grid=(n // tn,),
        in_specs=[pl.BlockSpec((1, d), lambda j: (0, 0)),
                  pl.BlockSpec((d, tn), lambda j: (0, j)),
                  pl.BlockSpec((1, tn), lambda j: (0, j))],
        out_specs=pl.BlockSpec((1, tn), lambda j: (0, j)),
        out_shape=jax.ShapeDtypeStruct((1, n), F32),
        compiler_params=_cparams("arbitrary"),
        name="ada",
    )(c, w, b)


def _rope_kernel(inv_ref, cos_ref, sin_ref):
    ts = cos_ref.shape[0]
    spans = ts // ROPE_SPAN
    inv = inv_ref[...]
    lane = lax.broadcasted_iota(jnp.int32, (ROPE_SPAN, LANES), 1)
    off = lax.broadcasted_iota(jnp.int32, (ROPE_SPAN, LANES), 0).astype(F32) * inv
    cos_o, sin_o = jnp.cos(off), jnp.sin(off)
    first = pl.program_id(0) * spans
    base = ((lax.broadcasted_iota(jnp.int32, (spans, LANES), 0) + first) * ROPE_SPAN).astype(F32) * inv
    cos_b, sin_b = jnp.cos(base), jnp.sin(base)
    for b in range(spans):
        cb, sb = cos_b[b:b + 1, :], sin_b[b:b + 1, :]
        sin = sb * cos_o + cb * sin_o
        cos_ref[b * ROPE_SPAN:(b + 1) * ROPE_SPAN, :] = cb * cos_o - sb * sin_o
        sin_ref[b * ROPE_SPAN:(b + 1) * ROPE_SPAN, :] = jnp.where(lane < RET_DK // 2, -sin, sin)


def _rope_tables(s, ts=2048):
    ts = min(ts, s)
    half = RET_DK // 2
    inv = ROPE_BASE ** (-jnp.arange(half, dtype=F32) / half)
    inv2 = jnp.concatenate([inv, inv])[None, :]
    return pl.pallas_call(
        _rope_kernel,
        grid=(s // ts,),
        in_specs=[pl.BlockSpec((1, LANES), lambda i: (0, 0))],
        out_specs=[pl.BlockSpec((ts, LANES), lambda i: (i, 0))] * 2,
        out_shape=[jax.ShapeDtypeStruct((s, LANES), F32)] * 2,
        compiler_params=_cparams("arbitrary"),
        name="rope",
    )(inv2)


def _inproj_kernel(x_ref, nw_ref, sc_ref, sh_ref, cos_ref, sin_ref, w_ref, o_ref, u_ref):
    j = pl.program_id(1)
    tn = o_ref.shape[1]

    @pl.when(j == 0)
    def _():
        def finish(rows, x):
            y = _rms(x) * nw_ref[...]
            u_ref[rows, :] = (y * (1.0 + sc_ref[...]) + sh_ref[...]).astype(BF16)

        _for_row_chunks(x_ref.shape[0], lambda rows: x_ref[rows, :], finish)

    r = _dot(u_ref[...], w_ref[...])
    q_tiles = RET_QK_W // tn
    is_mq = (j >= OFF_MQ // tn) & (j < OFF_MK // tn)
    o_ref[...] = (r * jnp.where(is_mq, MOBA_QSCALE, 1.0).astype(F32)).astype(BF16)

    @pl.when(j < 2 * q_tiles)
    def _():
        scale = jnp.where(j >= q_tiles, RET_DK ** -0.5, 1.0).astype(F32)
        cos = cos_ref[...]
        sin = sin_ref[...]
        for h in range(tn // RET_DK):
            xh = r[:, h * RET_DK:(h + 1) * RET_DK]
            rot = xh * cos + pltpu.roll(xh, RET_DK // 2, axis=1) * sin
            o_ref[:, h * RET_DK:(h + 1) * RET_DK] = (rot * scale).astype(BF16)


def _inproj(x, nw, sc, sh, cos2, sin2, w, tm=1024, tn=1024):
    s, d = x.shape
    n = w.shape[1]
    tm = min(tm, s)
    vec = pl.BlockSpec((1, d), lambda i, j: (0, 0))
    tab = pl.BlockSpec((tm, LANES), lambda i, j: (i, 0))
    return pl.pallas_call(
        _inproj_kernel,
        grid=(s // tm, n // tn),
        in_specs=[pl.BlockSpec((tm, d), lambda i, j: (i, 0)), vec, vec, vec, tab, tab,
                  pl.BlockSpec((d, tn), lambda i, j: (0, j))],
        out_specs=pl.BlockSpec((tm, tn), lambda i, j: (i, j)),
        out_shape=jax.ShapeDtypeStruct((s, n), BF16),
        scratch_shapes=[pltpu.VMEM((tm, d), BF16)],
        compiler_params=_cparams("arbitrary", "arbitrary"),
        name="inproj",
    )(x, nw, sc, sh, cos2, sin2, w)


def _ret_kernel(lg_ref, q_ref, k_ref, v_ref, g_ref, o_ref, r_ref, dec_ref, xi_ref, zeta_ref):
    n = pl.program_id(1)
    c = q_ref.shape[0]
    dk, dv = RET_DK, RET_DV

    @pl.when(n == 0)
    def _():
        r_ref[...] = jnp.zeros_like(r_ref)
        d = lax.broadcasted_iota(jnp.int32, (c, c), 0) - lax.broadcasted_iota(jnp.int32, (c, c), 1)
        row = lax.broadcasted_iota(jnp.int32, (c, dk), 0)
        for h in range(RET_GROUP):
            lg = lg_ref[h, 0:1, 0:1]
            dec_ref[h] = jnp.where(d >= 0, jnp.exp(jnp.maximum(d, 0).astype(F32) * lg), 0.0)
            xi_ref[h] = jnp.exp((row + 1).astype(F32) * lg)
            zeta_ref[h] = jnp.exp((c - 1 - row).astype(F32) * lg)

    for h in range(RET_GROUP):
        q = q_ref[:, h * dk:(h + 1) * dk]
        k = k_ref[:, h * dk:(h + 1) * dk]
        v = v_ref[:, h * dv:(h + 1) * dv]
        s = _dot_nt(q, k) * dec_ref[h]
        inner = _dot(s.astype(BF16), v)
        r = r_ref[h]
        cross = _dot((q.astype(F32) * xi_ref[h]).astype(BF16), r.astype(BF16))
        o = inner + cross
        kz = (k.astype(F32) * zeta_ref[h]).T.astype(BF16)
        r_ref[h] = r * jnp.exp(lg_ref[h, 0:1, 0:1] * c) + _dot(kz, v)
        oc = o - jnp.mean(o, axis=-1, keepdims=True)
        on = oc * lax.rsqrt(jnp.mean(oc * oc, axis=-1, keepdims=True) + GN_EPS)
        o_ref[:, h * dv:(h + 1) * dv] = (_silu(g_ref[:, h * dv:(h + 1) * dv].astype(F32)) * on).astype(BF16)


def _retention(proj, chunk=512):
    s = proj.shape[0]
    chunk = min(chunk, s)
    log_g = jnp.log1p(-jnp.exp2(-5.0 - jnp.arange(RET_HEADS, dtype=F32)))
    lg = jnp.broadcast_to(log_g[:, None, None], (RET_HEADS, SUBLANES, LANES))
    qk_w, v_w = RET_GROUP * RET_DK, RET_GROUP * RET_DV
    assert RET_HEADS % RET_GROUP == 0
    return pl.pallas_call(
        _ret_kernel,
        grid=(RET_HEADS // RET_GROUP, s // chunk),
        in_specs=[pl.BlockSpec((RET_GROUP, SUBLANES, LANES), lambda h, n: (h, 0, 0)),
                  pl.BlockSpec((chunk, qk_w), lambda h, n: (n, OFF_RQ // qk_w + h)),
                  pl.BlockSpec((chunk, qk_w), lambda h, n: (n, OFF_RK // qk_w + h)),
                  pl.BlockSpec((chunk, v_w), lambda h, n: (n, OFF_RV // v_w + h)),
                  pl.BlockSpec((chunk, v_w), lambda h, n: (n, OFF_RG // v_w + h))],
        out_specs=pl.BlockSpec((chunk, v_w), lambda h, n: (n, h)),
        out_shape=jax.ShapeDtypeStruct((s, RET_V_W), BF16),
        scratch_shapes=[pltpu.VMEM((RET_GROUP, RET_DK, RET_DV), F32),
                        pltpu.VMEM((RET_GROUP, chunk, chunk), F32),
                        pltpu.VMEM((RET_GROUP, chunk, RET_DK), F32),
                        pltpu.VMEM((RET_GROUP, chunk, RET_DK), F32)],
        compiler_params=_cparams("arbitrary", "arbitrary"),
        name="ret",
    )(lg, proj, proj, proj, proj)


def _moba_kernel(q_ref, k_ref, v_hbm, o_ref, km_ref, oh_ref, vt_ref, qt_ref, acc_ref, m_ref, v_ref, v_sem,
                 *slots):
    hp = pl.program_id(0)
    bq = pl.program_id(1)
    tq = q_ref.shape[0]
    hd = MOBA_HD
    blk = MOBA_BLOCK
    nb = k_ref.shape[0] // blk
    gw = MOBA_GROUP * blk
    vrows = vt_ref.shape[2]

    @pl.when((hp == 0) & (bq == 0))
    def _():
        lane_i = lax.broadcasted_iota(jnp.int32, (blk, LANES), 1)

        def body(n, carry):
            oh_ref[pl.ds(pl.multiple_of(n * blk, blk), blk), :] = jnp.where(lane_i == n, 1.0, 0.0).astype(BF16)
            return carry

        lax.fori_loop(0, nb, body, 0)

    def value_copy(pair):
        cols = pl.ds(pl.multiple_of(OFF_MV + pair * (MOBA_PAIR * hd), MOBA_PAIR * hd), MOBA_PAIR * hd)
        return pltpu.make_async_copy(v_hbm.at[:, cols], v_ref, v_sem)

    @pl.when((hp == 0) & (bq == 0))
    def _():
        value_copy(0).start()

    @pl.when(bq == 0)
    def _():
        value_copy(hp).wait()
        km_ref[...] = jnp.zeros_like(km_ref)

        def body(g, carry):
            for t in range(MOBA_GROUP):
                n = g * MOBA_GROUP + t
                rows = pl.ds(pl.multiple_of(n * blk, blk), blk)
                km_ref[pl.ds(n, 1), :] = jnp.mean(k_ref[rows, :].astype(F32), axis=0, keepdims=True)
                for h in range(MOBA_PAIR):
                    vt_ref[h, g, 0:hd, t * blk:(t + 1) * blk] = (
                        v_ref[rows, h * hd:(h + 1) * hd].astype(F32).T.astype(BF16))
            for h in range(MOBA_PAIR):
                vt_ref[h, g, hd:vrows, :] = jnp.ones((vrows - hd, gw), BF16)
            return carry

        lax.fori_loop(0, nb // MOBA_GROUP, body, 0)

        @pl.when(hp + 1 < pl.num_programs(0))
        def _():
            value_copy(hp + 1).start()

    gr = km_ref.shape[0]
    row_i = lax.broadcasted_iota(jnp.int32, (gr, tq), 0)
    row_f = row_i.astype(F32)
    own = bq * (tq // blk) + lax.broadcasted_iota(jnp.int32, (gr, tq), 1) // blk
    for h in range(MOBA_PAIR):
        qt = q_ref[:, h * hd:(h + 1) * hd].astype(F32).T.astype(BF16)
        km_hi, km_lo = _split_bf16(km_ref[:, h * hd:(h + 1) * hd])
        g = jnp.where(row_i < own, _dot(km_hi, qt) + _dot(km_lo, qt), NEG)
        bias = jnp.full((gr, tq), NEG, F32)
        for _ in range(MOBA_TOPK):
            mx = jnp.max(g, axis=0, keepdims=True)
            idx = jnp.min(jnp.where(g == mx, row_f, float(gr)), axis=0, keepdims=True)
            hit = row_f == idx
            bias = jnp.where(hit, jnp.where(mx > 0.5 * NEG, 0.0, NEG), bias)
            g = jnp.where(hit, 2.0 * NEG, g)
        bias = jnp.where(row_i == own, 0.0, bias)
        qt_ref[h] = jnp.concatenate(
            [qt, bias.astype(BF16), jnp.full((LANES - gr, tq), NEG, BF16)], axis=0)
        m_ref[h] = jnp.full((1, tq), NEG, F32)
        acc_ref[h] = jnp.zeros((vrows, tq), F32)

    def scores(g, slot_ref):
        rows = pl.ds(pl.multiple_of(g * gw, gw), gw)
        oh = oh_ref[rows, :]
        for h in range(MOBA_PAIR):
            k_ext = jnp.concatenate([k_ref[rows, h * hd:(h + 1) * hd], oh], axis=1)
            slot_ref[h] = _dot(k_ext, qt_ref[h])

    def accumulate(g, slot_ref, causal):
        if causal:
            own_rows = pl.ds(pl.multiple_of((bq * tq) % gw, tq), tq)
            keep = lax.broadcasted_iota(jnp.int32, (tq, tq), 0) <= lax.broadcasted_iota(jnp.int32, (tq, tq), 1)
            for h in range(MOBA_PAIR):
                slot_ref[h, own_rows, :] = jnp.where(keep, slot_ref[h, own_rows, :], NEG)
        for h in range(MOBA_PAIR):
            s = slot_ref[h]
            m_old = m_ref[h]
            m_new = jnp.maximum(m_old, jnp.max(s, axis=0, keepdims=True))
            p = jnp.exp2(s - m_new).astype(BF16)
            acc_ref[h] = jnp.exp2(m_old - m_new) * acc_ref[h] + _dot(vt_ref[h, g], p)
            m_ref[h] = m_new

    nslot = len(slots)
    full_groups = (bq * tq) // gw
    scores(0, slots[0])

    def body(i, carry):
        g = nslot * i
        for u in range(nslot):
            scores(g + u + 1, slots[(u + 1) % nslot])
            accumulate(g + u, slots[u], False)
        return carry

    lax.fori_loop(0, full_groups // nslot, body, 0)
    base = (full_groups // nslot) * nslot
    for r in range(nslot):
        @pl.when(full_groups - base == r)
        def _(r=r):
            for u in range(r):
                scores(base + u + 1, slots[u + 1])
                accumulate(base + u, slots[u], False)
            accumulate(base + r, slots[r], True)

    for h in range(MOBA_PAIR):
        acc = acc_ref[h]
        o_ref[:, h * hd:(h + 1) * hd] = (acc[0:hd, :] / acc[hd:hd + 1, :]).T.astype(BF16)


def _moba(proj, tq=2 * MOBA_BLOCK):
    s = proj.shape[0]
    nb = s // MOBA_BLOCK
    w = MOBA_PAIR * MOBA_HD
    gw = MOBA_GROUP * MOBA_BLOCK
    bf16_rows = 2 * SUBLANES
    vrows = MOBA_HD + bf16_rows
    gate_rows = -(-nb // bf16_rows) * bf16_rows
    assert s % gw == 0 and gate_rows <= LANES and MOBA_HEADS % MOBA_PAIR == 0
    assert tq % MOBA_BLOCK == 0 and gw % tq == 0
    return pl.pallas_call(
        _moba_kernel,
        grid=(MOBA_HEADS // MOBA_PAIR, s // tq),
        in_specs=[pl.BlockSpec((tq, w), lambda h, b: (b, OFF_MQ // w + h)),
                  pl.BlockSpec((s, w), lambda h, b: (0, OFF_MK // w + h)),
                  pl.BlockSpec(memory_space=pl.ANY)],
        out_specs=pl.BlockSpec((tq, w), lambda h, b: (b, h)),
        out_shape=jax.ShapeDtypeStruct((s, MOBA_W), BF16),
        scratch_shapes=[pltpu.VMEM((gate_rows, w), F32),
                        pltpu.VMEM((s, LANES), BF16),
                        pltpu.VMEM((MOBA_PAIR, s // gw, vrows, gw), BF16),
                        pltpu.VMEM((MOBA_PAIR, 2 * MOBA_HD, tq), BF16),
                        pltpu.VMEM((MOBA_PAIR, vrows, tq), F32),
                        pltpu.VMEM((MOBA_PAIR, 1, tq), F32),
                        pltpu.VMEM((s, w), BF16),
                        pltpu.SemaphoreType.DMA(())]
                       + [pltpu.VMEM((MOBA_PAIR, gw, tq), F32)] * MOBA_SLOTS,
        compiler_params=_cparams("arbitrary", "arbitrary"),
        name="moba",
    )(proj, proj, proj)


def _merge_kernel(or_ref, om_ref, ga_ref, gb_ref, wr_ref, wm_ref, o_ref):
    yr = _dot(or_ref[...], wr_ref[...])
    ym = _dot(om_ref[...], wm_ref[...])
    ga = jax.nn.sigmoid(ga_ref[...].astype(F32))
    gb = jax.nn.sigmoid(gb_ref[...].astype(F32))
    o_ref[...] = (ga * yr + gb * ym).astype(BF16)


def _merge(o_ret, o_moba, proj, w_r, w_m, tm=1024, tn=512):
    s = o_ret.shape[0]
    tm = min(tm, s)
    d = D_MODEL
    return pl.pallas_call(
        _merge_kernel,
        grid=(s // tm, d // tn),
        in_specs=[pl.BlockSpec((tm, RET_V_W), lambda i, j: (i, 0)),
                  pl.BlockSpec((tm, MOBA_W), lambda i, j: (i, 0)),
                  pl.BlockSpec((tm, tn), lambda i, j: (i, OFF_GA // tn + j)),
                  pl.BlockSpec((tm, tn), lambda i, j: (i, OFF_GB // tn + j)),
                  pl.BlockSpec((RET_V_W, tn), lambda i, j: (0, j)),
                  pl.BlockSpec((MOBA_W, tn), lambda i, j: (0, j))],
        out_specs=pl.BlockSpec((tm, tn), lambda i, j: (i, j)),
        out_shape=jax.ShapeDtypeStruct((s, d), BF16),
        compiler_params=_cparams("arbitrary", "arbitrary"),
        name="merge",
    )(o_ret, o_moba, proj, proj, w_r, w_m)


def _mixout_kernel(m_ref, wo_ref, x_ref, npost_ref, g1_ref, npre_ref, sc_ref, sh_ref, x1_ref, u2_ref):
    for r0 in range(0, m_ref.shape[0], MIX_ROWS):
        rows = slice(r0, r0 + MIX_ROWS)
        y = _dot(m_ref[rows, :], wo_ref[...])
        x1 = x_ref[rows, :] + (1.0 + g1_ref[...]) * (_rms(y) * npost_ref[...])
        x1_ref[rows, :] = x1
        u2_ref[rows, :] = ((_rms(x1) * npre_ref[...]) * (1.0 + sc_ref[...]) + sh_ref[...]).astype(BF16)


def _mixout(merged, w_o, x, npost, g1, npre, sc2, sh2, tm=512):
    s, d = x.shape
    tm = min(tm, s)
    vec = pl.BlockSpec((1, d), lambda i: (0, 0))
    rows = pl.BlockSpec((tm, d), lambda i: (i, 0))
    return pl.pallas_call(
        _mixout_kernel,
        grid=(s // tm,),
        in_specs=[rows, pl.BlockSpec((d, d), lambda i: (0, 0), pipeline_mode=pl.Buffered(1)), rows,
                  vec, vec, vec, vec, vec],
        out_specs=[rows, rows],
        out_shape=[jax.ShapeDtypeStruct((s, d), F32), jax.ShapeDtypeStruct((s, d), BF16)],
        compiler_params=_cparams("arbitrary"),
        name="mixout",
    )(merged, w_o, x, npost, g1, npre, sc2, sh2)


def _ffn_kernel(u_ref, wa_ref, wb_ref, cwa_ref, cwb_ref, cba_ref, cbb_ref, wd_ref, x1_hbm, npost_ref, g2_ref,
                o_ref, ha_ref, hb_ref, act_ref, tail_ref, x1_ref, x1_sem):
    i = pl.program_id(0)
    j = pl.program_id(1)
    tm = u_ref.shape[0]
    tn = wa_ref.shape[1]
    halo = SUBLANES

    def residual_copy():
        return pltpu.make_async_copy(x1_hbm.at[pl.ds(pl.multiple_of(i * tm, tm), tm), :], x1_ref, x1_sem)

    @pl.when((i == 0) & (j == 0))
    def _():
        tail_ref[...] = jnp.zeros_like(tail_ref)

    @pl.when(j == 0)
    def _():
        residual_copy().start()
        o_ref[...] = jnp.zeros_like(o_ref)

    ha_ref[0:halo, :] = tail_ref[j, 0]
    hb_ref[0:halo, :] = tail_ref[j, 1]
    for c in range(tn // FFN_CHUNK):
        cs = slice(c * FFN_CHUNK, (c + 1) * FFN_CHUNK)
        ha_ref[halo:halo + tm, cs] = _dot(u_ref[...], wa_ref[:, cs])
        hb_ref[halo:halo + tm, cs] = _dot(u_ref[...], wb_ref[:, cs])
    tail_ref[j, 0] = ha_ref[tm:tm + halo, :]
    tail_ref[j, 1] = hb_ref[tm:tm + halo, :]

    def conv(hbuf, cw, cb, r0, cs):
        out = cb
        for t in range(CONV_WIDTH):
            out = out + hbuf[halo + r0 - t:halo + r0 - t + FFN_ROWS, cs] * cw[CONV_WIDTH - 1 - t:CONV_WIDTH - t, :]
        return out

    for c in range(tn // FFN_CHUNK):
        cs = slice(c * FFN_CHUNK, (c + 1) * FFN_CHUNK)
        cwa, cwb, cba, cbb = cwa_ref[:, cs], cwb_ref[:, cs], cba_ref[:, cs], cbb_ref[:, cs]
        for r0 in range(0, tm, FFN_ROWS):
            a = conv(ha_ref, cwa, cba, r0, cs)
            bgate = conv(hb_ref, cwb, cbb, r0, cs)
            act_ref[r0:r0 + FFN_ROWS, cs] = (_silu(a) * bgate).astype(BF16)
        for n0 in range(0, o_ref.shape[1], FFN_DOWN_COLS):
            ns = slice(n0, n0 + FFN_DOWN_COLS)
            o_ref[:, ns] += _dot(act_ref[:, cs], wd_ref[cs, ns])

    @pl.when(j == pl.num_programs(1) - 1)
    def _():
        residual_copy().wait()

        def finish(rows, y):
            o_ref[rows, :] = x1_ref[rows, :] + (1.0 + g2_ref[...]) * (_rms(y) * npost_ref[...])

        _for_row_chunks(tm, lambda rows: o_ref[rows, :], finish)


def _ffn(u2, w_up, conv_w, conv_b, w_down, x1, npost, g2, tm=1024, tn=512):
    s, d = x1.shape
    tm = min(tm, s)
    nj = D_FF // tn
    assert D_FF % tn == 0
    vec = pl.BlockSpec((1, d), lambda i, j: (0, 0))
    rows = pl.BlockSpec((tm, d), lambda i, j: (i, 0))
    resid = pl.BlockSpec(memory_space=pl.ANY)
    return pl.pallas_call(
        _ffn_kernel,
        grid=(s // tm, nj),
        in_specs=[rows,
                  pl.BlockSpec((d, tn), lambda i, j: (0, j)),
                  pl.BlockSpec((d, tn), lambda i, j: (0, j + nj)),
                  pl.BlockSpec((CONV_WIDTH, tn), lambda i, j: (0, j)),
                  pl.BlockSpec((CONV_WIDTH, tn), lambda i, j: (0, j + nj)),
                  pl.BlockSpec((1, tn), lambda i, j: (0, j)),
                  pl.BlockSpec((1, tn), lambda i, j: (0, j + nj)),
                  pl.BlockSpec((tn, d), lambda i, j: (j, 0)),
                  resid, vec, vec],
        out_specs=rows,
        out_shape=jax.ShapeDtypeStruct((s, d), F32),
        scratch_shapes=[pltpu.VMEM((tm + SUBLANES, tn), F32),
                        pltpu.VMEM((tm + SUBLANES, tn), F32),
                        pltpu.VMEM((tm, tn), BF16),
                        pltpu.VMEM((nj, 2, SUBLANES, tn), F32),
                        pltpu.VMEM((tm, d), F32),
                        pltpu.SemaphoreType.DMA(())],
        compiler_params=_cparams("arbitrary", "arbitrary"),
        name="ffn",
    )(u2, w_up, w_up, conv_w, conv_w, conv_b, conv_b, w_down, x1, npost, g2)


def _block(x, c, ada_w, ada_b, mix_norm_pre, mix_norm_post, w_in, ret_w_o, moba_w_o, w_out,
           ffn_norm_pre, ffn_norm_post, ffn_w_up, ffn_conv_w, ffn_conv_b, ffn_w_down, cos2, sin2):
    d = x.shape[1]
    mod = _ada(c, ada_w, ada_b[None, :])
    sh1, sc1, g1, sh2, sc2, g2 = [mod[:, t * d:(t + 1) * d] for t in range(6)]
    proj = _inproj(x, mix_norm_pre[None, :], sc1, sh1, cos2, sin2, w_in.astype(BF16))
    o_ret = _retention(proj)
    o_moba = _moba(proj)
    merged = _merge(o_ret, o_moba, proj, ret_w_o.astype(BF16), moba_w_o.astype(BF16))
    x1, u2 = _mixout(merged, w_out.astype(BF16), x, mix_norm_post[None, :], g1,
                     ffn_norm_pre[None, :], sc2, sh2)
    return _ffn(u2, ffn_w_up.astype(BF16), ffn_conv_w, ffn_conv_b[None, :], ffn_w_down.astype(BF16),
                x1, ffn_norm_post[None, :], g2)


def kernel(x, c, ada_w, ada_b, mix_norm_pre, mix_norm_post, w_in, ret_w_o, moba_w_o, w_out,
           ffn_norm_pre, ffn_norm_post, ffn_w_up, ffn_conv_w, ffn_conv_b, ffn_w_down):
    batch, s, _ = x.shape
    depth = ada_w.shape[0]
    cos2, sin2 = _rope_tables(s)
    outs = []
    for bi in range(batch):
        xb = x[bi]
        for l in range(depth):
            xb = _block(xb, c[bi:bi + 1], ada_w[l], ada_b[l], mix_norm_pre[l], mix_norm_post[l], w_in[l],
                        ret_w_o[l], moba_w_o[l], w_out[l], ffn_norm_pre[l], ffn_norm_post[l], ffn_w_up[l],
                        ffn_conv_w[l], ffn_conv_b[l], ffn_w_down[l], cos2, sin2)
        outs.append(xb)
    return jnp.stack(outs)
```

```python
import math

import jax
import jax.numpy as jnp
from jax import lax
from jax.experimental import pallas as pl
from jax.experimental.pallas import tpu as pltpu

D_MODEL = 2048
RET_HEADS = 8
RET_DK = 128
RET_DV = 256
ROPE_BASE = 10000.0
MOBA_HEADS = 16
MOBA_HD = 128
MOBA_BLOCK = 256
MOBA_TOPK = 3
D_FF = 5632
CONV_WIDTH = 3
NORM_EPS = 1e-6
GN_EPS = 1e-5

RET_QK_W = RET_HEADS * RET_DK
RET_V_W = RET_HEADS * RET_DV
MOBA_W = MOBA_HEADS * MOBA_HD
OFF_RQ = 0
OFF_RK = OFF_RQ + RET_QK_W
OFF_RV = OFF_RK + RET_QK_W
OFF_RG = OFF_RV + RET_V_W
OFF_MQ = OFF_RG + RET_V_W
OFF_MK = OFF_MQ + MOBA_W
OFF_MV = OFF_MK + MOBA_W
OFF_GA = OFF_MV + MOBA_W
OFF_GB = OFF_GA + D_MODEL
IN_WIDTH = OFF_GB + D_MODEL

LANES = 128
SUBLANES = 8
NEG = -1e30
MOBA_QSCALE = math.log2(math.e) / math.sqrt(MOBA_HD)
ROPE_SPAN = 128
RET_GROUP = 4
MOBA_GROUP = 2
MOBA_SLOTS = 6
MOBA_PAIR = 2
FFN_CHUNK = 256
FFN_ROWS = 64
FFN_DOWN_COLS = 512
MIX_ROWS = 256
NORM_ROWS = 32
NORM_UNROLL = 4
VMEM_LIMIT = 58 * 1024 * 1024

F32 = jnp.float32
BF16 = jnp.bfloat16


def _cparams(*sem):
    return pltpu.CompilerParams(dimension_semantics=sem, vmem_limit_bytes=VMEM_LIMIT)


def _dot(a, b):
    return jnp.dot(a, b, preferred_element_type=F32)


def _dot_nt(a, b):
    return lax.dot_general(a, b, (((1,), (1,)), ((), ())), preferred_element_type=F32)


def _split_bf16(x):
    hi = x.astype(BF16)
    lo = (x - hi.astype(F32)).astype(BF16)
    return hi, lo


def _rms(x):
    return x * lax.rsqrt(jnp.mean(x * x, axis=-1, keepdims=True) + NORM_EPS)


def _silu(x):
    return x * jax.nn.sigmoid(x)


def _for_row_chunks(n_rows, load, finish):
    step = NORM_ROWS * NORM_UNROLL

    def body(r, carry):
        pieces = [pl.ds(pl.multiple_of(r * step + t * NORM_ROWS, NORM_ROWS), NORM_ROWS)
                  for t in range(NORM_UNROLL)]
        loaded = [load(rows) for rows in pieces]
        for rows, vals in zip(pieces, loaded):
            finish(rows, vals)
        return carry

    lax.fori_loop(0, n_rows // step, body, 0)


def _ada_kernel(c_ref, w_ref, b_ref, o_ref):
    c = jnp.broadcast_to(_silu(c_ref[...]), (SUBLANES, c_ref.shape[1]))
    c_hi, c_lo = _split_bf16(c)
    w_hi, w_lo = _split_bf16(w_ref[...])
    r = _dot(c_hi, w_hi) + _dot(c_lo, w_hi) + _dot(c_hi, w_lo)
    o_ref[...] = r[0:1] + b_ref[...]


def _ada(c, w, b, tn=1024):
    d, n = w.shape
    return pl.pallas_call(
        _ada_kernel,
        grid=(n // tn,),
        in_specs=[pl.BlockSpec((1, d), lambda j: (0, 0)),
                  pl.BlockSpec((d, tn), lambda j: (0, j)),
                  pl.BlockSpec((1, tn), lambda j: (0, j))],
        out_specs=pl.BlockSpec((1, tn), lambda j: (0, j)),
        out_shape=jax.ShapeDtypeStruct((1, n), F32),
        compiler_params=_cparams("arbitrary"),
        name="ada",
    )(c, w, b)


def _rope_kernel(inv_ref, cos_ref, sin_ref):
    ts = cos_ref.shape[0]
    spans = ts // ROPE_SPAN
    inv = inv_ref[...]
    lane = lax.broadcasted_iota(jnp.int32, (ROPE_SPAN, LANES), 1)
    off = lax.broadcasted_iota(jnp.int32, (ROPE_SPAN, LANES), 0).astype(F32) * inv
    cos_o, sin_o = jnp.cos(off), jnp.sin(off)
    first = pl.program_id(0) * spans
    base = ((lax.broadcasted_iota(jnp.int32, (spans, LANES), 0) + first) * ROPE_SPAN).astype(F32) * inv
    cos_b, sin_b = jnp.cos(base), jnp.sin(base)
    for b in range(spans):
        cb, sb = cos_b[b:b + 1, :], sin_b[b:b + 1, :]
        sin = sb * cos_o + cb * sin_o
        cos_ref[b * ROPE_SPAN:(b + 1) * ROPE_SPAN, :] = cb * cos_o - sb * sin_o
        sin_ref[b * ROPE_SPAN:(b + 1) * ROPE_SPAN, :] = jnp.where(lane < RET_DK // 2, -sin, sin)


def _rope_tables(s, ts=2048):
    ts = min(ts, s)
    half = RET_DK // 2
    inv = ROPE_BASE ** (-jnp.arange(half, dtype=F32) / half)
    inv2 = jnp.concatenate([inv, inv])[None, :]
    return pl.pallas_call(
        _rope_kernel,
        grid=(s // ts,),
        in_specs=[pl.BlockSpec((1, LANES), lambda i: (0, 0))],
        out_specs=[pl.BlockSpec((ts, LANES), lambda i: (i, 0))] * 2,
        out_shape=[jax.ShapeDtypeStruct((s, LANES), F32)] * 2,
        compiler_params=_cparams("arbitrary"),
        name="rope",
    )(inv2)


def _inproj_kernel(x_hbm, nw_ref, sc_ref, sh_ref, cos_ref, sin_ref, w_ref, o_ref, u_ref, x_ref, x_sem):
    i = pl.program_id(0)
    j = pl.program_id(1)
    tm, tn = o_ref.shape
    nj = IN_WIDTH // tn
    chunk = tm // nj

    def x_copy(tile, c):
        src = pl.ds(pl.multiple_of(tile * tm + c * chunk, chunk), chunk)
        dst = pl.ds(pl.multiple_of(c * chunk, chunk), chunk)
        return pltpu.make_async_copy(x_hbm.at[src, :], x_ref.at[dst, :], x_sem)

    @pl.when(j == 0)
    def _():
        @pl.when(i == 0)
        def _():
            for c in range(nj):
                x_copy(0, c).start()

        for c in range(nj):
            x_copy(i, c).wait()

        def finish(rows, x):
            y = _rms(x) * nw_ref[...]
            u_ref[rows, :] = (y * (1.0 + sc_ref[...]) + sh_ref[...]).astype(BF16)

        _for_row_chunks(tm, lambda rows: x_ref[rows, :], finish)

    @pl.when(i + 1 < pl.num_programs(0))
    def _():
        x_copy(i + 1, j).start()

    r = _dot(u_ref[...], w_ref[...])
    is_mq = (j >= OFF_MQ // tn) & (j < OFF_MK // tn)
    o_ref[...] = (r * jnp.where(is_mq, MOBA_QSCALE, 1.0).astype(F32)).astype(BF16)

    @pl.when(j * tn < 2 * RET_QK_W)
    def _():
        cos = cos_ref[...]
        sin = sin_ref[...]
        for h in range(tn // RET_DK):
            scale = jnp.where(j * tn + h * RET_DK >= RET_QK_W, RET_DK ** -0.5, 1.0).astype(F32)
            xh = r[:, h * RET_DK:(h + 1) * RET_DK]
            rot = xh * cos + pltpu.roll(xh, RET_DK // 2, axis=1) * sin
            o_ref[:, h * RET_DK:(h + 1) * RET_DK] = (rot * scale).astype(BF16)


def _inproj(x, nw, sc, sh, cos2, sin2, w, tm=1024, tn=2048):
    s, d = x.shape
    n = w.shape[1]
    tm = min(tm, s)
    nj = n // tn
    assert n == IN_WIDTH and tm % nj == 0 and (2 * RET_QK_W) % tn == 0 and OFF_MQ % tn == 0 and OFF_MK % tn == 0
    vec = pl.BlockSpec((1, d), lambda i, j: (0, 0))
    tab = pl.BlockSpec((tm, LANES), lambda i, j: (i, 0))
    return pl.pallas_call(
        _inproj_kernel,
        grid=(s // tm, nj),
        in_specs=[pl.BlockSpec(memory_space=pl.ANY), vec, vec, vec, tab, tab,
                  pl.BlockSpec((d, tn), lambda i, j: (0, j))],
        out_specs=pl.BlockSpec((tm, tn), lambda i, j: (i, j)),
        out_shape=jax.ShapeDtypeStruct((s, n), BF16),
        scratch_shapes=[pltpu.VMEM((tm, d), BF16),
                        pltpu.VMEM((tm, d), F32),
                        pltpu.SemaphoreType.DMA(())],
        compiler_params=_cparams("arbitrary", "arbitrary"),
        name="inproj",
    )(x, nw, sc, sh, cos2, sin2, w)


def _ret_kernel(lg_ref, q_ref, k_ref, v_ref, g_ref, o_ref, r_ref, dec_ref, xi_ref, zeta_ref):
    n = pl.program_id(1)
    c = q_ref.shape[0]
    dk, dv = RET_DK, RET_DV

    @pl.when(n == 0)
    def _():
        r_ref[...] = jnp.zeros_like(r_ref)
        d = lax.broadcasted_iota(jnp.int32, (c, c), 0) - lax.broadcasted_iota(jnp.int32, (c, c), 1)
        row = lax.broadcasted_iota(jnp.int32, (c, dk), 0)
        for h in range(RET_GROUP):
            lg = lg_ref[h, 0:1, 0:1]
            dec_ref[h] = jnp.where(d >= 0, jnp.exp(jnp.maximum(d, 0).astype(F32) * lg), 0.0)
            xi_ref[h] = jnp.exp((row + 1).astype(F32) * lg)
            zeta_ref[h] = jnp.exp((c - 1 - row).astype(F32) * lg)

    for h in range(RET_GROUP):
        q = q_ref[:, h * dk:(h + 1) * dk]
        k = k_ref[:, h * dk:(h + 1) * dk]
        v = v_ref[:, h * dv:(h + 1) * dv]
        s = _dot_nt(q, k) * dec_ref[h]
        inner = _dot(s.astype(BF16), v)
        r = r_ref[h]
        cross = _dot((q.astype(F32) * xi_ref[h]).astype(BF16), r.astype(BF16))
        o = inner + cross
        kz = (k.astype(F32) * zeta_ref[h]).T.astype(BF16)
        r_ref[h] = r * jnp.exp(lg_ref[h, 0:1, 0:1] * c) + _dot(kz, v)
        oc = o - jnp.mean(o, axis=-1, keepdims=True)
        on = oc * lax.rsqrt(jnp.mean(oc * oc, axis=-1, keepdims=True) + GN_EPS)
        o_ref[:, h * dv:(h + 1) * dv] = (_silu(g_ref[:, h * dv:(h + 1) * dv].astype(F32)) * on).astype(BF16)


def _retention(proj, chunk=512):
    s = proj.shape[0]
    chunk = min(chunk, s)
    log_g = jnp.log1p(-jnp.exp2(-5.0 - jnp.arange(RET_HEADS, dtype=F32)))
    lg = jnp.broadcast_to(log_g[:, None, None], (RET_HEADS, SUBLANES, LANES))
    qk_w, v_w = RET_GROUP * RET_DK, RET_GROUP * RET_DV
    assert RET_HEADS % RET_GROUP == 0
    return pl.pallas_call(
        _ret_kernel,
        grid=(RET_HEADS // RET_GROUP, s // chunk),
        in_specs=[pl.BlockSpec((RET_GROUP, SUBLANES, LANES), lambda h, n: (h, 0, 0)),
                  pl.BlockSpec((chunk, qk_w), lambda h, n: (n, OFF_RQ // qk_w + h)),
                  pl.BlockSpec((chunk, qk_w), lambda h, n: (n, OFF_RK // qk_w + h)),
                  pl.BlockSpec((chunk, v_w), lambda h, n: (n, OFF_RV // v_w + h)),
                  pl.BlockSpec((chunk, v_w), lambda h, n: (n, OFF_RG // v_w + h))],
        out_specs=pl.BlockSpec((chunk, v_w), lambda h, n: (n, h)),
        out_shape=jax.ShapeDtypeStruct((s, RET_V_W), BF16),
        scratch_shapes=[pltpu.VMEM((RET_GROUP, RET_DK, RET_DV), F32),
                        pltpu.VMEM((RET_GROUP, chunk, chunk), F32),
                        pltpu.VMEM((RET_GROUP, chunk, RET_DK), F32),
                        pltpu.VMEM((RET_GROUP, chunk, RET_DK), F32)],
        compiler_params=_cparams("arbitrary", "arbitrary"),
        name="ret",
    )(lg, proj, proj, proj, proj)


def _moba_kernel(q_ref, k_ref, v_hbm, o_ref, km_ref, oh_ref, vt_ref, qt_ref, acc_ref, m_ref, v_ref, v_sem,
                 *slots):
    hp = pl.program_id(0)
    bq = pl.program_id(1)
    tq = q_ref.shape[0]
    hd = MOBA_HD
    blk = MOBA_BLOCK
    nb = k_ref.shape[0] // blk
    gw = MOBA_GROUP * blk
    vrows = vt_ref.shape[2]

    @pl.when((hp == 0) & (bq == 0))
    def _():
        lane_i = lax.broadcasted_iota(jnp.int32, (blk, LANES), 1)

        def body(n, carry):
            oh_ref[pl.ds(pl.multiple_of(n * blk, blk), blk), :] = jnp.where(lane_i == n, 1.0, 0.0).astype(BF16)
            return carry

        lax.fori_loop(0, nb, body, 0)

    def value_copy(pair):
        cols = pl.ds(pl.multiple_of(OFF_MV + pair * (MOBA_PAIR * hd), MOBA_PAIR * hd), MOBA_PAIR * hd)
        return pltpu.make_async_copy(v_hbm.at[:, cols], v_ref, v_sem)

    @pl.when((hp == 0) & (bq == 0))
    def _():
        value_copy(0).start()

    @pl.when(bq == 0)
    def _():
        value_copy(hp).wait()
        km_ref[...] = jnp.zeros_like(km_ref)

        def body(g, carry):
            for t in range(MOBA_GROUP):
                n = g * MOBA_GROUP + t
                rows = pl.ds(pl.multiple_of(n * blk, blk), blk)
                km_ref[pl.ds(n, 1), :] = jnp.mean(k_ref[rows, :].astype(F32), axis=0, keepdims=True)
                for h in range(MOBA_PAIR):
                    vt_ref[h, g, 0:hd, t * blk:(t + 1) * blk] = (
                        v_ref[rows, h * hd:(h + 1) * hd].astype(F32).T.astype(BF16))
            for h in range(MOBA_PAIR):
                vt_ref[h, g, hd:vrows, :] = jnp.ones((vrows - hd, gw), BF16)
            return carry

        lax.fori_loop(0, nb // MOBA_GROUP, body, 0)

        @pl.when(hp + 1 < pl.num_programs(0))
        def _():
            value_copy(hp + 1).start()

    gr = km_ref.shape[0]
    row_i = lax.broadcasted_iota(jnp.int32, (gr, tq), 0)
    row_f = row_i.astype(F32)
    own = bq * (tq // blk) + lax.broadcasted_iota(jnp.int32, (gr, tq), 1) // blk
    for h in range(MOBA_PAIR):
        qt = q_ref[:, h * hd:(h + 1) * hd].astype(F32).T.astype(BF16)
        km_hi, km_lo = _split_bf16(km_ref[:, h * hd:(h + 1) * hd])
        g = jnp.where(row_i < own, _dot(km_hi, qt) + _dot(km_lo, qt), NEG)
        bias = jnp.full((gr, tq), NEG, F32)
        for _ in range(MOBA_TOPK):
            mx = jnp.max(g, axis=0, keepdims=True)
            idx = jnp.min(jnp.where(g == mx, row_f, float(gr)), axis=0, keepdims=True)
            hit = row_f == idx
            bias = jnp.where(hit, jnp.where(mx > 0.5 * NEG, 0.0, NEG), bias)
            g = jnp.where(hit, 2.0 * NEG, g)
        bias = jnp.where(row_i == own, 0.0, bias)
        qt_ref[h] = jnp.concatenate(
            [qt, bias.astype(BF16), jnp.full((LANES - gr, tq), NEG, BF16)], axis=0)
        m_ref[h] = jnp.full((1, tq), NEG, F32)
        acc_ref[h] = jnp.zeros((vrows, tq), F32)

    def scores(g, slot_ref):
        rows = pl.ds(pl.multiple_of(g * gw, gw), gw)
        oh = oh_ref[rows, :]
        for h in range(MOBA_PAIR):
            k_ext = jnp.concatenate([k_ref[rows, h * hd:(h + 1) * hd], oh], axis=1)
            slot_ref[h] = _dot(k_ext, qt_ref[h])

    def accumulate(g, slot_ref, causal):
        if causal:
            own_rows = pl.ds(pl.multiple_of((bq * tq) % gw, tq), tq)
            keep = lax.broadcasted_iota(jnp.int32, (tq, tq), 0) <= lax.broadcasted_iota(jnp.int32, (tq, tq), 1)
            for h in range(MOBA_PAIR):
                slot_ref[h, own_rows, :] = jnp.where(keep, slot_ref[h, own_rows, :], NEG)
        for h in range(MOBA_PAIR):
            s = slot_ref[h]
            m_old = m_ref[h]
            m_new = jnp.maximum(m_old, jnp.max(s, axis=0, keepdims=True))
            p = jnp.exp2(s - m_new).astype(BF16)
            acc_ref[h] = jnp.exp2(m_old - m_new) * acc_ref[h] + _dot(vt_ref[h, g], p)
            m_ref[h] = m_new

    nslot = len(slots)
    full_groups = (bq * tq) // gw
    scores(0, slots[0])

    def body(i, carry):
        g = nslot * i
        for u in range(nslot):
            scores(g + u + 1, slots[(u + 1) % nslot])
            accumulate(g + u, slots[u], False)
        return carry

    lax.fori_loop(0, full_groups // nslot, body, 0)
    base = (full_groups // nslot) * nslot
    for r in range(nslot):
        @pl.when(full_groups - base == r)
        def _(r=r):
            for u in range(r):
                scores(base + u + 1, slots[u + 1])
                accumulate(base + u, slots[u], False)
            accumulate(base + r, slots[r], True)

    for h in range(MOBA_PAIR):
        acc = acc_ref[h]
        o_ref[:, h * hd:(h + 1) * hd] = (acc[0:hd, :] / acc[hd:hd + 1, :]).T.astype(BF16)


def _moba(proj, tq=2 * MOBA_BLOCK):
    s = proj.shape[0]
    nb = s // MOBA_BLOCK
    w = MOBA_PAIR * MOBA_HD
    gw = MOBA_GROUP * MOBA_BLOCK
    bf16_rows = 2 * SUBLANES
    vrows = MOBA_HD + bf16_rows
    gate_rows = -(-nb // bf16_rows) * bf16_rows
    assert s % gw == 0 and gate_rows <= LANES and MOBA_HEADS % MOBA_PAIR == 0
    assert tq % MOBA_BLOCK == 0 and gw % tq == 0
    return pl.pallas_call(
        _moba_kernel,
        grid=(MOBA_HEADS // MOBA_PAIR, s // tq),
        in_specs=[pl.BlockSpec((tq, w), lambda h, b: (b, OFF_MQ // w + h)),
                  pl.BlockSpec((s, w), lambda h, b: (0, OFF_MK // w + h)),
                  pl.BlockSpec(memory_space=pl.ANY)],
        out_specs=pl.BlockSpec((tq, w), lambda h, b: (b, h)),
        out_shape=jax.ShapeDtypeStruct((s, MOBA_W), BF16),
        scratch_shapes=[pltpu.VMEM((gate_rows, w), F32),
                        pltpu.VMEM((s, LANES), BF16),
                        pltpu.VMEM((MOBA_PAIR, s // gw, vrows, gw), BF16),
                        pltpu.VMEM((MOBA_PAIR, 2 * MOBA_HD, tq), BF16),
                        pltpu.VMEM((MOBA_PAIR, vrows, tq), F32),
                        pltpu.VMEM((MOBA_PAIR, 1, tq), F32),
                        pltpu.VMEM((s, w), BF16),
                        pltpu.SemaphoreType.DMA(())]
                       + [pltpu.VMEM((MOBA_PAIR, gw, tq), F32)] * MOBA_SLOTS,
        compiler_params=_cparams("arbitrary", "arbitrary"),
        name="moba",
    )(proj, proj, proj)


def _merge_kernel(or_ref, om_ref, ga_ref, gb_ref, wr_ref, wm_ref, o_ref):
    yr = _dot(or_ref[...], wr_ref[...])
    ym = _dot(om_ref[...], wm_ref[...])
    ga = jax.nn.sigmoid(ga_ref[...].astype(F32))
    gb = jax.nn.sigmoid(gb_ref[...].astype(F32))
    o_ref[...] = (ga * yr + gb * ym).astype(BF16)


def _merge(o_ret, o_moba, proj, w_r, w_m, tm=1024, tn=512):
    s = o_ret.shape[0]
    tm = min(tm, s)
    d = D_MODEL
    return pl.pallas_call(
        _merge_kernel,
        grid=(s // tm, d // tn),
        in_specs=[pl.BlockSpec((tm, RET_V_W), lambda i, j: (i, 0)),
                  pl.BlockSpec((tm, MOBA_W), lambda i, j: (i, 0)),
                  pl.BlockSpec((tm, tn), lambda i, j: (i, OFF_GA // tn + j)),
                  pl.BlockSpec((tm, tn), lambda i, j: (i, OFF_GB // tn + j)),
                  pl.BlockSpec((RET_V_W, tn), lambda i, j: (0, j)),
                  pl.BlockSpec((MOBA_W, tn), lambda i, j: (0, j))],
        out_specs=pl.BlockSpec((tm, tn), lambda i, j: (i, j)),
        out_shape=jax.ShapeDtypeStruct((s, d), BF16),
        compiler_params=_cparams("arbitrary", "arbitrary"),
        name="merge",
    )(o_ret, o_moba, proj, proj, w_r, w_m)


def _mixout_kernel(m_ref, wo_ref, x_ref, npost_ref, g1_ref, npre_ref, sc_ref, sh_ref, x1_ref, u2_ref):
    for r0 in range(0, m_ref.shape[0], MIX_ROWS):
        rows = slice(r0, r0 + MIX_ROWS)
        y = _dot(m_ref[rows, :], wo_ref[...])
        x1 = x_ref[rows, :] + (1.0 + g1_ref[...]) * (_rms(y) * npost_ref[...])
        x1_ref[rows, :] = x1
        u2_ref[rows, :] = ((_rms(x1) * npre_ref[...]) * (1.0 + sc_ref[...]) + sh_ref[...]).astype(BF16)


def _mixout(merged, w_o, x, npost, g1, npre, sc2, sh2, tm=512):
    s, d = x.shape
    tm = min(tm, s)
    vec = pl.BlockSpec((1, d), lambda i: (0, 0))
    rows = pl.BlockSpec((tm, d), lambda i: (i, 0))
    return pl.pallas_call(
        _mixout_kernel,
        grid=(s // tm,),
        in_specs=[rows, pl.BlockSpec((d, d), lambda i: (0, 0), pipeline_mode=pl.Buffered(1)), rows,
                  vec, vec, vec, vec, vec],
        out_specs=[rows, rows],
        out_shape=[jax.ShapeDtypeStruct((s, d), F32), jax.ShapeDtypeStruct((s, d), BF16)],
        compiler_params=_cparams("arbitrary"),
        name="mixout",
    )(merged, w_o, x, npost, g1, npre, sc2, sh2)


def _ffn_kernel(u_ref, wa_ref, wb_ref, cwa_ref, cwb_ref, cba_ref, cbb_ref, wd_ref, x1_hbm, npost_ref, g2_ref,
                o_ref, ha_ref, hb_ref, act_ref, tail_ref, x1_ref, x1_sem):
    i = pl.program_id(0)
    j = pl.program_id(1)
    tm = u_ref.shape[0]
    tn = wa_ref.shape[1]
    halo = SUBLANES

    def residual_copy():
        return pltpu.make_async_copy(x1_hbm.at[pl.ds(pl.multiple_of(i * tm, tm), tm), :], x1_ref, x1_sem)

    @pl.when((i == 0) & (j == 0))
    def _():
        tail_ref[...] = jnp.zeros_like(tail_ref)

    @pl.when(j == 0)
    def _():
        residual_copy().start()
        o_ref[...] = jnp.zeros_like(o_ref)

    ha_ref[0:halo, :] = tail_ref[j, 0]
    hb_ref[0:halo, :] = tail_ref[j, 1]
    for c in range(tn // FFN_CHUNK):
        cs = slice(c * FFN_CHUNK, (c + 1) * FFN_CHUNK)
        ha_ref[halo:halo + tm, cs] = _dot(u_ref[...], wa_ref[:, cs])
        hb_ref[halo:halo + tm, cs] = _dot(u_ref[...], wb_ref[:, cs])
    tail_ref[j, 0] = ha_ref[tm:tm + halo, :]
    tail_ref[j, 1] = hb_ref[tm:tm + halo, :]

    def conv(hbuf, cw, cb, r0, cs):
        out = cb
        for t in range(CONV_WIDTH):
            out = out + hbuf[halo + r0 - t:halo + r0 - t + FFN_ROWS, cs] * cw[CONV_WIDTH - 1 - t:CONV_WIDTH - t, :]
        return out

    for c in range(tn // FFN_CHUNK):
        cs = slice(c * FFN_CHUNK, (c + 1) * FFN_CHUNK)
        cwa, cwb, cba, cbb = cwa_ref[:, cs], cwb_ref[:, cs], cba_ref[:, cs], cbb_ref[:, cs]
        for r0 in range(0, tm, FFN_ROWS):
            a = conv(ha_ref, cwa, cba, r0, cs)
            bgate = conv(hb_ref, cwb, cbb, r0, cs)
            act_ref[r0:r0 + FFN_ROWS, cs] = (_silu(a) * bgate).astype(BF16)
        for n0 in range(0, o_ref.shape[1], FFN_DOWN_COLS):
            ns = slice(n0, n0 + FFN_DOWN_COLS)
            o_ref[:, ns] += _dot(act_ref[:, cs], wd_ref[cs, ns])

    @pl.when(j == pl.num_programs(1) - 1)
    def _():
        residual_copy().wait()

        def finish(rows, y):
            o_ref[rows, :] = x1_ref[rows, :] + (1.0 + g2_ref[...]) * (_rms(y) * npost_ref[...])

        _for_row_chunks(tm, lambda rows: o_ref[rows, :], finish)


def _ffn(u2, w_up, conv_w, conv_b, w_down, x1, npost, g2, tm=1024, tn=512):
    s, d = x1.shape
    tm = min(tm, s)
    nj = D_FF // tn
    assert D_FF % tn == 0
    vec = pl.BlockSpec((1, d), lambda i, j: (0, 0))
    rows = pl.BlockSpec((tm, d), lambda i, j: (i, 0))
    resid = pl.BlockSpec(memory_space=pl.ANY)
    return pl.pallas_call(
        _ffn_kernel,
        grid=(s // tm, nj),
        in_specs=[rows,
                  pl.BlockSpec((d, tn), lambda i, j: (0, j)),
                  pl.BlockSpec((d, tn), lambda i, j: (0, j + nj)),
                  pl.BlockSpec((CONV_WIDTH, tn), lambda i, j: (0, j)),
                  pl.BlockSpec((CONV_WIDTH, tn), lambda i, j: (0, j + nj)),
                  pl.BlockSpec((1, tn), lambda i, j: (0, j)),
                  pl.BlockSpec((1, tn), lambda i, j: (0, j + nj)),
                  pl.BlockSpec((tn, d), lambda i, j: (j, 0)),
                  resid, vec, vec],
        out_specs=rows,
        out_shape=jax.ShapeDtypeStruct((s, d), F32),
        scratch_shapes=[pltpu.VMEM((tm + SUBLANES, tn), F32),
                        pltpu.VMEM((tm + SUBLANES, tn), F32),
                        pltpu.VMEM((tm, tn), BF16),
                        pltpu.VMEM((nj, 2, SUBLANES, tn), F32),
                        pltpu.VMEM((tm, d), F32),
                        pltpu.SemaphoreType.DMA(())],
        compiler_params=_cparams("arbitrary", "arbitrary"),
        name="ffn",
    )(u2, w_up, w_up, conv_w, conv_w, conv_b, conv_b, w_down, x1, npost, g2)


def _block(x, c, ada_w, ada_b, mix_norm_pre, mix_norm_post, w_in, ret_w_o, moba_w_o, w_out,
           ffn_norm_pre, ffn_norm_post, ffn_w_up, ffn_conv_w, ffn_conv_b, ffn_w_down, cos2, sin2):
    d = x.shape[1]
    mod = _ada(c, ada_w, ada_b[None, :])
    sh1, sc1, g1, sh2, sc2, g2 = [mod[:, t * d:(t + 1) * d] for t in range(6)]
    proj = _inproj(x, mix_norm_pre[None, :], sc1, sh1, cos2, sin2, w_in.astype(BF16))
    o_ret = _retention(proj)
    o_moba = _moba(proj)
    merged = _merge(o_ret, o_moba, proj, ret_w_o.astype(BF16), moba_w_o.astype(BF16))
    x1, u2 = _mixout(merged, w_out.astype(BF16), x, mix_norm_post[None, :], g1,
                     ffn_norm_pre[None, :], sc2, sh2)
    return _ffn(u2, ffn_w_up.astype(BF16), ffn_conv_w, ffn_conv_b[None, :], ffn_w_down.astype(BF16),
                x1, ffn_norm_post[None, :], g2)


def kernel(x, c, ada_w, ada_b, mix_norm_pre, mix_norm_post, w_in, ret_w_o, moba_w_o, w_out,
           ffn_norm_pre, ffn_norm_post, ffn_w_up, ffn_conv_w, ffn_conv_b, ffn_w_down):
    batch, s, _ = x.shape
    depth = ada_w.shape[0]
    cos2, sin2 = _rope_tables(s)
    outs = []
    for bi in range(batch):
        xb = x[bi]
        for l in range(depth):
            xb = _block(xb, c[bi:bi + 1], ada_w[l], ada_b[l], mix_norm_pre[l], mix_norm_post[l], w_in[l],
                        ret_w_o[l], moba_w_o[l], w_out[l], ffn_norm_pre[l], ffn_norm_post[l], ffn_w_up[l],
                        ffn_conv_w[l], ffn_conv_b[l], ffn_w_down[l], cos2, sin2)
        outs.append(xb)
    return jnp.stack(outs)
```

```python
import math

import jax
import jax.numpy as jnp
from jax import lax
from jax.experimental import pallas as pl
from jax.experimental.pallas import tpu as pltpu

D_MODEL = 2048
RET_HEADS = 8
RET_DK = 128
RET_DV = 256
ROPE_BASE = 10000.0
MOBA_HEADS = 16
MOBA_HD = 128
MOBA_BLOCK = 256
MOBA_TOPK = 3
D_FF = 5632
CONV_WIDTH = 3
NORM_EPS = 1e-6
GN_EPS = 1e-5

RET_QK_W = RET_HEADS * RET_DK
RET_V_W = RET_HEADS * RET_DV
MOBA_W = MOBA_HEADS * MOBA_HD
OFF_RQ = 0
OFF_RK = OFF_RQ + RET_QK_W
OFF_RV = OFF_RK + RET_QK_W
OFF_RG = OFF_RV + RET_V_W
OFF_MQ = OFF_RG + RET_V_W
OFF_MK = OFF_MQ + MOBA_W
OFF_MV = OFF_MK + MOBA_W
OFF_GA = OFF_MV + MOBA_W
OFF_GB = OFF_GA + D_MODEL
IN_WIDTH = OFF_GB + D_MODEL

LANES = 128
SUBLANES = 8
NEG = -1e30
MOBA_QSCALE = math.log2(math.e) / math.sqrt(MOBA_HD)
ROPE_SPAN = 128
RET_GROUP = 4
MOBA_GROUP = 2
MOBA_SLOTS = 8
MOBA_PAIR = 2
FFN_CHUNK = 256
FFN_ROWS = 64
FFN_DOWN_COLS = 512
FFN_RESID_CHUNKS = 8
MIX_ROWS = 256
NORM_ROWS = 32
NORM_UNROLL = 4
VMEM_LIMIT = 58 * 1024 * 1024

F32 = jnp.float32
BF16 = jnp.bfloat16


def _cparams(*sem):
    return pltpu.CompilerParams(dimension_semantics=sem, vmem_limit_bytes=VMEM_LIMIT)


def _dot(a, b):
    return jnp.dot(a, b, preferred_element_type=F32)


def _dot_nt(a, b):
    return lax.dot_general(a, b, (((1,), (1,)), ((), ())), preferred_element_type=F32)


def _split_bf16(x):
    hi = x.astype(BF16)
    lo = (x - hi.astype(F32)).astype(BF16)
    return hi, lo


def _rms(x):
    return x * lax.rsqrt(jnp.mean(x * x, axis=-1, keepdims=True) + NORM_EPS)


def _silu(x):
    return x * jax.nn.sigmoid(x)


def _for_row_chunks(n_rows, load, finish):
    step = NORM_ROWS * NORM_UNROLL

    def body(r, carry):
        pieces = [pl.ds(pl.multiple_of(r * step + t * NORM_ROWS, NORM_ROWS), NORM_ROWS)
                  for t in range(NORM_UNROLL)]
        loaded = [load(rows) for rows in pieces]
        for rows, vals in zip(pieces, loaded):
            finish(rows, vals)
        return carry

    lax.fori_loop(0, n_rows // step, body, 0)


def _ada_kernel(c_ref, w_ref, b_ref, o_ref):
    c = jnp.broadcast_to(_silu(c_ref[...]), (SUBLANES, c_ref.shape[1]))
    c_hi, c_lo = _split_bf16(c)
    w_hi, w_lo = _split_bf16(w_ref[...])
    r = _dot(c_hi, w_hi) + _dot(c_lo, w_hi) + _dot(c_hi, w_lo)
    o_ref[...] = r[0:1] + b_ref[...]


def _ada(c, w, b, tn=1024):
    d, n = w.shape
    return pl.pallas_call(
        _ada_kernel,
        grid=(n // tn,),
        in_specs=[pl.BlockSpec((1, d), lambda j: (0, 0)),
                  pl.BlockSpec((d, tn), lambda j: (0, j)),
                  pl.BlockSpec((1, tn), lambda j: (0, j))],
        out_specs=pl.BlockSpec((1, tn), lambda j: (0, j)),
        out_shape=jax.ShapeDtypeStruct((1, n), F32),
        compiler_params=_cparams("arbitrary"),
        name="ada",
    )(c, w, b)


def _rope_kernel(inv_ref, cos_ref, sin_ref):
    ts = cos_ref.shape[0]
    spans = ts // ROPE_SPAN
    inv = inv_ref[...]
    lane = lax.broadcasted_iota(jnp.int32, (ROPE_SPAN, LANES), 1)
    off = lax.broadcasted_iota(jnp.int32, (ROPE_SPAN, LANES), 0).astype(F32) * inv
    cos_o, sin_o = jnp.cos(off), jnp.sin(off)
    first = pl.program_id(0) * spans
    base = ((lax.broadcasted_iota(jnp.int32, (spans, LANES), 0) + first) * ROPE_SPAN).astype(F32) * inv
    cos_b, sin_b = jnp.cos(base), jnp.sin(base)
    for b in range(spans):
        cb, sb = cos_b[b:b + 1, :], sin_b[b:b + 1, :]
        sin = sb * cos_o + cb * sin_o
        cos_ref[b * ROPE_SPAN:(b + 1) * ROPE_SPAN, :] = cb * cos_o - sb * sin_o
        sin_ref[b * ROPE_SPAN:(b + 1) * ROPE_SPAN, :] = jnp.where(lane < RET_DK // 2, -sin, sin)


def _rope_tables(s, ts=2048):
    ts = min(ts, s)
    half = RET_DK // 2
    inv = ROPE_BASE ** (-jnp.arange(half, dtype=F32) / half)
    inv2 = jnp.concatenate([inv, inv])[None, :]
    return pl.pallas_call(
        _rope_kernel,
        grid=(s // ts,),
        in_specs=[pl.BlockSpec((1, LANES), lambda i: (0, 0))],
        out_specs=[pl.BlockSpec((ts, LANES), lambda i: (i, 0))] * 2,
        out_shape=[jax.ShapeDtypeStruct((s, LANES), F32)] * 2,
        compiler_params=_cparams("arbitrary"),
        name="rope",
    )(inv2)


def _inproj_kernel(x_hbm, nw_ref, sc_ref, sh_ref, cos_ref, sin_ref, w_ref, o_ref, u_ref, x_ref, x_sem):
    i = pl.program_id(0)
    j = pl.program_id(1)
    tm, tn = o_ref.shape
    nj = IN_WIDTH // tn
    chunk = tm // nj

    def x_copy(tile, c):
        src = pl.ds(pl.multiple_of(tile * tm + c * chunk, chunk), chunk)
        dst = pl.ds(pl.multiple_of(c * chunk, chunk), chunk)
        return pltpu.make_async_copy(x_hbm.at[src, :], x_ref.at[dst, :], x_sem)

    @pl.when(j == 0)
    def _():
        @pl.when(i == 0)
        def _():
            for c in range(nj):
                x_copy(0, c).start()

        for c in range(nj):
            x_copy(i, c).wait()

        def finish(rows, x):
            y = _rms(x) * nw_ref[...]
            u_ref[rows, :] = (y * (1.0 + sc_ref[...]) + sh_ref[...]).astype(BF16)

        _for_row_chunks(tm, lambda rows: x_ref[rows, :], finish)

    @pl.when(i + 1 < pl.num_programs(0))
    def _():
        x_copy(i + 1, j).start()

    r = _dot(u_ref[...], w_ref[...])
    is_mq = (j >= OFF_MQ // tn) & (j < OFF_MK // tn)
    o_ref[...] = (r * jnp.where(is_mq, MOBA_QSCALE, 1.0).astype(F32)).astype(BF16)

    @pl.when(j * tn < 2 * RET_QK_W)
    def _():
        cos = cos_ref[...]
        sin = sin_ref[...]
        for h in range(tn // RET_DK):
            scale = jnp.where(j * tn + h * RET_DK >= RET_QK_W, RET_DK ** -0.5, 1.0).astype(F32)
            xh = r[:, h * RET_DK:(h + 1) * RET_DK]
            rot = xh * cos + pltpu.roll(xh, RET_DK // 2, axis=1) * sin
            o_ref[:, h * RET_DK:(h + 1) * RET_DK] = (rot * scale).astype(BF16)


def _inproj(x, nw, sc, sh, cos2, sin2, w, tm=1024, tn=2048):
    s, d = x.shape
    n = w.shape[1]
    tm = min(tm, s)
    nj = n // tn
    assert n == IN_WIDTH and tm % nj == 0 and (2 * RET_QK_W) % tn == 0 and OFF_MQ % tn == 0 and OFF_MK % tn == 0
    vec = pl.BlockSpec((1, d), lambda i, j: (0, 0))
    tab = pl.BlockSpec((tm, LANES), lambda i, j: (i, 0))
    return pl.pallas_call(
        _inproj_kernel,
        grid=(s // tm, nj),
        in_specs=[pl.BlockSpec(memory_space=pl.ANY), vec, vec, vec, tab, tab,
                  pl.BlockSpec((d, tn), lambda i, j: (0, j))],
        out_specs=pl.BlockSpec((tm, tn), lambda i, j: (i, j)),
        out_shape=jax.ShapeDtypeStruct((s, n), BF16),
        scratch_shapes=[pltpu.VMEM((tm, d), BF16),
                        pltpu.VMEM((tm, d), F32),
                        pltpu.SemaphoreType.DMA(())],
        compiler_params=_cparams("arbitrary", "arbitrary"),
        name="inproj",
    )(x, nw, sc, sh, cos2, sin2, w)


def _ret_kernel(lg_ref, q_ref, k_ref, v_ref, g_ref, o_ref, r_ref, dec_ref, xi_ref, zeta_ref):
    n = pl.program_id(1)
    c = q_ref.shape[0]
    dk, dv = RET_DK, RET_DV

    @pl.when(n == 0)
    def _():
        r_ref[...] = jnp.zeros_like(r_ref)
        d = lax.broadcasted_iota(jnp.int32, (c, c), 0) - lax.broadcasted_iota(jnp.int32, (c, c), 1)
        row = lax.broadcasted_iota(jnp.int32, (c, dk), 0)
        for h in range(RET_GROUP):
            lg = lg_ref[h, 0:1, 0:1]
            dec_ref[h] = jnp.where(d >= 0, jnp.exp(jnp.maximum(d, 0).astype(F32) * lg), 0.0)
            xi_ref[h] = jnp.exp((row + 1).astype(F32) * lg)
            zeta_ref[h] = jnp.exp((c - 1 - row).astype(F32) * lg)

    for h in range(RET_GROUP):
        q = q_ref[:, h * dk:(h + 1) * dk]
        k = k_ref[:, h * dk:(h + 1) * dk]
        v = v_ref[:, h * dv:(h + 1) * dv]
        s = _dot_nt(q, k) * dec_ref[h]
        inner = _dot(s.astype(BF16), v)
        r = r_ref[h]
        cross = _dot((q.astype(F32) * xi_ref[h]).astype(BF16), r.astype(BF16))
        o = inner + cross
        kz = (k.astype(F32) * zeta_ref[h]).T.astype(BF16)
        r_ref[h] = r * jnp.exp(lg_ref[h, 0:1, 0:1] * c) + _dot(kz, v)
        oc = o - jnp.mean(o, axis=-1, keepdims=True)
        on = oc * lax.rsqrt(jnp.mean(oc * oc, axis=-1, keepdims=True) + GN_EPS)
        o_ref[:, h * dv:(h + 1) * dv] = (_silu(g_ref[:, h * dv:(h + 1) * dv].astype(F32)) * on).astype(BF16)


def _retention(proj, chunk=512):
    s = proj.shape[0]
    chunk = min(chunk, s)
    log_g = jnp.log1p(-jnp.exp2(-5.0 - jnp.arange(RET_HEADS, dtype=F32)))
    lg = jnp.broadcast_to(log_g[:, None, None], (RET_HEADS, SUBLANES, LANES))
    qk_w, v_w = RET_GROUP * RET_DK, RET_GROUP * RET_DV
    assert RET_HEADS % RET_GROUP == 0
    return pl.pallas_call(
        _ret_kernel,
        grid=(RET_HEADS // RET_GROUP, s // chunk),
        in_specs=[pl.BlockSpec((RET_GROUP, SUBLANES, LANES), lambda h, n: (h, 0, 0)),
                  pl.BlockSpec((chunk, qk_w), lambda h, n: (n, OFF_RQ // qk_w + h)),
                  pl.BlockSpec((chunk, qk_w), lambda h, n: (n, OFF_RK // qk_w + h)),
                  pl.BlockSpec((chunk, v_w), lambda h, n: (n, OFF_RV // v_w + h)),
                  pl.BlockSpec((chunk, v_w), lambda h, n: (n, OFF_RG // v_w + h))],
        out_specs=pl.BlockSpec((chunk, v_w), lambda h, n: (n, h)),
        out_shape=jax.ShapeDtypeStruct((s, RET_V_W), BF16),
        scratch_shapes=[pltpu.VMEM((RET_GROUP, RET_DK, RET_DV), F32),
                        pltpu.VMEM((RET_GROUP, chunk, chunk), F32),
                        pltpu.VMEM((RET_GROUP, chunk, RET_DK), F32),
                        pltpu.VMEM((RET_GROUP, chunk, RET_DK), F32)],
        compiler_params=_cparams("arbitrary", "arbitrary"),
        name="ret",
    )(lg, proj, proj, proj, proj)


def _moba_kernel(q_ref, k_ref, v_hbm, o_ref, km_ref, oh_ref, vt_ref, qt_ref, acc_ref, m_ref, v_ref, v_sem,
                 *slots):
    hp = pl.program_id(0)
    bq = pl.program_id(1)
    tq = q_ref.shape[0]
    hd = MOBA_HD
    blk = MOBA_BLOCK
    nb = k_ref.shape[0] // blk
    gw = MOBA_GROUP * blk
    vrows = vt_ref.shape[2]

    @pl.when((hp == 0) & (bq == 0))
    def _():
        lane_i = lax.broadcasted_iota(jnp.int32, (blk, LANES), 1)

        def body(n, carry):
            oh_ref[pl.ds(pl.multiple_of(n * blk, blk), blk), :] = jnp.where(lane_i == n, 1.0, 0.0).astype(BF16)
            return carry

        lax.fori_loop(0, nb, body, 0)

    def value_copy(pair):
        cols = pl.ds(pl.multiple_of(OFF_MV + pair * (MOBA_PAIR * hd), MOBA_PAIR * hd), MOBA_PAIR * hd)
        return pltpu.make_async_copy(v_hbm.at[:, cols], v_ref, v_sem)

    @pl.when((hp == 0) & (bq == 0))
    def _():
        value_copy(0).start()

    @pl.when(bq == 0)
    def _():
        value_copy(hp).wait()
        km_ref[...] = jnp.zeros_like(km_ref)

        def body(g, carry):
            for t in range(MOBA_GROUP):
                n = g * MOBA_GROUP + t
                rows = pl.ds(pl.multiple_of(n * blk, blk), blk)
                km_ref[pl.ds(n, 1), :] = jnp.mean(k_ref[rows, :].astype(F32), axis=0, keepdims=True)
                for h in range(MOBA_PAIR):
                    vt_ref[h, g, 0:hd, t * blk:(t + 1) * blk] = (
                        v_ref[rows, h * hd:(h + 1) * hd].astype(F32).T.astype(BF16))
            for h in range(MOBA_PAIR):
                vt_ref[h, g, hd:vrows, :] = jnp.ones((vrows - hd, gw), BF16)
            return carry

        lax.fori_loop(0, nb // MOBA_GROUP, body, 0)

        @pl.when(hp + 1 < pl.num_programs(0))
        def _():
            value_copy(hp + 1).start()

    gr = km_ref.shape[0]
    row_i = lax.broadcasted_iota(jnp.int32, (gr, tq), 0)
    row_f = row_i.astype(F32)
    own = bq * (tq // blk) + lax.broadcasted_iota(jnp.int32, (gr, tq), 1) // blk
    for h in range(MOBA_PAIR):
        qt = q_ref[:, h * hd:(h + 1) * hd].astype(F32).T.astype(BF16)
        km_hi, km_lo = _split_bf16(km_ref[:, h * hd:(h + 1) * hd])
        g = jnp.where(row_i < own, _dot(km_hi, qt) + _dot(km_lo, qt), NEG)
        bias = jnp.full((gr, tq), NEG, F32)
        for _ in range(MOBA_TOPK):
            mx = jnp.max(g, axis=0, keepdims=True)
            idx = jnp.min(jnp.where(g == mx, row_f, float(gr)), axis=0, keepdims=True)
            hit = row_f == idx
            bias = jnp.where(hit, jnp.where(mx > 0.5 * NEG, 0.0, NEG), bias)
            g = jnp.where(hit, 2.0 * NEG, g)
        bias = jnp.where(row_i == own, 0.0, bias)
        qt_ref[h] = jnp.concatenate(
            [qt, bias.astype(BF16), jnp.full((LANES - gr, tq), NEG, BF16)], axis=0)
        m_ref[h] = jnp.full((1, tq), NEG, F32)
        acc_ref[h] = jnp.zeros((vrows, tq), F32)

    def scores(g, slot_ref):
        rows = pl.ds(pl.multiple_of(g * gw, gw), gw)
        oh = oh_ref[rows, :]
        for h in range(MOBA_PAIR):
            k_ext = jnp.concatenate([k_ref[rows, h * hd:(h + 1) * hd], oh], axis=1)
            slot_ref[h] = _dot(k_ext, qt_ref[h])

    def accumulate(g, slot_ref, causal):
        if causal:
            own_rows = pl.ds(pl.multiple_of((bq * tq) % gw, tq), tq)
            keep = lax.broadcasted_iota(jnp.int32, (tq, tq), 0) <= lax.broadcasted_iota(jnp.int32, (tq, tq), 1)
            for h in range(MOBA_PAIR):
                slot_ref[h, own_rows, :] = jnp.where(keep, slot_ref[h, own_rows, :], NEG)
        for h in range(MOBA_PAIR):
            s = slot_ref[h]
            m_old = m_ref[h]
            m_new = jnp.maximum(m_old, jnp.max(s, axis=0, keepdims=True))
            p = jnp.exp2(s - m_new).astype(BF16)
            acc_ref[h] = jnp.exp2(m_old - m_new) * acc_ref[h] + _dot(vt_ref[h, g], p)
            m_ref[h] = m_new

    nslot = len(slots)
    full_groups = (bq * tq) // gw
    scores(0, slots[0])

    def body(i, carry):
        g = nslot * i
        for u in range(nslot):
            scores(g + u + 1, slots[(u + 1) % nslot])
            accumulate(g + u, slots[u], False)
        return carry

    lax.fori_loop(0, full_groups // nslot, body, 0)
    base = (full_groups // nslot) * nslot
    for r in range(nslot):
        @pl.when(full_groups - base == r)
        def _(r=r):
            for u in range(r):
                scores(base + u + 1, slots[u + 1])
                accumulate(base + u, slots[u], False)
            accumulate(base + r, slots[r], True)

    for h in range(MOBA_PAIR):
        acc = acc_ref[h]
        o_ref[:, h * hd:(h + 1) * hd] = (acc[0:hd, :] / acc[hd:hd + 1, :]).T.astype(BF16)


def _moba(proj, tq=2 * MOBA_BLOCK):
    s = proj.shape[0]
    nb = s // MOBA_BLOCK
    w = MOBA_PAIR * MOBA_HD
    gw = MOBA_GROUP * MOBA_BLOCK
    bf16_rows = 2 * SUBLANES
    vrows = MOBA_HD + bf16_rows
    gate_rows = -(-nb // bf16_rows) * bf16_rows
    assert s % gw == 0 and gate_rows <= LANES and MOBA_HEADS % MOBA_PAIR == 0
    assert tq % MOBA_BLOCK == 0 and gw % tq == 0
    return pl.pallas_call(
        _moba_kernel,
        grid=(MOBA_HEADS // MOBA_PAIR, s // tq),
        in_specs=[pl.BlockSpec((tq, w), lambda h, b: (b, OFF_MQ // w + h)),
                  pl.BlockSpec((s, w), lambda h, b: (0, OFF_MK // w + h)),
                  pl.BlockSpec(memory_space=pl.ANY)],
        out_specs=pl.BlockSpec((tq, w), lambda h, b: (b, h)),
        out_shape=jax.ShapeDtypeStruct((s, MOBA_W), BF16),
        scratch_shapes=[pltpu.VMEM((gate_rows, w), F32),
                        pltpu.VMEM((s, LANES), BF16),
                        pltpu.VMEM((MOBA_PAIR, s // gw, vrows, gw), BF16),
                        pltpu.VMEM((MOBA_PAIR, 2 * MOBA_HD, tq), BF16),
                        pltpu.VMEM((MOBA_PAIR, vrows, tq), F32),
                        pltpu.VMEM((MOBA_PAIR, 1, tq), F32),
                        pltpu.VMEM((s, w), BF16),
                        pltpu.SemaphoreType.DMA(())]
                       + [pltpu.VMEM((MOBA_PAIR, gw, tq), F32)] * MOBA_SLOTS,
        compiler_params=_cparams("arbitrary", "arbitrary"),
        name="moba",
    )(proj, proj, proj)


def _merge_kernel(or_ref, om_ref, ga_ref, gb_ref, wr_ref, wm_ref, o_ref):
    yr = _dot(or_ref[...], wr_ref[...])
    ym = _dot(om_ref[...], wm_ref[...])
    ga = jax.nn.sigmoid(ga_ref[...].astype(F32))
    gb = jax.nn.sigmoid(gb_ref[...].astype(F32))
    o_ref[...] = (ga * yr + gb * ym).astype(BF16)


def _merge(o_ret, o_moba, proj, w_r, w_m, tm=1024, tn=512):
    s = o_ret.shape[0]
    tm = min(tm, s)
    d = D_MODEL
    return pl.pallas_call(
        _merge_kernel,
        grid=(s // tm, d // tn),
        in_specs=[pl.BlockSpec((tm, RET_V_W), lambda i, j: (i, 0)),
                  pl.BlockSpec((tm, MOBA_W), lambda i, j: (i, 0)),
                  pl.BlockSpec((tm, tn), lambda i, j: (i, OFF_GA // tn + j)),
                  pl.BlockSpec((tm, tn), lambda i, j: (i, OFF_GB // tn + j)),
                  pl.BlockSpec((RET_V_W, tn), lambda i, j: (0, j)),
                  pl.BlockSpec((MOBA_W, tn), lambda i, j: (0, j))],
        out_specs=pl.BlockSpec((tm, tn), lambda i, j: (i, j)),
        out_shape=jax.ShapeDtypeStruct((s, d), BF16),
        compiler_params=_cparams("arbitrary", "arbitrary"),
        name="merge",
    )(o_ret, o_moba, proj, proj, w_r, w_m)


def _mixout_kernel(m_ref, wo_ref, x_ref, npost_ref, g1_ref, npre_ref, sc_ref, sh_ref, x1_ref, u2_ref):
    for r0 in range(0, m_ref.shape[0], MIX_ROWS):
        rows = slice(r0, r0 + MIX_ROWS)
        y = _dot(m_ref[rows, :], wo_ref[...])
        x1 = x_ref[rows, :] + (1.0 + g1_ref[...]) * (_rms(y) * npost_ref[...])
        x1_ref[rows, :] = x1
        u2_ref[rows, :] = ((_rms(x1) * npre_ref[...]) * (1.0 + sc_ref[...]) + sh_ref[...]).astype(BF16)


def _mixout(merged, w_o, x, npost, g1, npre, sc2, sh2, tm=512):
    s, d = x.shape
    tm = min(tm, s)
    vec = pl.BlockSpec((1, d), lambda i: (0, 0))
    rows = pl.BlockSpec((tm, d), lambda i: (i, 0))
    return pl.pallas_call(
        _mixout_kernel,
        grid=(s // tm,),
        in_specs=[rows, pl.BlockSpec((d, d), lambda i: (0, 0), pipeline_mode=pl.Buffered(1)), rows,
                  vec, vec, vec, vec, vec],
        out_specs=[rows, rows],
        out_shape=[jax.ShapeDtypeStruct((s, d), F32), jax.ShapeDtypeStruct((s, d), BF16)],
        compiler_params=_cparams("arbitrary"),
        name="mixout",
    )(merged, w_o, x, npost, g1, npre, sc2, sh2)


def _ffn_kernel(u_ref, wa_ref, wb_ref, cwa_ref, cwb_ref, cba_ref, cbb_ref, wd_ref, x1_hbm, npost_ref, g2_ref,
                o_ref, ha_ref, hb_ref, act_ref, tail_ref, x1_ref, x1_sem):
    i = pl.program_id(0)
    j = pl.program_id(1)
    tm = u_ref.shape[0]
    tn = wa_ref.shape[1]
    halo = SUBLANES

    chunk = tm // FFN_RESID_CHUNKS

    def residual_copy(c):
        src = pl.ds(pl.multiple_of(i * tm + c * chunk, chunk), chunk)
        dst = pl.ds(pl.multiple_of(c * chunk, chunk), chunk)
        return pltpu.make_async_copy(x1_hbm.at[src, :], x1_ref.at[dst, :], x1_sem)

    @pl.when((i == 0) & (j == 0))
    def _():
        tail_ref[...] = jnp.zeros_like(tail_ref)

    @pl.when(j == 0)
    def _():
        o_ref[...] = jnp.zeros_like(o_ref)

    @pl.when(j < FFN_RESID_CHUNKS)
    def _():
        residual_copy(j).start()

    ha_ref[0:halo, :] = tail_ref[j, 0]
    hb_ref[0:halo, :] = tail_ref[j, 1]
    for c in range(tn // FFN_CHUNK):
        cs = slice(c * FFN_CHUNK, (c + 1) * FFN_CHUNK)
        ha_ref[halo:halo + tm, cs] = _dot(u_ref[...], wa_ref[:, cs])
        hb_ref[halo:halo + tm, cs] = _dot(u_ref[...], wb_ref[:, cs])
    tail_ref[j, 0] = ha_ref[tm:tm + halo, :]
    tail_ref[j, 1] = hb_ref[tm:tm + halo, :]

    def conv(hbuf, cw, cb, r0, cs):
        out = cb
        for t in range(CONV_WIDTH):
            out = out + hbuf[halo + r0 - t:halo + r0 - t + FFN_ROWS, cs] * cw[CONV_WIDTH - 1 - t:CONV_WIDTH - t, :]
        return out

    for c in range(tn // FFN_CHUNK):
        cs = slice(c * FFN_CHUNK, (c + 1) * FFN_CHUNK)
        cwa, cwb, cba, cbb = cwa_ref[:, cs], cwb_ref[:, cs], cba_ref[:, cs], cbb_ref[:, cs]
        for r0 in range(0, tm, FFN_ROWS):
            a = conv(ha_ref, cwa, cba, r0, cs)
            bgate = conv(hb_ref, cwb, cbb, r0, cs)
            act_ref[r0:r0 + FFN_ROWS, cs] = (_silu(a) * bgate).astype(BF16)
        for n0 in range(0, o_ref.shape[1], FFN_DOWN_COLS):
            ns = slice(n0, n0 + FFN_DOWN_COLS)
            o_ref[:, ns] += _dot(act_ref[:, cs], wd_ref[cs, ns])

    @pl.when(j == pl.num_programs(1) - 1)
    def _():
        for c in range(FFN_RESID_CHUNKS):
            residual_copy(c).wait()

        def finish(rows, y):
            o_ref[rows, :] = x1_ref[rows, :] + (1.0 + g2_ref[...]) * (_rms(y) * npost_ref[...])

        _for_row_chunks(tm, lambda rows: o_ref[rows, :], finish)


def _ffn(u2, w_up, conv_w, conv_b, w_down, x1, npost, g2, tm=1024, tn=512):
    s, d = x1.shape
    tm = min(tm, s)
    nj = D_FF // tn
    assert D_FF % tn == 0 and nj >= FFN_RESID_CHUNKS and tm % FFN_RESID_CHUNKS == 0
    vec = pl.BlockSpec((1, d), lambda i, j: (0, 0))
    rows = pl.BlockSpec((tm, d), lambda i, j: (i, 0))
    resid = pl.BlockSpec(memory_space=pl.ANY)
    return pl.pallas_call(
        _ffn_kernel,
        grid=(s // tm, nj),
        in_specs=[rows,
                  pl.BlockSpec((d, tn), lambda i, j: (0, j)),
                  pl.BlockSpec((d, tn), lambda i, j: (0, j + nj)),
                  pl.BlockSpec((CONV_WIDTH, tn), lambda i, j: (0, j)),
                  pl.BlockSpec((CONV_WIDTH, tn), lambda i, j: (0, j + nj)),
                  pl.BlockSpec((1, tn), lambda i, j: (0, j)),
                  pl.BlockSpec((1, tn), lambda i, j: (0, j + nj)),
                  pl.BlockSpec((tn, d), lambda i, j: (j, 0)),
                  resid, vec, vec],
        out_specs=rows,
        out_shape=jax.ShapeDtypeStruct((s, d), F32),
        scratch_shapes=[pltpu.VMEM((tm + SUBLANES, tn), F32),
                        pltpu.VMEM((tm + SUBLANES, tn), F32),
                        pltpu.VMEM((tm, tn), BF16),
                        pltpu.VMEM((nj, 2, SUBLANES, tn), F32),
                        pltpu.VMEM((tm, d), F32),
                        pltpu.SemaphoreType.DMA(())],
        compiler_params=_cparams("arbitrary", "arbitrary"),
        name="ffn",
    )(u2, w_up, w_up, conv_w, conv_w, conv_b, conv_b, w_down, x1, npost, g2)


def _block(x, c, ada_w, ada_b, mix_norm_pre, mix_norm_post, w_in, ret_w_o, moba_w_o, w_out,
           ffn_norm_pre, ffn_norm_post, ffn_w_up, ffn_conv_w, ffn_conv_b, ffn_w_down, cos2, sin2):
    d = x.shape[1]
    mod = _ada(c, ada_w, ada_b[None, :])
    sh1, sc1, g1, sh2, sc2, g2 = [mod[:, t * d:(t + 1) * d] for t in range(6)]
    proj = _inproj(x, mix_norm_pre[None, :], sc1, sh1, cos2, sin2, w_in.astype(BF16))
    o_ret = _retention(proj)
    o_moba = _moba(proj)
    merged = _merge(o_ret, o_moba, proj, ret_w_o.astype(BF16), moba_w_o.astype(BF16))
    x1, u2 = _mixout(merged, w_out.astype(BF16), x, mix_norm_post[None, :], g1,
                     ffn_norm_pre[None, :], sc2, sh2)
    return _ffn(u2, ffn_w_up.astype(BF16), ffn_conv_w, ffn_conv_b[None, :], ffn_w_down.astype(BF16),
                x1, ffn_norm_post[None, :], g2)


def kernel(x, c, ada_w, ada_b, mix_norm_pre, mix_norm_post, w_in, ret_w_o, moba_w_o, w_out,
           ffn_norm_pre, ffn_norm_post, ffn_w_up, ffn_conv_w, ffn_conv_b, ffn_w_down):
    batch, s, _ = x.shape
    depth = ada_w.shape[0]
    cos2, sin2 = _rope_tables(s)
    outs = []
    for bi in range(batch):
        xb = x[bi]
        for l in range(depth):
            xb = _block(xb, c[bi:bi + 1], ada_w[l], ada_b[l], mix_norm_pre[l], mix_norm_post[l], w_in[l],
                        ret_w_o[l], moba_w_o[l], w_out[l], ffn_norm_pre[l], ffn_norm_post[l], ffn_w_up[l],
                        ffn_conv_w[l], ffn_conv_b[l], ffn_w_down[l], cos2, sin2)
        outs.append(xb)
    return jnp.stack(outs)
```

```python
import math

import jax
import jax.numpy as jnp
from jax import lax
from jax.experimental import pallas as pl
from jax.experimental.pallas import tpu as pltpu

D_MODEL = 2048
RET_HEADS = 8
RET_DK = 128
RET_DV = 256
ROPE_BASE = 10000.0
MOBA_HEADS = 16
MOBA_HD = 128
MOBA_BLOCK = 256
MOBA_TOPK = 3
D_FF = 5632
CONV_WIDTH = 3
NORM_EPS = 1e-6
GN_EPS = 1e-5

RET_QK_W = RET_HEADS * RET_DK
RET_V_W = RET_HEADS * RET_DV
MOBA_W = MOBA_HEADS * MOBA_HD
OFF_RQ = 0
OFF_RK = OFF_RQ + RET_QK_W
OFF_RV = OFF_RK + RET_QK_W
OFF_RG = OFF_RV + RET_V_W
OFF_MQ = OFF_RG + RET_V_W
OFF_MK = OFF_MQ + MOBA_W
OFF_MV = OFF_MK + MOBA_W
OFF_GA = OFF_MV + MOBA_W
OFF_GB = OFF_GA + D_MODEL
IN_WIDTH = OFF_GB + D_MODEL

LANES = 128
SUBLANES = 8
NEG = -1e30
MOBA_QSCALE = math.log2(math.e) / math.sqrt(MOBA_HD)
ROPE_SPAN = 128
RET_GROUP = 4
MOBA_GROUP = 2
MOBA_SLOTS = 6
MOBA_PAIR = 2
FFN_CHUNK = 256
FFN_ROWS = 64
FFN_DOWN_COLS = 512
MIX_ROWS = 256
NORM_ROWS = 32
NORM_UNROLL = 4
VMEM_LIMIT = 58 * 1024 * 1024

F32 = jnp.float32
BF16 = jnp.bfloat16


def _cparams(*sem):
    return pltpu.CompilerParams(dimension_semantics=sem, vmem_limit_bytes=VMEM_LIMIT)


def _dot(a, b):
    return jnp.dot(a, b, preferred_element_type=F32)


def _dot_nt(a, b):
    return lax.dot_general(a, b, (((1,), (1,)), ((), ())), preferred_element_type=F32)


def _split_bf16(x):
    hi = x.astype(BF16)
    lo = (x - hi.astype(F32)).astype(BF16)
    return hi, lo


def _rms(x):
    return x * lax.rsqrt(jnp.mean(x * x, axis=-1, keepdims=True) + NORM_EPS)


def _silu(x):
    return x * jax.nn.sigmoid(x)


def _for_row_chunks(n_rows, load, finish):
    step = NORM_ROWS * NORM_UNROLL

    def body(r, carry):
        pieces = [pl.ds(pl.multiple_of(r * step + t * NORM_ROWS, NORM_ROWS), NORM_ROWS)
                  for t in range(NORM_UNROLL)]
        loaded = [load(rows) for rows in pieces]
        for rows, vals in zip(pieces, loaded):
            finish(rows, vals)
        return carry

    lax.fori_loop(0, n_rows // step, body, 0)


def _ada_kernel(c_ref, w_ref, b_ref, o_ref):
    c = jnp.broadcast_to(_silu(c_ref[...]), (SUBLANES, c_ref.shape[1]))
    c_hi, c_lo = _split_bf16(c)
    w_hi, w_lo = _split_bf16(w_ref[...])
    r = _dot(c_hi, w_hi) + _dot(c_lo, w_hi) + _dot(c_hi, w_lo)
    o_ref[...] = r[0:1] + b_ref[...]


def _ada(c, w, b, tn=1024):
    d, n = w.shape
    return pl.pallas_call(
        _ada_kernel,
        grid=(n // tn,),
        in_specs=[pl.BlockSpec((1, d), lambda j: (0, 0)),
                  pl.BlockSpec((d, tn), lambda j: (0, j)),
                  pl.BlockSpec((1, tn), lambda j: (0, j))],
        out_specs=pl.BlockSpec((1, tn), lambda j: (0, j)),
        out_shape=jax.ShapeDtypeStruct((1, n), F32),
        compiler_params=_cparams("arbitrary"),
        name="ada",
    )(c, w, b)


def _rope_kernel(inv_ref, cos_ref, sin_ref):
    ts = cos_ref.shape[0]
    spans = ts // ROPE_SPAN
    inv = inv_ref[...]
    lane = lax.broadcasted_iota(jnp.int32, (ROPE_SPAN, LANES), 1)
    off = lax.broadcasted_iota(jnp.int32, (ROPE_SPAN, LANES), 0).astype(F32) * inv
    cos_o, sin_o = jnp.cos(off), jnp.sin(off)
    first = pl.program_id(0) * spans
    base = ((lax.broadcasted_iota(jnp.int32, (spans, LANES), 0) + first) * ROPE_SPAN).astype(F32) * inv
    cos_b, sin_b = jnp.cos(base), jnp.sin(base)
    for b in range(spans):
        cb, sb = cos_b[b:b + 1, :], sin_b[b:b + 1, :]
        sin = sb * cos_o + cb * sin_o
        cos_ref[b * ROPE_SPAN:(b + 1) * ROPE_SPAN, :] = cb * cos_o - sb * sin_o
        sin_ref[b * ROPE_SPAN:(b + 1) * ROPE_SPAN, :] = jnp.where(lane < RET_DK // 2, -sin, sin)


def _rope_tables(s, ts=2048):
    ts = min(ts, s)
    half = RET_DK // 2
    inv = ROPE_BASE ** (-jnp.arange(half, dtype=F32) / half)
    inv2 = jnp.concatenate([inv, inv])[None, :]
    return pl.pallas_call(
        _rope_kernel,
        grid=(s // ts,),
        in_specs=[pl.BlockSpec((1, LANES), lambda i: (0, 0))],
        out_specs=[pl.BlockSpec((ts, LANES), lambda i: (i, 0))] * 2,
        out_shape=[jax.ShapeDtypeStruct((s, LANES), F32)] * 2,
        compiler_params=_cparams("arbitrary"),
        name="rope",
    )(inv2)


def _inproj_kernel(x_hbm, nw_ref, sc_ref, sh_ref, cos_ref, sin_ref, w_ref, o_ref, u_ref, x_ref, x_sem):
    i = pl.program_id(0)
    j = pl.program_id(1)
    tm, tn = o_ref.shape
    nj = IN_WIDTH // tn
    chunk = tm // nj

    def x_copy(tile, c):
        src = pl.ds(pl.multiple_of(tile * tm + c * chunk, chunk), chunk)
        dst = pl.ds(pl.multiple_of(c * chunk, chunk), chunk)
        return pltpu.make_async_copy(x_hbm.at[src, :], x_ref.at[dst, :], x_sem)

    @pl.when(j == 0)
    def _():
        @pl.when(i == 0)
        def _():
            for c in range(nj):
                x_copy(0, c).start()

        for c in range(nj):
            x_copy(i, c).wait()

        def finish(rows, x):
            y = _rms(x) * nw_ref[...]
            u_ref[rows, :] = (y * (1.0 + sc_ref[...]) + sh_ref[...]).astype(BF16)

        _for_row_chunks(tm, lambda rows: x_ref[rows, :], finish)

    @pl.when(i + 1 < pl.num_programs(0))
    def _():
        x_copy(i + 1, j).start()

    r = _dot(u_ref[...], w_ref[...])
    is_mq = (j >= OFF_MQ // tn) & (j < OFF_MK // tn)
    o_ref[...] = (r * jnp.where(is_mq, MOBA_QSCALE, 1.0).astype(F32)).astype(BF16)

    @pl.when(j * tn < 2 * RET_QK_W)
    def _():
        cos = cos_ref[...]
        sin = sin_ref[...]
        for h in range(tn // RET_DK):
            scale = jnp.where(j * tn + h * RET_DK >= RET_QK_W, RET_DK ** -0.5, 1.0).astype(F32)
            xh = r[:, h * RET_DK:(h + 1) * RET_DK]
            rot = xh * cos + pltpu.roll(xh, RET_DK // 2, axis=1) * sin
            o_ref[:, h * RET_DK:(h + 1) * RET_DK] = (rot * scale).astype(BF16)


def _inproj(x, nw, sc, sh, cos2, sin2, w, tm=1024, tn=2048):
    s, d = x.shape
    n = w.shape[1]
    tm = min(tm, s)
    nj = n // tn
    assert n == IN_WIDTH and tm % nj == 0 and (2 * RET_QK_W) % tn == 0 and OFF_MQ % tn == 0 and OFF_MK % tn == 0
    vec = pl.BlockSpec((1, d), lambda i, j: (0, 0))
    tab = pl.BlockSpec((tm, LANES), lambda i, j: (i, 0))
    return pl.pallas_call(
        _inproj_kernel,
        grid=(s // tm, nj),
        in_specs=[pl.BlockSpec(memory_space=pl.ANY), vec, vec, vec, tab, tab,
                  pl.BlockSpec((d, tn), lambda i, j: (0, j))],
        out_specs=pl.BlockSpec((tm, tn), lambda i, j: (i, j)),
        out_shape=jax.ShapeDtypeStruct((s, n), BF16),
        scratch_shapes=[pltpu.VMEM((tm, d), BF16),
                        pltpu.VMEM((tm, d), F32),
                        pltpu.SemaphoreType.DMA(())],
        compiler_params=_cparams("arbitrary", "arbitrary"),
        name="inproj",
    )(x, nw, sc, sh, cos2, sin2, w)


def _ret_kernel(lg_ref, q_ref, k_ref, v_ref, g_ref, o_ref, r_ref, dec_ref, xi_ref, zeta_ref):
    n = pl.program_id(1)
    c = q_ref.shape[0]
    dk, dv = RET_DK, RET_DV

    @pl.when(n == 0)
    def _():
        r_ref[...] = jnp.zeros_like(r_ref)
        d = lax.broadcasted_iota(jnp.int32, (c, c), 0) - lax.broadcasted_iota(jnp.int32, (c, c), 1)
        row = lax.broadcasted_iota(jnp.int32, (c, dk), 0)
        for h in range(RET_GROUP):
            lg = lg_ref[h, 0:1, 0:1]
            dec_ref[h] = jnp.where(d >= 0, jnp.exp(jnp.maximum(d, 0).astype(F32) * lg), 0.0)
            xi_ref[h] = jnp.exp((row + 1).astype(F32) * lg)
            zeta_ref[h] = jnp.exp((c - 1 - row).astype(F32) * lg)

    for h in range(RET_GROUP):
        q = q_ref[:, h * dk:(h + 1) * dk]
        k = k_ref[:, h * dk:(h + 1) * dk]
        v = v_ref[:, h * dv:(h + 1) * dv]
        s = _dot_nt(q, k) * dec_ref[h]
        inner = _dot(s.astype(BF16), v)
        r = r_ref[h]
        cross = _dot((q.astype(F32) * xi_ref[h]).astype(BF16), r.astype(BF16))
        o = inner + cross
        kz = (k.astype(F32) * zeta_ref[h]).T.astype(BF16)
        r_ref[h] = r * jnp.exp(lg_ref[h, 0:1, 0:1] * c) + _dot(kz, v)
        oc = o - jnp.mean(o, axis=-1, keepdims=True)
        on = oc * lax.rsqrt(jnp.mean(oc * oc, axis=-1, keepdims=True) + GN_EPS)
        o_ref[:, h * dv:(h + 1) * dv] = (_silu(g_ref[:, h * dv:(h + 1) * dv].astype(F32)) * on).astype(BF16)


def _retention(proj, chunk=512):
    s = proj.shape[0]
    chunk = min(chunk, s)
    log_g = jnp.log1p(-jnp.exp2(-5.0 - jnp.arange(RET_HEADS, dtype=F32)))
    lg = jnp.broadcast_to(log_g[:, None, None], (RET_HEADS, SUBLANES, LANES))
    qk_w, v_w = RET_GROUP * RET_DK, RET_GROUP * RET_DV
    assert RET_HEADS % RET_GROUP == 0
    return pl.pallas_call(
        _ret_kernel,
        grid=(RET_HEADS // RET_GROUP, s // chunk),
        in_specs=[pl.BlockSpec((RET_GROUP, SUBLANES, LANES), lambda h, n: (h, 0, 0)),
                  pl.BlockSpec((chunk, qk_w), lambda h, n: (n, OFF_RQ // qk_w + h)),
                  pl.BlockSpec((chunk, qk_w), lambda h, n: (n, OFF_RK // qk_w + h)),
                  pl.BlockSpec((chunk, v_w), lambda h, n: (n, OFF_RV // v_w + h)),
                  pl.BlockSpec((chunk, v_w), lambda h, n: (n, OFF_RG // v_w + h))],
        out_specs=pl.BlockSpec((chunk, v_w), lambda h, n: (n, h)),
        out_shape=jax.ShapeDtypeStruct((s, RET_V_W), BF16),
        scratch_shapes=[pltpu.VMEM((RET_GROUP, RET_DK, RET_DV), F32),
                        pltpu.VMEM((RET_GROUP, chunk, chunk), F32),
                        pltpu.VMEM((RET_GROUP, chunk, RET_DK), F32),
                        pltpu.VMEM((RET_GROUP, chunk, RET_DK), F32)],
        compiler_params=_cparams("arbitrary", "arbitrary"),
        name="ret",
    )(lg, proj, proj, proj, proj)


def _moba_kernel(q_ref, k_ref, v_hbm, o_ref, km_ref, oh_ref, vt_ref, qt_ref, acc_ref, m_ref, v_ref, v_sem,
                 *slots):
    hp = pl.program_id(0)
    bq = pl.program_id(1)
    tq = q_ref.shape[0]
    hd = MOBA_HD
    blk = MOBA_BLOCK
    nb = k_ref.shape[0] // blk
    gw = MOBA_GROUP * blk
    vrows = vt_ref.shape[2]

    @pl.when((hp == 0) & (bq == 0))
    def _():
        lane_i = lax.broadcasted_iota(jnp.int32, (blk, LANES), 1)

        def body(n, carry):
            oh_ref[pl.ds(pl.multiple_of(n * blk, blk), blk), :] = jnp.where(lane_i == n, 1.0, 0.0).astype(BF16)
            return carry

        lax.fori_loop(0, nb, body, 0)

    def value_copy(pair):
        cols = pl.ds(pl.multiple_of(OFF_MV + pair * (MOBA_PAIR * hd), MOBA_PAIR * hd), MOBA_PAIR * hd)
        return pltpu.make_async_copy(v_hbm.at[:, cols], v_ref, v_sem)

    @pl.when((hp == 0) & (bq == 0))
    def _():
        value_copy(0).start()

    @pl.when(bq == 0)
    def _():
        value_copy(hp).wait()
        km_ref[...] = jnp.zeros_like(km_ref)

        def body(g, carry):
            for t in range(MOBA_GROUP):
                n = g * MOBA_GROUP + t
                rows = pl.ds(pl.multiple_of(n * blk, blk), blk)
                km_ref[pl.ds(n, 1), :] = jnp.mean(k_ref[rows, :].astype(F32), axis=0, keepdims=True)
                for h in range(MOBA_PAIR):
                    vt_ref[h, g, 0:hd, t * blk:(t + 1) * blk] = (
                        v_ref[rows, h * hd:(h + 1) * hd].astype(F32).T.astype(BF16))
            for h in range(MOBA_PAIR):
                vt_ref[h, g, hd:vrows, :] = jnp.ones((vrows - hd, gw), BF16)
            return carry

        lax.fori_loop(0, nb // MOBA_GROUP, body, 0)

        @pl.when(hp + 1 < pl.num_programs(0))
        def _():
            value_copy(hp + 1).start()

    gr = km_ref.shape[0]
    row_i = lax.broadcasted_iota(jnp.int32, (gr, tq), 0)
    row_f = row_i.astype(F32)
    own = bq * (tq // blk) + lax.broadcasted_iota(jnp.int32, (gr, tq), 1) // blk
    for h in range(MOBA_PAIR):
        qt = q_ref[:, h * hd:(h + 1) * hd].astype(F32).T.astype(BF16)
        km_hi, km_lo = _split_bf16(km_ref[:, h * hd:(h + 1) * hd])
        g = jnp.where(row_i < own, _dot(km_hi, qt) + _dot(km_lo, qt), NEG)
        bias = jnp.full((gr, tq), NEG, F32)
        for _ in range(MOBA_TOPK):
            mx = jnp.max(g, axis=0, keepdims=True)
            idx = jnp.min(jnp.where(g == mx, row_f, float(gr)), axis=0, keepdims=True)
            hit = row_f == idx
            bias = jnp.where(hit, jnp.where(mx > 0.5 * NEG, 0.0, NEG), bias)
            g = jnp.where(hit, 2.0 * NEG, g)
        bias = jnp.where(row_i == own, 0.0, bias)
        qt_ref[h] = jnp.concatenate(
            [qt, bias.astype(BF16), jnp.full((LANES - gr, tq), NEG, BF16)], axis=0)
        m_ref[h] = jnp.full((1, tq), NEG, F32)
        acc_ref[h] = jnp.zeros((vrows, tq), F32)

    def scores(g, slot_ref):
        rows = pl.ds(pl.multiple_of(g * gw, gw), gw)
        oh = oh_ref[rows, :]
        for h in range(MOBA_PAIR):
            k_ext = jnp.concatenate([k_ref[rows, h * hd:(h + 1) * hd], oh], axis=1)
            slot_ref[h] = _dot(k_ext, qt_ref[h])

    def accumulate(g, slot_ref, causal):
        if causal:
            own_rows = pl.ds(pl.multiple_of((bq * tq) % gw, tq), tq)
            keep = lax.broadcasted_iota(jnp.int32, (tq, tq), 0) <= lax.broadcasted_iota(jnp.int32, (tq, tq), 1)
            for h in range(MOBA_PAIR):
                slot_ref[h, own_rows, :] = jnp.where(keep, slot_ref[h, own_rows, :], NEG)
        for h in range(MOBA_PAIR):
            s = slot_ref[h]
            m_old = m_ref[h]
            m_new = jnp.maximum(m_old, jnp.max(s, axis=0, keepdims=True))
            p = jnp.exp2(s - m_new).astype(BF16)
            acc_ref[h] = jnp.exp2(m_old - m_new) * acc_ref[h] + _dot(vt_ref[h, g], p)
            m_ref[h] = m_new

    nslot = len(slots)
    full_groups = (bq * tq) // gw
    scores(0, slots[0])

    def body(i, carry):
        g = nslot * i
        for u in range(nslot):
            scores(g + u + 1, slots[(u + 1) % nslot])
            accumulate(g + u, slots[u], False)
        return carry

    lax.fori_loop(0, full_groups // nslot, body, 0)
    base = (full_groups // nslot) * nslot
    for r in range(nslot):
        @pl.when(full_groups - base == r)
        def _(r=r):
            for u in range(r):
                scores(base + u + 1, slots[u + 1])
                accumulate(base + u, slots[u], False)
            accumulate(base + r, slots[r], True)

    for h in range(MOBA_PAIR):
        acc = acc_ref[h]
        o_ref[:, h * hd:(h + 1) * hd] = (acc[0:hd, :] / acc[hd:hd + 1, :]).T.astype(BF16)


def _moba(proj, tq=2 * MOBA_BLOCK):
    s = proj.shape[0]
    nb = s // MOBA_BLOCK
    w = MOBA_PAIR * MOBA_HD
    gw = MOBA_GROUP * MOBA_BLOCK
    bf16_rows = 2 * SUBLANES
    vrows = MOBA_HD + bf16_rows
    gate_rows = -(-nb // bf16_rows) * bf16_rows
    assert s % gw == 0 and gate_rows <= LANES and MOBA_HEADS % MOBA_PAIR == 0
    assert tq % MOBA_BLOCK == 0 and gw % tq == 0
    return pl.pallas_call(
        _moba_kernel,
        grid=(MOBA_HEADS // MOBA_PAIR, s // tq),
        in_specs=[pl.BlockSpec((tq, w), lambda h, b: (b, OFF_MQ // w + h)),
                  pl.BlockSpec((s, w), lambda h, b: (0, OFF_MK // w + h)),
                  pl.BlockSpec(memory_space=pl.ANY)],
        out_specs=pl.BlockSpec((tq, w), lambda h, b: (b, h)),
        out_shape=jax.ShapeDtypeStruct((s, MOBA_W), BF16),
        scratch_shapes=[pltpu.VMEM((gate_rows, w), F32),
                        pltpu.VMEM((s, LANES), BF16),
                        pltpu.VMEM((MOBA_PAIR, s // gw, vrows, gw), BF16),
                        pltpu.VMEM((MOBA_PAIR, 2 * MOBA_HD, tq), BF16),
                        pltpu.VMEM((MOBA_PAIR, vrows, tq), F32),
                        pltpu.VMEM((MOBA_PAIR, 1, tq), F32),
                        pltpu.VMEM((s, w), BF16),
                        pltpu.SemaphoreType.DMA(())]
                       + [pltpu.VMEM((MOBA_PAIR, gw, tq), F32)] * MOBA_SLOTS,
        compiler_params=_cparams("arbitrary", "arbitrary"),
        name="moba",
    )(proj, proj, proj)


def _merge_kernel(or_ref, om_ref, ga_ref, gb_ref, wr_ref, wm_ref, o_ref):
    yr = _dot(or_ref[...], wr_ref[...])
    ym = _dot(om_ref[...], wm_ref[...])
    ga = jax.nn.sigmoid(ga_ref[...].astype(F32))
    gb = jax.nn.sigmoid(gb_ref[...].astype(F32))
    o_ref[...] = (ga * yr + gb * ym).astype(BF16)


def _merge(o_ret, o_moba, proj, w_r, w_m, tm=1024, tn=1024):
    s = o_ret.shape[0]
    tm = min(tm, s)
    d = D_MODEL
    return pl.pallas_call(
        _merge_kernel,
        grid=(s // tm, d // tn),
        in_specs=[pl.BlockSpec((tm, RET_V_W), lambda i, j: (i, 0)),
                  pl.BlockSpec((tm, MOBA_W), lambda i, j: (i, 0)),
                  pl.BlockSpec((tm, tn), lambda i, j: (i, OFF_GA // tn + j)),
                  pl.BlockSpec((tm, tn), lambda i, j: (i, OFF_GB // tn + j)),
                  pl.BlockSpec((RET_V_W, tn), lambda i, j: (0, j)),
                  pl.BlockSpec((MOBA_W, tn), lambda i, j: (0, j))],
        out_specs=pl.BlockSpec((tm, tn), lambda i, j: (i, j)),
        out_shape=jax.ShapeDtypeStruct((s, d), BF16),
        compiler_params=_cparams("arbitrary", "arbitrary"),
        name="merge",
    )(o_ret, o_moba, proj, proj, w_r, w_m)


def _mixout_kernel(m_ref, wo_ref, x_ref, npost_ref, g1_ref, npre_ref, sc_ref, sh_ref, x1_ref, u2_ref):
    for r0 in range(0, m_ref.shape[0], MIX_ROWS):
        rows = slice(r0, r0 + MIX_ROWS)
        y = _dot(m_ref[rows, :], wo_ref[...])
        x1 = x_ref[rows, :] + (1.0 + g1_ref[...]) * (_rms(y) * npost_ref[...])
        x1_ref[rows, :] = x1
        u2_ref[rows, :] = ((_rms(x1) * npre_ref[...]) * (1.0 + sc_ref[...]) + sh_ref[...]).astype(BF16)


def _mixout(merged, w_o, x, npost, g1, npre, sc2, sh2, tm=512):
    s, d = x.shape
    tm = min(tm, s)
    vec = pl.BlockSpec((1, d), lambda i: (0, 0))
    rows = pl.BlockSpec((tm, d), lambda i: (i, 0))
    return pl.pallas_call(
        _mixout_kernel,
        grid=(s // tm,),
        in_specs=[rows, pl.BlockSpec((d, d), lambda i: (0, 0), pipeline_mode=pl.Buffered(1)), rows,
                  vec, vec, vec, vec, vec],
        out_specs=[rows, rows],
        out_shape=[jax.ShapeDtypeStruct((s, d), F32), jax.ShapeDtypeStruct((s, d), BF16)],
        compiler_params=_cparams("arbitrary"),
        name="mixout",
    )(merged, w_o, x, npost, g1, npre, sc2, sh2)


def _ffn_kernel(u_ref, wa_ref, wb_ref, cwa_ref, cwb_ref, cba_ref, cbb_ref, wd_ref, x1_hbm, npost_ref, g2_ref,
                o_ref, ha_ref, hb_ref, act_ref, tail_ref, x1_ref, x1_sem):
    i = pl.program_id(0)
    j = pl.program_id(1)
    tm = u_ref.shape[0]
    tn = wa_ref.shape[1]
    halo = SUBLANES

    def residual_copy():
        return pltpu.make_async_copy(x1_hbm.at[pl.ds(pl.multiple_of(i * tm, tm), tm), :], x1_ref, x1_sem)

    @pl.when((i == 0) & (j == 0))
    def _():
        tail_ref[...] = jnp.zeros_like(tail_ref)

    @pl.when(j == 0)
    def _():
        residual_copy().start()
        o_ref[...] = jnp.zeros_like(o_ref)

    ha_ref[0:halo, :] = tail_ref[j, 0]
    hb_ref[0:halo, :] = tail_ref[j, 1]
    for c in range(tn // FFN_CHUNK):
        cs = slice(c * FFN_CHUNK, (c + 1) * FFN_CHUNK)
        ha_ref[halo:halo + tm, cs] = _dot(u_ref[...], wa_ref[:, cs])
        hb_ref[halo:halo + tm, cs] = _dot(u_ref[...], wb_ref[:, cs])
    tail_ref[j, 0] = ha_ref[tm:tm + halo, :]
    tail_ref[j, 1] = hb_ref[tm:tm + halo, :]

    def conv(hbuf, cw, cb, r0, cs):
        out = cb
        for t in range(CONV_WIDTH):
            out = out + hbuf[halo + r0 - t:halo + r0 - t + FFN_ROWS, cs] * cw[CONV_WIDTH - 1 - t:CONV_WIDTH - t, :]
        return out

    for c in range(tn // FFN_CHUNK):
        cs = slice(c * FFN_CHUNK, (c + 1) * FFN_CHUNK)
        cwa, cwb, cba, cbb = cwa_ref[:, cs], cwb_ref[:, cs], cba_ref[:, cs], cbb_ref[:, cs]
        for r0 in range(0, tm, FFN_ROWS):
            a = conv(ha_ref, cwa, cba, r0, cs)
            bgate = conv(hb_ref, cwb, cbb, r0, cs)
            act_ref[r0:r0 + FFN_ROWS, cs] = (_silu(a) * bgate).astype(BF16)
        for n0 in range(0, o_ref.shape[1], FFN_DOWN_COLS):
            ns = slice(n0, n0 + FFN_DOWN_COLS)
            o_ref[:, ns] += _dot(act_ref[:, cs], wd_ref[cs, ns])

    @pl.when(j == pl.num_programs(1) - 1)
    def _():
        residual_copy().wait()

        def finish(rows, y):
            o_ref[rows, :] = x1_ref[rows, :] + (1.0 + g2_ref[...]) * (_rms(y) * npost_ref[...])

        _for_row_chunks(tm, lambda rows: o_ref[rows, :], finish)


def _ffn(u2, w_up, conv_w, conv_b, w_down, x1, npost, g2, tm=1024, tn=512):
    s, d = x1.shape
    tm = min(tm, s)
    nj = D_FF // tn
    assert D_FF % tn == 0
    vec = pl.BlockSpec((1, d), lambda i, j: (0, 0))
    rows = pl.BlockSpec((tm, d), lambda i, j: (i, 0))
    resid = pl.BlockSpec(memory_space=pl.ANY)
    return pl.pallas_call(
        _ffn_kernel,
        grid=(s // tm, nj),
        in_specs=[rows,
                  pl.BlockSpec((d, tn), lambda i, j: (0, j)),
                  pl.BlockSpec((d, tn), lambda i, j: (0, j + nj)),
                  pl.BlockSpec((CONV_WIDTH, tn), lambda i, j: (0, j)),
                  pl.BlockSpec((CONV_WIDTH, tn), lambda i, j: (0, j + nj)),
                  pl.BlockSpec((1, tn), lambda i, j: (0, j)),
                  pl.BlockSpec((1, tn), lambda i, j: (0, j + nj)),
                  pl.BlockSpec((tn, d), lambda i, j: (j, 0)),
                  resid, vec, vec],
        out_specs=rows,
        out_shape=jax.ShapeDtypeStruct((s, d), F32),
        scratch_shapes=[pltpu.VMEM((tm + SUBLANES, tn), F32),
                        pltpu.VMEM((tm + SUBLANES, tn), F32),
                        pltpu.VMEM((tm, tn), BF16),
                        pltpu.VMEM((nj, 2, SUBLANES, tn), F32),
                        pltpu.VMEM((tm, d), F32),
                        pltpu.SemaphoreType.DMA(())],
        compiler_params=_cparams("arbitrary", "arbitrary"),
        name="ffn",
    )(u2, w_up, w_up, conv_w, conv_w, conv_b, conv_b, w_down, x1, npost, g2)


def _block(x, c, ada_w, ada_b, mix_norm_pre, mix_norm_post, w_in, ret_w_o, moba_w_o, w_out,
           ffn_norm_pre, ffn_norm_post, ffn_w_up, ffn_conv_w, ffn_conv_b, ffn_w_down, cos2, sin2):
    d = x.shape[1]
    mod = _ada(c, ada_w, ada_b[None, :])
    sh1, sc1, g1, sh2, sc2, g2 = [mod[:, t * d:(t + 1) * d] for t in range(6)]
    proj = _inproj(x, mix_norm_pre[None, :], sc1, sh1, cos2, sin2, w_in.astype(BF16))
    o_ret = _retention(proj)
    o_moba = _moba(proj)
    merged = _merge(o_ret, o_moba, proj, ret_w_o.astype(BF16), moba_w_o.astype(BF16))
    x1, u2 = _mixout(merged, w_out.astype(BF16), x, mix_norm_post[None, :], g1,
                     ffn_norm_pre[None, :], sc2, sh2)
    return _ffn(u2, ffn_w_up.astype(BF16), ffn_conv_w, ffn_conv_b[None, :], ffn_w_down.astype(BF16),
                x1, ffn_norm_post[None, :], g2)


def kernel(x, c, ada_w, ada_b, mix_norm_pre, mix_norm_post, w_in, ret_w_o, moba_w_o, w_out,
           ffn_norm_pre, ffn_norm_post, ffn_w_up, ffn_conv_w, ffn_conv_b, ffn_w_down):
    batch, s, _ = x.shape
    depth = ada_w.shape[0]
    cos2, sin2 = _rope_tables(s)
    outs = []
    for bi in range(batch):
        xb = x[bi]
        for l in range(depth):
            xb = _block(xb, c[bi:bi + 1], ada_w[l], ada_b[l], mix_norm_pre[l], mix_norm_post[l], w_in[l],
                        ret_w_o[l], moba_w_o[l], w_out[l], ffn_norm_pre[l], ffn_norm_post[l], ffn_w_up[l],
                        ffn_conv_w[l], ffn_conv_b[l], ffn_w_down[l], cos2, sin2)
        outs.append(xb)
    return jnp.stack(outs)
```

```python
import math

import jax
import jax.numpy as jnp
from jax import lax
from jax.experimental import pallas as pl
from jax.experimental.pallas import tpu as pltpu

D_MODEL = 2048
RET_HEADS = 8
RET_DK = 128
RET_DV = 256
ROPE_BASE = 10000.0
MOBA_HEADS = 16
MOBA_HD = 128
MOBA_BLOCK = 256
MOBA_TOPK = 3
D_FF = 5632
CONV_WIDTH = 3
NORM_EPS = 1e-6
GN_EPS = 1e-5

RET_QK_W = RET_HEADS * RET_DK
RET_V_W = RET_HEADS * RET_DV
MOBA_W = MOBA_HEADS * MOBA_HD
OFF_RQ = 0
OFF_RK = OFF_RQ + RET_QK_W
OFF_RV = OFF_RK + RET_QK_W
OFF_RG = OFF_RV + RET_V_W
OFF_MQ = OFF_RG + RET_V_W
OFF_MK = OFF_MQ + MOBA_W
OFF_MV = OFF_MK + MOBA_W
OFF_GA = OFF_MV + MOBA_W
OFF_GB = OFF_GA + D_MODEL
IN_WIDTH = OFF_GB + D_MODEL

LANES = 128
SUBLANES = 8
NEG = -1e30
MOBA_QSCALE = math.log2(math.e) / math.sqrt(MOBA_HD)
ROPE_SPAN = 128
RET_GROUP = 8
MOBA_GROUP = 2
MOBA_SLOTS = 6
MOBA_PAIR = 2
FFN_CHUNK = 256
FFN_ROWS = 64
FFN_DOWN_COLS = 512
MIX_ROWS = 256
NORM_ROWS = 32
NORM_UNROLL = 4
VMEM_LIMIT = 58 * 1024 * 1024

F32 = jnp.float32
BF16 = jnp.bfloat16


def _cparams(*sem):
    return pltpu.CompilerParams(dimension_semantics=sem, vmem_limit_bytes=VMEM_LIMIT)


def _dot(a, b):
    return jnp.dot(a, b, preferred_element_type=F32)


def _dot_nt(a, b):
    return lax.dot_general(a, b, (((1,), (1,)), ((), ())), preferred_element_type=F32)


def _split_bf16(x):
    hi = x.astype(BF16)
    lo = (x - hi.astype(F32)).astype(BF16)
    return hi, lo


def _rms(x):
    return x * lax.rsqrt(jnp.mean(x * x, axis=-1, keepdims=True) + NORM_EPS)


def _silu(x):
    return x * jax.nn.sigmoid(x)


def _for_row_chunks(n_rows, load, finish):
    step = NORM_ROWS * NORM_UNROLL

    def body(r, carry):
        pieces = [pl.ds(pl.multiple_of(r * step + t * NORM_ROWS, NORM_ROWS), NORM_ROWS)
                  for t in range(NORM_UNROLL)]
        loaded = [load(rows) for rows in pieces]
        for rows, vals in zip(pieces, loaded):
            finish(rows, vals)
        return carry

    lax.fori_loop(0, n_rows // step, body, 0)


def _ada_kernel(c_ref, w_ref, b_ref, o_ref):
    c = jnp.broadcast_to(_silu(c_ref[...]), (SUBLANES, c_ref.shape[1]))
    c_hi, c_lo = _split_bf16(c)
    w_hi, w_lo = _split_bf16(w_ref[...])
    r = _dot(c_hi, w_hi) + _dot(c_lo, w_hi) + _dot(c_hi, w_lo)
    o_ref[...] = r[0:1] + b_ref[...]


def _ada(c, w, b, tn=1024):
    d, n = w.shape
    return pl.pallas_call(
        _ada_kernel,
        grid=(n // tn,),
        in_specs=[pl.BlockSpec((1, d), lambda j: (0, 0)),
                  pl.BlockSpec((d, tn), lambda j: (0, j)),
                  pl.BlockSpec((1, tn), lambda j: (0, j))],
        out_specs=pl.BlockSpec((1, tn), lambda j: (0, j)),
        out_shape=jax.ShapeDtypeStruct((1, n), F32),
        compiler_params=_cparams("arbitrary"),
        name="ada",
    )(c, w, b)


def _rope_kernel(inv_ref, cos_ref, sin_ref):
    ts = cos_ref.shape[0]
    spans = ts // ROPE_SPAN
    inv = inv_ref[...]
    lane = lax.broadcasted_iota(jnp.int32, (ROPE_SPAN, LANES), 1)
    off = lax.broadcasted_iota(jnp.int32, (ROPE_SPAN, LANES), 0).astype(F32) * inv
    cos_o, sin_o = jnp.cos(off), jnp.sin(off)
    first = pl.program_id(0) * spans
    base = ((lax.broadcasted_iota(jnp.int32, (spans, LANES), 0) + first) * ROPE_SPAN).astype(F32) * inv
    cos_b, sin_b = jnp.cos(base), jnp.sin(base)
    for b in range(spans):
        cb, sb = cos_b[b:b + 1, :], sin_b[b:b + 1, :]
        sin = sb * cos_o + cb * sin_o
        cos_ref[b * ROPE_SPAN:(b + 1) * ROPE_SPAN, :] = cb * cos_o - sb * sin_o
        sin_ref[b * ROPE_SPAN:(b + 1) * ROPE_SPAN, :] = jnp.where(lane < RET_DK // 2, -sin, sin)


def _rope_tables(s, ts=2048):
    ts = min(ts, s)
    half = RET_DK // 2
    inv = ROPE_BASE ** (-jnp.arange(half, dtype=F32) / half)
    inv2 = jnp.concatenate([inv, inv])[None, :]
    return pl.pallas_call(
        _rope_kernel,
        grid=(s // ts,),
        in_specs=[pl.BlockSpec((1, LANES), lambda i: (0, 0))],
        out_specs=[pl.BlockSpec((ts, LANES), lambda i: (i, 0))] * 2,
        out_shape=[jax.ShapeDtypeStruct((s, LANES), F32)] * 2,
        compiler_params=_cparams("arbitrary"),
        name="rope",
    )(inv2)


def _inproj_kernel(x_hbm, nw_ref, sc_ref, sh_ref, cos_ref, sin_ref, w_ref, o_ref, u_ref, x_ref, x_sem):
    i = pl.program_id(0)
    j = pl.program_id(1)
    tm, tn = o_ref.shape
    nj = IN_WIDTH // tn
    chunk = tm // nj

    def x_copy(tile, c):
        src = pl.ds(pl.multiple_of(tile * tm + c * chunk, chunk), chunk)
        dst = pl.ds(pl.multiple_of(c * chunk, chunk), chunk)
        return pltpu.make_async_copy(x_hbm.at[src, :], x_ref.at[dst, :], x_sem)

    @pl.when(j == 0)
    def _():
        @pl.when(i == 0)
        def _():
            for c in range(nj):
                x_copy(0, c).start()

        for c in range(nj):
            x_copy(i, c).wait()

        def finish(rows, x):
            y = _rms(x) * nw_ref[...]
            u_ref[rows, :] = (y * (1.0 + sc_ref[...]) + sh_ref[...]).astype(BF16)

        _for_row_chunks(tm, lambda rows: x_ref[rows, :], finish)

    @pl.when(i + 1 < pl.num_programs(0))
    def _():
        x_copy(i + 1, j).start()

    r = _dot(u_ref[...], w_ref[...])
    is_mq = (j >= OFF_MQ // tn) & (j < OFF_MK // tn)
    o_ref[...] = (r * jnp.where(is_mq, MOBA_QSCALE, 1.0).astype(F32)).astype(BF16)

    @pl.when(j * tn < 2 * RET_QK_W)
    def _():
        cos = cos_ref[...]
        sin = sin_ref[...]
        for h in range(tn // RET_DK):
            scale = jnp.where(j * tn + h * RET_DK >= RET_QK_W, RET_DK ** -0.5, 1.0).astype(F32)
            xh = r[:, h * RET_DK:(h + 1) * RET_DK]
            rot = xh * cos + pltpu.roll(xh, RET_DK // 2, axis=1) * sin
            o_ref[:, h * RET_DK:(h + 1) * RET_DK] = (rot * scale).astype(BF16)


def _inproj(x, nw, sc, sh, cos2, sin2, w, tm=1024, tn=2048):
    s, d = x.shape
    n = w.shape[1]
    tm = min(tm, s)
    nj = n // tn
    assert n == IN_WIDTH and tm % nj == 0 and (2 * RET_QK_W) % tn == 0 and OFF_MQ % tn == 0 and OFF_MK % tn == 0
    vec = pl.BlockSpec((1, d), lambda i, j: (0, 0))
    tab = pl.BlockSpec((tm, LANES), lambda i, j: (i, 0))
    return pl.pallas_call(
        _inproj_kernel,
        grid=(s // tm, nj),
        in_specs=[pl.BlockSpec(memory_space=pl.ANY), vec, vec, vec, tab, tab,
                  pl.BlockSpec((d, tn), lambda i, j: (0, j))],
        out_specs=pl.BlockSpec((tm, tn), lambda i, j: (i, j)),
        out_shape=jax.ShapeDtypeStruct((s, n), BF16),
        scratch_shapes=[pltpu.VMEM((tm, d), BF16),
                        pltpu.VMEM((tm, d), F32),
                        pltpu.SemaphoreType.DMA(())],
        compiler_params=_cparams("arbitrary", "arbitrary"),
        name="inproj",
    )(x, nw, sc, sh, cos2, sin2, w)


def _ret_kernel(lg_ref, q_ref, k_ref, v_ref, g_ref, o_ref, r_ref, dec_ref, xi_ref, zeta_ref):
    n = pl.program_id(1)
    c = q_ref.shape[0]
    dk, dv = RET_DK, RET_DV

    @pl.when(n == 0)
    def _():
        r_ref[...] = jnp.zeros_like(r_ref)
        d = lax.broadcasted_iota(jnp.int32, (c, c), 0) - lax.broadcasted_iota(jnp.int32, (c, c), 1)
        row = lax.broadcasted_iota(jnp.int32, (c, dk), 0)
        for h in range(RET_GROUP):
            lg = lg_ref[h, 0:1, 0:1]
            dec_ref[h] = jnp.where(d >= 0, jnp.exp(jnp.maximum(d, 0).astype(F32) * lg), 0.0)
            xi_ref[h] = jnp.exp((row + 1).astype(F32) * lg)
            zeta_ref[h] = jnp.exp((c - 1 - row).astype(F32) * lg)

    for h in range(RET_GROUP):
        q = q_ref[:, h * dk:(h + 1) * dk]
        k = k_ref[:, h * dk:(h + 1) * dk]
        v = v_ref[:, h * dv:(h + 1) * dv]
        s = _dot_nt(q, k) * dec_ref[h]
        inner = _dot(s.astype(BF16), v)
        r = r_ref[h]
        cross = _dot((q.astype(F32) * xi_ref[h]).astype(BF16), r.astype(BF16))
        o = inner + cross
        kz = (k.astype(F32) * zeta_ref[h]).T.astype(BF16)
        r_ref[h] = r * jnp.exp(lg_ref[h, 0:1, 0:1] * c) + _dot(kz, v)
        oc = o - jnp.mean(o, axis=-1, keepdims=True)
        on = oc * lax.rsqrt(jnp.mean(oc * oc, axis=-1, keepdims=True) + GN_EPS)
        o_ref[:, h * dv:(h + 1) * dv] = (_silu(g_ref[:, h * dv:(h + 1) * dv].astype(F32)) * on).astype(BF16)


def _retention(proj, chunk=512):
    s = proj.shape[0]
    chunk = min(chunk, s)
    log_g = jnp.log1p(-jnp.exp2(-5.0 - jnp.arange(RET_HEADS, dtype=F32)))
    lg = jnp.broadcast_to(log_g[:, None, None], (RET_HEADS, SUBLANES, LANES))
    qk_w, v_w = RET_GROUP * RET_DK, RET_GROUP * RET_DV
    assert RET_HEADS % RET_GROUP == 0
    return pl.pallas_call(
        _ret_kernel,
        grid=(RET_HEADS // RET_GROUP, s // chunk),
        in_specs=[pl.BlockSpec((RET_GROUP, SUBLANES, LANES), lambda h, n: (h, 0, 0)),
                  pl.BlockSpec((chunk, qk_w), lambda h, n: (n, OFF_RQ // qk_w + h)),
                  pl.BlockSpec((chunk, qk_w), lambda h, n: (n, OFF_RK // qk_w + h)),
                  pl.BlockSpec((chunk, v_w), lambda h, n: (n, OFF_RV // v_w + h)),
                  pl.BlockSpec((chunk, v_w), lambda h, n: (n, OFF_RG // v_w + h))],
        out_specs=pl.BlockSpec((chunk, v_w), lambda h, n: (n, h)),
        out_shape=jax.ShapeDtypeStruct((s, RET_V_W), BF16),
        scratch_shapes=[pltpu.VMEM((RET_GROUP, RET_DK, RET_DV), F32),
                        pltpu.VMEM((RET_GROUP, chunk, chunk), F32),
                        pltpu.VMEM((RET_GROUP, chunk, RET_DK), F32),
                        pltpu.VMEM((RET_GROUP, chunk, RET_DK), F32)],
        compiler_params=_cparams("arbitrary", "arbitrary"),
        name="ret",
    )(lg, proj, proj, proj, proj)


def _moba_kernel(q_ref, k_ref, v_hbm, o_ref, km_ref, oh_ref, vt_ref, qt_ref, acc_ref, m_ref, v_ref, v_sem,
                 *slots):
    hp = pl.program_id(0)
    bq = pl.program_id(1)
    tq = q_ref.shape[0]
    hd = MOBA_HD
    blk = MOBA_BLOCK
    nb = k_ref.shape[0] // blk
    gw = MOBA_GROUP * blk
    vrows = vt_ref.shape[2]

    @pl.when((hp == 0) & (bq == 0))
    def _():
        lane_i = lax.broadcasted_iota(jnp.int32, (blk, LANES), 1)

        def body(n, carry):
            oh_ref[pl.ds(pl.multiple_of(n * blk, blk), blk), :] = jnp.where(lane_i == n, 1.0, 0.0).astype(BF16)
            return carry

        lax.fori_loop(0, nb, body, 0)

    def value_copy(pair):
        cols = pl.ds(pl.multiple_of(OFF_MV + pair * (MOBA_PAIR * hd), MOBA_PAIR * hd), MOBA_PAIR * hd)
        return pltpu.make_async_copy(v_hbm.at[:, cols], v_ref, v_sem)

    @pl.when((hp == 0) & (bq == 0))
    def _():
        value_copy(0).start()

    @pl.when(bq == 0)
    def _():
        value_copy(hp).wait()
        km_ref[...] = jnp.zeros_like(km_ref)

        def body(g, carry):
            for t in range(MOBA_GROUP):
                n = g * MOBA_GROUP + t
                rows = pl.ds(pl.multiple_of(n * blk, blk), blk)
                km_ref[pl.ds(n, 1), :] = jnp.mean(k_ref[rows, :].astype(F32), axis=0, keepdims=True)
                for h in range(MOBA_PAIR):
                    vt_ref[h, g, 0:hd, t * blk:(t + 1) * blk] = (
                        v_ref[rows, h * hd:(h + 1) * hd].astype(F32).T.astype(BF16))
            for h in range(MOBA_PAIR):
                vt_ref[h, g, hd:vrows, :] = jnp.ones((vrows - hd, gw), BF16)
            return carry

        lax.fori_loop(0, nb // MOBA_GROUP, body, 0)

        @pl.when(hp + 1 < pl.num_programs(0))
        def _():
            value_copy(hp + 1).start()

    gr = km_ref.shape[0]
    row_i = lax.broadcasted_iota(jnp.int32, (gr, tq), 0)
    row_f = row_i.astype(F32)
    own = bq * (tq // blk) + lax.broadcasted_iota(jnp.int32, (gr, tq), 1) // blk
    for h in range(MOBA_PAIR):
        qt = q_ref[:, h * hd:(h + 1) * hd].astype(F32).T.astype(BF16)
        km_hi, km_lo = _split_bf16(km_ref[:, h * hd:(h + 1) * hd])
        g = jnp.where(row_i < own, _dot(km_hi, qt) + _dot(km_lo, qt), NEG)
        bias = jnp.full((gr, tq), NEG, F32)
        for _ in range(MOBA_TOPK):
            mx = jnp.max(g, axis=0, keepdims=True)
            idx = jnp.min(jnp.where(g == mx, row_f, float(gr)), axis=0, keepdims=True)
            hit = row_f == idx
            bias = jnp.where(hit, jnp.where(mx > 0.5 * NEG, 0.0, NEG), bias)
            g = jnp.where(hit, 2.0 * NEG, g)
        bias = jnp.where(row_i == own, 0.0, bias)
        qt_ref[h] = jnp.concatenate(
            [qt, bias.astype(BF16), jnp.full((LANES - gr, tq), NEG, BF16)], axis=0)
        m_ref[h] = jnp.full((1, tq), NEG, F32)
        acc_ref[h] = jnp.zeros((vrows, tq), F32)

    def scores(g, slot_ref):
        rows = pl.ds(pl.multiple_of(g * gw, gw), gw)
        oh = oh_ref[rows, :]
        for h in range(MOBA_PAIR):
            k_ext = jnp.concatenate([k_ref[rows, h * hd:(h + 1) * hd], oh], axis=1)
            slot_ref[h] = _dot(k_ext, qt_ref[h])

    def accumulate(g, slot_ref, causal):
        if causal:
            own_rows = pl.ds(pl.multiple_of((bq * tq) % gw, tq), tq)
            keep = lax.broadcasted_iota(jnp.int32, (tq, tq), 0) <= lax.broadcasted_iota(jnp.int32, (tq, tq), 1)
            for h in range(MOBA_PAIR):
                slot_ref[h, own_rows, :] = jnp.where(keep, slot_ref[h, own_rows, :], NEG)
        for h in range(MOBA_PAIR):
            s = slot_ref[h]
            m_old = m_ref[h]
            m_new = jnp.maximum(m_old, jnp.max(s, axis=0, keepdims=True))
            p = jnp.exp2(s - m_new).astype(BF16)
            acc_ref[h] = jnp.exp2(m_old - m_new) * acc_ref[h] + _dot(vt_ref[h, g], p)
            m_ref[h] = m_new

    nslot = len(slots)
    full_groups = (bq * tq) // gw
    scores(0, slots[0])

    def body(i, carry):
        g = nslot * i
        for u in range(nslot):
            scores(g + u + 1, slots[(u + 1) % nslot])
            accumulate(g + u, slots[u], False)
        return carry

    lax.fori_loop(0, full_groups // nslot, body, 0)
    base = (full_groups // nslot) * nslot
    for r in range(nslot):
        @pl.when(full_groups - base == r)
        def _(r=r):
            for u in range(r):
                scores(base + u + 1, slots[u + 1])
                accumulate(base + u, slots[u], False)
            accumulate(base + r, slots[r], True)

    for h in range(MOBA_PAIR):
        acc = acc_ref[h]
        o_ref[:, h * hd:(h + 1) * hd] = (acc[0:hd, :] / acc[hd:hd + 1, :]).T.astype(BF16)


def _moba(proj, tq=2 * MOBA_BLOCK):
    s = proj.shape[0]
    nb = s // MOBA_BLOCK
    w = MOBA_PAIR * MOBA_HD
    gw = MOBA_GROUP * MOBA_BLOCK
    bf16_rows = 2 * SUBLANES
    vrows = MOBA_HD + bf16_rows
    gate_rows = -(-nb // bf16_rows) * bf16_rows
    assert s % gw == 0 and gate_rows <= LANES and MOBA_HEADS % MOBA_PAIR == 0
    assert tq % MOBA_BLOCK == 0 and gw % tq == 0
    return pl.pallas_call(
        _moba_kernel,
        grid=(MOBA_HEADS // MOBA_PAIR, s // tq),
        in_specs=[pl.BlockSpec((tq, w), lambda h, b: (b, OFF_MQ // w + h)),
                  pl.BlockSpec((s, w), lambda h, b: (0, OFF_MK // w + h)),
                  pl.BlockSpec(memory_space=pl.ANY)],
        out_specs=pl.BlockSpec((tq, w), lambda h, b: (b, h)),
        out_shape=jax.ShapeDtypeStruct((s, MOBA_W), BF16),
        scratch_shapes=[pltpu.VMEM((gate_rows, w), F32),
                        pltpu.VMEM((s, LANES), BF16),
                        pltpu.VMEM((MOBA_PAIR, s // gw, vrows, gw), BF16),
                        pltpu.VMEM((MOBA_PAIR, 2 * MOBA_HD, tq), BF16),
                        pltpu.VMEM((MOBA_PAIR, vrows, tq), F32),
                        pltpu.VMEM((MOBA_PAIR, 1, tq), F32),
                        pltpu.VMEM((s, w), BF16),
                        pltpu.SemaphoreType.DMA(())]
                       + [pltpu.VMEM((MOBA_PAIR, gw, tq), F32)] * MOBA_SLOTS,
        compiler_params=_cparams("arbitrary", "arbitrary"),
        name="moba",
    )(proj, proj, proj)


def _merge_kernel(or_ref, om_ref, ga_ref, gb_ref, wr_ref, wm_ref, o_ref):
    yr = _dot(or_ref[...], wr_ref[...])
    ym = _dot(om_ref[...], wm_ref[...])
    ga = jax.nn.sigmoid(ga_ref[...].astype(F32))
    gb = jax.nn.sigmoid(gb_ref[...].astype(F32))
    o_ref[...] = (ga * yr + gb * ym).astype(BF16)


def _merge(o_ret, o_moba, proj, w_r, w_m, tm=1024, tn=1024):
    s = o_ret.shape[0]
    tm = min(tm, s)
    d = D_MODEL
    return pl.pallas_call(
        _merge_kernel,
        grid=(s // tm, d // tn),
        in_specs=[pl.BlockSpec((tm, RET_V_W), lambda i, j: (i, 0)),
                  pl.BlockSpec((tm, MOBA_W), lambda i, j: (i, 0)),
                  pl.BlockSpec((tm, tn), lambda i, j: (i, OFF_GA // tn + j)),
                  pl.BlockSpec((tm, tn), lambda i, j: (i, OFF_GB // tn + j)),
                  pl.BlockSpec((RET_V_W, tn), lambda i, j: (0, j)),
                  pl.BlockSpec((MOBA_W, tn), lambda i, j: (0, j))],
        out_specs=pl.BlockSpec((tm, tn), lambda i, j: (i, j)),
        out_shape=jax.ShapeDtypeStruct((s, d), BF16),
        compiler_params=_cparams("arbitrary", "arbitrary"),
        name="merge",
    )(o_ret, o_moba, proj, proj, w_r, w_m)


def _mixout_kernel(m_ref, wo_ref, x_ref, npost_ref, g1_ref, npre_ref, sc_ref, sh_ref, x1_ref, u2_ref):
    for r0 in range(0, m_ref.shape[0], MIX_ROWS):
        rows = slice(r0, r0 + MIX_ROWS)
        y = _dot(m_ref[rows, :], wo_ref[...])
        x1 = x_ref[rows, :] + (1.0 + g1_ref[...]) * (_rms(y) * npost_ref[...])
        x1_ref[rows, :] = x1
        u2_ref[rows, :] = ((_rms(x1) * npre_ref[...]) * (1.0 + sc_ref[...]) + sh_ref[...]).astype(BF16)


def _mixout(merged, w_o, x, npost, g1, npre, sc2, sh2, tm=512):
    s, d = x.shape
    tm = min(tm, s)
    vec = pl.BlockSpec((1, d), lambda i: (0, 0))
    rows = pl.BlockSpec((tm, d), lambda i: (i, 0))
    return pl.pallas_call(
        _mixout_kernel,
        grid=(s // tm,),
        in_specs=[rows, pl.BlockSpec((d, d), lambda i: (0, 0), pipeline_mode=pl.Buffered(1)), rows,
                  vec, vec, vec, vec, vec],
        out_specs=[rows, rows],
        out_shape=[jax.ShapeDtypeStruct((s, d), F32), jax.ShapeDtypeStruct((s, d), BF16)],
        compiler_params=_cparams("arbitrary"),
        name="mixout",
    )(merged, w_o, x, npost, g1, npre, sc2, sh2)


def _ffn_kernel(u_ref, wa_ref, wb_ref, cwa_ref, cwb_ref, cba_ref, cbb_ref, wd_ref, x1_hbm, npost_ref, g2_ref,
                o_ref, ha_ref, hb_ref, act_ref, tail_ref, x1_ref, x1_sem):
    i = pl.program_id(0)
    j = pl.program_id(1)
    tm = u_ref.shape[0]
    tn = wa_ref.shape[1]
    halo = SUBLANES

    def residual_copy():
        return pltpu.make_async_copy(x1_hbm.at[pl.ds(pl.multiple_of(i * tm, tm), tm), :], x1_ref, x1_sem)

    @pl.when((i == 0) & (j == 0))
    def _():
        tail_ref[...] = jnp.zeros_like(tail_ref)

    @pl.when(j == 0)
    def _():
        residual_copy().start()
        o_ref[...] = jnp.zeros_like(o_ref)

    ha_ref[0:halo, :] = tail_ref[j, 0]
    hb_ref[0:halo, :] = tail_ref[j, 1]
    for c in range(tn // FFN_CHUNK):
        cs = slice(c * FFN_CHUNK, (c + 1) * FFN_CHUNK)
        ha_ref[halo:halo + tm, cs] = _dot(u_ref[...], wa_ref[:, cs])
        hb_ref[halo:halo + tm, cs] = _dot(u_ref[...], wb_ref[:, cs])
    tail_ref[j, 0] = ha_ref[tm:tm + halo, :]
    tail_ref[j, 1] = hb_ref[tm:tm + halo, :]

    def conv(hbuf, cw, cb, r0, cs):
        out = cb
        for t in range(CONV_WIDTH):
            out = out + hbuf[halo + r0 - t:halo + r0 - t + FFN_ROWS, cs] * cw[CONV_WIDTH - 1 - t:CONV_WIDTH - t, :]
        return out

    for c in range(tn // FFN_CHUNK):
        cs = slice(c * FFN_CHUNK, (c + 1) * FFN_CHUNK)
        cwa, cwb, cba, cbb = cwa_ref[:, cs], cwb_ref[:, cs], cba_ref[:, cs], cbb_ref[:, cs]
        for r0 in range(0, tm, FFN_ROWS):
            a = conv(ha_ref, cwa, cba, r0, cs)
            bgate = conv(hb_ref, cwb, cbb, r0, cs)
            act_ref[r0:r0 + FFN_ROWS, cs] = (_silu(a) * bgate).astype(BF16)
        for n0 in range(0, o_ref.shape[1], FFN_DOWN_COLS):
            ns = slice(n0, n0 + FFN_DOWN_COLS)
            o_ref[:, ns] += _dot(act_ref[:, cs], wd_ref[cs, ns])

    @pl.when(j == pl.num_programs(1) - 1)
    def _():
        residual_copy().wait()

        def finish(rows, y):
            o_ref[rows, :] = x1_ref[rows, :] + (1.0 + g2_ref[...]) * (_rms(y) * npost_ref[...])

        _for_row_chunks(tm, lambda rows: o_ref[rows, :], finish)


def _ffn(u2, w_up, conv_w, conv_b, w_down, x1, npost, g2, tm=1024, tn=512):
    s, d = x1.shape
    tm = min(tm, s)
    nj = D_FF // tn
    assert D_FF % tn == 0
    vec = pl.BlockSpec((1, d), lambda i, j: (0, 0))
    rows = pl.BlockSpec((tm, d), lambda i, j: (i, 0))
    resid = pl.BlockSpec(memory_space=pl.ANY)
    return pl.pallas_call(
        _ffn_kernel,
        grid=(s // tm, nj),
        in_specs=[rows,
                  pl.BlockSpec((d, tn), lambda i, j: (0, j)),
                  pl.BlockSpec((d, tn), lambda i, j: (0, j + nj)),
                  pl.BlockSpec((CONV_WIDTH, tn), lambda i, j: (0, j)),
                  pl.BlockSpec((CONV_WIDTH, tn), lambda i, j: (0, j + nj)),
                  pl.BlockSpec((1, tn), lambda i, j: (0, j)),
                  pl.BlockSpec((1, tn), lambda i, j: (0, j + nj)),
                  pl.BlockSpec((tn, d), lambda i, j: (j, 0)),
                  resid, vec, vec],
        out_specs=rows,
        out_shape=jax.ShapeDtypeStruct((s, d), F32),
        scratch_shapes=[pltpu.VMEM((tm + SUBLANES, tn), F32),
                        pltpu.VMEM((tm + SUBLANES, tn), F32),
                        pltpu.VMEM((tm, tn), BF16),
                        pltpu.VMEM((nj, 2, SUBLANES, tn), F32),
                        pltpu.VMEM((tm, d), F32),
                        pltpu.SemaphoreType.DMA(())],
        compiler_params=_cparams("arbitrary", "arbitrary"),
        name="ffn",
    )(u2, w_up, w_up, conv_w, conv_w, conv_b, conv_b, w_down, x1, npost, g2)


def _block(x, c, ada_w, ada_b, mix_norm_pre, mix_norm_post, w_in, ret_w_o, moba_w_o, w_out,
           ffn_norm_pre, ffn_norm_post, ffn_w_up, ffn_conv_w, ffn_conv_b, ffn_w_down, cos2, sin2):
    d = x.shape[1]
    mod = _ada(c, ada_w, ada_b[None, :])
    sh1, sc1, g1, sh2, sc2, g2 = [mod[:, t * d:(t + 1) * d] for t in range(6)]
    proj = _inproj(x, mix_norm_pre[None, :], sc1, sh1, cos2, sin2, w_in.astype(BF16))
    o_ret = _retention(proj)
    o_moba = _moba(proj)
    merged = _merge(o_ret, o_moba, proj, ret_w_o.astype(BF16), moba_w_o.astype(BF16))
    x1, u2 = _mixout(merged, w_out.astype(BF16), x, mix_norm_post[None, :], g1,
                     ffn_norm_pre[None, :], sc2, sh2)
    return _ffn(u2, ffn_w_up.astype(BF16), ffn_conv_w, ffn_conv_b[None, :], ffn_w_down.astype(BF16),
                x1, ffn_norm_post[None, :], g2)


def kernel(x, c, ada_w, ada_b, mix_norm_pre, mix_norm_post, w_in, ret_w_o, moba_w_o, w_out,
           ffn_norm_pre, ffn_norm_post, ffn_w_up, ffn_conv_w, ffn_conv_b, ffn_w_down):
    batch, s, _ = x.shape
    depth = ada_w.shape[0]
    cos2, sin2 = _rope_tables(s)
    outs = []
    for bi in range(batch):
        xb = x[bi]
        for l in range(depth):
            xb = _block(xb, c[bi:bi + 1], ada_w[l], ada_b[l], mix_norm_pre[l], mix_norm_post[l], w_in[l],
                        ret_w_o[l], moba_w_o[l], w_out[l], ffn_norm_pre[l], ffn_norm_post[l], ffn_w_up[l],
                        ffn_conv_w[l], ffn_conv_b[l], ffn_w_down[l], cos2, sin2)
        outs.append(xb)
    return jnp.stack(outs)
```

```python
import math

import jax
import jax.numpy as jnp
from jax import lax
from jax.experimental import pallas as pl
from jax.experimental.pallas import tpu as pltpu

D_MODEL = 2048
RET_HEADS = 8
RET_DK = 128
RET_DV = 256
ROPE_BASE = 10000.0
MOBA_HEADS = 16
MOBA_HD = 128
MOBA_BLOCK = 256
MOBA_TOPK = 3
D_FF = 5632
CONV_WIDTH = 3
NORM_EPS = 1e-6
GN_EPS = 1e-5

RET_QK_W = RET_HEADS * RET_DK
RET_V_W = RET_HEADS * RET_DV
MOBA_W = MOBA_HEADS * MOBA_HD
OFF_RQ = 0
OFF_RK = OFF_RQ + RET_QK_W
OFF_RV = OFF_RK + RET_QK_W
OFF_RG = OFF_RV + RET_V_W
OFF_MQ = OFF_RG + RET_V_W
OFF_MK = OFF_MQ + MOBA_W
OFF_MV = OFF_MK + MOBA_W
OFF_GA = OFF_MV + MOBA_W
OFF_GB = OFF_GA + D_MODEL
IN_WIDTH = OFF_GB + D_MODEL

LANES = 128
SUBLANES = 8
NEG = -1e30
MOBA_QSCALE = math.log2(math.e) / math.sqrt(MOBA_HD)
ROPE_SPAN = 128
RET_GROUP = 4
MOBA_GROUP = 2
MOBA_SLOTS = 6
MOBA_PAIR = 2
FFN_CHUNK = 256
FFN_ROWS = 64
FFN_DOWN_COLS = 512
MIX_ROWS = 256
NORM_ROWS = 32
NORM_UNROLL = 4
VMEM_LIMIT = 58 * 1024 * 1024

F32 = jnp.float32
BF16 = jnp.bfloat16


def _cparams(*sem):
    return pltpu.CompilerParams(dimension_semantics=sem, vmem_limit_bytes=VMEM_LIMIT)


def _dot(a, b):
    return jnp.dot(a, b, preferred_element_type=F32)


def _dot_nt(a, b):
    return lax.dot_general(a, b, (((1,), (1,)), ((), ())), preferred_element_type=F32)


def _split_bf16(x):
    hi = x.astype(BF16)
    lo = (x - hi.astype(F32)).astype(BF16)
    return hi, lo


def _rms(x):
    return x * lax.rsqrt(jnp.mean(x * x, axis=-1, keepdims=True) + NORM_EPS)


def _silu(x):
    return x * jax.nn.sigmoid(x)


def _for_row_chunks(n_rows, load, finish):
    step = NORM_ROWS * NORM_UNROLL

    def body(r, carry):
        pieces = [pl.ds(pl.multiple_of(r * step + t * NORM_ROWS, NORM_ROWS), NORM_ROWS)
                  for t in range(NORM_UNROLL)]
        loaded = [load(rows) for rows in pieces]
        for rows, vals in zip(pieces, loaded):
            finish(rows, vals)
        return carry

    lax.fori_loop(0, n_rows // step, body, 0)


def _ada_kernel(c_ref, w_ref, b_ref, o_ref):
    c = jnp.broadcast_to(_silu(c_ref[...]), (SUBLANES, c_ref.shape[1]))
    c_hi, c_lo = _split_bf16(c)
    w_hi, w_lo = _split_bf16(w_ref[...])
    r = _dot(c_hi, w_hi) + _dot(c_lo, w_hi) + _dot(c_hi, w_lo)
    o_ref[...] = r[0:1] + b_ref[...]


def _ada(c, w, b, tn=1024):
    d, n = w.shape
    return pl.pallas_call(
        _ada_kernel,
        grid=(n // tn,),
        in_specs=[pl.BlockSpec((1, d), lambda j: (0, 0)),
                  pl.BlockSpec((d, tn), lambda j: (0, j)),
                  pl.BlockSpec((1, tn), lambda j: (0, j))],
        out_specs=pl.BlockSpec((1, tn), lambda j: (0, j)),
        out_shape=jax.ShapeDtypeStruct((1, n), F32),
        compiler_params=_cparams("arbitrary"),
        name="ada",
    )(c, w, b)


def _rope_kernel(inv_ref, cos_ref, sin_ref):
    ts = cos_ref.shape[0]
    spans = ts // ROPE_SPAN
    inv = inv_ref[...]
    lane = lax.broadcasted_iota(jnp.int32, (ROPE_SPAN, LANES), 1)
    off = lax.broadcasted_iota(jnp.int32, (ROPE_SPAN, LANES), 0).astype(F32) * inv
    cos_o, sin_o = jnp.cos(off), jnp.sin(off)
    first = pl.program_id(0) * spans
    base = ((lax.broadcasted_iota(jnp.int32, (spans, LANES), 0) + first) * ROPE_SPAN).astype(F32) * inv
    cos_b, sin_b = jnp.cos(base), jnp.sin(base)
    for b in range(spans):
        cb, sb = cos_b[b:b + 1, :], sin_b[b:b + 1, :]
        sin = sb * cos_o + cb * sin_o
        cos_ref[b * ROPE_SPAN:(b + 1) * ROPE_SPAN, :] = cb * cos_o - sb * sin_o
        sin_ref[b * ROPE_SPAN:(b + 1) * ROPE_SPAN, :] = jnp.where(lane < RET_DK // 2, -sin, sin)


def _rope_tables(s, ts=2048):
    ts = min(ts, s)
    half = RET_DK // 2
    inv = ROPE_BASE ** (-jnp.arange(half, dtype=F32) / half)
    inv2 = jnp.concatenate([inv, inv])[None, :]
    return pl.pallas_call(
        _rope_kernel,
        grid=(s // ts,),
        in_specs=[pl.BlockSpec((1, LANES), lambda i: (0, 0))],
        out_specs=[pl.BlockSpec((ts, LANES), lambda i: (i, 0))] * 2,
        out_shape=[jax.ShapeDtypeStruct((s, LANES), F32)] * 2,
        compiler_params=_cparams("arbitrary"),
        name="rope",
    )(inv2)


def _inproj_kernel(x_hbm, nw_ref, sc_ref, sh_ref, cos_ref, sin_ref, w_ref, o_ref, u_ref, x_ref, x_sem):
    i = pl.program_id(0)
    j = pl.program_id(1)
    tm, tn = o_ref.shape
    nj = IN_WIDTH // tn
    chunk = tm // nj

    def x_copy(tile, c):
        src = pl.ds(pl.multiple_of(tile * tm + c * chunk, chunk), chunk)
        dst = pl.ds(pl.multiple_of(c * chunk, chunk), chunk)
        return pltpu.make_async_copy(x_hbm.at[src, :], x_ref.at[dst, :], x_sem)

    @pl.when(j == 0)
    def _():
        @pl.when(i == 0)
        def _():
            for c in range(nj):
                x_copy(0, c).start()

        for c in range(nj):
            x_copy(i, c).wait()

        def finish(rows, x):
            y = _rms(x) * nw_ref[...]
            u_ref[rows, :] = (y * (1.0 + sc_ref[...]) + sh_ref[...]).astype(BF16)

        _for_row_chunks(tm, lambda rows: x_ref[rows, :], finish)

    @pl.when(i + 1 < pl.num_programs(0))
    def _():
        x_copy(i + 1, j).start()

    r = _dot(u_ref[...], w_ref[...])
    is_mq = (j >= OFF_MQ // tn) & (j < OFF_MK // tn)
    o_ref[...] = (r * jnp.where(is_mq, MOBA_QSCALE, 1.0).astype(F32)).astype(BF16)

    @pl.when(j * tn < 2 * RET_QK_W)
    def _():
        cos = cos_ref[...]
        sin = sin_ref[...]
        for h in range(tn // RET_DK):
            scale = jnp.where(j * tn + h * RET_DK >= RET_QK_W, RET_DK ** -0.5, 1.0).astype(F32)
            xh = r[:, h * RET_DK:(h + 1) * RET_DK]
            rot = xh * cos + pltpu.roll(xh, RET_DK // 2, axis=1) * sin
            o_ref[:, h * RET_DK:(h + 1) * RET_DK] = (rot * scale).astype(BF16)


def _inproj(x, nw, sc, sh, cos2, sin2, w, tm=1024, tn=2048):
    s, d = x.shape
    n = w.shape[1]
    tm = min(tm, s)
    nj = n // tn
    assert n == IN_WIDTH and tm % nj == 0 and (2 * RET_QK_W) % tn == 0 and OFF_MQ % tn == 0 and OFF_MK % tn == 0
    vec = pl.BlockSpec((1, d), lambda i, j: (0, 0))
    tab = pl.BlockSpec((tm, LANES), lambda i, j: (i, 0))
    return pl.pallas_call(
        _inproj_kernel,
        grid=(s // tm, nj),
        in_specs=[pl.BlockSpec(memory_space=pl.ANY), vec, vec, vec, tab, tab,
                  pl.BlockSpec((d, tn), lambda i, j: (0, j))],
        out_specs=pl.BlockSpec((tm, tn), lambda i, j: (i, j)),
        out_shape=jax.ShapeDtypeStruct((s, n), BF16),
        scratch_shapes=[pltpu.VMEM((tm, d), BF16),
                        pltpu.VMEM((tm, d), F32),
                        pltpu.SemaphoreType.DMA(())],
        compiler_params=_cparams("arbitrary", "arbitrary"),
        name="inproj",
    )(x, nw, sc, sh, cos2, sin2, w)


def _ret_kernel(lg_ref, q_ref, k_ref, v_ref, g_ref, o_ref, r_ref, dec_ref, xi_ref, zeta_ref):
    n = pl.program_id(1)
    c = q_ref.shape[0]
    dk, dv = RET_DK, RET_DV

    @pl.when(n == 0)
    def _():
        r_ref[...] = jnp.zeros_like(r_ref)
        d = lax.broadcasted_iota(jnp.int32, (c, c), 0) - lax.broadcasted_iota(jnp.int32, (c, c), 1)
        row = lax.broadcasted_iota(jnp.int32, (c, dk), 0)
        for h in range(RET_GROUP):
            lg = lg_ref[h, 0:1, 0:1]
            dec_ref[h] = jnp.where(d >= 0, jnp.exp(jnp.maximum(d, 0).astype(F32) * lg), 0.0)
            xi_ref[h] = jnp.exp((row + 1).astype(F32) * lg)
            zeta_ref[h] = jnp.exp((c - 1 - row).astype(F32) * lg)

    for h in range(RET_GROUP):
        q = q_ref[:, h * dk:(h + 1) * dk]
        k = k_ref[:, h * dk:(h + 1) * dk]
        v = v_ref[:, h * dv:(h + 1) * dv]
        s = _dot_nt(q, k) * dec_ref[h]
        inner = _dot(s.astype(BF16), v)
        r = r_ref[h]
        cross = _dot((q.astype(F32) * xi_ref[h]).astype(BF16), r.astype(BF16))
        o = inner + cross
        kz = (k.astype(F32) * zeta_ref[h]).T.astype(BF16)
        r_ref[h] = r * jnp.exp(lg_ref[h, 0:1, 0:1] * c) + _dot(kz, v)
        oc = o - jnp.mean(o, axis=-1, keepdims=True)
        on = oc * lax.rsqrt(jnp.mean(oc * oc, axis=-1, keepdims=True) + GN_EPS)
        o_ref[:, h * dv:(h + 1) * dv] = (_silu(g_ref[:, h * dv:(h + 1) * dv].astype(F32)) * on).astype(BF16)


def _retention(proj, chunk=512):
    s = proj.shape[0]
    chunk = min(chunk, s)
    log_g = jnp.log1p(-jnp.exp2(-5.0 - jnp.arange(RET_HEADS, dtype=F32)))
    lg = jnp.broadcast_to(log_g[:, None, None], (RET_HEADS, SUBLANES, LANES))
    qk_w, v_w = RET_GROUP * RET_DK, RET_GROUP * RET_DV
    assert RET_HEADS % RET_GROUP == 0
    return pl.pallas_call(
        _ret_kernel,
        grid=(RET_HEADS // RET_GROUP, s // chunk),
        in_specs=[pl.BlockSpec((RET_GROUP, SUBLANES, LANES), lambda h, n: (h, 0, 0)),
                  pl.BlockSpec((chunk, qk_w), lambda h, n: (n, OFF_RQ // qk_w + h)),
                  pl.BlockSpec((chunk, qk_w), lambda h, n: (n, OFF_RK // qk_w + h)),
                  pl.BlockSpec((chunk, v_w), lambda h, n: (n, OFF_RV // v_w + h)),
                  pl.BlockSpec((chunk, v_w), lambda h, n: (n, OFF_RG // v_w + h))],
        out_specs=pl.BlockSpec((chunk, v_w), lambda h, n: (n, h)),
        out_shape=jax.ShapeDtypeStruct((s, RET_V_W), BF16),
        scratch_shapes=[pltpu.VMEM((RET_GROUP, RET_DK, RET_DV), F32),
                        pltpu.VMEM((RET_GROUP, chunk, chunk), F32),
                        pltpu.VMEM((RET_GROUP, chunk, RET_DK), F32),
                        pltpu.VMEM((RET_GROUP, chunk, RET_DK), F32)],
        compiler_params=_cparams("arbitrary", "arbitrary"),
        name="ret",
    )(lg, proj, proj, proj, proj)


def _moba_kernel(q_ref, k_ref, v_hbm, o_ref, km_ref, oh_ref, vt_ref, qt_ref, acc_ref, m_ref, v_ref, v_sem,
                 *slots):
    hp = pl.program_id(0)
    bq = pl.program_id(1)
    tq = q_ref.shape[0]
    hd = MOBA_HD
    blk = MOBA_BLOCK
    nb = k_ref.shape[0] // blk
    gw = MOBA_GROUP * blk
    vrows = vt_ref.shape[2]

    @pl.when((hp == 0) & (bq == 0))
    def _():
        lane_i = lax.broadcasted_iota(jnp.int32, (blk, LANES), 1)

        def body(n, carry):
            oh_ref[pl.ds(pl.multiple_of(n * blk, blk), blk), :] = jnp.where(lane_i == n, 1.0, 0.0).astype(BF16)
            return carry

        lax.fori_loop(0, nb, body, 0)

    def value_copy(pair):
        cols = pl.ds(pl.multiple_of(OFF_MV + pair * (MOBA_PAIR * hd), MOBA_PAIR * hd), MOBA_PAIR * hd)
        return pltpu.make_async_copy(v_hbm.at[:, cols], v_ref, v_sem)

    @pl.when((hp == 0) & (bq == 0))
    def _():
        value_copy(0).start()

    @pl.when(bq == 0)
    def _():
        value_copy(hp).wait()
        km_ref[...] = jnp.zeros_like(km_ref)

        def body(g, carry):
            for t in range(MOBA_GROUP):
                n = g * MOBA_GROUP + t
                rows = pl.ds(pl.multiple_of(n * blk, blk), blk)
                km_ref[pl.ds(n, 1), :] = jnp.mean(k_ref[rows, :].astype(F32), axis=0, keepdims=True)
                for h in range(MOBA_PAIR):
                    vt_ref[h, g, 0:hd, t * blk:(t + 1) * blk] = (
                        v_ref[rows, h * hd:(h + 1) * hd].astype(F32).T.astype(BF16))
            for h in range(MOBA_PAIR):
                vt_ref[h, g, hd:vrows, :] = jnp.ones((vrows - hd, gw), BF16)
            return carry

        lax.fori_loop(0, nb // MOBA_GROUP, body, 0)

        @pl.when(hp + 1 < pl.num_programs(0))
        def _():
            value_copy(hp + 1).start()

    gr = km_ref.shape[0]
    row_i = lax.broadcasted_iota(jnp.int32, (gr, tq), 0)
    row_f = row_i.astype(F32)
    own = bq * (tq // blk) + lax.broadcasted_iota(jnp.int32, (gr, tq), 1) // blk
    for h in range(MOBA_PAIR):
        qt = q_ref[:, h * hd:(h + 1) * hd].astype(F32).T.astype(BF16)
        km_hi, km_lo = _split_bf16(km_ref[:, h * hd:(h + 1) * hd])
        g = jnp.where(row_i < own, _dot(km_hi, qt) + _dot(km_lo, qt), NEG)
        bias = jnp.full((gr, tq), NEG, F32)
        for _ in range(MOBA_TOPK):
            mx = jnp.max(g, axis=0, keepdims=True)
            idx = jnp.min(jnp.where(g == mx, row_f, float(gr)), axis=0, keepdims=True)
            hit = row_f == idx
            bias = jnp.where(hit, jnp.where(mx > 0.5 * NEG, 0.0, NEG), bias)
            g = jnp.where(hit, 2.0 * NEG, g)
        bias = jnp.where(row_i == own, 0.0, bias)
        qt_ref[h] = jnp.concatenate(
            [qt, bias.astype(BF16), jnp.full((LANES - gr, tq), NEG, BF16)], axis=0)
        m_ref[h] = jnp.full((1, tq), NEG, F32)
        acc_ref[h] = jnp.zeros((vrows, tq), F32)

    def scores(g, slot_ref):
        rows = pl.ds(pl.multiple_of(g * gw, gw), gw)
        oh = oh_ref[rows, :]
        for h in range(MOBA_PAIR):
            k_ext = jnp.concatenate([k_ref[rows, h * hd:(h + 1) * hd], oh], axis=1)
            slot_ref[h] = _dot(k_ext, qt_ref[h])

    def accumulate(g, slot_ref, causal):
        if causal:
            own_rows = pl.ds(pl.multiple_of((bq * tq) % gw, tq), tq)
            keep = lax.broadcasted_iota(jnp.int32, (tq, tq), 0) <= lax.broadcasted_iota(jnp.int32, (tq, tq), 1)
            for h in range(MOBA_PAIR):
                slot_ref[h, own_rows, :] = jnp.where(keep, slot_ref[h, own_rows, :], NEG)
        for h in range(MOBA_PAIR):
            s = slot_ref[h]
            m_old = m_ref[h]
            m_new = jnp.maximum(m_old, jnp.max(s, axis=0, keepdims=True))
            p = jnp.exp2(s - m_new).astype(BF16)
            acc_ref[h] = jnp.exp2(m_old - m_new) * acc_ref[h] + _dot(vt_ref[h, g], p)
            m_ref[h] = m_new

    nslot = len(slots)
    full_groups = (bq * tq) // gw
    scores(0, slots[0])

    def body(i, carry):
        g = nslot * i
        for u in range(nslot):
            scores(g + u + 1, slots[(u + 1) % nslot])
            accumulate(g + u, slots[u], False)
        return carry

    lax.fori_loop(0, full_groups // nslot, body, 0)
    base = (full_groups // nslot) * nslot
    for r in range(nslot):
        @pl.when(full_groups - base == r)
        def _(r=r):
            for u in range(r):
                scores(base + u + 1, slots[u + 1])
                accumulate(base + u, slots[u], False)
            accumulate(base + r, slots[r], True)

    for h in range(MOBA_PAIR):
        acc = acc_ref[h]
        o_ref[:, h * hd:(h + 1) * hd] = (acc[0:hd, :] / acc[hd:hd + 1, :]).T.astype(BF16)


def _moba(proj, tq=2 * MOBA_BLOCK):
    s = proj.shape[0]
    nb = s // MOBA_BLOCK
    w = MOBA_PAIR * MOBA_HD
    gw = MOBA_GROUP * MOBA_BLOCK
    bf16_rows = 2 * SUBLANES
    vrows = MOBA_HD + bf16_rows
    gate_rows = -(-nb // bf16_rows) * bf16_rows
    assert s % gw == 0 and gate_rows <= LANES and MOBA_HEADS % MOBA_PAIR == 0
    assert tq % MOBA_BLOCK == 0 and gw % tq == 0
    return pl.pallas_call(
        _moba_kernel,
        grid=(MOBA_HEADS // MOBA_PAIR, s // tq),
        in_specs=[pl.BlockSpec((tq, w), lambda h, b: (b, OFF_MQ // w + h)),
                  pl.BlockSpec((s, w), lambda h, b: (0, OFF_MK // w + h)),
                  pl.BlockSpec(memory_space=pl.ANY)],
        out_specs=pl.BlockSpec((tq, w), lambda h, b: (b, h)),
        out_shape=jax.ShapeDtypeStruct((s, MOBA_W), BF16),
        scratch_shapes=[pltpu.VMEM((gate_rows, w), F32),
                        pltpu.VMEM((s, LANES), BF16),
                        pltpu.VMEM((MOBA_PAIR, s // gw, vrows, gw), BF16),
                        pltpu.VMEM((MOBA_PAIR, 2 * MOBA_HD, tq), BF16),
                        pltpu.VMEM((MOBA_PAIR, vrows, tq), F32),
                        pltpu.VMEM((MOBA_PAIR, 1, tq), F32),
                        pltpu.VMEM((s, w), BF16),
                        pltpu.SemaphoreType.DMA(())]
                       + [pltpu.VMEM((MOBA_PAIR, gw, tq), F32)] * MOBA_SLOTS,
        compiler_params=_cparams("arbitrary", "arbitrary"),
        name="moba",
    )(proj, proj, proj)


def _merge_kernel(or_ref, om_ref, ga_ref, gb_ref, wr_ref, wm_ref, o_ref):
    yr = _dot(or_ref[...], wr_ref[...])
    ym = _dot(om_ref[...], wm_ref[...])
    ga = jax.nn.sigmoid(ga_ref[...].astype(F32))
    gb = jax.nn.sigmoid(gb_ref[...].astype(F32))
    o_ref[...] = (ga * yr + gb * ym).astype(BF16)


def _merge(o_ret, o_moba, proj, w_r, w_m, tm=1024, tn=1024):
    s = o_ret.shape[0]
    tm = min(tm, s)
    d = D_MODEL
    return pl.pallas_call(
        _merge_kernel,
        grid=(s // tm, d // tn),
        in_specs=[pl.BlockSpec((tm, RET_V_W), lambda i, j: (i, 0)),
                  pl.BlockSpec((tm, MOBA_W), lambda i, j: (i, 0)),
                  pl.BlockSpec((tm, tn), lambda i, j: (i, OFF_GA // tn + j)),
                  pl.BlockSpec((tm, tn), lambda i, j: (i, OFF_GB // tn + j)),
                  pl.BlockSpec((RET_V_W, tn), lambda i, j: (0, j)),
                  pl.BlockSpec((MOBA_W, tn), lambda i, j: (0, j))],
        out_specs=pl.BlockSpec((tm, tn), lambda i, j: (i, j)),
        out_shape=jax.ShapeDtypeStruct((s, d), BF16),
        compiler_params=_cparams("arbitrary", "arbitrary"),
        name="merge",
    )(o_ret, o_moba, proj, proj, w_r, w_m)


def _mixout_kernel(m_ref, wo_ref, x_ref, npost_ref, g1_ref, npre_ref, sc_ref, sh_ref, x1_ref, u2_ref):
    for r0 in range(0, m_ref.shape[0], MIX_ROWS):
        rows = slice(r0, r0 + MIX_ROWS)
        y = _dot(m_ref[rows, :], wo_ref[...])
        x1 = x_ref[rows, :] + (1.0 + g1_ref[...]) * (_rms(y) * npost_ref[...])
        x1_ref[rows, :] = x1
        u2_ref[rows, :] = ((_rms(x1) * npre_ref[...]) * (1.0 + sc_ref[...]) + sh_ref[...]).astype(BF16)


def _mixout(merged, w_o, x, npost, g1, npre, sc2, sh2, tm=512):
    s, d = x.shape
    tm = min(tm, s)
    vec = pl.BlockSpec((1, d), lambda i: (0, 0))
    rows = pl.BlockSpec((tm, d), lambda i: (i, 0))
    return pl.pallas_call(
        _mixout_kernel,
        grid=(s // tm,),
        in_specs=[rows, pl.BlockSpec((d, d), lambda i: (0, 0), pipeline_mode=pl.Buffered(1)), rows,
                  vec, vec, vec, vec, vec],
        out_specs=[rows, rows],
        out_shape=[jax.ShapeDtypeStruct((s, d), F32), jax.ShapeDtypeStruct((s, d), BF16)],
        compiler_params=_cparams("arbitrary"),
        name="mixout",
    )(merged, w_o, x, npost, g1, npre, sc2, sh2)


def _ffn_kernel(u_ref, wa_ref, wb_ref, cwa_ref, cwb_ref, cba_ref, cbb_ref, wd_ref, x1_hbm, npost_ref, g2_ref,
                o_ref, tail_ref, x1_ref, x1_sem, *bufs):
    i = pl.program_id(0)
    j = pl.program_id(1)
    tm = u_ref.shape[0]
    tn = wa_ref.shape[1]
    halo = SUBLANES

    def residual_copy():
        return pltpu.make_async_copy(x1_hbm.at[pl.ds(pl.multiple_of(i * tm, tm), tm), :], x1_ref, x1_sem)

    @pl.when((i == 0) & (j == 0))
    def _():
        tail_ref[...] = jnp.zeros_like(tail_ref)

    @pl.when(j == 0)
    def _():
        residual_copy().start()
        o_ref[...] = jnp.zeros_like(o_ref)

    nchunk = tn // FFN_CHUNK
    ha_refs, hb_refs, act_refs = bufs[:nchunk], bufs[nchunk:2 * nchunk], bufs[2 * nchunk:]

    def cols(c):
        return slice(c * FFN_CHUNK, (c + 1) * FFN_CHUNK)

    def up(c, half, hbuf, w_ref, after=None):
        w = w_ref[:, cols(c)]
        if after is not None:
            w = jnp.concatenate([w[0:2 * SUBLANES, :] + after, w[2 * SUBLANES:, :]], axis=0)
        hbuf[0:halo, :] = tail_ref[j, half, :, cols(c)]
        hbuf[halo:halo + tm, :] = _dot(u_ref[...], w)
        tail_ref[j, half, :, cols(c)] = hbuf[tm:tm + halo, :]

    def conv(hbuf, cw, cb, r0):
        out = cb
        for t in range(CONV_WIDTH):
            out = out + hbuf[halo + r0 - t:halo + r0 - t + FFN_ROWS, :] * cw[CONV_WIDTH - 1 - t:CONV_WIDTH - t, :]
        return out

    def gate(c):
        cs = cols(c)
        cwa, cwb, cba, cbb = cwa_ref[:, cs], cwb_ref[:, cs], cba_ref[:, cs], cbb_ref[:, cs]
        zero = jnp.zeros((FFN_ROWS // 2, FFN_CHUNK), jnp.uint32)
        for r0 in range(0, tm, FFN_ROWS):
            a = conv(ha_refs[c], cwa, cba, r0)
            bgate = conv(hb_refs[c], cwb, cbb, r0)
            act = (_silu(a) * bgate).astype(BF16)
            act_refs[c][r0:r0 + FFN_ROWS, :] = act
            bits = pltpu.bitcast(act, jnp.uint32)
            zero = zero | ((bits >> 16) >> 16)
        zero = zero[0:2 * SUBLANES, :] | zero[2 * SUBLANES:4 * SUBLANES, :]
        return zero.astype(F32).astype(BF16)

    done = None
    for c in range(nchunk):
        up(c, 0, ha_refs[c], wa_ref)
        up(c, 1, hb_refs[c], wb_ref, after=done)
        done = gate(c)
    for c in range(nchunk):
        for n0 in range(0, o_ref.shape[1], FFN_DOWN_COLS):
            ns = slice(n0, n0 + FFN_DOWN_COLS)
            o_ref[:, ns] += _dot(act_refs[c][...], wd_ref[cols(c), ns])

    @pl.when(j == pl.num_programs(1) - 1)
    def _():
        residual_copy().wait()

        def finish(rows, y):
            o_ref[rows, :] = x1_ref[rows, :] + (1.0 + g2_ref[...]) * (_rms(y) * npost_ref[...])

        _for_row_chunks(tm, lambda rows: o_ref[rows, :], finish)


def _ffn(u2, w_up, conv_w, conv_b, w_down, x1, npost, g2, tm=1024, tn=512):
    s, d = x1.shape
    tm = min(tm, s)
    nj = D_FF // tn
    nchunk = tn // FFN_CHUNK
    assert D_FF % tn == 0 and tn % FFN_CHUNK == 0
    vec = pl.BlockSpec((1, d), lambda i, j: (0, 0))
    rows = pl.BlockSpec((tm, d), lambda i, j: (i, 0))
    resid = pl.BlockSpec(memory_space=pl.ANY)
    return pl.pallas_call(
        _ffn_kernel,
        grid=(s // tm, nj),
        in_specs=[rows,
                  pl.BlockSpec((d, tn), lambda i, j: (0, j)),
                  pl.BlockSpec((d, tn), lambda i, j: (0, j + nj)),
                  pl.BlockSpec((CONV_WIDTH, tn), lambda i, j: (0, j)),
                  pl.BlockSpec((CONV_WIDTH, tn), lambda i, j: (0, j + nj)),
                  pl.BlockSpec((1, tn), lambda i, j: (0, j)),
                  pl.BlockSpec((1, tn), lambda i, j: (0, j + nj)),
                  pl.BlockSpec((tn, d), lambda i, j: (j, 0)),
                  resid, vec, vec],
        out_specs=rows,
        out_shape=jax.ShapeDtypeStruct((s, d), F32),
        scratch_shapes=[pltpu.VMEM((nj, 2, SUBLANES, tn), F32),
                        pltpu.VMEM((tm, d), F32),
                        pltpu.SemaphoreType.DMA(())]
                       + [pltpu.VMEM((tm + SUBLANES, FFN_CHUNK), F32)] * (2 * nchunk)
                       + [pltpu.VMEM((tm, FFN_CHUNK), BF16)] * nchunk,
        compiler_params=_cparams("arbitrary", "arbitrary"),
        name="ffn",
    )(u2, w_up, w_up, conv_w, conv_w, conv_b, conv_b, w_down, x1, npost, g2)


def _block(x, c, ada_w, ada_b, mix_norm_pre, mix_norm_post, w_in, ret_w_o, moba_w_o, w_out,
           ffn_norm_pre, ffn_norm_post, ffn_w_up, ffn_conv_w, ffn_conv_b, ffn_w_down, cos2, sin2):
    d = x.shape[1]
    mod = _ada(c, ada_w, ada_b[None, :])
    sh1, sc1, g1, sh2, sc2, g2 = [mod[:, t * d:(t + 1) * d] for t in range(6)]
    proj = _inproj(x, mix_norm_pre[None, :], sc1, sh1, cos2, sin2, w_in.astype(BF16))
    o_ret = _retention(proj)
    o_moba = _moba(proj)
    merged = _merge(o_ret, o_moba, proj, ret_w_o.astype(BF16), moba_w_o.astype(BF16))
    x1, u2 = _mixout(merged, w_out.astype(BF16), x, mix_norm_post[None, :], g1,
                     ffn_norm_pre[None, :], sc2, sh2)
    return _ffn(u2, ffn_w_up.astype(BF16), ffn_conv_w, ffn_conv_b[None, :], ffn_w_down.astype(BF16),
                x1, ffn_norm_post[None, :], g2)


def kernel(x, c, ada_w, ada_b, mix_norm_pre, mix_norm_post, w_in, ret_w_o, moba_w_o, w_out,
           ffn_norm_pre, ffn_norm_post, ffn_w_up, ffn_conv_w, ffn_conv_b, ffn_w_down):
    batch, s, _ = x.shape
    depth = ada_w.shape[0]
    cos2, sin2 = _rope_tables(s)
    outs = []
    for bi in range(batch):
        xb = x[bi]
        for l in range(depth):
            xb = _block(xb, c[bi:bi + 1], ada_w[l], ada_b[l], mix_norm_pre[l], mix_norm_post[l], w_in[l],
                        ret_w_o[l], moba_w_o[l], w_out[l], ffn_norm_pre[l], ffn_norm_post[l], ffn_w_up[l],
                        ffn_conv_w[l], ffn_conv_b[l], ffn_w_down[l], cos2, sin2)
        outs.append(xb)
    return jnp.stack(outs)
```

```python
import math

import jax
import jax.numpy as jnp
from jax import lax
from jax.experimental import pallas as pl
from jax.experimental.pallas import tpu as pltpu

D_MODEL = 2048
RET_HEADS = 8
RET_DK = 128
RET_DV = 256
ROPE_BASE = 10000.0
MOBA_HEADS = 16
MOBA_HD = 128
MOBA_BLOCK = 256
MOBA_TOPK = 3
D_FF = 5632
CONV_WIDTH = 3
NORM_EPS = 1e-6
GN_EPS = 1e-5

RET_QK_W = RET_HEADS * RET_DK
RET_V_W = RET_HEADS * RET_DV
MOBA_W = MOBA_HEADS * MOBA_HD
OFF_RQ = 0
OFF_RK = OFF_RQ + RET_QK_W
OFF_RV = OFF_RK + RET_QK_W
OFF_RG = OFF_RV + RET_V_W
OFF_MQ = OFF_RG + RET_V_W
OFF_MK = OFF_MQ + MOBA_W
OFF_MV = OFF_MK + MOBA_W
OFF_GA = OFF_MV + MOBA_W
OFF_GB = OFF_GA + D_MODEL
IN_WIDTH = OFF_GB + D_MODEL

LANES = 128
SUBLANES = 8
NEG = -1e30
MOBA_QSCALE = math.log2(math.e) / math.sqrt(MOBA_HD)
ROPE_SPAN = 128
RET_GROUP = 4
MOBA_GROUP = 2
MOBA_SLOTS = 6
MOBA_PAIR = 2
FFN_CHUNK = 256
FFN_ROWS = 64
FFN_DOWN_COLS = 512
MIX_ROWS = 256
NORM_ROWS = 32
NORM_UNROLL = 4
VMEM_LIMIT = 58 * 1024 * 1024

F32 = jnp.float32
BF16 = jnp.bfloat16


def _cparams(*sem):
    return pltpu.CompilerParams(dimension_semantics=sem, vmem_limit_bytes=VMEM_LIMIT)


def _dot(a, b):
    return jnp.dot(a, b, preferred_element_type=F32)


def _dot_nt(a, b):
    return lax.dot_general(a, b, (((1,), (1,)), ((), ())), preferred_element_type=F32)


def _split_bf16(x):
    hi = x.astype(BF16)
    lo = (x - hi.astype(F32)).astype(BF16)
    return hi, lo


def _rms(x):
    return x * lax.rsqrt(jnp.mean(x * x, axis=-1, keepdims=True) + NORM_EPS)


def _silu(x):
    return x * jax.nn.sigmoid(x)


def _for_row_chunks(n_rows, load, finish):
    step = NORM_ROWS * NORM_UNROLL

    def body(r, carry):
        pieces = [pl.ds(pl.multiple_of(r * step + t * NORM_ROWS, NORM_ROWS), NORM_ROWS)
                  for t in range(NORM_UNROLL)]
        loaded = [load(rows) for rows in pieces]
        for rows, vals in zip(pieces, loaded):
            finish(rows, vals)
        return carry

    lax.fori_loop(0, n_rows // step, body, 0)


def _ada_kernel(c_ref, w_ref, b_ref, o_ref):
    c = jnp.broadcast_to(_silu(c_ref[...]), (SUBLANES, c_ref.shape[1]))
    c_hi, c_lo = _split_bf16(c)
    w_hi, w_lo = _split_bf16(w_ref[...])
    r = _dot(c_hi, w_hi) + _dot(c_lo, w_hi) + _dot(c_hi, w_lo)
    o_ref[...] = r[0:1] + b_ref[...]


def _ada(c, w, b, tn=1024):
    d, n = w.shape
    return pl.pallas_call(
        _ada_kernel,
        grid=(n // tn,),
        in_specs=[pl.BlockSpec((1, d), lambda j: (0, 0)),
                  pl.BlockSpec((d, tn), lambda j: (0, j)),
                  pl.BlockSpec((1, tn), lambda j: (0, j))],
        out_specs=pl.BlockSpec((1, tn), lambda j: (0, j)),
        out_shape=jax.ShapeDtypeStruct((1, n), F32),
        compiler_params=_cparams("arbitrary"),
        name="ada",
    )(c, w, b)


def _rope_kernel(inv_ref, cos_ref, sin_ref):
    ts = cos_ref.shape[0]
    spans = ts // ROPE_SPAN
    inv = inv_ref[...]
    lane = lax.broadcasted_iota(jnp.int32, (ROPE_SPAN, LANES), 1)
    off = lax.broadcasted_iota(jnp.int32, (ROPE_SPAN, LANES), 0).astype(F32) * inv
    cos_o, sin_o = jnp.cos(off), jnp.sin(off)
    first = pl.program_id(0) * spans
    base = ((lax.broadcasted_iota(jnp.int32, (spans, LANES), 0) + first) * ROPE_SPAN).astype(F32) * inv
    cos_b, sin_b = jnp.cos(base), jnp.sin(base)
    for b in range(spans):
        cb, sb = cos_b[b:b + 1, :], sin_b[b:b + 1, :]
        sin = sb * cos_o + cb * sin_o
        cos_ref[b * ROPE_SPAN:(b + 1) * ROPE_SPAN, :] = cb * cos_o - sb * sin_o
        sin_ref[b * ROPE_SPAN:(b + 1) * ROPE_SPAN, :] = jnp.where(lane < RET_DK // 2, -sin, sin)


def _rope_tables(s, ts=2048):
    ts = min(ts, s)
    half = RET_DK // 2
    inv = ROPE_BASE ** (-jnp.arange(half, dtype=F32) / half)
    inv2 = jnp.concatenate([inv, inv])[None, :]
    return pl.pallas_call(
        _rope_kernel,
        grid=(s // ts,),
        in_specs=[pl.BlockSpec((1, LANES), lambda i: (0, 0))],
        out_specs=[pl.BlockSpec((ts, LANES), lambda i: (i, 0))] * 2,
        out_shape=[jax.ShapeDtypeStruct((s, LANES), F32)] * 2,
        compiler_params=_cparams("arbitrary"),
        name="rope",
    )(inv2)


def _inproj_kernel(x_hbm, nw_ref, sc_ref, sh_ref, cos_ref, sin_ref, w_ref, o_ref, u_ref, x_ref, x_sem):
    i = pl.program_id(0)
    j = pl.program_id(1)
    tm, tn = o_ref.shape
    nj = IN_WIDTH // tn
    chunk = tm // nj

    def x_copy(tile, c):
        src = pl.ds(pl.multiple_of(tile * tm + c * chunk, chunk), chunk)
        dst = pl.ds(pl.multiple_of(c * chunk, chunk), chunk)
        return pltpu.make_async_copy(x_hbm.at[src, :], x_ref.at[dst, :], x_sem)

    @pl.when(j == 0)
    def _():
        @pl.when(i == 0)
        def _():
            for c in range(nj):
                x_copy(0, c).start()

        for c in range(nj):
            x_copy(i, c).wait()

        def finish(rows, x):
            y = _rms(x) * nw_ref[...]
            u_ref[rows, :] = (y * (1.0 + sc_ref[...]) + sh_ref[...]).astype(BF16)

        _for_row_chunks(tm, lambda rows: x_ref[rows, :], finish)

    @pl.when(i + 1 < pl.num_programs(0))
    def _():
        x_copy(i + 1, j).start()

    r = _dot(u_ref[...], w_ref[...])
    is_mq = (j >= OFF_MQ // tn) & (j < OFF_MK // tn)
    o_ref[...] = (r * jnp.where(is_mq, MOBA_QSCALE, 1.0).astype(F32)).astype(BF16)

    @pl.when(j * tn < 2 * RET_QK_W)
    def _():
        cos = cos_ref[...]
        sin = sin_ref[...]
        for h in range(tn // RET_DK):
            scale = jnp.where(j * tn + h * RET_DK >= RET_QK_W, RET_DK ** -0.5, 1.0).astype(F32)
            xh = r[:, h * RET_DK:(h + 1) * RET_DK]
            rot = xh * cos + pltpu.roll(xh, RET_DK // 2, axis=1) * sin
            o_ref[:, h * RET_DK:(h + 1) * RET_DK] = (rot * scale).astype(BF16)


def _inproj(x, nw, sc, sh, cos2, sin2, w, tm=1024, tn=2048):
    s, d = x.shape
    n = w.shape[1]
    tm = min(tm, s)
    nj = n // tn
    assert n == IN_WIDTH and tm % nj == 0 and (2 * RET_QK_W) % tn == 0 and OFF_MQ % tn == 0 and OFF_MK % tn == 0
    vec = pl.BlockSpec((1, d), lambda i, j: (0, 0))
    tab = pl.BlockSpec((tm, LANES), lambda i, j: (i, 0))
    return pl.pallas_call(
        _inproj_kernel,
        grid=(s // tm, nj),
        in_specs=[pl.BlockSpec(memory_space=pl.ANY), vec, vec, vec, tab, tab,
                  pl.BlockSpec((d, tn), lambda i, j: (0, j))],
        out_specs=pl.BlockSpec((tm, tn), lambda i, j: (i, j)),
        out_shape=jax.ShapeDtypeStruct((s, n), BF16),
        scratch_shapes=[pltpu.VMEM((tm, d), BF16),
                        pltpu.VMEM((tm, d), F32),
                        pltpu.SemaphoreType.DMA(())],
        compiler_params=_cparams("arbitrary", "arbitrary"),
        name="inproj",
    )(x, nw, sc, sh, cos2, sin2, w)


def _ret_kernel(lg_ref, q_ref, k_ref, v_ref, g_ref, o_ref, r_ref, dec_ref, xi_ref, zeta_ref):
    n = pl.program_id(1)
    c = q_ref.shape[0]
    dk, dv = RET_DK, RET_DV

    @pl.when(n == 0)
    def _():
        r_ref[...] = jnp.zeros_like(r_ref)
        d = lax.broadcasted_iota(jnp.int32, (c, c), 0) - lax.broadcasted_iota(jnp.int32, (c, c), 1)
        row = lax.broadcasted_iota(jnp.int32, (c, dk), 0)
        for h in range(RET_GROUP):
            lg = lg_ref[h, 0:1, 0:1]
            dec_ref[h] = jnp.where(d >= 0, jnp.exp(jnp.maximum(d, 0).astype(F32) * lg), 0.0)
            xi_ref[h] = jnp.exp((row + 1).astype(F32) * lg)
            zeta_ref[h] = jnp.exp((c - 1 - row).astype(F32) * lg)

    for h in range(RET_GROUP):
        q = q_ref[:, h * dk:(h + 1) * dk]
        k = k_ref[:, h * dk:(h + 1) * dk]
        v = v_ref[:, h * dv:(h + 1) * dv]
        s = _dot_nt(q, k) * dec_ref[h]
        inner = _dot(s.astype(BF16), v)
        r = r_ref[h]
        cross = _dot((q.astype(F32) * xi_ref[h]).astype(BF16), r.astype(BF16))
        o = inner + cross
        kz = (k.astype(F32) * zeta_ref[h]).T.astype(BF16)
        r_ref[h] = r * jnp.exp(lg_ref[h, 0:1, 0:1] * c) + _dot(kz, v)
        oc = o - jnp.mean(o, axis=-1, keepdims=True)
        on = oc * lax.rsqrt(jnp.mean(oc * oc, axis=-1, keepdims=True) + GN_EPS)
        o_ref[:, h * dv:(h + 1) * dv] = (_silu(g_ref[:, h * dv:(h + 1) * dv].astype(F32)) * on).astype(BF16)


def _retention(proj, chunk=512):
    s = proj.shape[0]
    chunk = min(chunk, s)
    log_g = jnp.log1p(-jnp.exp2(-5.0 - jnp.arange(RET_HEADS, dtype=F32)))
    lg = jnp.broadcast_to(log_g[:, None, None], (RET_HEADS, SUBLANES, LANES))
    qk_w, v_w = RET_GROUP * RET_DK, RET_GROUP * RET_DV
    assert RET_HEADS % RET_GROUP == 0
    return pl.pallas_call(
        _ret_kernel,
        grid=(RET_HEADS // RET_GROUP, s // chunk),
        in_specs=[pl.BlockSpec((RET_GROUP, SUBLANES, LANES), lambda h, n: (h, 0, 0)),
                  pl.BlockSpec((chunk, qk_w), lambda h, n: (n, OFF_RQ // qk_w + h)),
                  pl.BlockSpec((chunk, qk_w), lambda h, n: (n, OFF_RK // qk_w + h)),
                  pl.BlockSpec((chunk, v_w), lambda h, n: (n, OFF_RV // v_w + h)),
                  pl.BlockSpec((chunk, v_w), lambda h, n: (n, OFF_RG // v_w + h))],
        out_specs=pl.BlockSpec((chunk, v_w), lambda h, n: (n, h)),
        out_shape=jax.ShapeDtypeStruct((s, RET_V_W), BF16),
        scratch_shapes=[pltpu.VMEM((RET_GROUP, RET_DK, RET_DV), F32),
                        pltpu.VMEM((RET_GROUP, chunk, chunk), F32),
                        pltpu.VMEM((RET_GROUP, chunk, RET_DK), F32),
                        pltpu.VMEM((RET_GROUP, chunk, RET_DK), F32)],
        compiler_params=_cparams("arbitrary", "arbitrary"),
        name="ret",
    )(lg, proj, proj, proj, proj)


def _moba_kernel(q_ref, k_ref, v_hbm, o_ref, km_ref, oh_ref, vt_ref, qt_ref, acc_ref, m_ref, v_ref, v_sem,
                 *slots):
    hp = pl.program_id(0)
    bq = pl.program_id(1)
    tq = q_ref.shape[0]
    hd = MOBA_HD
    blk = MOBA_BLOCK
    nb = k_ref.shape[0] // blk
    gw = MOBA_GROUP * blk
    vrows = vt_ref.shape[2]

    @pl.when((hp == 0) & (bq == 0))
    def _():
        lane_i = lax.broadcasted_iota(jnp.int32, (blk, LANES), 1)

        def body(n, carry):
            oh_ref[pl.ds(pl.multiple_of(n * blk, blk), blk), :] = jnp.where(lane_i == n, 1.0, 0.0).astype(BF16)
            return carry

        lax.fori_loop(0, nb, body, 0)

    def value_copy(pair):
        cols = pl.ds(pl.multiple_of(OFF_MV + pair * (MOBA_PAIR * hd), MOBA_PAIR * hd), MOBA_PAIR * hd)
        return pltpu.make_async_copy(v_hbm.at[:, cols], v_ref, v_sem)

    @pl.when((hp == 0) & (bq == 0))
    def _():
        value_copy(0).start()

    @pl.when(bq == 0)
    def _():
        value_copy(hp).wait()
        km_ref[...] = jnp.zeros_like(km_ref)

        def body(g, carry):
            for t in range(MOBA_GROUP):
                n = g * MOBA_GROUP + t
                rows = pl.ds(pl.multiple_of(n * blk, blk), blk)
                km_ref[pl.ds(n, 1), :] = jnp.mean(k_ref[rows, :].astype(F32), axis=0, keepdims=True)
                for h in range(MOBA_PAIR):
                    vt_ref[h, g, 0:hd, t * blk:(t + 1) * blk] = (
                        v_ref[rows, h * hd:(h + 1) * hd].astype(F32).T.astype(BF16))
            for h in range(MOBA_PAIR):
                vt_ref[h, g, hd:vrows, :] = jnp.ones((vrows - hd, gw), BF16)
            return carry

        lax.fori_loop(0, nb // MOBA_GROUP, body, 0)

        @pl.when(hp + 1 < pl.num_programs(0))
        def _():
            value_copy(hp + 1).start()

    gr = km_ref.shape[0]
    row_i = lax.broadcasted_iota(jnp.int32, (gr, tq), 0)
    row_f = row_i.astype(F32)
    own = bq * (tq // blk) + lax.broadcasted_iota(jnp.int32, (gr, tq), 1) // blk
    for h in range(MOBA_PAIR):
        qt = q_ref[:, h * hd:(h + 1) * hd].astype(F32).T.astype(BF16)
        km_hi, km_lo = _split_bf16(km_ref[:, h * hd:(h + 1) * hd])
        g = jnp.where(row_i < own, _dot(km_hi, qt) + _dot(km_lo, qt), NEG)
        bias = jnp.full((gr, tq), NEG, F32)
        for _ in range(MOBA_TOPK):
            mx = jnp.max(g, axis=0, keepdims=True)
            idx = jnp.min(jnp.where(g == mx, row_f, float(gr)), axis=0, keepdims=True)
            hit = row_f == idx
            bias = jnp.where(hit, jnp.where(mx > 0.5 * NEG, 0.0, NEG), bias)
            g = jnp.where(hit, 2.0 * NEG, g)
        bias = jnp.where(row_i == own, 0.0, bias)
        qt_ref[h] = jnp.concatenate(
            [qt, bias.astype(BF16), jnp.full((LANES - gr, tq), NEG, BF16)], axis=0)
        m_ref[h] = jnp.full((1, tq), NEG, F32)
        acc_ref[h] = jnp.zeros((vrows, tq), F32)

    def scores(g, slot_ref):
        rows = pl.ds(pl.multiple_of(g * gw, gw), gw)
        oh = oh_ref[rows, :]
        for h in range(MOBA_PAIR):
            k_ext = jnp.concatenate([k_ref[rows, h * hd:(h + 1) * hd], oh], axis=1)
            slot_ref[h] = _dot(k_ext, qt_ref[h])

    def accumulate(g, slot_ref, causal):
        if causal:
            own_rows = pl.ds(pl.multiple_of((bq * tq) % gw, tq), tq)
            keep = lax.broadcasted_iota(jnp.int32, (tq, tq), 0) <= lax.broadcasted_iota(jnp.int32, (tq, tq), 1)
            for h in range(MOBA_PAIR):
                slot_ref[h, own_rows, :] = jnp.where(keep, slot_ref[h, own_rows, :], NEG)
        for h in range(MOBA_PAIR):
            s = slot_ref[h]
            m_old = m_ref[h]
            m_new = jnp.maximum(m_old, jnp.max(s, axis=0, keepdims=True))
            p = jnp.exp2(s - m_new).astype(BF16)
            acc_ref[h] = jnp.exp2(m_old - m_new) * acc_ref[h] + _dot(vt_ref[h, g], p)
            m_ref[h] = m_new

    nslot = len(slots)
    full_groups = (bq * tq) // gw
    scores(0, slots[0])

    def body(i, carry):
        g = nslot * i
        for u in range(nslot):
            scores(g + u + 1, slots[(u + 1) % nslot])
            accumulate(g + u, slots[u], False)
        return carry

    lax.fori_loop(0, full_groups // nslot, body, 0)
    base = (full_groups // nslot) * nslot
    for r in range(nslot):
        @pl.when(full_groups - base == r)
        def _(r=r):
            for u in range(r):
                scores(base + u + 1, slots[u + 1])
                accumulate(base + u, slots[u], False)
            accumulate(base + r, slots[r], True)

    for h in range(MOBA_PAIR):
        acc = acc_ref[h]
        o_ref[:, h * hd:(h + 1) * hd] = (acc[0:hd, :] / acc[hd:hd + 1, :]).T.astype(BF16)


def _moba(proj, tq=2 * MOBA_BLOCK):
    s = proj.shape[0]
    nb = s // MOBA_BLOCK
    w = MOBA_PAIR * MOBA_HD
    gw = MOBA_GROUP * MOBA_BLOCK
    bf16_rows = 2 * SUBLANES
    vrows = MOBA_HD + bf16_rows
    gate_rows = -(-nb // bf16_rows) * bf16_rows
    assert s % gw == 0 and gate_rows <= LANES and MOBA_HEADS % MOBA_PAIR == 0
    assert tq % MOBA_BLOCK == 0 and gw % tq == 0
    return pl.pallas_call(
        _moba_kernel,
        grid=(MOBA_HEADS // MOBA_PAIR, s // tq),
        in_specs=[pl.BlockSpec((tq, w), lambda h, b: (b, OFF_MQ // w + h)),
                  pl.BlockSpec((s, w), lambda h, b: (0, OFF_MK // w + h)),
                  pl.BlockSpec(memory_space=pl.ANY)],
        out_specs=pl.BlockSpec((tq, w), lambda h, b: (b, h)),
        out_shape=jax.ShapeDtypeStruct((s, MOBA_W), BF16),
        scratch_shapes=[pltpu.VMEM((gate_rows, w), F32),
                        pltpu.VMEM((s, LANES), BF16),
                        pltpu.VMEM((MOBA_PAIR, s // gw, vrows, gw), BF16),
                        pltpu.VMEM((MOBA_PAIR, 2 * MOBA_HD, tq), BF16),
                        pltpu.VMEM((MOBA_PAIR, vrows, tq), F32),
                        pltpu.VMEM((MOBA_PAIR, 1, tq), F32),
                        pltpu.VMEM((s, w), BF16),
                        pltpu.SemaphoreType.DMA(())]
                       + [pltpu.VMEM((MOBA_PAIR, gw, tq), F32)] * MOBA_SLOTS,
        compiler_params=_cparams("arbitrary", "arbitrary"),
        name="moba",
    )(proj, proj, proj)


def _merge_kernel(or_ref, om_ref, ga_ref, gb_ref, wr_ref, wm_ref, o_ref):
    yr = _dot(or_ref[...], wr_ref[...])
    ym = _dot(om_ref[...], wm_ref[...])
    ga = jax.nn.sigmoid(ga_ref[...].astype(F32))
    gb = jax.nn.sigmoid(gb_ref[...].astype(F32))
    o_ref[...] = (ga * yr + gb * ym).astype(BF16)


def _merge(o_ret, o_moba, proj, w_r, w_m, tm=1024, tn=1024):
    s = o_ret.shape[0]
    tm = min(tm, s)
    d = D_MODEL
    return pl.pallas_call(
        _merge_kernel,
        grid=(s // tm, d // tn),
        in_specs=[pl.BlockSpec((tm, RET_V_W), lambda i, j: (i, 0)),
                  pl.BlockSpec((tm, MOBA_W), lambda i, j: (i, 0)),
                  pl.BlockSpec((tm, tn), lambda i, j: (i, OFF_GA // tn + j)),
                  pl.BlockSpec((tm, tn), lambda i, j: (i, OFF_GB // tn + j)),
                  pl.BlockSpec((RET_V_W, tn), lambda i, j: (0, j)),
                  pl.BlockSpec((MOBA_W, tn), lambda i, j: (0, j))],
        out_specs=pl.BlockSpec((tm, tn), lambda i, j: (i, j)),
        out_shape=jax.ShapeDtypeStruct((s, d), BF16),
        compiler_params=_cparams("arbitrary", "arbitrary"),
        name="merge",
    )(o_ret, o_moba, proj, proj, w_r, w_m)


def _mixout_kernel(m_ref, wo_ref, x_ref, npost_ref, g1_ref, npre_ref, sc_ref, sh_ref, x1_ref, u2_ref):
    for r0 in range(0, m_ref.shape[0], MIX_ROWS):
        rows = slice(r0, r0 + MIX_ROWS)
        y = _dot(m_ref[rows, :], wo_ref[...])
        x1 = x_ref[rows, :] + (1.0 + g1_ref[...]) * (_rms(y) * npost_ref[...])
        x1_ref[rows, :] = x1
        u2_ref[rows, :] = ((_rms(x1) * npre_ref[...]) * (1.0 + sc_ref[...]) + sh_ref[...]).astype(BF16)


def _mixout(merged, w_o, x, npost, g1, npre, sc2, sh2, tm=512):
    s, d = x.shape
    tm = min(tm, s)
    vec = pl.BlockSpec((1, d), lambda i: (0, 0))
    rows = pl.BlockSpec((tm, d), lambda i: (i, 0))
    return pl.pallas_call(
        _mixout_kernel,
        grid=(s // tm,),
        in_specs=[rows, pl.BlockSpec((d, d), lambda i: (0, 0), pipeline_mode=pl.Buffered(1)), rows,
                  vec, vec, vec, vec, vec],
        out_specs=[rows, rows],
        out_shape=[jax.ShapeDtypeStruct((s, d), F32), jax.ShapeDtypeStruct((s, d), BF16)],
        compiler_params=_cparams("arbitrary"),
        name="mixout",
    )(merged, w_o, x, npost, g1, npre, sc2, sh2)


def _ffn_kernel(u_ref, wa_ref, wb_ref, cwa_ref, cwb_ref, cba_ref, cbb_ref, wd_ref, x1_hbm, npost_ref, g2_ref,
                o_ref, ha_ref, hb_ref, act_ref, tail_ref, x1_ref, x1_sem):
    i = pl.program_id(0)
    j = pl.program_id(1)
    tm = u_ref.shape[0]
    tn = wa_ref.shape[1]
    halo = SUBLANES

    def residual_copy():
        return pltpu.make_async_copy(x1_hbm.at[pl.ds(pl.multiple_of(i * tm, tm), tm), :], x1_ref, x1_sem)

    @pl.when((i == 0) & (j == 0))
    def _():
        tail_ref[...] = jnp.zeros_like(tail_ref)

    @pl.when(j == 0)
    def _():
        residual_copy().start()
        o_ref[...] = jnp.zeros_like(o_ref)

    ha_ref[0:halo, :] = tail_ref[j, 0]
    hb_ref[0:halo, :] = tail_ref[j, 1]
    for c in range(tn // FFN_CHUNK):
        cs = slice(c * FFN_CHUNK, (c + 1) * FFN_CHUNK)
        ha_ref[halo:halo + tm, cs] = _dot(u_ref[...], wa_ref[:, cs])
        hb_ref[halo:halo + tm, cs] = _dot(u_ref[...], wb_ref[:, cs])
    tail_ref[j, 0] = ha_ref[tm:tm + halo, :]
    tail_ref[j, 1] = hb_ref[tm:tm + halo, :]

    def conv(hbuf, cw, cb, r0, cs):
        out = cb
        for t in range(CONV_WIDTH):
            out = out + hbuf[halo + r0 - t:halo + r0 - t + FFN_ROWS, cs] * cw[CONV_WIDTH - 1 - t:CONV_WIDTH - t, :]
        return out

    for c in range(tn // FFN_CHUNK):
        cs = slice(c * FFN_CHUNK, (c + 1) * FFN_CHUNK)
        cwa, cwb, cba, cbb = cwa_ref[:, cs], cwb_ref[:, cs], cba_ref[:, cs], cbb_ref[:, cs]
        for r0 in range(0, tm, FFN_ROWS):
            a = conv(ha_ref, cwa, cba, r0, cs)
            bgate = conv(hb_ref, cwb, cbb, r0, cs)
            half = 0.5 * a
            act_ref[r0:r0 + FFN_ROWS, cs] = ((half + half * jnp.tanh(half)) * bgate).astype(BF16)
        for n0 in range(0, o_ref.shape[1], FFN_DOWN_COLS):
            ns = slice(n0, n0 + FFN_DOWN_COLS)
            o_ref[:, ns] += _dot(act_ref[:, cs], wd_ref[cs, ns])

    @pl.when(j == pl.num_programs(1) - 1)
    def _():
        residual_copy().wait()

        def finish(rows, y):
            o_ref[rows, :] = x1_ref[rows, :] + (1.0 + g2_ref[...]) * (_rms(y) * npost_ref[...])

        _for_row_chunks(tm, lambda rows: o_ref[rows, :], finish)


def _ffn(u2, w_up, conv_w, conv_b, w_down, x1, npost, g2, tm=1024, tn=512):
    s, d = x1.shape
    tm = min(tm, s)
    nj = D_FF // tn
    assert D_FF % tn == 0
    vec = pl.BlockSpec((1, d), lambda i, j: (0, 0))
    rows = pl.BlockSpec((tm, d), lambda i, j: (i, 0))
    resid = pl.BlockSpec(memory_space=pl.ANY)
    return pl.pallas_call(
        _ffn_kernel,
        grid=(s // tm, nj),
        in_specs=[rows,
                  pl.BlockSpec((d, tn), lambda i, j: (0, j)),
                  pl.BlockSpec((d, tn), lambda i, j: (0, j + nj)),
                  pl.BlockSpec((CONV_WIDTH, tn), lambda i, j: (0, j)),
                  pl.BlockSpec((CONV_WIDTH, tn), lambda i, j: (0, j + nj)),
                  pl.BlockSpec((1, tn), lambda i, j: (0, j)),
                  pl.BlockSpec((1, tn), lambda i, j: (0, j + nj)),
                  pl.BlockSpec((tn, d), lambda i, j: (j, 0)),
                  resid, vec, vec],
        out_specs=rows,
        out_shape=jax.ShapeDtypeStruct((s, d), F32),
        scratch_shapes=[pltpu.VMEM((tm + SUBLANES, tn), F32),
                        pltpu.VMEM((tm + SUBLANES, tn), F32),
                        pltpu.VMEM((tm, tn), BF16),
                        pltpu.VMEM((nj, 2, SUBLANES, tn), F32),
                        pltpu.VMEM((tm, d), F32),
                        pltpu.SemaphoreType.DMA(())],
        compiler_params=_cparams("arbitrary", "arbitrary"),
        name="ffn",
    )(u2, w_up, w_up, conv_w, conv_w, conv_b, conv_b, w_down, x1, npost, g2)


def _block(x, c, ada_w, ada_b, mix_norm_pre, mix_norm_post, w_in, ret_w_o, moba_w_o, w_out,
           ffn_norm_pre, ffn_norm_post, ffn_w_up, ffn_conv_w, ffn_conv_b, ffn_w_down, cos2, sin2):
    d = x.shape[1]
    mod = _ada(c, ada_w, ada_b[None, :])
    sh1, sc1, g1, sh2, sc2, g2 = [mod[:, t * d:(t + 1) * d] for t in range(6)]
    proj = _inproj(x, mix_norm_pre[None, :], sc1, sh1, cos2, sin2, w_in.astype(BF16))
    o_ret = _retention(proj)
    o_moba = _moba(proj)
    merged = _merge(o_ret, o_moba, proj, ret_w_o.astype(BF16), moba_w_o.astype(BF16))
    x1, u2 = _mixout(merged, w_out.astype(BF16), x, mix_norm_post[None, :], g1,
                     ffn_norm_pre[None, :], sc2, sh2)
    return _ffn(u2, ffn_w_up.astype(BF16), ffn_conv_w, ffn_conv_b[None, :], ffn_w_down.astype(BF16),
                x1, ffn_norm_post[None, :], g2)


def kernel(x, c, ada_w, ada_b, mix_norm_pre, mix_norm_post, w_in, ret_w_o, moba_w_o, w_out,
           ffn_norm_pre, ffn_norm_post, ffn_w_up, ffn_conv_w, ffn_conv_b, ffn_w_down):
    batch, s, _ = x.shape
    depth = ada_w.shape[0]
    cos2, sin2 = _rope_tables(s)
    outs = []
    for bi in range(batch):
        xb = x[bi]
        for l in range(depth):
            xb = _block(xb, c[bi:bi + 1], ada_w[l], ada_b[l], mix_norm_pre[l], mix_norm_post[l], w_in[l],
                        ret_w_o[l], moba_w_o[l], w_out[l], ffn_norm_pre[l], ffn_norm_post[l], ffn_w_up[l],
                        ffn_conv_w[l], ffn_conv_b[l], ffn_w_down[l], cos2, sin2)
        outs.append(xb)
    return jnp.stack(outs)
```

```python
import math

import jax
import jax.numpy as jnp
from jax import lax
from jax.experimental import pallas as pl
from jax.experimental.pallas import tpu as pltpu

D_MODEL = 2048
RET_HEADS = 8
RET_DK = 128
RET_DV = 256
ROPE_BASE = 10000.0
MOBA_HEADS = 16
MOBA_HD = 128
MOBA_BLOCK = 256
MOBA_TOPK = 3
D_FF = 5632
CONV_WIDTH = 3
NORM_EPS = 1e-6
GN_EPS = 1e-5

RET_QK_W = RET_HEADS * RET_DK
RET_V_W = RET_HEADS * RET_DV
MOBA_W = MOBA_HEADS * MOBA_HD
OFF_RQ = 0
OFF_RK = OFF_RQ + RET_QK_W
OFF_RV = OFF_RK + RET_QK_W
OFF_RG = OFF_RV + RET_V_W
OFF_MQ = OFF_RG + RET_V_W
OFF_MK = OFF_MQ + MOBA_W
OFF_MV = OFF_MK + MOBA_W
OFF_GA = OFF_MV + MOBA_W
OFF_GB = OFF_GA + D_MODEL
IN_WIDTH = OFF_GB + D_MODEL

LANES = 128
SUBLANES = 8
NEG = -1e30
MOBA_QSCALE = math.log2(math.e) / math.sqrt(MOBA_HD)
ROPE_SPAN = 128
RET_GROUP = 8
MOBA_GROUP = 2
MOBA_SLOTS = 6
MOBA_PAIR = 2
FFN_CHUNK = 256
FFN_ROWS = 64
FFN_DOWN_COLS = 512
MIX_ROWS = 256
NORM_ROWS = 32
NORM_UNROLL = 4
VMEM_LIMIT = 58 * 1024 * 1024

F32 = jnp.float32
BF16 = jnp.bfloat16


def _cparams(*sem):
    return pltpu.CompilerParams(dimension_semantics=sem, vmem_limit_bytes=VMEM_LIMIT)


def _dot(a, b):
    return jnp.dot(a, b, preferred_element_type=F32)


def _dot_nt(a, b):
    return lax.dot_general(a, b, (((1,), (1,)), ((), ())), preferred_element_type=F32)


def _split_bf16(x):
    hi = x.astype(BF16)
    lo = (x - hi.astype(F32)).astype(BF16)
    return hi, lo


def _rms(x):
    return x * lax.rsqrt(jnp.mean(x * x, axis=-1, keepdims=True) + NORM_EPS)


def _silu(x):
    return x * jax.nn.sigmoid(x)


def _for_row_chunks(n_rows, load, finish):
    step = NORM_ROWS * NORM_UNROLL

    def body(r, carry):
        pieces = [pl.ds(pl.multiple_of(r * step + t * NORM_ROWS, NORM_ROWS), NORM_ROWS)
                  for t in range(NORM_UNROLL)]
        loaded = [load(rows) for rows in pieces]
        for rows, vals in zip(pieces, loaded):
            finish(rows, vals)
        return carry

    lax.fori_loop(0, n_rows // step, body, 0)


def _ada_kernel(c_ref, w_ref, b_ref, o_ref):
    c = jnp.broadcast_to(_silu(c_ref[...]), (SUBLANES, c_ref.shape[1]))
    c_hi, c_lo = _split_bf16(c)
    w_hi, w_lo = _split_bf16(w_ref[...])
    r = _dot(c_hi, w_hi) + _dot(c_lo, w_hi) + _dot(c_hi, w_lo)
    o_ref[...] = r[0:1] + b_ref[...]


def _ada(c, w, b, tn=1024):
    d, n = w.shape
    return pl.pallas_call(
        _ada_kernel,
        grid=(n // tn,),
        in_specs=[pl.BlockSpec((1, d), lambda j: (0, 0)),
                  pl.BlockSpec((d, tn), lambda j: (0, j)),
                  pl.BlockSpec((1, tn), lambda j: (0, j))],
        out_specs=pl.BlockSpec((1, tn), lambda j: (0, j)),
        out_shape=jax.ShapeDtypeStruct((1, n), F32),
        compiler_params=_cparams("arbitrary"),
        name="ada",
    )(c, w, b)


def _rope_kernel(inv_ref, cos_ref, sin_ref):
    ts = cos_ref.shape[0]
    spans = ts // ROPE_SPAN
    inv = inv_ref[...]
    lane = lax.broadcasted_iota(jnp.int32, (ROPE_SPAN, LANES), 1)
    off = lax.broadcasted_iota(jnp.int32, (ROPE_SPAN, LANES), 0).astype(F32) * inv
    cos_o, sin_o = jnp.cos(off), jnp.sin(off)
    first = pl.program_id(0) * spans
    base = ((lax.broadcasted_iota(jnp.int32, (spans, LANES), 0) + first) * ROPE_SPAN).astype(F32) * inv
    cos_b, sin_b = jnp.cos(base), jnp.sin(base)
    for b in range(spans):
        cb, sb = cos_b[b:b + 1, :], sin_b[b:b + 1, :]
        sin = sb * cos_o + cb * sin_o
        cos_ref[b * ROPE_SPAN:(b + 1) * ROPE_SPAN, :] = cb * cos_o - sb * sin_o
        sin_ref[b * ROPE_SPAN:(b + 1) * ROPE_SPAN, :] = jnp.where(lane < RET_DK // 2, -sin, sin)


def _rope_tables(s, ts=2048):
    ts = min(ts, s)
    half = RET_DK // 2
    inv = ROPE_BASE ** (-jnp.arange(half, dtype=F32) / half)
    inv2 = jnp.concatenate([inv, inv])[None, :]
    return pl.pallas_call(
        _rope_kernel,
        grid=(s // ts,),
        in_specs=[pl.BlockSpec((1, LANES), lambda i: (0, 0))],
        out_specs=[pl.BlockSpec((ts, LANES), lambda i: (i, 0))] * 2,
        out_shape=[jax.ShapeDtypeStruct((s, LANES), F32)] * 2,
        compiler_params=_cparams("arbitrary"),
        name="rope",
    )(inv2)


def _inproj_kernel(x_hbm, nw_ref, sc_ref, sh_ref, cos_ref, sin_ref, w_ref, o_ref, u_ref, x_ref, x_sem):
    i = pl.program_id(0)
    j = pl.program_id(1)
    tm, tn = o_ref.shape
    nj = IN_WIDTH // tn
    chunk = tm // nj

    def x_copy(tile, c):
        src = pl.ds(pl.multiple_of(tile * tm + c * chunk, chunk), chunk)
        dst = pl.ds(pl.multiple_of(c * chunk, chunk), chunk)
        return pltpu.make_async_copy(x_hbm.at[src, :], x_ref.at[dst, :], x_sem)

    @pl.when(j == 0)
    def _():
        @pl.when(i == 0)
        def _():
            for c in range(nj):
                x_copy(0, c).start()

        for c in range(nj):
            x_copy(i, c).wait()

        def finish(rows, x):
            y = _rms(x) * nw_ref[...]
            u_ref[rows, :] = (y * (1.0 + sc_ref[...]) + sh_ref[...]).astype(BF16)

        _for_row_chunks(tm, lambda rows: x_ref[rows, :], finish)

    @pl.when(i + 1 < pl.num_programs(0))
    def _():
        x_copy(i + 1, j).start()

    r = _dot(u_ref[...], w_ref[...])
    is_mq = (j >= OFF_MQ // tn) & (j < OFF_MK // tn)
    o_ref[...] = (r * jnp.where(is_mq, MOBA_QSCALE, 1.0).astype(F32)).astype(BF16)

    @pl.when(j * tn < 2 * RET_QK_W)
    def _():
        cos = cos_ref[...]
        sin = sin_ref[...]
        for h in range(tn // RET_DK):
            scale = jnp.where(j * tn + h * RET_DK >= RET_QK_W, RET_DK ** -0.5, 1.0).astype(F32)
            xh = r[:, h * RET_DK:(h + 1) * RET_DK]
            rot = xh * cos + pltpu.roll(xh, RET_DK // 2, axis=1) * sin
            o_ref[:, h * RET_DK:(h + 1) * RET_DK] = (rot * scale).astype(BF16)


def _inproj(x, nw, sc, sh, cos2, sin2, w, tm=1024, tn=2048):
    s, d = x.shape
    n = w.shape[1]
    tm = min(tm, s)
    nj = n // tn
    assert n == IN_WIDTH and tm % nj == 0 and (2 * RET_QK_W) % tn == 0 and OFF_MQ % tn == 0 and OFF_MK % tn == 0
    vec = pl.BlockSpec((1, d), lambda i, j: (0, 0))
    tab = pl.BlockSpec((tm, LANES), lambda i, j: (i, 0))
    return pl.pallas_call(
        _inproj_kernel,
        grid=(s // tm, nj),
        in_specs=[pl.BlockSpec(memory_space=pl.ANY), vec, vec, vec, tab, tab,
                  pl.BlockSpec((d, tn), lambda i, j: (0, j))],
        out_specs=pl.BlockSpec((tm, tn), lambda i, j: (i, j)),
        out_shape=jax.ShapeDtypeStruct((s, n), BF16),
        scratch_shapes=[pltpu.VMEM((tm, d), BF16),
                        pltpu.VMEM((tm, d), F32),
                        pltpu.SemaphoreType.DMA(())],
        compiler_params=_cparams("arbitrary", "arbitrary"),
        name="inproj",
    )(x, nw, sc, sh, cos2, sin2, w)


def _ret_kernel(lg_ref, q_ref, k_ref, v_ref, g_ref, o_ref, r_ref, dec_ref, xi_ref, zeta_ref):
    n = pl.program_id(1)
    c = q_ref.shape[0]
    dk, dv = RET_DK, RET_DV

    @pl.when(n == 0)
    def _():
        r_ref[...] = jnp.zeros_like(r_ref)
        d = lax.broadcasted_iota(jnp.int32, (c, c), 0) - lax.broadcasted_iota(jnp.int32, (c, c), 1)
        row = lax.broadcasted_iota(jnp.int32, (c, dk), 0)
        for h in range(RET_GROUP):
            lg = lg_ref[h, 0:1, 0:1]
            dec_ref[h] = jnp.where(d >= 0, jnp.exp(jnp.maximum(d, 0).astype(F32) * lg), 0.0)
            xi_ref[h] = jnp.exp((row + 1).astype(F32) * lg)
            zeta_ref[h] = jnp.exp((c - 1 - row).astype(F32) * lg)

    for h in range(RET_GROUP):
        q = q_ref[:, h * dk:(h + 1) * dk]
        k = k_ref[:, h * dk:(h + 1) * dk]
        v = v_ref[:, h * dv:(h + 1) * dv]
        s = _dot_nt(q, k) * dec_ref[h]
        inner = _dot(s.astype(BF16), v)
        r = r_ref[h]
        cross = _dot((q.astype(F32) * xi_ref[h]).astype(BF16), r.astype(BF16))
        o = inner + cross
        kz = (k.astype(F32) * zeta_ref[h]).T.astype(BF16)
        r_ref[h] = r * jnp.exp(lg_ref[h, 0:1, 0:1] * c) + _dot(kz, v)
        oc = o - jnp.mean(o, axis=-1, keepdims=True)
        on = oc * lax.rsqrt(jnp.mean(oc * oc, axis=-1, keepdims=True) + GN_EPS)
        o_ref[:, h * dv:(h + 1) * dv] = (_silu(g_ref[:, h * dv:(h + 1) * dv].astype(F32)) * on).astype(BF16)


def _retention(proj, chunk=512):
    s = proj.shape[0]
    chunk = min(chunk, s)
    log_g = jnp.log1p(-jnp.exp2(-5.0 - jnp.arange(RET_HEADS, dtype=F32)))
    lg = jnp.broadcast_to(log_g[:, None, None], (RET_HEADS, SUBLANES, LANES))
    qk_w, v_w = RET_GROUP * RET_DK, RET_GROUP * RET_DV
    assert RET_HEADS % RET_GROUP == 0
    return pl.pallas_call(
        _ret_kernel,
        grid=(RET_HEADS // RET_GROUP, s // chunk),
        in_specs=[pl.BlockSpec((RET_GROUP, SUBLANES, LANES), lambda h, n: (h, 0, 0)),
                  pl.BlockSpec((chunk, qk_w), lambda h, n: (n, OFF_RQ // qk_w + h)),
                  pl.BlockSpec((chunk, qk_w), lambda h, n: (n, OFF_RK // qk_w + h)),
                  pl.BlockSpec((chunk, v_w), lambda h, n: (n, OFF_RV // v_w + h)),
                  pl.BlockSpec((chunk, v_w), lambda h, n: (n, OFF_RG // v_w + h))],
        out_specs=pl.BlockSpec((chunk, v_w), lambda h, n: (n, h)),
        out_shape=jax.ShapeDtypeStruct((s, RET_V_W), BF16),
        scratch_shapes=[pltpu.VMEM((RET_GROUP, RET_DK, RET_DV), F32),
                        pltpu.VMEM((RET_GROUP, chunk, chunk), F32),
                        pltpu.VMEM((RET_GROUP, chunk, RET_DK), F32),
                        pltpu.VMEM((RET_GROUP, chunk, RET_DK), F32)],
        compiler_params=_cparams("arbitrary", "arbitrary"),
        name="ret",
    )(lg, proj, proj, proj, proj)


def _moba_kernel(q_ref, k_ref, v_hbm, o_ref, km_ref, oh_ref, vt_ref, qt_ref, acc_ref, m_ref, v_ref, v_sem,
                 *slots):
    hp = pl.program_id(0)
    bq = pl.program_id(1)
    tq = q_ref.shape[0]
    hd = MOBA_HD
    blk = MOBA_BLOCK
    nb = k_ref.shape[0] // blk
    gw = MOBA_GROUP * blk
    vrows = vt_ref.shape[2]

    @pl.when((hp == 0) & (bq == 0))
    def _():
        lane_i = lax.broadcasted_iota(jnp.int32, (blk, LANES), 1)

        def body(n, carry):
            oh_ref[pl.ds(pl.multiple_of(n * blk, blk), blk), :] = jnp.where(lane_i == n, 1.0, 0.0).astype(BF16)
            return carry

        lax.fori_loop(0, nb, body, 0)

    def value_copy(pair):
        cols = pl.ds(pl.multiple_of(OFF_MV + pair * (MOBA_PAIR * hd), MOBA_PAIR * hd), MOBA_PAIR * hd)
        return pltpu.make_async_copy(v_hbm.at[:, cols], v_ref, v_sem)

    @pl.when((hp == 0) & (bq == 0))
    def _():
        value_copy(0).start()

    @pl.when(bq == 0)
    def _():
        value_copy(hp).wait()
        km_ref[...] = jnp.zeros_like(km_ref)

        def body(g, carry):
            for t in range(MOBA_GROUP):
                n = g * MOBA_GROUP + t
                rows = pl.ds(pl.multiple_of(n * blk, blk), blk)
                km_ref[pl.ds(n, 1), :] = jnp.mean(k_ref[rows, :].astype(F32), axis=0, keepdims=True)
                for h in range(MOBA_PAIR):
                    vt_ref[h, g, 0:hd, t * blk:(t + 1) * blk] = (
                        v_ref[rows, h * hd:(h + 1) * hd].astype(F32).T.astype(BF16))
            for h in range(MOBA_PAIR):
                vt_ref[h, g, hd:vrows, :] = jnp.ones((vrows - hd, gw), BF16)
            return carry

        lax.fori_loop(0, nb // MOBA_GROUP, body, 0)

        @pl.when(hp + 1 < pl.num_programs(0))
        def _():
            value_copy(hp + 1).start()

    gr = km_ref.shape[0]
    row_i = lax.broadcasted_iota(jnp.int32, (gr, tq), 0)
    row_f = row_i.astype(F32)
    own = bq * (tq // blk) + lax.broadcasted_iota(jnp.int32, (gr, tq), 1) // blk
    for h in range(MOBA_PAIR):
        qt = q_ref[:, h * hd:(h + 1) * hd].astype(F32).T.astype(BF16)
        km_hi, km_lo = _split_bf16(km_ref[:, h * hd:(h + 1) * hd])
        g = jnp.where(row_i < own, _dot(km_hi, qt) + _dot(km_lo, qt), NEG)
        bias = jnp.full((gr, tq), NEG, F32)
        for _ in range(MOBA_TOPK):
            mx = jnp.max(g, axis=0, keepdims=True)
            idx = jnp.min(jnp.where(g == mx, row_f, float(gr)), axis=0, keepdims=True)
            hit = row_f == idx
            bias = jnp.where(hit, jnp.where(mx > 0.5 * NEG, 0.0, NEG), bias)
            g = jnp.where(hit, 2.0 * NEG, g)
        bias = jnp.where(row_i == own, 0.0, bias)
        qt_ref[h] = jnp.concatenate(
            [qt, bias.astype(BF16), jnp.full((LANES - gr, tq), NEG, BF16)], axis=0)
        m_ref[h] = jnp.full((1, tq), NEG, F32)
        acc_ref[h] = jnp.zeros((vrows, tq), F32)

    def scores(g, slot_ref):
        rows = pl.ds(pl.multiple_of(g * gw, gw), gw)
        oh = oh_ref[rows, :]
        for h in range(MOBA_PAIR):
            k_ext = jnp.concatenate([k_ref[rows, h * hd:(h + 1) * hd], oh], axis=1)
            slot_ref[h] = _dot(k_ext, qt_ref[h])

    def accumulate(g, slot_ref, causal):
        if causal:
            own_rows = pl.ds(pl.multiple_of((bq * tq) % gw, tq), tq)
            keep = lax.broadcasted_iota(jnp.int32, (tq, tq), 0) <= lax.broadcasted_iota(jnp.int32, (tq, tq), 1)
            for h in range(MOBA_PAIR):
                slot_ref[h, own_rows, :] = jnp.where(keep, slot_ref[h, own_rows, :], NEG)
        for h in range(MOBA_PAIR):
            s = slot_ref[h]
            m_old = m_ref[h]
            m_new = jnp.maximum(m_old, jnp.max(s, axis=0, keepdims=True))
            p = jnp.exp2(s - m_new).astype(BF16)
            acc_ref[h] = jnp.exp2(m_old - m_new) * acc_ref[h] + _dot(vt_ref[h, g], p)
            m_ref[h] = m_new

    nslot = len(slots)
    full_groups = (bq * tq) // gw
    scores(0, slots[0])

    def body(i, carry):
        g = nslot * i
        for u in range(nslot):
            scores(g + u + 1, slots[(u + 1) % nslot])
            accumulate(g + u, slots[u], False)
        return carry

    lax.fori_loop(0, full_groups // nslot, body, 0)
    base = (full_groups // nslot) * nslot
    for r in range(nslot):
        @pl.when(full_groups - base == r)
        def _(r=r):
            for u in range(r):
                scores(base + u + 1, slots[u + 1])
                accumulate(base + u, slots[u], False)
            accumulate(base + r, slots[r], True)

    for h in range(MOBA_PAIR):
        acc = acc_ref[h]
        o_ref[:, h * hd:(h + 1) * hd] = (acc[0:hd, :] / acc[hd:hd + 1, :]).T.astype(BF16)


def _moba(proj, tq=2 * MOBA_BLOCK):
    s = proj.shape[0]
    nb = s // MOBA_BLOCK
    w = MOBA_PAIR * MOBA_HD
    gw = MOBA_GROUP * MOBA_BLOCK
    bf16_rows = 2 * SUBLANES
    vrows = MOBA_HD + bf16_rows
    gate_rows = -(-nb // bf16_rows) * bf16_rows
    assert s % gw == 0 and gate_rows <= LANES and MOBA_HEADS % MOBA_PAIR == 0
    assert tq % MOBA_BLOCK == 0 and gw % tq == 0
    return pl.pallas_call(
        _moba_kernel,
        grid=(MOBA_HEADS // MOBA_PAIR, s // tq),
        in_specs=[pl.BlockSpec((tq, w), lambda h, b: (b, OFF_MQ // w + h)),
                  pl.BlockSpec((s, w), lambda h, b: (0, OFF_MK // w + h)),
                  pl.BlockSpec(memory_space=pl.ANY)],
        out_specs=pl.BlockSpec((tq, w), lambda h, b: (b, h)),
        out_shape=jax.ShapeDtypeStruct((s, MOBA_W), BF16),
        scratch_shapes=[pltpu.VMEM((gate_rows, w), F32),
                        pltpu.VMEM((s, LANES), BF16),
                        pltpu.VMEM((MOBA_PAIR, s // gw, vrows, gw), BF16),
                        pltpu.VMEM((MOBA_PAIR, 2 * MOBA_HD, tq), BF16),
                        pltpu.VMEM((MOBA_PAIR, vrows, tq), F32),
                        pltpu.VMEM((MOBA_PAIR, 1, tq), F32),
                        pltpu.VMEM((s, w), BF16),
                        pltpu.SemaphoreType.DMA(())]
                       + [pltpu.VMEM((MOBA_PAIR, gw, tq), F32)] * MOBA_SLOTS,
        compiler_params=_cparams("arbitrary", "arbitrary"),
        name="moba",
    )(proj, proj, proj)


def _merge_kernel(or_ref, om_ref, ga_ref, gb_ref, wr_ref, wm_ref, o_ref):
    yr = _dot(or_ref[...], wr_ref[...])
    ym = _dot(om_ref[...], wm_ref[...])
    ga = jax.nn.sigmoid(ga_ref[...].astype(F32))
    gb = jax.nn.sigmoid(gb_ref[...].astype(F32))
    o_ref[...] = (ga * yr + gb * ym).astype(BF16)


def _merge(o_ret, o_moba, proj, w_r, w_m, tm=1024, tn=1024):
    s = o_ret.shape[0]
    tm = min(tm, s)
    d = D_MODEL
    return pl.pallas_call(
        _merge_kernel,
        grid=(s // tm, d // tn),
        in_specs=[pl.BlockSpec((tm, RET_V_W), lambda i, j: (i, 0)),
                  pl.BlockSpec((tm, MOBA_W), lambda i, j: (i, 0)),
                  pl.BlockSpec((tm, tn), lambda i, j: (i, OFF_GA // tn + j)),
                  pl.BlockSpec((tm, tn), lambda i, j: (i, OFF_GB // tn + j)),
                  pl.BlockSpec((RET_V_W, tn), lambda i, j: (0, j)),
                  pl.BlockSpec((MOBA_W, tn), lambda i, j: (0, j))],
        out_specs=pl.BlockSpec((tm, tn), lambda i, j: (i, j)),
        out_shape=jax.ShapeDtypeStruct((s, d), BF16),
        compiler_params=_cparams("arbitrary", "arbitrary"),
        name="merge",
    )(o_ret, o_moba, proj, proj, w_r, w_m)


def _mixout_kernel(m_ref, wo_ref, x_ref, npost_ref, g1_ref, npre_ref, sc_ref, sh_ref, x1_ref, u2_ref):
    for r0 in range(0, m_ref.shape[0], MIX_ROWS):
        rows = slice(r0, r0 + MIX_ROWS)
        y = _dot(m_ref[rows, :], wo_ref[...])
        x1 = x_ref[rows, :] + (1.0 + g1_ref[...]) * (_rms(y) * npost_ref[...])
        x1_ref[rows, :] = x1
        u2_ref[rows, :] = ((_rms(x1) * npre_ref[...]) * (1.0 + sc_ref[...]) + sh_ref[...]).astype(BF16)


def _mixout(merged, w_o, x, npost, g1, npre, sc2, sh2, tm=512):
    s, d = x.shape
    tm = min(tm, s)
    vec = pl.BlockSpec((1, d), lambda i: (0, 0))
    rows = pl.BlockSpec((tm, d), lambda i: (i, 0))
    return pl.pallas_call(
        _mixout_kernel,
        grid=(s // tm,),
        in_specs=[rows, pl.BlockSpec((d, d), lambda i: (0, 0), pipeline_mode=pl.Buffered(1)), rows,
                  vec, vec, vec, vec, vec],
        out_specs=[rows, rows],
        out_shape=[jax.ShapeDtypeStruct((s, d), F32), jax.ShapeDtypeStruct((s, d), BF16)],
        compiler_params=_cparams("arbitrary"),
        name="mixout",
    )(merged, w_o, x, npost, g1, npre, sc2, sh2)


def _ffn_kernel(u_ref, wa_ref, wb_ref, cwa_ref, cwb_ref, cba_ref, cbb_ref, wd_ref, x1_hbm, npost_ref, g2_ref,
                o_ref, ha_ref, hb_ref, act_ref, tail_ref, x1_ref, x1_sem):
    i = pl.program_id(0)
    j = pl.program_id(1)
    tm = u_ref.shape[0]
    tn = wa_ref.shape[1]
    halo = SUBLANES

    def residual_copy():
        return pltpu.make_async_copy(x1_hbm.at[pl.ds(pl.multiple_of(i * tm, tm), tm), :], x1_ref, x1_sem)

    @pl.when((i == 0) & (j == 0))
    def _():
        tail_ref[...] = jnp.zeros_like(tail_ref)

    @pl.when(j == 0)
    def _():
        residual_copy().start()
        o_ref[...] = jnp.zeros_like(o_ref)

    ha_ref[0:halo, :] = tail_ref[j, 0]
    hb_ref[0:halo, :] = tail_ref[j, 1]
    for c in range(tn // FFN_CHUNK):
        cs = slice(c * FFN_CHUNK, (c + 1) * FFN_CHUNK)
        ha_ref[halo:halo + tm, cs] = _dot(u_ref[...], wa_ref[:, cs])
        hb_ref[halo:halo + tm, cs] = _dot(u_ref[...], wb_ref[:, cs])
    tail_ref[j, 0] = ha_ref[tm:tm + halo, :]
    tail_ref[j, 1] = hb_ref[tm:tm + halo, :]

    def conv(hbuf, cw, cb, r0, cs):
        out = cb
        for t in range(CONV_WIDTH):
            out = out + hbuf[halo + r0 - t:halo + r0 - t + FFN_ROWS, cs] * cw[CONV_WIDTH - 1 - t:CONV_WIDTH - t, :]
        return out

    for c in range(tn // FFN_CHUNK):
        cs = slice(c * FFN_CHUNK, (c + 1) * FFN_CHUNK)
        cwa, cwb, cba, cbb = cwa_ref[:, cs], cwb_ref[:, cs], cba_ref[:, cs], cbb_ref[:, cs]
        for r0 in range(0, tm, FFN_ROWS):
            a = conv(ha_ref, cwa, cba, r0, cs)
            bgate = conv(hb_ref, cwb, cbb, r0, cs)
            half = 0.5 * a
            act_ref[r0:r0 + FFN_ROWS, cs] = ((half + half * jnp.tanh(half)) * bgate).astype(BF16)
        for n0 in range(0, o_ref.shape[1], FFN_DOWN_COLS):
            ns = slice(n0, n0 + FFN_DOWN_COLS)
            o_ref[:, ns] += _dot(act_ref[:, cs], wd_ref[cs, ns])

    @pl.when(j == pl.num_programs(1) - 1)
    def _():
        residual_copy().wait()

        def finish(rows, y):
            o_ref[rows, :] = x1_ref[rows, :] + (1.0 + g2_ref[...]) * (_rms(y) * npost_ref[...])

        _for_row_chunks(tm, lambda rows: o_ref[rows, :], finish)


def _ffn(u2, w_up, conv_w, conv_b, w_down, x1, npost, g2, tm=1024, tn=512):
    s, d = x1.shape
    tm = min(tm, s)
    nj = D_FF // tn
    assert D_FF % tn == 0
    vec = pl.BlockSpec((1, d), lambda i, j: (0, 0))
    rows = pl.BlockSpec((tm, d), lambda i, j: (i, 0))
    resid = pl.BlockSpec(memory_space=pl.ANY)
    return pl.pallas_call(
        _ffn_kernel,
        grid=(s // tm, nj),
        in_specs=[rows,
                  pl.BlockSpec((d, tn), lambda i, j: (0, j)),
                  pl.BlockSpec((d, tn), lambda i, j: (0, j + nj)),
                  pl.BlockSpec((CONV_WIDTH, tn), lambda i, j: (0, j)),
                  pl.BlockSpec((CONV_WIDTH, tn), lambda i, j: (0, j + nj)),
                  pl.BlockSpec((1, tn), lambda i, j: (0, j)),
                  pl.BlockSpec((1, tn), lambda i, j: (0, j + nj)),
                  pl.BlockSpec((tn, d), lambda i, j: (j, 0)),
                  resid, vec, vec],
        out_specs=rows,
        out_shape=jax.ShapeDtypeStruct((s, d), F32),
        scratch_shapes=[pltpu.VMEM((tm + SUBLANES, tn), F32),
                        pltpu.VMEM((tm + SUBLANES, tn), F32),
                        pltpu.VMEM((tm, tn), BF16),
                        pltpu.VMEM((nj, 2, SUBLANES, tn), F32),
                        pltpu.VMEM((tm, d), F32),
                        pltpu.SemaphoreType.DMA(())],
        compiler_params=_cparams("arbitrary", "arbitrary"),
        name="ffn",
    )(u2, w_up, w_up, conv_w, conv_w, conv_b, conv_b, w_down, x1, npost, g2)


def _block(x, c, ada_w, ada_b, mix_norm_pre, mix_norm_post, w_in, ret_w_o, moba_w_o, w_out,
           ffn_norm_pre, ffn_norm_post, ffn_w_up, ffn_conv_w, ffn_conv_b, ffn_w_down, cos2, sin2):
    d = x.shape[1]
    mod = _ada(c, ada_w, ada_b[None, :])
    sh1, sc1, g1, sh2, sc2, g2 = [mod[:, t * d:(t + 1) * d] for t in range(6)]
    proj = _inproj(x, mix_norm_pre[None, :], sc1, sh1, cos2, sin2, w_in.astype(BF16))
    o_ret = _retention(proj)
    o_moba = _moba(proj)
    merged = _merge(o_ret, o_moba, proj, ret_w_o.astype(BF16), moba_w_o.astype(BF16))
    x1, u2 = _mixout(merged, w_out.astype(BF16), x, mix_norm_post[None, :], g1,
                     ffn_norm_pre[None, :], sc2, sh2)
    return _ffn(u2, ffn_w_up.astype(BF16), ffn_conv_w, ffn_conv_b[None, :], ffn_w_down.astype(BF16),
                x1, ffn_norm_post[None, :], g2)


def kernel(x, c, ada_w, ada_b, mix_norm_pre, mix_norm_post, w_in, ret_w_o, moba_w_o, w_out,
           ffn_norm_pre, ffn_norm_post, ffn_w_up, ffn_conv_w, ffn_conv_b, ffn_w_down):
    batch, s, _ = x.shape
    depth = ada_w.shape[0]
    cos2, sin2 = _rope_tables(s)
    outs = []
    for bi in range(batch):
        xb = x[bi]
        for l in range(depth):
            xb = _block(xb, c[bi:bi + 1], ada_w[l], ada_b[l], mix_norm_pre[l], mix_norm_post[l], w_in[l],
                        ret_w_o[l], moba_w_o[l], w_out[l], ffn_norm_pre[l], ffn_norm_post[l], ffn_w_up[l],
                        ffn_conv_w[l], ffn_conv_b[l], ffn_w_down[l], cos2, sin2)
        outs.append(xb)
    return jnp.stack(outs)
```
